```python
import math
import jax, jax.numpy as jnp
from jax import lax
import numpy as np

D_MODEL = 1024
BATCH = 16
SEQ = 2048
DEPTH = 4

MEM_LEN = 256
RMS_EPS = 1e-6
SB_HEADS = 16
SB_HEAD_DIM = 64
SB_WIDTH = SB_HEADS * SB_HEAD_DIM
SB_BLOCK = 128
SSM_EXPAND = 2
SSM_INNER = SSM_EXPAND * D_MODEL
SSM_HEAD_DIM = 64
SSM_HEADS = SSM_INNER // SSM_HEAD_DIM
SSM_GROUPS = 4
SSM_STATE = 128
SSM_CONV = 4
SSM_CHUNK = 128
SSM_CONV_DIM = SSM_INNER + 2 * SSM_GROUPS * SSM_STATE
XA_HEADS = 4
XA_HEAD_DIM = D_MODEL // XA_HEADS
FFN_HIDDEN = ((8 * D_MODEL + 767) // 768) * 256
IN_SIZES = (SB_WIDTH, SB_WIDTH, SB_WIDTH, SSM_INNER, SSM_CONV_DIM, SSM_HEADS, D_MODEL, D_MODEL)
IN_WIDTH = 3 * SB_WIDTH + SSM_INNER + SSM_CONV_DIM + SSM_HEADS + 2 * D_MODEL

kernel_name = "hybrid_stickbreak_ssd_gated_block"


def _split(a, sizes):
    idx = tuple(int(i) for i in np.cumsum(sizes)[:-1])
    return jnp.split(a, idx, axis=-1)


def rms_norm(x, g):
    xf = x.astype(jnp.float32)
    y = xf * lax.rsqrt(jnp.mean(xf * xf, axis=-1, keepdims=True) + RMS_EPS)
    return (y * g.astype(jnp.float32)).astype(x.dtype)


def stick_breaking_attention(q, k, v):
    bsz, seq = q.shape[0], q.shape[1]
    qf = jnp.swapaxes(q.astype(jnp.float32), 1, 2) * (SB_HEAD_DIM ** -0.5)
    kf = jnp.swapaxes(k.astype(jnp.float32), 1, 2)
    vf = jnp.swapaxes(v.astype(jnp.float32), 1, 2)
    outs = []
    for blk in range(seq // SB_BLOCK):
        t0 = blk * SB_BLOCK
        t1 = t0 + SB_BLOCK
        z = jnp.einsum('bhtd,bhsd->bhts', qf[:, :, t0:t1], kf[:, :, :t1])
        causal = jnp.arange(t1)[None, :] < (t0 + jnp.arange(SB_BLOCK))[:, None]
        log_1mb = jnp.where(causal, jax.nn.log_sigmoid(-z), 0.0)
        tail = lax.cumsum(log_1mb, axis=3, reverse=True)
        w = jnp.where(causal, jnp.exp(z + tail), 0.0)
        outs.append(jnp.einsum('bhts,bhsd->bhtd', w, vf[:, :, :t1]))
    o = jnp.concatenate(outs, axis=2)
    return jnp.swapaxes(o, 1, 2).reshape(bsz, seq, SB_WIDTH)


def ssd_chunked(x, dt, a, bm, cm):
    b, l, h, p = x.shape
    g, n = bm.shape[2], bm.shape[3]
    e = h // g
    c = l // SSM_CHUNK
    xc = (x * dt[..., None]).reshape(b, c, SSM_CHUNK, g, e, p)
    a_dt = (dt * a).reshape(b, c, SSM_CHUNK, g, e).transpose(0, 1, 3, 4, 2)
    a_cs = jnp.cumsum(a_dt, axis=-1)
    bc = bm.reshape(b, c, SSM_CHUNK, g, n)
    cc = cm.reshape(b, c, SSM_CHUNK, g, n)
    tril = jnp.tril(jnp.ones((SSM_CHUNK, SSM_CHUNK), dtype=bool))
    decay = jnp.exp(jnp.where(tril, a_cs[..., :, None] - a_cs[..., None, :], -jnp.inf))
    cb = jnp.einsum('bclgn,bcsgn->bcgls', cc, bc)
    y_diag = jnp.einsum('bcgls,bcgels,bcsgep->bclgep', cb, decay, xc)
    decay_to_end = jnp.exp(a_cs[..., -1:] - a_cs)
    states = jnp.einsum('bclgn,bcgel,bclgep->bcgepn', bc, decay_to_end, xc)
    chunk_decay = jnp.exp(a_cs[..., -1])

    def step(prev, inp):
        st, dec = inp
        return prev * dec[..., None, None] + st, prev

    init = jnp.zeros((b, g, e, p, n), dtype=states.dtype)
    _, prev_states = lax.scan(step, init, (jnp.moveaxis(states, 1, 0), jnp.moveaxis(chunk_decay, 1, 0)))
    prev_states = jnp.moveaxis(prev_states, 0, 1)
    y_off = jnp.einsum('bclgn,bcgepn,bcgel->bclgep', cc, prev_states, jnp.exp(a_cs))
    return (y_diag + y_off).reshape(b, l, h, p)


def ssd_branch(z, xbc, dt_raw, conv_w, conv_b, dt_bias, a_log, d_skip, g_norm):
    bsz, seq = xbc.shape[0], xbc.shape[1]
    xbc = lax.conv_general_dilated(
        xbc, conv_w[:, None, :].astype(xbc.dtype), window_strides=(1,),
        padding=[(SSM_CONV - 1, 0)], dimension_numbers=('NWC', 'WIO', 'NWC'),
        feature_group_count=SSM_CONV_DIM)
    xbc = jax.nn.silu(xbc.astype(jnp.float32) + conv_b.astype(jnp.float32))
    xs, bm, cm = _split(xbc, (SSM_INNER, SSM_GROUPS * SSM_STATE, SSM_GROUPS * SSM_STATE))
    dt = jax.nn.softplus(dt_raw.astype(jnp.float32) + dt_bias.astype(jnp.float32))
    a = -jnp.exp(a_log.astype(jnp.float32))
    xh = xs.reshape(bsz, seq, SSM_HEADS, SSM_HEAD_DIM)
    y = ssd_chunked(xh, dt, a,
                    bm.reshape(bsz, seq, SSM_GROUPS, SSM_STATE),
                    cm.reshape(bsz, seq, SSM_GROUPS, SSM_STATE))
    y = y + d_skip.astype(jnp.float32)[:, None] * xh
    y = y.reshape(bsz, seq, SSM_INNER) * jax.nn.silu(z.astype(jnp.float32))
    yg = y.reshape(bsz, seq, SSM_GROUPS, SSM_INNER // SSM_GROUPS)
    yg = yg * lax.rsqrt(jnp.mean(yg * yg, axis=-1, keepdims=True) + RMS_EPS)
    return yg.reshape(bsz, seq, SSM_INNER) * g_norm.astype(jnp.float32)


def memory_cross_attention(h, mem_n, w_xq, w_xkv, w_xo):
    bsz, seq = h.shape[0], h.shape[1]
    q = (h @ w_xq).reshape(bsz, seq, XA_HEADS, XA_HEAD_DIM)
    k, v = _split(mem_n @ w_xkv, (D_MODEL, D_MODEL))
    k = k.reshape(bsz, MEM_LEN, XA_HEADS, XA_HEAD_DIM)
    v = v.reshape(bsz, MEM_LEN, XA_HEADS, XA_HEAD_DIM)
    s = jnp.einsum('bshd,bmhd->bhsm', q.astype(jnp.float32), k.astype(jnp.float32)) * (XA_HEAD_DIM ** -0.5)
    p = jax.nn.softmax(s, axis=-1)
    o = jnp.einsum('bhsm,bmhd->bshd', p, v.astype(jnp.float32)).reshape(bsz, seq, D_MODEL)
    return o.astype(h.dtype) @ w_xo


def swiglu(h, w_gu, w_down):
    gate, up = _split(h @ w_gu, (FFN_HIDDEN, FFN_HIDDEN))
    return (jax.nn.silu(gate) * up) @ w_down


def _fwd_setup_inputs(seed: int = 0) -> dict:
    key = jax.random.key(seed)
    ks = jax.random.split(key, 32)
    f32 = jnp.float32

    def dense(k, shape, fan_in):
        return jax.random.normal(k, shape, f32) * (fan_in ** -0.5)

    def gain(k, shape):
        return 1.0 + 0.02 * jax.random.normal(k, shape, f32)

    dt0 = jnp.exp(jax.random.uniform(ks[5], (DEPTH, SSM_HEADS), f32,
                                     minval=math.log(1e-3), maxval=math.log(1e-1)))
    dt_bias = dt0 + jnp.log(-jnp.expm1(-dt0))
    a_log = jnp.log(jax.random.uniform(ks[6], (DEPTH, SSM_HEADS), f32, minval=1.0, maxval=16.0))
    return {
        "x": jax.random.normal(ks[0], (BATCH, SEQ, D_MODEL), f32),
        "mem": jax.random.normal(ks[1], (BATCH, MEM_LEN, D_MODEL), f32),
        "g_pre_mix": gain(ks[2], (DEPTH, D_MODEL)),
        "w_in": dense(ks[3], (DEPTH, D_MODEL, IN_WIDTH), D_MODEL),
        "conv_w": dense(ks[4], (DEPTH, SSM_CONV, SSM_CONV_DIM), SSM_CONV),
        "conv_b": 0.01 * jax.random.normal(ks[7], (DEPTH, SSM_CONV_DIM), f32),
        "dt_bias": dt_bias,
        "a_log": a_log,
        "d_skip": 1.0 + 0.1 * jax.random.normal(ks[8], (DEPTH, SSM_HEADS), f32),
        "g_ssm_norm": gain(ks[9], (DEPTH, SSM_INNER)),
        "w_br_att": dense(ks[10], (DEPTH, SB_WIDTH, D_MODEL), SB_WIDTH),
        "w_br_ssm": dense(ks[11], (DEPTH, SSM_INNER, D_MODEL), SSM_INNER),
        "w_mix_out": dense(ks[12], (DEPTH, D_MODEL, D_MODEL), D_MODEL),
        "g_post_mix": gain(ks[13], (DEPTH, D_MODEL)),
        "g_pre_xa": gain(ks[14], (DEPTH, D_MODEL)),
        "g_mem": gain(ks[15], (DEPTH, D_MODEL)),
        "w_xq": dense(ks[16], (DEPTH, D_MODEL, D_MODEL), D_MODEL),
        "w_xkv": dense(ks[17], (DEPTH, D_MODEL, 2 * D_MODEL), D_MODEL),
        "w_xo": dense(ks[18], (DEPTH, D_MODEL, D_MODEL), D_MODEL),
        "g_post_xa": gain(ks[19], (DEPTH, D_MODEL)),
        "g_pre_ffn": gain(ks[20], (DEPTH, D_MODEL)),
        "w_gu": dense(ks[21], (DEPTH, D_MODEL, 2 * FFN_HIDDEN), D_MODEL),
        "w_down": dense(ks[22], (DEPTH, FFN_HIDDEN, D_MODEL), FFN_HIDDEN),
        "g_post_ffn": gain(ks[23], (DEPTH, D_MODEL)),
    }


def _fwd_reference(x, mem, g_pre_mix, w_in, conv_w, conv_b, dt_bias, a_log, d_skip, g_ssm_norm,
              w_br_att, w_br_ssm, w_mix_out, g_post_mix, g_pre_xa, g_mem, w_xq, w_xkv, w_xo,
              g_post_xa, g_pre_ffn, w_gu, w_down, g_post_ffn):
    bsz, seq = x.shape[0], x.shape[1]
    for l in range(DEPTH):
        h = rms_norm(x, g_pre_mix[l])
        q, k, v, z, xbc, dt_raw, ga, gs = _split(h @ w_in[l], IN_SIZES)
        o_att = stick_breaking_attention(
            q.reshape(bsz, seq, SB_HEADS, SB_HEAD_DIM),
            k.reshape(bsz, seq, SB_HEADS, SB_HEAD_DIM),
            v.reshape(bsz, seq, SB_HEADS, SB_HEAD_DIM)).astype(x.dtype)
        o_ssm = ssd_branch(z, xbc, dt_raw, conv_w[l], conv_b[l], dt_bias[l], a_log[l],
                           d_skip[l], g_ssm_norm[l]).astype(x.dtype)
        merged = jax.nn.sigmoid(ga) * (o_att @ w_br_att[l]) + jax.nn.sigmoid(gs) * (o_ssm @ w_br_ssm[l])
        x = x + rms_norm(merged @ w_mix_out[l], g_post_mix[l])
        h = rms_norm(x, g_pre_xa[l])
        mem_n = rms_norm(mem, g_mem[l])
        x = x + rms_norm(memory_cross_attention(h, mem_n, w_xq[l], w_xkv[l], w_xo[l]), g_post_xa[l])
        h = rms_norm(x, g_pre_ffn[l])
        x = x + rms_norm(swiglu(h, w_gu[l], w_down[l]), g_post_ffn[l])
    return x


import jax as _jax
import jax.numpy as _jnp

TWIN_FORMAT = 'train_step'
FWD_PARAMS = ['x', 'mem', 'g_pre_mix', 'w_in', 'conv_w', 'conv_b', 'dt_bias', 'a_log', 'd_skip', 'g_ssm_norm', 'w_br_att', 'w_br_ssm', 'w_mix_out', 'g_post_mix', 'g_pre_xa', 'g_mem', 'w_xq', 'w_xkv', 'w_xo', 'g_post_xa', 'g_pre_ffn', 'w_gu', 'w_down', 'g_post_ffn']
TWIN_WEIGHTS = ['g_pre_mix', 'w_in', 'conv_w', 'conv_b', 'dt_bias', 'a_log', 'd_skip', 'g_ssm_norm', 'w_br_att', 'w_br_ssm', 'w_mix_out', 'g_post_mix', 'g_pre_xa', 'g_mem', 'w_xq', 'w_xkv', 'w_xo', 'g_post_xa', 'g_pre_ffn', 'w_gu', 'w_down', 'g_post_ffn']
TWIN_DIFF_INPUT = 'x'
TWIN_INPUTS = ['x', 'mem', 'g_pre_mix', 'w_in', 'conv_w', 'conv_b', 'dt_bias', 'a_log', 'd_skip', 'g_ssm_norm', 'w_br_att', 'w_br_ssm', 'w_mix_out', 'g_post_mix', 'g_pre_xa', 'g_mem', 'w_xq', 'w_xkv', 'w_xo', 'g_post_xa', 'g_pre_ffn', 'w_gu', 'w_down', 'g_post_ffn', 'loss_target', 'm_g_pre_mix', 'm_w_in', 'm_conv_w', 'm_conv_b', 'm_dt_bias', 'm_a_log', 'm_d_skip', 'm_g_ssm_norm', 'm_w_br_att', 'm_w_br_ssm', 'm_w_mix_out', 'm_g_post_mix', 'm_g_pre_xa', 'm_g_mem', 'm_w_xq', 'm_w_xkv', 'm_w_xo', 'm_g_post_xa', 'm_g_pre_ffn', 'm_w_gu', 'm_w_down', 'm_g_post_ffn', 'v_g_pre_mix', 'v_w_in', 'v_conv_w', 'v_conv_b', 'v_dt_bias', 'v_a_log', 'v_d_skip', 'v_g_ssm_norm', 'v_w_br_att', 'v_w_br_ssm', 'v_w_mix_out', 'v_g_post_mix', 'v_g_pre_xa', 'v_g_mem', 'v_w_xq', 'v_w_xkv', 'v_w_xo', 'v_g_post_xa', 'v_g_pre_ffn', 'v_w_gu', 'v_w_down', 'v_g_post_ffn']
TWIN_OUTPUTS = ['loss', 'grad_x', 'grad_g_pre_mix', 'grad_w_in', 'grad_conv_w', 'grad_conv_b', 'grad_dt_bias', 'grad_a_log', 'grad_d_skip', 'grad_g_ssm_norm', 'grad_w_br_att', 'grad_w_br_ssm', 'grad_w_mix_out', 'grad_g_post_mix', 'grad_g_pre_xa', 'grad_g_mem', 'grad_w_xq', 'grad_w_xkv', 'grad_w_xo', 'grad_g_post_xa', 'grad_g_pre_ffn', 'grad_w_gu', 'grad_w_down', 'grad_g_post_ffn', 'delta_g_pre_mix', 'delta_w_in', 'delta_conv_w', 'delta_conv_b', 'delta_dt_bias', 'delta_a_log', 'delta_d_skip', 'delta_g_ssm_norm', 'delta_w_br_att', 'delta_w_br_ssm', 'delta_w_mix_out', 'delta_g_post_mix', 'delta_g_pre_xa', 'delta_g_mem', 'delta_w_xq', 'delta_w_xkv', 'delta_w_xo', 'delta_g_post_xa', 'delta_g_pre_ffn', 'delta_w_gu', 'delta_w_down', 'delta_g_post_ffn', 'new_m_g_pre_mix', 'new_m_w_in', 'new_m_conv_w', 'new_m_conv_b', 'new_m_dt_bias', 'new_m_a_log', 'new_m_d_skip', 'new_m_g_ssm_norm', 'new_m_w_br_att', 'new_m_w_br_ssm', 'new_m_w_mix_out', 'new_m_g_post_mix', 'new_m_g_pre_xa', 'new_m_g_mem', 'new_m_w_xq', 'new_m_w_xkv', 'new_m_w_xo', 'new_m_g_post_xa', 'new_m_g_pre_ffn', 'new_m_w_gu', 'new_m_w_down', 'new_m_g_post_ffn', 'new_v_g_pre_mix', 'new_v_w_in', 'new_v_conv_w', 'new_v_conv_b', 'new_v_dt_bias', 'new_v_a_log', 'new_v_d_skip', 'new_v_g_ssm_norm', 'new_v_w_br_att', 'new_v_w_br_ssm', 'new_v_w_mix_out', 'new_v_g_post_mix', 'new_v_g_pre_xa', 'new_v_g_mem', 'new_v_w_xq', 'new_v_w_xkv', 'new_v_w_xo', 'new_v_g_post_xa', 'new_v_g_pre_ffn', 'new_v_w_gu', 'new_v_w_down', 'new_v_g_post_ffn']
TWIN_LEAF_KINDS = {'loss': 'loss', 'grad_x': 'grad_x', 'grad_g_pre_mix': 'grad_w', 'grad_w_in': 'grad_w', 'grad_conv_w': 'grad_w', 'grad_conv_b': 'grad_w', 'grad_dt_bias': 'grad_w', 'grad_a_log': 'grad_w', 'grad_d_skip': 'grad_w', 'grad_g_ssm_norm': 'grad_w', 'grad_w_br_att': 'grad_w', 'grad_w_br_ssm': 'grad_w', 'grad_w_mix_out': 'grad_w', 'grad_g_post_mix': 'grad_w', 'grad_g_pre_xa': 'grad_w', 'grad_g_mem': 'grad_w', 'grad_w_xq': 'grad_w', 'grad_w_xkv': 'grad_w', 'grad_w_xo': 'grad_w', 'grad_g_post_xa': 'grad_w', 'grad_g_pre_ffn': 'grad_w', 'grad_w_gu': 'grad_w', 'grad_w_down': 'grad_w', 'grad_g_post_ffn': 'grad_w', 'delta_g_pre_mix': 'delta_w', 'delta_w_in': 'delta_w', 'delta_conv_w': 'delta_w', 'delta_conv_b': 'delta_w', 'delta_dt_bias': 'delta_w', 'delta_a_log': 'delta_w', 'delta_d_skip': 'delta_w', 'delta_g_ssm_norm': 'delta_w', 'delta_w_br_att': 'delta_w', 'delta_w_br_ssm': 'delta_w', 'delta_w_mix_out': 'delta_w', 'delta_g_post_mix': 'delta_w', 'delta_g_pre_xa': 'delta_w', 'delta_g_mem': 'delta_w', 'delta_w_xq': 'delta_w', 'delta_w_xkv': 'delta_w', 'delta_w_xo': 'delta_w', 'delta_g_post_xa': 'delta_w', 'delta_g_pre_ffn': 'delta_w', 'delta_w_gu': 'delta_w', 'delta_w_down': 'delta_w', 'delta_g_post_ffn': 'delta_w', 'new_m_g_pre_mix': 'new_m', 'new_m_w_in': 'new_m', 'new_m_conv_w': 'new_m', 'new_m_conv_b': 'new_m', 'new_m_dt_bias': 'new_m', 'new_m_a_log': 'new_m', 'new_m_d_skip': 'new_m', 'new_m_g_ssm_norm': 'new_m', 'new_m_w_br_att': 'new_m', 'new_m_w_br_ssm': 'new_m', 'new_m_w_mix_out': 'new_m', 'new_m_g_post_mix': 'new_m', 'new_m_g_pre_xa': 'new_m', 'new_m_g_mem': 'new_m', 'new_m_w_xq': 'new_m', 'new_m_w_xkv': 'new_m', 'new_m_w_xo': 'new_m', 'new_m_g_post_xa': 'new_m', 'new_m_g_pre_ffn': 'new_m', 'new_m_w_gu': 'new_m', 'new_m_w_down': 'new_m', 'new_m_g_post_ffn': 'new_m', 'new_v_g_pre_mix': 'new_v', 'new_v_w_in': 'new_v', 'new_v_conv_w': 'new_v', 'new_v_conv_b': 'new_v', 'new_v_dt_bias': 'new_v', 'new_v_a_log': 'new_v', 'new_v_d_skip': 'new_v', 'new_v_g_ssm_norm': 'new_v', 'new_v_w_br_att': 'new_v', 'new_v_w_br_ssm': 'new_v', 'new_v_w_mix_out': 'new_v', 'new_v_g_post_mix': 'new_v', 'new_v_g_pre_xa': 'new_v', 'new_v_g_mem': 'new_v', 'new_v_w_xq': 'new_v', 'new_v_w_xkv': 'new_v', 'new_v_w_xo': 'new_v', 'new_v_g_post_xa': 'new_v', 'new_v_g_pre_ffn': 'new_v', 'new_v_w_gu': 'new_v', 'new_v_w_down': 'new_v', 'new_v_g_post_ffn': 'new_v'}


def _forward(args):
    return _fwd_reference(*[args[k] for k in FWD_PARAMS])


def _output_shape():
    out = _jax.eval_shape(lambda: _forward(_fwd_setup_inputs(0)))
    return out.shape, out.dtype

N_MICROBATCH = 1
ADAM_LR = 0.001
ADAM_B1 = 0.9
ADAM_B2 = 0.999
ADAM_EPS = 1e-08
ADAM_WD = 0.01
ADAM_STEP = 10
PER_EXAMPLE_BATCH_AXIS = {'x': 0, 'mem': 0, 'loss_target': 0}
SHARED_INPUTS = []
_WEIGHT_DTYPES = {'g_pre_mix': _jnp.float32, 'w_in': _jnp.float32, 'conv_w': _jnp.float32, 'conv_b': _jnp.float32, 'dt_bias': _jnp.float32, 'a_log': _jnp.float32, 'd_skip': _jnp.float32, 'g_ssm_norm': _jnp.float32, 'w_br_att': _jnp.float32, 'w_br_ssm': _jnp.float32, 'w_mix_out': _jnp.float32, 'g_post_mix': _jnp.float32, 'g_pre_xa': _jnp.float32, 'g_mem': _jnp.float32, 'w_xq': _jnp.float32, 'w_xkv': _jnp.float32, 'w_xo': _jnp.float32, 'g_post_xa': _jnp.float32, 'g_pre_ffn': _jnp.float32, 'w_gu': _jnp.float32, 'w_down': _jnp.float32, 'g_post_ffn': _jnp.float32}
MOMENT_SCALE = {'g_pre_mix': 5.573269e+00, 'w_in': 1.705942e+00, 'conv_w': 1.942312e+00, 'conv_b': 5.445661e+00, 'dt_bias': 2.481202e+00, 'a_log': 1.018363e+01, 'd_skip': 9.341952e+00, 'g_ssm_norm': 3.183555e+00, 'w_br_att': 4.775136e+00, 'w_br_ssm': 4.477723e+00, 'w_mix_out': 6.305893e+00, 'g_post_mix': 3.221296e+01, 'g_pre_xa': 3.997305e+00, 'g_mem': 1.529148e+01, 'w_xq': 3.842002e+00, 'w_xkv': 1.046465e+01, 'w_xo': 1.453434e+01, 'g_post_xa': 3.518280e+01, 'g_pre_ffn': 4.477373e+00, 'w_gu': 1.937619e+00, 'w_down': 3.834252e+00, 'g_post_ffn': 3.196050e+01}


def _to_microbatches(a, axis):
    t = _jnp.moveaxis(a, axis, 0)
    t = t.reshape((N_MICROBATCH, t.shape[0] // N_MICROBATCH) + t.shape[1:])
    return _jnp.moveaxis(t, 1, axis + 1)


def setup_inputs(seed: int = 0) -> dict:
    inp = _fwd_setup_inputs(seed)
    key = _jax.random.fold_in(_jax.random.key(seed), 7919)
    shape, _ = _output_shape()
    out = dict(inp)
    out["loss_target"] = _jax.random.normal(_jax.random.fold_in(key, 0), shape, _jnp.float32)
    for i, name in enumerate(TWIN_WEIGHTS):
        w = inp[name].astype(_jnp.float32)
        if MOMENT_SCALE is None:
            s = _jnp.sqrt(_jnp.mean(_jnp.square(w)) + 1e-30)
        else:
            s = MOMENT_SCALE[name]
        km, kv = _jax.random.split(_jax.random.fold_in(key, i + 1))
        out[name] = w
        out["m_" + name] = s * _jax.random.normal(km, w.shape, _jnp.float32)
        out["v_" + name] = (s * s) * _jax.random.uniform(kv, w.shape, _jnp.float32, 0.5, 1.5)
    if N_MICROBATCH > 1:
        for name, axis in PER_EXAMPLE_BATCH_AXIS.items():
            out[name] = _to_microbatches(out[name], axis)
    return {'x': out['x'], 'mem': out['mem'], 'g_pre_mix': out['g_pre_mix'], 'w_in': out['w_in'], 'conv_w': out['conv_w'], 'conv_b': out['conv_b'], 'dt_bias': out['dt_bias'], 'a_log': out['a_log'], 'd_skip': out['d_skip'], 'g_ssm_norm': out['g_ssm_norm'], 'w_br_att': out['w_br_att'], 'w_br_ssm': out['w_br_ssm'], 'w_mix_out': out['w_mix_out'], 'g_post_mix': out['g_post_mix'], 'g_pre_xa': out['g_pre_xa'], 'g_mem': out['g_mem'], 'w_xq': out['w_xq'], 'w_xkv': out['w_xkv'], 'w_xo': out['w_xo'], 'g_post_xa': out['g_post_xa'], 'g_pre_ffn': out['g_pre_ffn'], 'w_gu': out['w_gu'], 'w_down': out['w_down'], 'g_post_ffn': out['g_post_ffn'], 'loss_target': out['loss_target'], 'm_g_pre_mix': out['m_g_pre_mix'], 'm_w_in': out['m_w_in'], 'm_conv_w': out['m_conv_w'], 'm_conv_b': out['m_conv_b'], 'm_dt_bias': out['m_dt_bias'], 'm_a_log': out['m_a_log'], 'm_d_skip': out['m_d_skip'], 'm_g_ssm_norm': out['m_g_ssm_norm'], 'm_w_br_att': out['m_w_br_att'], 'm_w_br_ssm': out['m_w_br_ssm'], 'm_w_mix_out': out['m_w_mix_out'], 'm_g_post_mix': out['m_g_post_mix'], 'm_g_pre_xa': out['m_g_pre_xa'], 'm_g_mem': out['m_g_mem'], 'm_w_xq': out['m_w_xq'], 'm_w_xkv': out['m_w_xkv'], 'm_w_xo': out['m_w_xo'], 'm_g_post_xa': out['m_g_post_xa'], 'm_g_pre_ffn': out['m_g_pre_ffn'], 'm_w_gu': out['m_w_gu'], 'm_w_down': out['m_w_down'], 'm_g_post_ffn': out['m_g_post_ffn'], 'v_g_pre_mix': out['v_g_pre_mix'], 'v_w_in': out['v_w_in'], 'v_conv_w': out['v_conv_w'], 'v_conv_b': out['v_conv_b'], 'v_dt_bias': out['v_dt_bias'], 'v_a_log': out['v_a_log'], 'v_d_skip': out['v_d_skip'], 'v_g_ssm_norm': out['v_g_ssm_norm'], 'v_w_br_att': out['v_w_br_att'], 'v_w_br_ssm': out['v_w_br_ssm'], 'v_w_mix_out': out['v_w_mix_out'], 'v_g_post_mix': out['v_g_post_mix'], 'v_g_pre_xa': out['v_g_pre_xa'], 'v_g_mem': out['v_g_mem'], 'v_w_xq': out['v_w_xq'], 'v_w_xkv': out['v_w_xkv'], 'v_w_xo': out['v_w_xo'], 'v_g_post_xa': out['v_g_post_xa'], 'v_g_pre_ffn': out['v_g_pre_ffn'], 'v_w_gu': out['v_w_gu'], 'v_w_down': out['v_w_down'], 'v_g_post_ffn': out['v_g_post_ffn']}


def _loss(weights, diff, rest, loss_target):
    with _jax.named_scope("forward"):
        args = {**rest, TWIN_DIFF_INPUT: diff, **{k: w.astype(_WEIGHT_DTYPES[k]) for k, w in weights.items()}}
        y = _forward(args)
    with _jax.named_scope("loss_head"):
        err = _jnp.square(y.astype(_jnp.float32) - loss_target)
        return 0.5 * _jnp.sum(_jnp.mean(err, axis=-1)) if err.ndim else 0.5 * err


def _adamw(w, g, m, v):
    m = ADAM_B1 * m + (1.0 - ADAM_B1) * g
    v = ADAM_B2 * v + (1.0 - ADAM_B2) * _jnp.square(g)
    m_hat = m / (1.0 - ADAM_B1 ** ADAM_STEP)
    v_hat = v / (1.0 - ADAM_B2 ** ADAM_STEP)
    delta = -ADAM_LR * (m_hat / (_jnp.sqrt(v_hat) + ADAM_EPS) + ADAM_WD * w)
    return delta, m, v


def reference(x, mem, g_pre_mix, w_in, conv_w, conv_b, dt_bias, a_log, d_skip, g_ssm_norm, w_br_att, w_br_ssm, w_mix_out, g_post_mix, g_pre_xa, g_mem, w_xq, w_xkv, w_xo, g_post_xa, g_pre_ffn, w_gu, w_down, g_post_ffn, loss_target, m_g_pre_mix, m_w_in, m_conv_w, m_conv_b, m_dt_bias, m_a_log, m_d_skip, m_g_ssm_norm, m_w_br_att, m_w_br_ssm, m_w_mix_out, m_g_post_mix, m_g_pre_xa, m_g_mem, m_w_xq, m_w_xkv, m_w_xo, m_g_post_xa, m_g_pre_ffn, m_w_gu, m_w_down, m_g_post_ffn, v_g_pre_mix, v_w_in, v_conv_w, v_conv_b, v_dt_bias, v_a_log, v_d_skip, v_g_ssm_norm, v_w_br_att, v_w_br_ssm, v_w_mix_out, v_g_post_mix, v_g_pre_xa, v_g_mem, v_w_xq, v_w_xkv, v_w_xo, v_g_post_xa, v_g_pre_ffn, v_w_gu, v_w_down, v_g_post_ffn):
    given = dict(x=x, mem=mem, g_pre_mix=g_pre_mix, w_in=w_in, conv_w=conv_w, conv_b=conv_b, dt_bias=dt_bias, a_log=a_log, d_skip=d_skip, g_ssm_norm=g_ssm_norm, w_br_att=w_br_att, w_br_ssm=w_br_ssm, w_mix_out=w_mix_out, g_post_mix=g_post_mix, g_pre_xa=g_pre_xa, g_mem=g_mem, w_xq=w_xq, w_xkv=w_xkv, w_xo=w_xo, g_post_xa=g_post_xa, g_pre_ffn=g_pre_ffn, w_gu=w_gu, w_down=w_down, g_post_ffn=g_post_ffn, loss_target=loss_target, m_g_pre_mix=m_g_pre_mix, m_w_in=m_w_in, m_conv_w=m_conv_w, m_conv_b=m_conv_b, m_dt_bias=m_dt_bias, m_a_log=m_a_log, m_d_skip=m_d_skip, m_g_ssm_norm=m_g_ssm_norm, m_w_br_att=m_w_br_att, m_w_br_ssm=m_w_br_ssm, m_w_mix_out=m_w_mix_out, m_g_post_mix=m_g_post_mix, m_g_pre_xa=m_g_pre_xa, m_g_mem=m_g_mem, m_w_xq=m_w_xq, m_w_xkv=m_w_xkv, m_w_xo=m_w_xo, m_g_post_xa=m_g_post_xa, m_g_pre_ffn=m_g_pre_ffn, m_w_gu=m_w_gu, m_w_down=m_w_down, m_g_post_ffn=m_g_post_ffn, v_g_pre_mix=v_g_pre_mix, v_w_in=v_w_in, v_conv_w=v_conv_w, v_conv_b=v_conv_b, v_dt_bias=v_dt_bias, v_a_log=v_a_log, v_d_skip=v_d_skip, v_g_ssm_norm=v_g_ssm_norm, v_w_br_att=v_w_br_att, v_w_br_ssm=v_w_br_ssm, v_w_mix_out=v_w_mix_out, v_g_post_mix=v_g_post_mix, v_g_pre_xa=v_g_pre_xa, v_g_mem=v_g_mem, v_w_xq=v_w_xq, v_w_xkv=v_w_xkv, v_w_xo=v_w_xo, v_g_post_xa=v_g_post_xa, v_g_pre_ffn=v_g_pre_ffn, v_w_gu=v_w_gu, v_w_down=v_w_down, v_g_post_ffn=v_g_post_ffn)
    weights = {n: given[n] for n in TWIN_WEIGHTS}
    shared = {n: given[n] for n in SHARED_INPUTS}
    per_example = {n: given[n] for n in ['x', 'mem']}
    grad_fn = _jax.value_and_grad(_loss, argnums=(0, 1))

    def one_microbatch(ex, loss_target):
        ex = dict(ex)
        diff = ex.pop(TWIN_DIFF_INPUT)
        return grad_fn(weights, diff, {**shared, **ex}, loss_target)

    if N_MICROBATCH == 1:
        loss, (grad_w, grad_x) = one_microbatch(per_example, given["loss_target"])
    else:
        def body(carry, xs):
            loss_sum, grad_sum = carry
            l_k, (gw_k, gx_k) = one_microbatch(xs[0], xs[1])
            with _jax.named_scope("update"):
                return (loss_sum + l_k, _jax.tree.map(_jnp.add, grad_sum, gw_k)), gx_k

        init = (_jnp.zeros((), _jnp.float32), _jax.tree.map(_jnp.zeros_like, weights))
        (loss, grad_w), grad_x = _jax.lax.scan(body, init, (per_example, given["loss_target"]))
    with _jax.named_scope("update"):
        delta_w, new_m, new_v = {}, {}, {}
        for n in TWIN_WEIGHTS:
            delta_w[n], new_m[n], new_v[n] = _adamw(weights[n], grad_w[n], given["m_" + n], given["v_" + n])
    return (loss, grad_x, *[grad_w[n] for n in TWIN_WEIGHTS], *[delta_w[n] for n in TWIN_WEIGHTS],
            *[new_m[n] for n in TWIN_WEIGHTS], *[new_v[n] for n in TWIN_WEIGHTS])
```

```python
import functools
import math

import jax
import jax.numpy as jnp
from jax import lax
from jax.experimental import pallas as pl
from jax.experimental.pallas import tpu as pltpu

f32, bf16 = jnp.float32, jnp.bfloat16

DEPTH = 4
D = 1024
SB_HEADS, SB_HD = 16, 64
SSM_INNER, SSM_HD, SSM_HEADS, SSM_GROUPS, SSM_STATE, SSM_CONV, SSM_CHUNK = 2048, 64, 32, 4, 128, 4, 128
HPG = SSM_HEADS // SSM_GROUPS
CONV_DIM = SSM_INNER + 2 * SSM_GROUPS * SSM_STATE
XA_HEADS, XA_HD = 4, 256
FFN = 2816
IN_WIDTH = 10272
RMS_EPS = 1e-6
LR, B1, B2, EPS, WD, STEP = 0.001, 0.9, 0.999, 1e-08, 0.01, 10

PM_W = 10240
OFF_Z, OFF_XBC, OFF_Q, OFF_K, OFF_V, OFF_GA, OFF_GS = 0, 2048, 5120, 6144, 7168, 8192, 9216
DT_W = SSM_GROUPS * 128

VMEM_LIMIT = 48 * 1024 * 1024
MESH = pl.DeviceIdType.MESH


def _cparams(sem):
    return pltpu.CompilerParams(dimension_semantics=sem, vmem_limit_bytes=VMEM_LIMIT)


def _tile(n):
    for t in (512, 256, 128):
        if n % t == 0:
            return t
    raise ValueError(f"dimension {n} is not a multiple of 128")


def _mm(name, a, b, mode, out_dtype=f32):
    if mode == "nn":
        (m, k), (k2, n) = a.shape, b.shape
    elif mode == "nt":
        (m, k), (n, k2) = a.shape, b.shape
    else:
        (k, m), (k2, n) = a.shape, b.shape
    assert k == k2, (name, a.shape, b.shape, mode)
    tm, tn, tk = _tile(m), _tile(n), _tile(k)
    nk = k // tk
    dn = {"nn": (((1,), (0,)), ((), ())), "nt": (((1,), (1,)), ((), ())), "tn": (((0,), (0,)), ((), ()))}[mode]

    def body(a_ref, b_ref, o_ref, acc_ref):
        kk = pl.program_id(2)

        @pl.when(kk == 0)
        def _():
            acc_ref[...] = jnp.zeros_like(acc_ref)

        acc_ref[...] += lax.dot_general(a_ref[...].astype(bf16), b_ref[...].astype(bf16), dn,
                                        preferred_element_type=f32)

        @pl.when(kk == nk - 1)
        def _():
            o_ref[...] = acc_ref[...].astype(o_ref.dtype)

    a_spec = pl.BlockSpec((tk, tm), lambda i, j, kk: (kk, i)) if mode == "tn" else pl.BlockSpec((tm, tk), lambda i, j, kk: (i, kk))
    b_spec = pl.BlockSpec((tn, tk), lambda i, j, kk: (j, kk)) if mode == "nt" else pl.BlockSpec((tk, tn), lambda i, j, kk: (kk, j))
    return pl.pallas_call(
        body, name=name, grid=(m // tm, n // tn, nk),
        in_specs=[a_spec, b_spec],
        out_specs=pl.BlockSpec((tm, tn), lambda i, j, kk: (i, j)),
        out_shape=jax.ShapeDtypeStruct((m, n), out_dtype),
        scratch_shapes=[pltpu.VMEM((tm, tn), f32)],
        compiler_params=_cparams(("parallel", "parallel", "arbitrary")),
    )(a, b)


def _rowwise(name, fn, rows, consts, out_rows, out_accs, *, width, ncol=1, bt=256):
    r = rows[0][0].shape[0]
    bt = min(bt, r)
    assert r % bt == 0, (name, r, bt)
    nrow = r // bt
    n_in = len(rows) + len(consts)
    n_or = len(out_rows)

    def body(*refs):
        ins = [ref[...] for ref in refs[:n_in]]
        outs = fn(*ins)
        if not isinstance(outs, (tuple, list)):
            outs = (outs,)
        o_refs = refs[n_in:]
        for o_ref, val in zip(o_refs[:n_or], outs[:n_or]):
            o_ref[...] = val.astype(o_ref.dtype)
        if out_accs:
            i = pl.program_id(1)
            for o_ref, val in zip(o_refs[n_or:], outs[n_or:]):
                @pl.when(i == 0)
                def _(o_ref=o_ref, val=val):
                    o_ref[...] = val

                @pl.when(i > 0)
                def _(o_ref=o_ref, val=val):
                    o_ref[...] += val

    in_specs = [pl.BlockSpec((bt, width), functools.partial(lambda j, i, off: (i, off + j), off=off)) for _, off in rows]
    in_specs += [pl.BlockSpec((c.shape[0], width), functools.partial(lambda j, i, off: (0, off + j), off=off)) for c, off in consts]
    out_specs = [pl.BlockSpec((bt, mlt * width), lambda j, i: (i, j)) for mlt, _ in out_rows]
    out_specs += [pl.BlockSpec((k, width), lambda j, i: (0, j)) for k in out_accs]
    out_shape = [jax.ShapeDtypeStruct((r, ncol * mlt * width), dt) for mlt, dt in out_rows]
    out_shape += [jax.ShapeDtypeStruct((k, ncol * width), f32) for k in out_accs]
    res = pl.pallas_call(
        body, name=name, grid=(ncol, nrow), in_specs=in_specs, out_specs=out_specs, out_shape=out_shape,
        compiler_params=_cparams(("parallel", "arbitrary" if out_accs else "parallel")),
    )(*[a for a, _ in rows], *[c for c, _ in consts])
    return res


def _rms(x, g):
    return x * lax.rsqrt(jnp.mean(x * x, axis=-1, keepdims=True) + RMS_EPS) * g


def _silu(x):
    return x * jax.nn.sigmoid(x)


def _softplus(x):
    return jnp.maximum(x, 0.0) + jnp.log(1.0 + jnp.exp(-jnp.abs(x)))


def _colsum(x):
    return jnp.sum(x, axis=0, keepdims=True)


def rms_fwd(name, x, g):
    return _rowwise(name, _rms, [(x, 0)], [(g, 0)], [(1, bf16)], [], width=D)[0]


def rms_bwd(name, x, g, dhs, dres=None):
    nd = len(dhs)

    def fn(x, *rest):
        dh = rest[0]
        for extra in rest[1:nd]:
            dh = dh + extra
        g = rest[-1]
        _, vjp = jax.vjp(_rms, x, g)
        dx, dg = vjp(dh.astype(f32))
        if dres is not None:
            dx = dx + rest[nd]
        return dx, dg

    rows = [(x, 0)] + [(d, 0) for d in dhs] + ([(dres, 0)] if dres is not None else [])
    return _rowwise(name, fn, rows, [(g, 0)], [(1, f32)], [1], width=D)


def addnorm_fwd(name, x, u, g):
    return _rowwise(name, lambda x, u, g: x + _rms(u, g), [(x, 0), (u, 0)], [(g, 0)], [(1, f32)], [], width=D)[0]


def addnorm_bwd(name, u, g, dx):
    def fn(u, dx, g):
        _, vjp = jax.vjp(_rms, u, g)
        return vjp(dx)

    return _rowwise(name, fn, [(u, 0), (dx, 0)], [(g, 0)], [(1, f32)], [1], width=D)


def _merge(ga, gs, a, s):
    return jax.nn.sigmoid(ga) * a + jax.nn.sigmoid(gs) * s


def merge_fwd(name, pm, a, s):
    return _rowwise(name, _merge, [(pm, OFF_GA // D), (pm, OFF_GS // D), (a, 0), (s, 0)], [], [(1, bf16)], [], width=D)[0]


def merge_bwd(name, pm, a, s, dm):
    def fn(ga, gs, a, s, dm):
        _, vjp = jax.vjp(_merge, ga, gs, a, s)
        dga, dgs, da, ds = vjp(dm)
        return jnp.concatenate([dga, dgs], axis=1), da, ds

    return _rowwise(name, fn, [(pm, OFF_GA // D), (pm, OFF_GS // D), (a, 0), (s, 0), (dm, 0)], [],
                    [(2, f32), (1, bf16), (1, bf16)], [], width=D)


def _swiglu(gate, up):
    return _silu(gate) * up


def swiglu_fwd(name, gu):
    return _rowwise(name, _swiglu, [(gu, 0), (gu, 1)], [], [(1, bf16)], [], width=FFN)[0]


def swiglu_bwd(name, gu, dact):
    def fn(gate, up, dact):
        _, vjp = jax.vjp(_swiglu, gate, up)
        dg, du = vjp(dact.astype(f32))
        return jnp.concatenate([dg, du], axis=1)

    return _rowwise(name, fn, [(gu, 0), (gu, 1), (dact, 0)], [], [(2, bf16)], [], width=FFN, bt=128)[0]


GW = SSM_INNER // SSM_GROUPS


def _gnorm(y, xs, z, dskip, gn):
    yy = (y + dskip * xs) * _silu(z)
    return yy * lax.rsqrt(jnp.mean(yy * yy, axis=-1, keepdims=True) + RMS_EPS) * gn


def gnorm_fwd(name, y, xbc, pm, dskip, gn):
    return _rowwise(name, _gnorm, [(y, 0), (xbc, 0), (pm, OFF_Z // GW)], [(dskip, 0), (gn, 0)], [(1, bf16)], [],
                    width=GW, ncol=SSM_GROUPS)[0]


def gnorm_bwd(name, y, xbc, pm, dskip, gn, do):
    def fn(y, xs, z, do, dskip, gn):
        _, vjp = jax.vjp(_gnorm, y, xs, z, dskip, gn)
        return vjp(do.astype(f32))

    return _rowwise(name, fn, [(y, 0), (xbc, 0), (pm, OFF_Z // GW), (do, 0)], [(dskip, 0), (gn, 0)],
                    [(1, f32), (1, f32), (1, f32)], [1, 1], width=GW, ncol=SSM_GROUPS)


def _dtfn(pdt, bias, alog):
    dt = _softplus(pdt + bias)
    return dt, -jnp.exp(alog) * dt


def dt_fwd(name, pdt, bias, alog):
    return _rowwise(name, _dtfn, [(pdt, 0)], [(bias, 0), (alog, 0)], [(1, f32), (1, f32)], [], width=DT_W)


def dt_bwd(name, pdt, bias, alog, ddt, dadt_c, dadt_r):
    def fn(pdt, ddt, dac, dar, bias, alog):
        _, vjp = jax.vjp(_dtfn, pdt, bias, alog)
        return vjp((ddt, dac + dar))

    return _rowwise(name, fn, [(pdt, 0), (ddt, 0), (dadt_c, 0), (dadt_r, 0)], [(bias, 0), (alog, 0)],
                    [(1, f32)], [1, 1], width=DT_W)


def loss_fwd_bwd(name, y, target):
    def fn(y, t):
        e = y - t
        return e * (1.0 / D), _colsum(e * e) * (0.5 / D)

    return _rowwise(name, fn, [(y, 0), (target, 0)], [], [(1, f32)], [1], width=D)


def adamw(name, w, g, m, v):
    r, c = w.shape

    def fn(w, g, m, v):
        m = B1 * m + (1.0 - B1) * g
        v = B2 * v + (1.0 - B2) * (g * g)
        m_hat = m / (1.0 - B1 ** STEP)
        v_hat = v / (1.0 - B2 ** STEP)
        return -LR * (m_hat / (jnp.sqrt(v_hat) + EPS) + WD * w), m, v

    bt = 256
    while bt > 8 and (r % bt or bt * c * 4 * 7 * 2 > 16 * 1024 * 1024):
        bt //= 2
    if r % bt:
        bt = r
    return _rowwise(name, fn, [(w, 0), (g, 0), (m, 0), (v, 0)], [], [(1, f32)] * 3, [], width=c, bt=bt)


SB_BQ, SB_BK = 256, 128


def _dot(a, b):
    return jnp.dot(a, b, preferred_element_type=f32)


def _dot_nt(a, b):
    return lax.dot_general(a, b, (((1,), (1,)), ((), ())), preferred_element_type=f32)


def _dot_tn(a, b):
    return lax.dot_general(a, b, (((0,), (0,)), ((), ())), preferred_element_type=f32)


def _dot2(x, tri):
    hi = x.astype(bf16)
    lo = (x - hi.astype(f32)).astype(bf16)
    return _dot(hi, tri) + _dot(lo, tri)


def _tri(n, rel):
    r = lax.broadcasted_iota(jnp.int32, (n, n), 0)
    c = lax.broadcasted_iota(jnp.int32, (n, n), 1)
    m = {"ge": r >= c, "lt": r < c, "le": r <= c}[rel]
    return jnp.where(m, 1.0, 0.0).astype(bf16)


def sb_fwd(name, pm, bsz, seq):
    bq, bk = min(SB_BQ, seq), SB_BK
    nq = seq // bq
    scale = SB_HD ** -0.5
    qb, kb_, vb_ = OFF_Q // 128, OFF_K // 128, OFF_V // 128

    def body(q_ref, k_ref, v_ref, o_ref, tot_ref):
        i = pl.program_id(2)
        lane = lax.broadcasted_iota(jnp.int32, (1, 128), 1)
        m0 = lane < SB_HD
        q = q_ref[...] * scale
        qs = (jnp.where(m0, q, 0.0).astype(bf16), jnp.where(m0, 0.0, q).astype(bf16))
        tri = _tri(bk, "ge")
        t_idx = i * bq + lax.broadcasted_iota(jnp.int32, (bq, 1), 0)
        nkb = (i + 1) * (bq // bk)

        def step(n, carry):
            o_acc, c0, c1 = carry
            ks = pl.multiple_of((nkb - 1 - n) * bk, bk)
            kblk = k_ref[pl.ds(ks, bk), :].astype(bf16)
            vblk = v_ref[pl.ds(ks, bk), :]
            vs = (jnp.where(m0, vblk, 0.0).astype(bf16), jnp.where(m0, 0.0, vblk).astype(bf16))
            valid = (ks + lax.broadcasted_iota(jnp.int32, (1, bk), 1)) < t_idx
            cs = [c0, c1]
            for h in range(2):
                z = _dot_nt(qs[h], kblk)
                lg = jnp.where(valid, -_softplus(z), 0.0)
                tail = _dot2(lg, tri) + cs[h]
                w = jnp.where(valid, jnp.exp(z + tail), 0.0)
                o_acc = o_acc + _dot(w.astype(bf16), vs[h])
                cs[h] = cs[h] + jnp.sum(lg, axis=1, keepdims=True)
            return o_acc, cs[0], cs[1]

        zc = jnp.zeros((bq, 1), f32)
        o, c0, c1 = lax.fori_loop(0, nkb, step, (jnp.zeros((bq, 128), f32), zc, zc))
        o_ref[...] = o
        tot_ref[0, 0] = jnp.where(m0, c0, c1)

    return pl.pallas_call(
        body, name=name, grid=(bsz, 8, nq),
        in_specs=[pl.BlockSpec((bq, 128), lambda b, p, i: (b * nq + i, qb + p)),
                  pl.BlockSpec((seq, 128), lambda b, p, i: (b, kb_ + p)),
                  pl.BlockSpec((seq, 128), lambda b, p, i: (b, vb_ + p))],
        out_specs=[pl.BlockSpec((bq, 128), lambda b, p, i: (b * nq + i, p)),
                   pl.BlockSpec((1, 1, bq, 128), lambda b, p, i: (b, p, i, 0))],
        out_shape=[jax.ShapeDtypeStruct((bsz * seq, 1024), f32), jax.ShapeDtypeStruct((bsz, 8, seq, 128), f32)],
        compiler_params=_cparams(("parallel", "parallel", "parallel")),
    )(pm, pm, pm)


def sb_bwd(name, pm, tot, do, bsz, seq):
    bq, bk = min(SB_BQ, seq), SB_BK
    nq = seq // bq
    scale = SB_HD ** -0.5
    qb, kb_, vb_ = OFF_Q // 128, OFF_K // 128, OFF_V // 128

    def body(q_ref, k_ref, v_ref, do_ref, tot_ref, dq_ref, dk_ref, dv_ref):
        i = pl.program_id(2)

        @pl.when(i == 0)
        def _():
            dk_ref[...] = jnp.zeros_like(dk_ref)
            dv_ref[...] = jnp.zeros_like(dv_ref)

        lane = lax.broadcasted_iota(jnp.int32, (1, 128), 1)
        m0 = lane < SB_HD
        ms = (m0, jnp.logical_not(m0))
        q = q_ref[...] * scale
        qpair = q.astype(bf16)
        qs = (jnp.where(m0, q, 0.0).astype(bf16), jnp.where(m0, 0.0, q).astype(bf16))
        dout = do_ref[...]
        dos = (jnp.where(m0, dout, 0.0).astype(bf16), jnp.where(m0, 0.0, dout).astype(bf16))
        tot = tot_ref[0, 0]
        tots = (tot[:, 0:1], tot[:, SB_HD:SB_HD + 1])
        tri_lt = _tri(bk, "lt")
        tri_le = _tri(bk, "le")
        t_idx = i * bq + lax.broadcasted_iota(jnp.int32, (bq, 1), 0)
        nkb = (i + 1) * (bq // bk)

        def step(j, carry):
            dq_acc, p0, p1, g0, g1 = carry
            ks = pl.multiple_of(j * bk, bk)
            kblk = k_ref[pl.ds(ks, bk), :].astype(bf16)
            vblk = v_ref[pl.ds(ks, bk), :].astype(bf16)
            valid = (ks + lax.broadcasted_iota(jnp.int32, (1, bk), 1)) < t_idx
            ps, gs = [p0, p1], [g0, g1]
            dk_blk = jnp.zeros((bk, 128), f32)
            dv_blk = jnp.zeros((bk, 128), f32)
            for h in range(2):
                z = _dot_nt(qs[h], kblk)
                sp = _softplus(z)
                lg = jnp.where(valid, -sp, 0.0)
                tail = tots[h] - ps[h] - _dot2(lg, tri_lt)
                w = jnp.where(valid, jnp.exp(z + tail), 0.0)
                dw = _dot_nt(dos[h], vblk)
                g = dw * w
                gsum = gs[h] + _dot2(g, tri_le)
                dz = jnp.where(valid, g - jnp.exp(z - sp) * gsum, 0.0).astype(bf16)
                dq_acc = dq_acc + jnp.where(ms[h], _dot(dz, kblk), 0.0)
                dk_blk = dk_blk + jnp.where(ms[h], _dot_tn(dz, qpair), 0.0)
                dv_blk = dv_blk + _dot_tn(w.astype(bf16), dos[h])
                ps[h] = ps[h] + jnp.sum(lg, axis=1, keepdims=True)
                gs[h] = gs[h] + jnp.sum(g, axis=1, keepdims=True)
            dk_ref[pl.ds(ks, bk), :] += dk_blk
            dv_ref[pl.ds(ks, bk), :] += dv_blk
            return dq_acc, ps[0], ps[1], gs[0], gs[1]

        zc = jnp.zeros((bq, 1), f32)
        dq = lax.fori_loop(0, nkb, step, (jnp.zeros((bq, 128), f32), zc, zc, zc, zc))[0]
        dq_ref[...] = dq * scale

    return pl.pallas_call(
        body, name=name, grid=(bsz, 8, nq),
        in_specs=[pl.BlockSpec((bq, 128), lambda b, p, i: (b * nq + i, qb + p)),
                  pl.BlockSpec((seq, 128), lambda b, p, i: (b, kb_ + p)),
                  pl.BlockSpec((seq, 128), lambda b, p, i: (b, vb_ + p)),
                  pl.BlockSpec((bq, 128), lambda b, p, i: (b * nq + i, p)),
                  pl.BlockSpec((1, 1, bq, 128), lambda b, p, i: (b, p, i, 0))],
        out_specs=[pl.BlockSpec((bq, 128), lambda b, p, i: (b * nq + i, p)),
                   pl.BlockSpec((seq, 128), lambda b, p, i: (b, p)),
                   pl.BlockSpec((seq, 128), lambda b, p, i: (b, p))],
        out_shape=[jax.ShapeDtypeStruct((bsz * seq, 1024), f32)] * 3,
        compiler_params=_cparams(("parallel", "parallel", "arbitrary")),
    )(pm, pm, pm, do, tot)


CONV_CB = 256


def _shift_down(x, d, rows):
    return x if d == 0 else jnp.where(rows >= d, pltpu.roll(x, d, axis=0), 0.0)


def _shift_up(x, d, rows, n):
    return x if d == 0 else jnp.where(rows < n - d, pltpu.roll(x, n - d, axis=0), 0.0)


def conv_fwd(name, pm, w, b, bsz, seq):
    nc = CONV_DIM // CONV_CB
    off = OFF_XBC // CONV_CB

    def body(x_ref, w_ref, b_ref, o_ref):
        x = x_ref[...]
        rows = lax.broadcasted_iota(jnp.int32, x.shape, 0)
        pre = b_ref[...] + jnp.zeros_like(x)
        for k in range(SSM_CONV):
            pre = pre + w_ref[k:k + 1, :] * _shift_down(x, SSM_CONV - 1 - k, rows)
        o_ref[...] = _silu(pre)

    return pl.pallas_call(
        body, name=name, grid=(nc, bsz),
        in_specs=[pl.BlockSpec((seq, CONV_CB), lambda j, bb: (bb, off + j)),
                  pl.BlockSpec((SSM_CONV, CONV_CB), lambda j, bb: (0, j)),
                  pl.BlockSpec((1, CONV_CB), lambda j, bb: (0, j))],
        out_specs=pl.BlockSpec((seq, CONV_CB), lambda j, bb: (bb, j)),
        out_shape=jax.ShapeDtypeStruct((bsz * seq, CONV_DIM), f32),
        compiler_params=_cparams(("parallel", "parallel")),
    )(pm, w, b)


def conv_bwd(name, pm, w, b, dact, dskipx, bsz, seq):
    nc = CONV_DIM // CONV_CB
    off = OFF_XBC // CONV_CB
    nxs = SSM_INNER // CONV_CB

    def body(x_ref, w_ref, b_ref, da_ref, ds_ref, dx_ref, dw_ref, db_ref):
        j, bb = pl.program_id(0), pl.program_id(1)
        x = x_ref[...]
        rows = lax.broadcasted_iota(jnp.int32, x.shape, 0)
        xsh = [_shift_down(x, SSM_CONV - 1 - k, rows) for k in range(SSM_CONV)]
        pre = b_ref[...] + jnp.zeros_like(x)
        for k in range(SSM_CONV):
            pre = pre + w_ref[k:k + 1, :] * xsh[k]
        sig = jax.nn.sigmoid(pre)
        dout = da_ref[...] + jnp.where(j < nxs, ds_ref[...], 0.0)
        dpre = dout * (sig * (1.0 + pre * (1.0 - sig)))
        dx = jnp.zeros_like(x)
        for k in range(SSM_CONV):
            dx = dx + w_ref[k:k + 1, :] * _shift_up(dpre, SSM_CONV - 1 - k, rows, seq)
        dx_ref[...] = dx
        dw = jnp.concatenate([_colsum(dpre * xsh[k]) for k in range(SSM_CONV)], axis=0)
        db = _colsum(dpre)

        @pl.when(bb == 0)
        def _():
            dw_ref[...] = dw
            db_ref[...] = db

        @pl.when(bb > 0)
        def _():
            dw_ref[...] += dw
            db_ref[...] += db

    return pl.pallas_call(
        body, name=name, grid=(nc, bsz),
        in_specs=[pl.BlockSpec((seq, CONV_CB), lambda j, bb: (bb, off + j)),
                  pl.BlockSpec((SSM_CONV, CONV_CB), lambda j, bb: (0, j)),
                  pl.BlockSpec((1, CONV_CB), lambda j, bb: (0, j)),
                  pl.BlockSpec((seq, CONV_CB), lambda j, bb: (bb, j)),
                  pl.BlockSpec((seq, CONV_CB), lambda j, bb: (bb, jnp.minimum(j, nxs - 1)))],
        out_specs=[pl.BlockSpec((seq, CONV_CB), lambda j, bb: (bb, j)),
                   pl.BlockSpec((SSM_CONV, CONV_CB), lambda j, bb: (0, j)),
                   pl.BlockSpec((1, CONV_CB), lambda j, bb: (0, j))],
        out_shape=[jax.ShapeDtypeStruct((bsz * seq, CONV_DIM), f32), jax.ShapeDtypeStruct((SSM_CONV, CONV_DIM), f32),
                   jax.ShapeDtypeStruct((1, CONV_DIM), f32)],
        compiler_params=_cparams(("parallel", "arbitrary")),
    )(pm, w, b, dact, dskipx)


CL = SSM_CHUNK


def _ssd_head(x, dtc, csc, csr, bm, cm, s_in):
    row = lax.broadcasted_iota(jnp.int32, (CL, CL), 0)
    col = lax.broadcasted_iota(jnp.int32, (CL, CL), 1)
    xd = x * dtc
    cb = _dot_nt(cm.astype(bf16), bm.astype(bf16))
    decay = jnp.exp(jnp.where(row >= col, csc - csr, -1e30))
    y = _dot((cb * decay).astype(bf16), xd.astype(bf16)) + _dot(cm.astype(bf16), s_in.astype(bf16)) * jnp.exp(csc)
    is_last = lax.broadcasted_iota(jnp.int32, (CL, 1), 0) == CL - 1
    cs_last = jnp.sum(jnp.where(is_last, csc, 0.0), axis=0, keepdims=True)
    w = xd * jnp.exp(cs_last - csc)
    s_out = s_in * jnp.exp(cs_last) + _dot_tn(bm.astype(bf16), w.astype(bf16))
    return y, s_out


def _dot3(a, b, split_a):
    x = a if split_a else b
    t1 = x.astype(bf16)
    r1 = x - t1.astype(f32)
    t2 = r1.astype(bf16)
    t3 = (r1 - t2.astype(f32)).astype(bf16)
    if split_a:
        return _dot(t1, b) + _dot(t2, b) + _dot(t3, b)
    return _dot(a, t1) + _dot(a, t2) + _dot(a, t3)


def _ssd_specs(bsz, seq, rev):
    nch = seq // CL

    def ch(c):
        return (nch - 1 - c) if rev else c

    xg = pl.BlockSpec((CL, GW), lambda b, g, c: (b * nch + ch(c), g))
    lane128 = pl.BlockSpec((CL, 128), lambda b, g, c: (b * nch + ch(c), g))
    adt_t = pl.BlockSpec((128, CL), lambda b, g, c: (g, b * nch + ch(c)))
    bspec = pl.BlockSpec((CL, 128), lambda b, g, c: (b * nch + ch(c), SSM_INNER // 128 + g))
    cspec = pl.BlockSpec((CL, 128), lambda b, g, c: (b * nch + ch(c), SSM_INNER // 128 + SSM_GROUPS + g))
    st = pl.BlockSpec((1, 1, 1, HPG, SSM_STATE, SSM_HD), lambda b, g, c: (b, ch(c), g, 0, 0, 0))
    return nch, xg, lane128, adt_t, bspec, cspec, st


def ssd_fwd(name, xbc, dt, adt, adt_t, bsz, seq):
    nch, xg, lane128, adt_t_spec, bspec, cspec, st = _ssd_specs(bsz, seq, False)

    def body(x_ref, dt_ref, adt_ref, adtt_ref, b_ref, c_ref, y_ref, st_ref, s_scr):
        @pl.when(pl.program_id(2) == 0)
        def _():
            s_scr[...] = jnp.zeros_like(s_scr)

        csc_all = _dot3(_tri(CL, "ge"), adt_ref[...], False)
        csr_all = _dot3(adtt_ref[0:HPG, :], _tri(CL, "le"), True)
        bm, cm = b_ref[...], c_ref[...]
        for h in range(HPG):
            s_in = s_scr[h]
            st_ref[0, 0, 0, h] = s_in
            y, s_out = _ssd_head(x_ref[:, h * SSM_HD:(h + 1) * SSM_HD], dt_ref[:, h:h + 1], csc_all[:, h:h + 1],
                                 csr_all[h:h + 1, :], bm, cm, s_in)
            y_ref[:, h * SSM_HD:(h + 1) * SSM_HD] = y
            s_scr[h] = s_out

    return pl.pallas_call(
        body, name=name, grid=(bsz, SSM_GROUPS, nch),
        in_specs=[xg, lane128, lane128, adt_t_spec, bspec, cspec],
        out_specs=[xg, st],
        out_shape=[jax.ShapeDtypeStruct((bsz * seq, SSM_INNER), f32),
                   jax.ShapeDtypeStruct((bsz, nch, SSM_GROUPS, HPG, SSM_STATE, SSM_HD), f32)],
        scratch_shapes=[pltpu.VMEM((HPG, SSM_STATE, SSM_HD), f32)],
        compiler_params=_cparams(("parallel", "parallel", "arbitrary")),
    )(xbc, dt, adt, adt_t, xbc, xbc)


def ssd_bwd(name, xbc, dt, adt, adt_t, states, dy, bsz, seq):
    nch, xg, lane128, adt_t_spec, bspec, cspec, st = _ssd_specs(bsz, seq, True)

    def body(x_ref, dt_ref, adt_ref, adtt_ref, b_ref, c_ref, st_ref, dy_ref,
             dx_ref, db_ref, dc_ref, ddt_ref, dac_ref, dar_ref, ds_scr):
        @pl.when(pl.program_id(2) == 0)
        def _():
            ds_scr[...] = jnp.zeros_like(ds_scr)

        csc_all = _dot3(_tri(CL, "ge"), adt_ref[...], False)
        csr_all = _dot3(adtt_ref[0:HPG, :], _tri(CL, "le"), True)
        bm, cm = b_ref[...], c_ref[...]
        lane = lax.broadcasted_iota(jnp.int32, (CL, 128), 1)
        sub = lax.broadcasted_iota(jnp.int32, (HPG, CL), 0)
        db = jnp.zeros((CL, 128), f32)
        dc = jnp.zeros((CL, 128), f32)
        ddt = jnp.zeros((CL, 128), f32)
        dcsc = jnp.zeros((CL, 128), f32)
        dcsr = jnp.zeros((HPG, CL), f32)
        for h in range(HPG):
            _, vjp = jax.vjp(_ssd_head, x_ref[:, h * SSM_HD:(h + 1) * SSM_HD], dt_ref[:, h:h + 1],
                             csc_all[:, h:h + 1], csr_all[h:h + 1, :], bm, cm, st_ref[0, 0, 0, h])
            dx, ddtc, dcc, dcr, db_h, dc_h, ds_in = vjp((dy_ref[:, h * SSM_HD:(h + 1) * SSM_HD], ds_scr[h]))
            dx_ref[:, h * SSM_HD:(h + 1) * SSM_HD] = dx
            ds_scr[h] = ds_in
            db, dc = db + db_h, dc + dc_h
            ddt = jnp.where(lane == h, ddtc, ddt)
            dcsc = jnp.where(lane == h, dcc, dcsc)
            dcsr = jnp.where(sub == h, dcr, dcsr)
        db_ref[...] = db
        dc_ref[...] = dc
        ddt_ref[...] = ddt
        dac_ref[...] = _dot3(_tri(CL, "le"), dcsc, False)
        dar_ref[...] = jnp.zeros_like(dar_ref)
        dar_ref[0:HPG, :] = _dot3(dcsr, _tri(CL, "ge"), True)

    t = bsz * seq
    return pl.pallas_call(
        body, name=name, grid=(bsz, SSM_GROUPS, nch),
        in_specs=[xg, lane128, lane128, adt_t_spec, bspec, cspec, st, xg],
        out_specs=[xg, lane128, lane128, lane128, lane128, adt_t_spec],
        out_shape=[jax.ShapeDtypeStruct((t, SSM_INNER), f32), jax.ShapeDtypeStruct((t, DT_W), f32),
                   jax.ShapeDtypeStruct((t, DT_W), f32), jax.ShapeDtypeStruct((t, DT_W), f32),
                   jax.ShapeDtypeStruct((t, DT_W), f32), jax.ShapeDtypeStruct((DT_W, t), f32)],
        scratch_shapes=[pltpu.VMEM((HPG, SSM_STATE, SSM_HD), f32)],
        compiler_params=_cparams(("parallel", "parallel", "arbitrary")),
    )(xbc, dt, adt, adt_t, xbc, xbc, states, dy)


XA_BQ = 512


def _xattn(q, k, v):
    s = _dot_nt(q.astype(bf16), k.astype(bf16)) * (XA_HD ** -0.5)
    p = jnp.exp(s - jnp.max(s, axis=-1, keepdims=True))
    p = p / jnp.sum(p, axis=-1, keepdims=True)
    return _dot(p.astype(bf16), v.astype(bf16))


def xattn_fwd(name, q, kv, bsz, seq, mlen):
    bq = min(XA_BQ, seq)
    nq = seq // bq

    def body(q_ref, k_ref, v_ref, o_ref):
        o_ref[...] = _xattn(q_ref[...], k_ref[...], v_ref[...])

    return pl.pallas_call(
        body, name=name, grid=(bsz, XA_HEADS, nq),
        in_specs=[pl.BlockSpec((bq, XA_HD), lambda b, h, i: (b * nq + i, h)),
                  pl.BlockSpec((mlen, XA_HD), lambda b, h, i: (b, h)),
                  pl.BlockSpec((mlen, XA_HD), lambda b, h, i: (b, XA_HEADS + h))],
        out_specs=pl.BlockSpec((bq, XA_HD), lambda b, h, i: (b * nq + i, h)),
        out_shape=jax.ShapeDtypeStruct((bsz * seq, D), f32),
        compiler_params=_cparams(("parallel", "parallel", "parallel")),
    )(q, kv, kv)


def xattn_bwd(name, q, kv, do, bsz, seq, mlen):
    bq = min(XA_BQ, seq)
    nq = seq // bq

    def body(q_ref, k_ref, v_ref, do_ref, dq_ref, dk_ref, dv_ref):
        _, vjp = jax.vjp(_xattn, q_ref[...], k_ref[...], v_ref[...])
        dq, dk, dv = vjp(do_ref[...])
        dq_ref[...] = dq
        i = pl.program_id(2)

        @pl.when(i == 0)
        def _():
            dk_ref[...] = dk
            dv_ref[...] = dv

        @pl.when(i > 0)
        def _():
            dk_ref[...] += dk
            dv_ref[...] += dv

    kspec = pl.BlockSpec((mlen, XA_HD), lambda b, h, i: (b, h))
    vspec = pl.BlockSpec((mlen, XA_HD), lambda b, h, i: (b, XA_HEADS + h))
    qspec = pl.BlockSpec((bq, XA_HD), lambda b, h, i: (b * nq + i, h))
    return pl.pallas_call(
        body, name=name, grid=(bsz, XA_HEADS, nq),
        in_specs=[qspec, kspec, vspec, qspec],
        out_specs=[qspec, kspec, kspec],
        out_shape=[jax.ShapeDtypeStruct((bsz * seq, D), f32), jax.ShapeDtypeStruct((bsz * mlen, D), f32),
                   jax.ShapeDtypeStruct((bsz * mlen, D), f32)],
        compiler_params=_cparams(("parallel", "parallel", "arbitrary")),
    )(q, kv, kv, do)


def _layer_fwd(l, x, mem, w, bsz, seq, mlen):
    n = f"l{l}_"
    sv = {"x0": x}
    sv["h1"] = h1 = rms_fwd(n + "rms_mix", x, w["g_pre_mix"])
    sv["pm"] = pm = _mm(n + "in_proj", h1, w["wm"], "nn")
    sv["pdt"] = pdt = _mm(n + "in_proj_dt", h1, w["wdt"], "nn")
    sv["o_att"], sv["tot"] = o_att, _ = sb_fwd(n + "sb_fwd", pm, bsz, seq)
    sv["xbc"] = xbc = conv_fwd(n + "conv_fwd", pm, w["conv_w"], w["conv_b"], bsz, seq)
    sv["dt"], sv["adt"] = dt, adt = dt_fwd(n + "dt_fwd", pdt, w["dt_bias"], w["a_log"])
    sv["adt_t"] = adt_t = adt.T
    sv["y_ssd"], sv["states"] = y_ssd, _ = ssd_fwd(n + "ssd_fwd", xbc, dt, adt, adt_t, bsz, seq)
    sv["o_ssm"] = o_ssm = gnorm_fwd(n + "gnorm_fwd", y_ssd, xbc, pm, w["d_skip"], w["g_ssm_norm"])
    sv["a"] = a = _mm(n + "br_att", o_att, w["w_br_att"], "nn")
    sv["s"] = s = _mm(n + "br_ssm", o_ssm, w["w_br_ssm"], "nn")
    sv["merged"] = merged = merge_fwd(n + "merge_fwd", pm, a, s)
    sv["u"] = u = _mm(n + "mix_out", merged, w["w_mix_out"], "nn")
    sv["x1"] = x1 = addnorm_fwd(n + "post_mix", x, u, w["g_post_mix"])
    sv["h2"] = h2 = rms_fwd(n + "rms_xa", x1, w["g_pre_xa"])
    sv["memn"] = memn = rms_fwd(n + "rms_mem", mem, w["g_mem"])
    sv["qx"] = qx = _mm(n + "xq", h2, w["w_xq"], "nn")
    sv["kv"] = kv = _mm(n + "xkv", memn, w["w_xkv"], "nn")
    sv["ox"] = ox = xattn_fwd(n + "xattn_fwd", qx, kv, bsz, seq, mlen)
    sv["yx"] = yx = _mm(n + "xo", ox, w["w_xo"], "nn")
    sv["x2"] = x2 = addnorm_fwd(n + "post_xa", x1, yx, w["g_post_xa"])
    sv["h3"] = h3 = rms_fwd(n + "rms_ffn", x2, w["g_pre_ffn"])
    sv["gu"] = gu = _mm(n + "gu", h3, w["w_gu"], "nn")
    sv["act"] = act = swiglu_fwd(n + "swiglu_fwd", gu)
    sv["d"] = d = _mm(n + "down", act, w["w_down"], "nn")
    x3 = addnorm_fwd(n + "post_ffn", x2, d, w["g_post_ffn"])
    return x3, sv


def _layer_bwd(l, dx, mem, w, sv, bsz, seq, mlen):
    n = f"l{l}_b_"
    g = {}
    dd, g["g_post_ffn"] = addnorm_bwd(n + "post_ffn", sv["d"], w["g_post_ffn"], dx)
    g["w_down"] = _mm(n + "dw_down", sv["act"], dd, "tn")
    dact = _mm(n + "dact", dd, w["w_down"], "nt", bf16)
    dgu = swiglu_bwd(n + "swiglu", sv["gu"], dact)
    g["w_gu"] = _mm(n + "dw_gu", sv["h3"], dgu, "tn")
    dh3 = _mm(n + "dh3", dgu, w["w_gu"], "nt")
    dx, g["g_pre_ffn"] = rms_bwd(n + "rms_ffn", sv["x2"], w["g_pre_ffn"], [dh3], dx)
    dyx, g["g_post_xa"] = addnorm_bwd(n + "post_xa", sv["yx"], w["g_post_xa"], dx)
    g["w_xo"] = _mm(n + "dw_xo", sv["ox"], dyx, "tn")
    dox = _mm(n + "dox", dyx, w["w_xo"], "nt")
    dqx, dk, dv = xattn_bwd(n + "xattn", sv["qx"], sv["kv"], dox, bsz, seq, mlen)
    g["w_xq"] = _mm(n + "dw_xq", sv["h2"], dqx, "tn")
    dh2 = _mm(n + "dh2", dqx, w["w_xq"], "nt")
    dkv = jnp.concatenate([dk, dv], axis=1)
    g["w_xkv"] = _mm(n + "dw_xkv", sv["memn"], dkv, "tn")
    dmemn = _mm(n + "dmemn", dkv, w["w_xkv"], "nt")
    _, g["g_mem"] = rms_bwd(n + "rms_mem", mem, w["g_mem"], [dmemn])
    dx, g["g_pre_xa"] = rms_bwd(n + "rms_xa", sv["x1"], w["g_pre_xa"], [dh2], dx)
    du, g["g_post_mix"] = addnorm_bwd(n + "post_mix", sv["u"], w["g_post_mix"], dx)
    g["w_mix_out"] = _mm(n + "dw_mix", sv["merged"], du, "tn")
    dmerged = _mm(n + "dmerged", du, w["w_mix_out"], "nt")
    dgates, da, ds = merge_bwd(n + "merge", sv["pm"], sv["a"], sv["s"], dmerged)
    g["w_br_att"] = _mm(n + "dw_att", sv["o_att"], da, "tn")
    do_att = _mm(n + "do_att", da, w["w_br_att"], "nt")
    g["w_br_ssm"] = _mm(n + "dw_ssm", sv["o_ssm"], ds, "tn")
    do_ssm = _mm(n + "do_ssm", ds, w["w_br_ssm"], "nt")
    dy_ssd, dxs_skip, dz, g["d_skip"], g["g_ssm_norm"] = gnorm_bwd(
        n + "gnorm", sv["y_ssd"], sv["xbc"], sv["pm"], w["d_skip"], w["g_ssm_norm"], do_ssm)
    dxs, dbm, dcm, ddt, dadt_c, dadt_r = ssd_bwd(n + "ssd", sv["xbc"], sv["dt"], sv["adt"], sv["adt_t"],
                                                   sv["states"], dy_ssd, bsz, seq)
    dxbc_act = jnp.concatenate([dxs, dbm, dcm], axis=1)
    dxbc, g["conv_w"], g["conv_b"] = conv_bwd(n + "conv", sv["pm"], w["conv_w"], w["conv_b"], dxbc_act, dxs_skip, bsz, seq)
    dpdt, g["dt_bias"], g["a_log"] = dt_bwd(n + "dt", sv["pdt"], w["dt_bias"], w["a_log"], ddt, dadt_c, dadt_r.T)
    dq, dk_, dv_ = sb_bwd(n + "sb", sv["pm"], sv["tot"], do_att, bsz, seq)
    dpm = jnp.concatenate([dz, dxbc, dq, dk_, dv_, dgates], axis=1)
    g["wm"] = _mm(n + "dw_in", sv["h1"], dpm, "tn")
    g["wdt"] = _mm(n + "dw_in_dt", sv["h1"], dpdt, "tn")
    dh1 = _mm(n + "dh1", dpm, w["wm"], "nt")
    dh1_dt = _mm(n + "dh1_dt", dpdt, w["wdt"], "nt")
    dx, g["g_pre_mix"] = rms_bwd(n + "rms_mix", sv["x0"], w["g_pre_mix"], [dh1, dh1_dt], dx)
    return dx, g


def _group_pad(v):
    lead = v.shape[:-1]
    v = v.reshape(*lead, SSM_GROUPS, HPG)
    return jnp.pad(v, [(0, 0)] * (len(lead) + 1) + [(0, 128 - HPG)]).reshape(*lead, DT_W)


def _group_unpad(v):
    lead = v.shape[:-1]
    return v.reshape(*lead, SSM_GROUPS, 128)[..., :HPG].reshape(*lead, SSM_HEADS)


BIG = ("w_in", "w_br_att", "w_br_ssm", "w_mix_out", "w_xq", "w_xkv", "w_xo", "w_gu", "w_down")
GAINS = ("g_pre_mix", "g_post_mix", "g_pre_xa", "g_mem", "g_post_xa", "g_pre_ffn", "g_post_ffn")
HEAD_VECS = ("dt_bias", "a_log", "d_skip")
SMALL = GAINS + ("conv_w", "conv_b", "g_ssm_norm") + HEAD_VECS


def _prep_layer(p):
    w_in = p["w_in"]
    w = {k: p[k] for k in BIG[1:]}
    w["wm"] = jnp.concatenate([w_in[:, 3072:8192], w_in[:, 0:3072], w_in[:, 8224:10272]], axis=1)
    w["wdt"] = _group_pad(w_in[:, 8192:8224])
    for k in GAINS + ("conv_b", "g_ssm_norm"):
        w[k] = p[k].reshape(1, -1)
    w["conv_w"] = p["conv_w"]
    w["dt_bias"] = _group_pad(p["dt_bias"]).reshape(1, DT_W)
    w["a_log"] = _group_pad(p["a_log"]).reshape(1, DT_W)
    w["d_skip"] = jnp.repeat(p["d_skip"], SSM_HD).reshape(1, SSM_INNER)
    return w


def _unprep_grads(g):
    out = {k: g[k] for k in BIG[1:]}
    gm = g["wm"]
    out["w_in"] = jnp.concatenate([gm[:, 5120:8192], gm[:, 0:5120], _group_unpad(g["wdt"]), gm[:, 8192:10240]], axis=1)
    for k in GAINS + ("conv_b", "g_ssm_norm"):
        out[k] = g[k].reshape(-1)
    out["conv_w"] = g["conv_w"]
    out["dt_bias"] = _group_unpad(g["dt_bias"]).reshape(-1)
    out["a_log"] = _group_unpad(g["a_log"]).reshape(-1)
    out["d_skip"] = g["d_skip"].reshape(SSM_HEADS, SSM_HD).sum(axis=1)
    return out


def _local_step(x, mem, target, ws, bsz, seq, mlen):
    saved = []
    for l in range(len(ws)):
        x, sv = _layer_fwd(l, x, mem, ws[l], bsz, seq, mlen)
        saved.append(sv)
    dx, loss_lanes = loss_fwd_bwd("loss", x, target)
    grads = [None] * len(ws)
    for l in reversed(range(len(ws))):
        dx, grads[l] = _layer_bwd(l, dx, mem, ws[l], saved[l], bsz, seq, mlen)
    return loss_lanes, dx, grads


HBM_SPEC = pl.BlockSpec(memory_space=pltpu.HBM)
FLIP_C = (0, 0, 1)
FLIPS_CHIP = ((1, 0, 0), (0, 1, 0), (1, 1, 0))
FLIPS_ALL = tuple(((f >> 2) & 1, (f >> 1) & 1, f & 1) for f in range(1, 8))


def _exchange(name, src, transfers, n_slots):
    _, r, w = src.shape
    nt = len(transfers)
    assert sorted(t[2] for t in transfers) == list(range(n_slots))

    def body(src_ref, out_ref, send_sems, recv_sems):
        pos = (lax.axis_index("x"), lax.axis_index("y"), lax.axis_index("c"))
        copies = []
        for t, (flip, index_fn, slot) in enumerate(transfers):
            peer = tuple(1 - p if f else p for p, f in zip(pos, flip))
            cp = pltpu.make_async_remote_copy(
                src_ref=src_ref.at[index_fn(*pos)], dst_ref=out_ref.at[slot],
                send_sem=send_sems.at[t], recv_sem=recv_sems.at[t], device_id=peer, device_id_type=MESH)
            cp.start()
            copies.append(cp)
        for cp in copies:
            cp.wait()

    return pl.pallas_call(
        body, name=name, out_shape=jax.ShapeDtypeStruct((n_slots, r, w), src.dtype),
        in_specs=[HBM_SPEC], out_specs=HBM_SPEC,
        scratch_shapes=[pltpu.SemaphoreType.DMA((nt,)), pltpu.SemaphoreType.DMA((nt,))],
    )(src)


def _const(i):
    return lambda x, y, c: i


def _allgather8(name, v, me):
    got = _exchange(name, v[None], [(fl, _const(0), j) for j, fl in enumerate(FLIPS_ALL)], 7)
    rel = jnp.concatenate([v[None], got], axis=0)
    return jnp.stack([lax.dynamic_index_in_dim(rel, k ^ me, 0, keepdims=False) for k in range(8)])


def _sum8(name, parts):
    def fn(*p):
        acc = p[0]
        for q in p[1:]:
            acc = acc + q
        return acc

    return _rowwise(name, fn, [(parts[k], 0) for k in range(8)], [], [(1, f32)], [], width=128, bt=parts.shape[1])[0]


RED_BT = 464


def _reduce_pairs(name, gf8, recv, ci):
    _, r, w = gf8.shape
    nb = r // RED_BT

    def body(c_ref, a_ref, b_ref, o32_ref, o16_ref):
        acc = a_ref[...] + b_ref[...].astype(f32)
        o32_ref[...] = acc
        o16_ref[...] = acc.astype(bf16)

    blk = (1, RED_BT, w)
    return pl.pallas_call(
        body, name=name,
        grid_spec=pltpu.PrefetchScalarGridSpec(
            num_scalar_prefetch=1, grid=(4, nb),
            in_specs=[pl.BlockSpec(blk, lambda s, i, c_ref: (2 * s + c_ref[0], i, 0)),
                      pl.BlockSpec(blk, lambda s, i, c_ref: (s, i, 0))],
            out_specs=[pl.BlockSpec(blk, lambda s, i, c_ref: (s, i, 0))] * 2),
        out_shape=[jax.ShapeDtypeStruct((4, r, w), f32), jax.ShapeDtypeStruct((4, r, w), bf16)],
        compiler_params=_cparams(("parallel", "parallel")),
    )(ci.reshape(1).astype(jnp.int32), gf8, recv)


def _reduce_final(name, p32, recv, shard):
    _, r, w = p32.shape
    nb = r // RED_BT

    def body(s_ref, a_ref, b_ref, o_ref):
        acc = a_ref[0]
        for j in range(3):
            acc = acc + b_ref[j].astype(f32)
        o_ref[0] = acc

    return pl.pallas_call(
        body, name=name,
        grid_spec=pltpu.PrefetchScalarGridSpec(
            num_scalar_prefetch=1, grid=(nb,),
            in_specs=[pl.BlockSpec((1, RED_BT, w), lambda i, s_ref: (s_ref[0], i, 0)),
                      pl.BlockSpec((3, RED_BT, w), lambda i, s_ref: (0, i, 0))],
            out_specs=pl.BlockSpec((1, RED_BT, w), lambda i, s_ref: (0, i, 0))),
        out_shape=jax.ShapeDtypeStruct((1, r, w), f32),
        compiler_params=_cparams(("parallel",)),
    )(shard.reshape(1).astype(jnp.int32), p32, recv)


COL_SHARDED = ("w_in", "w_xkv", "w_gu")
W = 1024


def _flat_rows(a):
    return a.reshape(-1, W)


def _pack(arrs, rows_multiple=8):
    flat = jnp.concatenate([a.reshape(-1) for a in arrs])
    pad = (-flat.shape[0]) % (128 * rows_multiple)
    return jnp.pad(flat, (0, pad)).reshape(-1, 128)


def _unpack(buf, shapes):
    flat, out, o = buf.reshape(-1), [], 0
    for s in shapes:
        n = math.prod(s)
        out.append(flat[o:o + n].reshape(s))
        o += n
    return out


def kernel(x, mem, g_pre_mix, w_in, conv_w, conv_b, dt_bias, a_log, d_skip, g_ssm_norm, w_br_att, w_br_ssm, w_mix_out, g_post_mix, g_pre_xa, g_mem, w_xq, w_xkv, w_xo, g_post_xa, g_pre_ffn, w_gu, w_down, g_post_ffn, loss_target, m_g_pre_mix, m_w_in, m_conv_w, m_conv_b, m_dt_bias, m_a_log, m_d_skip, m_g_ssm_norm, m_w_br_att, m_w_br_ssm, m_w_mix_out, m_g_post_mix, m_g_pre_xa, m_g_mem, m_w_xq, m_w_xkv, m_w_xo, m_g_post_xa, m_g_pre_ffn, m_w_gu, m_w_down, m_g_post_ffn, v_g_pre_mix, v_w_in, v_conv_w, v_conv_b, v_dt_bias, v_a_log, v_d_skip, v_g_ssm_norm, v_w_br_att, v_w_br_ssm, v_w_mix_out, v_g_post_mix, v_g_pre_xa, v_g_mem, v_w_xq, v_w_xkv, v_w_xo, v_g_post_xa, v_g_pre_ffn, v_w_gu, v_w_down, v_g_post_ffn):
    a = dict(locals())
    names = ("g_pre_mix", "w_in", "conv_w", "conv_b", "dt_bias", "a_log", "d_skip", "g_ssm_norm", "w_br_att", "w_br_ssm",
             "w_mix_out", "g_post_mix", "g_pre_xa", "g_mem", "w_xq", "w_xkv", "w_xo", "g_post_xa", "g_pre_ffn", "w_gu",
             "w_down", "g_post_ffn")
    depth = w_in.shape[0]
    bsz, seq, _ = x.shape
    mlen = mem.shape[1]
    xi, yi, ci = lax.axis_index("x"), lax.axis_index("y"), lax.axis_index("c")
    shard = 2 * xi + yi
    me = 2 * shard + ci

    segs = [(l, k) for l in range(depth) for k in BIG]
    seg_rows = [a[k].shape[1] * a[k].shape[2] // W for _, k in segs]
    wb = jnp.concatenate([_flat_rows(a[k][l].astype(bf16)) for l, k in segs], axis=0)
    rows = wb.shape[0]
    half = rows // 2
    wb2 = wb.reshape(2, half, W)
    got_ici = _exchange("ag_ici", wb2, [(fl, lambda x_, y_, c_: c_, j) for j, fl in enumerate(FLIPS_CHIP)], 3)
    got_d2d = _exchange("ag_d2d", got_ici, [(FLIP_C, _const(j), j) for j in range(3)], 3)
    rel = [None] * 4
    rel[0] = wb
    for j, fl in enumerate(FLIPS_CHIP):
        pair = jnp.stack([got_ici[j], got_d2d[j]])
        rel[2 * fl[0] + fl[1]] = jnp.concatenate([lax.dynamic_index_in_dim(pair, ci, 0, keepdims=False),
                                                  lax.dynamic_index_in_dim(pair, 1 - ci, 0, keepdims=False)], axis=0)
    rel = jnp.stack(rel)
    shards = [lax.dynamic_index_in_dim(rel, s ^ shard, 0, keepdims=False) for s in range(4)]

    cw_all = _allgather8("ag_conv_w", _pack([conv_w]), me)
    cw_shape = conv_w.shape
    conv_w_full = jnp.concatenate([_unpack(cw_all[2 * s], [cw_shape])[0] for s in range(4)], axis=2)

    ws, o = [], 0
    full = {}
    for (l, k), n in zip(segs, seg_rows):
        r, c = a[k].shape[1:]
        pieces = [shards[s][o:o + n].reshape(r, c) for s in range(4)]
        full[(l, k)] = jnp.concatenate(pieces, axis=1 if k in COL_SHARDED else 0)
        o += n
    for l in range(depth):
        p = {k: full[(l, k)] for k in BIG}
        for k in SMALL:
            p[k] = conv_w_full[l] if k == "conv_w" else a[k][l]
        ws.append(_prep_layer(p))

    loss_lanes, gx, grads = _local_step(x.reshape(bsz * seq, D), mem.reshape(bsz * mlen, D),
                                        loss_target.reshape(bsz * seq, D), ws, bsz, seq, mlen)
    grads = [_unprep_grads(g) for g in grads]

    pieces = []
    for (l, k), n in zip(segs, seg_rows):
        g = grads[l][k]
        r, c = a[k].shape[1:]
        g4 = g.reshape(g.shape[0], 4, c).transpose(1, 0, 2) if k in COL_SHARDED else g.reshape(4, r, c)
        pieces.append(g4.reshape(4, n, W))
    gf8 = jnp.concatenate(pieces, axis=1).reshape(8, half, W)
    wire1 = gf8.astype(bf16)
    from_sib = _exchange("rs_d2d", wire1, [(FLIP_C, functools.partial(lambda x_, y_, c_, s: 2 * s + 1 - c_, s=s), s)
                                           for s in range(4)], 4)
    p32, p16 = _reduce_pairs("rs_sum_pair", gf8, from_sib, ci)
    from_chips = _exchange("rs_ici", p16, [(fl, functools.partial(lambda x_, y_, c_, f: (2 * x_ + y_) ^ f, f=2 * fl[0] + fl[1]), j)
                                           for j, fl in enumerate(FLIPS_CHIP)], 3)
    red = _reduce_final("rs_sum_chips", p32, from_chips, shard)
    red_sib = _exchange("rs_swap", red, [(FLIP_C, _const(0), 0)], 1)
    pair = jnp.concatenate([red, red_sib], axis=0)
    gshard = jnp.concatenate([lax.dynamic_index_in_dim(pair, ci, 0, keepdims=False),
                              lax.dynamic_index_in_dim(pair, 1 - ci, 0, keepdims=False)], axis=0)

    out_g, out_d, out_m, out_v = {}, {}, {}, {}
    o, per = 0, {}
    for (l, k), n in zip(segs, seg_rows):
        per[(l, k)] = gshard[o:o + n].reshape(a[k].shape[1:])
        o += n
    for k in BIG:
        g = jnp.stack([per[(l, k)] for l in range(depth)])
        shp = a[k].shape
        two_d = (shp[0] * shp[1], shp[2])
        d_, m_, v_ = adamw("adamw_" + k, a[k].reshape(two_d), g.reshape(two_d), a["m_" + k].reshape(two_d), a["v_" + k].reshape(two_d))
        out_g[k], out_d[k], out_m[k], out_v[k] = g, d_.reshape(shp), m_.reshape(shp), v_.reshape(shp)

    small_shapes = [(depth,) + (conv_w_full.shape[1:] if k == "conv_w" else a[k].shape[1:]) for k in SMALL]
    small = _pack([jnp.stack([grads[l][k] for l in range(depth)]) for k in SMALL] + [loss_lanes])
    total = _sum8("small_sum", _allgather8("ag_small", small, me))
    *gsmall, loss_l = _unpack(total, small_shapes + [loss_lanes.shape])
    gsmall = dict(zip(SMALL, gsmall))
    gsmall["conv_w"] = lax.dynamic_slice_in_dim(gsmall["conv_w"], shard * cw_shape[2], cw_shape[2], axis=2)
    loc_shapes = [a[k].shape for k in SMALL]
    d_, m_, v_ = adamw("adamw_small", _pack([a[k] for k in SMALL]), _pack([gsmall[k] for k in SMALL]),
                       _pack([a["m_" + k] for k in SMALL]), _pack([a["v_" + k] for k in SMALL]))
    for k, dd, mm, vv in zip(SMALL, _unpack(d_, loc_shapes), _unpack(m_, loc_shapes), _unpack(v_, loc_shapes)):
        out_g[k], out_d[k], out_m[k], out_v[k] = gsmall[k], dd, mm, vv

    loss = jnp.sum(loss_l)
    return (loss, gx.reshape(x.shape), *[out_g[k] for k in names], *[out_d[k] for k in names],
            *[out_m[k] for k in names], *[out_v[k] for k in names])
```

```python
import functools
import math

import jax
import jax.numpy as jnp
from jax import lax
from jax.experimental import pallas as pl
from jax.experimental.pallas import tpu as pltpu

f32, bf16 = jnp.float32, jnp.bfloat16

DEPTH = 4
D = 1024
SB_HEADS, SB_HD = 16, 64
SSM_INNER, SSM_HD, SSM_HEADS, SSM_GROUPS, SSM_STATE, SSM_CONV, SSM_CHUNK = 2048, 64, 32, 4, 128, 4, 128
HPG = SSM_HEADS // SSM_GROUPS
CONV_DIM = SSM_INNER + 2 * SSM_GROUPS * SSM_STATE
XA_HEADS, XA_HD = 4, 256
FFN = 2816
IN_WIDTH = 10272
RMS_EPS = 1e-6
LR, B1, B2, EPS, WD, STEP = 0.001, 0.9, 0.999, 1e-08, 0.01, 10

PM_W = 10240
OFF_Z, OFF_XBC, OFF_Q, OFF_K, OFF_V, OFF_GA, OFF_GS = 0, 2048, 5120, 6144, 7168, 8192, 9216
DT_W = SSM_GROUPS * 128

VMEM_LIMIT = 48 * 1024 * 1024
MESH = pl.DeviceIdType.MESH


def _cparams(sem):
    return pltpu.CompilerParams(dimension_semantics=sem, vmem_limit_bytes=VMEM_LIMIT)


def _tile(n):
    for t in (512, 256, 128):
        if n % t == 0:
            return t
    raise ValueError(f"dimension {n} is not a multiple of 128")


MM_VMEM_BUDGET = 34 * 1024 * 1024


def _mm_tiles(m, n, k, sa, sb, so):
    best = None
    for tm in (2048, 1024, 512, 256, 128):
        if m % tm:
            continue
        for tn in (2048, 1024, 512, 256, 128):
            if n % tn:
                continue
            for tk in (k, 2048, 1024, 512):
                if tk > k or k % tk:
                    continue
                vmem = 2 * (tm * tk * sa + tk * tn * sb + tm * tn * so) + tm * tn * 4 * (2 if tk < k else 1)
                if vmem > MM_VMEM_BUDGET:
                    continue
                traffic = m * k * sa * (n // tn) + k * n * sb * (m // tm) + m * n * so
                steps = (m // tm) * (n // tn) * (k // tk)
                accumulate = (k // tk > 1) * (k // tk) * m * n * 2
                key = (traffic + steps * 800_000 + accumulate, steps)
                if best is None or key < best[0]:
                    best = (key, (tm, tn, tk))
    assert best is not None, (m, n, k)
    return best[1]


def _mm(name, a, b, mode, out_dtype=f32):
    if mode == "nn":
        (m, k), (k2, n) = a.shape, b.shape
    elif mode == "nt":
        (m, k), (n, k2) = a.shape, b.shape
    else:
        (k, m), (k2, n) = a.shape, b.shape
    assert k == k2, (name, a.shape, b.shape, mode)
    tm, tn, tk = _mm_tiles(m, n, k, a.dtype.itemsize, b.dtype.itemsize, jnp.dtype(out_dtype).itemsize)
    nk = k // tk
    dn = {"nn": (((1,), (0,)), ((), ())), "nt": (((1,), (1,)), ((), ())), "tn": (((0,), (0,)), ((), ()))}[mode]

    def product(a_ref, b_ref):
        return lax.dot_general(a_ref[...].astype(bf16), b_ref[...].astype(bf16), dn, preferred_element_type=f32)

    def body_whole_k(a_ref, b_ref, o_ref):
        o_ref[...] = product(a_ref, b_ref).astype(o_ref.dtype)

    def body_k_loop(a_ref, b_ref, o_ref, acc_ref):
        kk = pl.program_id(2)

        @pl.when(kk == 0)
        def _():
            acc_ref[...] = product(a_ref, b_ref)

        @pl.when(kk > 0)
        def _():
            acc_ref[...] += product(a_ref, b_ref)

        @pl.when(kk == nk - 1)
        def _():
            o_ref[...] = acc_ref[...].astype(o_ref.dtype)

    a_spec = pl.BlockSpec((tk, tm), lambda i, j, kk: (kk, i)) if mode == "tn" else pl.BlockSpec((tm, tk), lambda i, j, kk: (i, kk))
    b_spec = pl.BlockSpec((tn, tk), lambda i, j, kk: (j, kk)) if mode == "nt" else pl.BlockSpec((tk, tn), lambda i, j, kk: (kk, j))
    return pl.pallas_call(
        body_whole_k if nk == 1 else body_k_loop, name=name, grid=(m // tm, n // tn, nk),
        in_specs=[a_spec, b_spec],
        out_specs=pl.BlockSpec((tm, tn), lambda i, j, kk: (i, j)),
        out_shape=jax.ShapeDtypeStruct((m, n), out_dtype),
        scratch_shapes=[] if nk == 1 else [pltpu.VMEM((tm, tn), f32)],
        compiler_params=_cparams(("parallel", "parallel", "arbitrary")),
    )(a, b)


def _rowwise(name, fn, rows, consts, out_rows, out_accs, *, width, ncol=1, bt=256):
    r = rows[0][0].shape[0]
    bt = min(bt, r)
    assert r % bt == 0, (name, r, bt)
    nrow = r // bt
    n_in = len(rows) + len(consts)
    n_or = len(out_rows)

    def body(*refs):
        ins = [ref[...].astype(f32) for ref in refs[:n_in]]
        outs = fn(*ins)
        if not isinstance(outs, (tuple, list)):
            outs = (outs,)
        o_refs = refs[n_in:]
        for o_ref, val in zip(o_refs[:n_or], outs[:n_or]):
            o_ref[...] = val.astype(o_ref.dtype)
        if out_accs:
            i = pl.program_id(1)
            for o_ref, val in zip(o_refs[n_or:], outs[n_or:]):
                @pl.when(i == 0)
                def _(o_ref=o_ref, val=val):
                    o_ref[...] = val

                @pl.when(i > 0)
                def _(o_ref=o_ref, val=val):
                    o_ref[...] += val

    in_specs = [pl.BlockSpec((bt, width), functools.partial(lambda j, i, off: (i, off + j), off=off)) for _, off in rows]
    in_specs += [pl.BlockSpec((c.shape[0], width), functools.partial(lambda j, i, off: (0, off + j), off=off)) for c, off in consts]
    out_specs = [pl.BlockSpec((bt, mlt * width), lambda j, i: (i, j)) for mlt, _ in out_rows]
    out_specs += [pl.BlockSpec((k, width), lambda j, i: (0, j)) for k in out_accs]
    out_shape = [jax.ShapeDtypeStruct((r, ncol * mlt * width), dt) for mlt, dt in out_rows]
    out_shape += [jax.ShapeDtypeStruct((k, ncol * width), f32) for k in out_accs]
    res = pl.pallas_call(
        body, name=name, grid=(ncol, nrow), in_specs=in_specs, out_specs=out_specs, out_shape=out_shape,
        compiler_params=_cparams(("parallel", "arbitrary" if out_accs else "parallel")),
    )(*[a for a, _ in rows], *[c for c, _ in consts])
    return res


def _rms(x, g):
    return x * lax.rsqrt(jnp.mean(x * x, axis=-1, keepdims=True) + RMS_EPS) * g


def _silu(x):
    return x * jax.nn.sigmoid(x)


def _softplus(x):
    return jnp.maximum(x, 0.0) + jnp.log(1.0 + jnp.exp(-jnp.abs(x)))


def _colsum(x):
    return jnp.sum(x, axis=0, keepdims=True)


def rms_fwd(name, x, g):
    return _rowwise(name, _rms, [(x, 0)], [(g, 0)], [(1, bf16)], [], width=D)[0]


def rms_bwd(name, x, g, dhs, dres=None):
    nd = len(dhs)

    def fn(x, *rest):
        dh = rest[0]
        for extra in rest[1:nd]:
            dh = dh + extra
        g = rest[-1]
        _, vjp = jax.vjp(_rms, x, g)
        dx, dg = vjp(dh.astype(f32))
        if dres is not None:
            dx = dx + rest[nd]
        return dx, dg

    rows = [(x, 0)] + [(d, 0) for d in dhs] + ([(dres, 0)] if dres is not None else [])
    return _rowwise(name, fn, rows, [(g, 0)], [(1, f32)], [1], width=D)


def addnorm_fwd(name, x, u, g):
    return _rowwise(name, lambda x, u, g: x + _rms(u, g), [(x, 0), (u, 0)], [(g, 0)], [(1, f32)], [], width=D)[0]


def addnorm_bwd(name, u, g, dx):
    def fn(u, dx, g):
        _, vjp = jax.vjp(_rms, u, g)
        return vjp(dx)

    return _rowwise(name, fn, [(u, 0), (dx, 0)], [(g, 0)], [(1, bf16)], [1], width=D)


def _merge(ga, gs, a, s):
    return jax.nn.sigmoid(ga) * a + jax.nn.sigmoid(gs) * s


def merge_fwd(name, pm, a, s):
    return _rowwise(name, _merge, [(pm, OFF_GA // D), (pm, OFF_GS // D), (a, 0), (s, 0)], [], [(1, bf16)], [], width=D)[0]


def merge_bwd(name, pm, a, s, dm):
    def fn(ga, gs, a, s, dm):
        _, vjp = jax.vjp(_merge, ga, gs, a, s)
        dga, dgs, da, ds = vjp(dm)
        return jnp.concatenate([dga, dgs], axis=1), da, ds

    return _rowwise(name, fn, [(pm, OFF_GA // D), (pm, OFF_GS // D), (a, 0), (s, 0), (dm, 0)], [],
                    [(2, bf16), (1, bf16), (1, bf16)], [], width=D)


def _swiglu(gate, up):
    return _silu(gate) * up


def swiglu_fwd(name, gu):
    return _rowwise(name, _swiglu, [(gu, 0), (gu, 1)], [], [(1, bf16)], [], width=FFN)[0]


def swiglu_bwd(name, gu, dact):
    def fn(gate, up, dact):
        _, vjp = jax.vjp(_swiglu, gate, up)
        dg, du = vjp(dact.astype(f32))
        return jnp.concatenate([dg, du], axis=1)

    return _rowwise(name, fn, [(gu, 0), (gu, 1), (dact, 0)], [], [(2, bf16)], [], width=FFN, bt=128)[0]


GW = SSM_INNER // SSM_GROUPS


def _gnorm(y, xs, z, dskip, gn):
    yy = (y + dskip * xs) * _silu(z)
    return yy * lax.rsqrt(jnp.mean(yy * yy, axis=-1, keepdims=True) + RMS_EPS) * gn


def gnorm_fwd(name, y, xbc, pm, dskip, gn):
    return _rowwise(name, _gnorm, [(y, 0), (xbc, 0), (pm, OFF_Z // GW)], [(dskip, 0), (gn, 0)], [(1, bf16)], [],
                    width=GW, ncol=SSM_GROUPS)[0]


def gnorm_bwd(name, y, xbc, pm, dskip, gn, do):
    def fn(y, xs, z, do, dskip, gn):
        _, vjp = jax.vjp(_gnorm, y, xs, z, dskip, gn)
        return vjp(do.astype(f32))

    return _rowwise(name, fn, [(y, 0), (xbc, 0), (pm, OFF_Z // GW), (do, 0)], [(dskip, 0), (gn, 0)],
                    [(1, f32), (1, f32), (1, bf16)], [1, 1], width=GW, ncol=SSM_GROUPS)


def _dtfn(pdt, bias, alog):
    dt = _softplus(pdt + bias)
    return dt, -jnp.exp(alog) * dt


def dt_fwd(name, pdt, bias, alog):
    return _rowwise(name, _dtfn, [(pdt, 0)], [(bias, 0), (alog, 0)], [(1, f32), (1, f32)], [], width=DT_W)


def dt_bwd(name, pdt, bias, alog, ddt, dadt_c, dadt_r):
    def fn(pdt, ddt, dac, dar, bias, alog):
        _, vjp = jax.vjp(_dtfn, pdt, bias, alog)
        return vjp((ddt, dac + dar))

    return _rowwise(name, fn, [(pdt, 0), (ddt, 0), (dadt_c, 0), (dadt_r, 0)], [(bias, 0), (alog, 0)],
                    [(1, f32)], [1, 1], width=DT_W)


def loss_fwd_bwd(name, y, target):
    def fn(y, t):
        e = y - t
        return e * (1.0 / D), _colsum(e * e) * (0.5 / D)

    return _rowwise(name, fn, [(y, 0), (target, 0)], [], [(1, f32)], [1], width=D)


def adamw(name, w, g, m, v):
    r, c = w.shape

    def fn(w, g, m, v):
        m = B1 * m + (1.0 - B1) * g
        v = B2 * v + (1.0 - B2) * (g * g)
        m_hat = m / (1.0 - B1 ** STEP)
        v_hat = v / (1.0 - B2 ** STEP)
        return -LR * (m_hat / (jnp.sqrt(v_hat) + EPS) + WD * w), m, v

    bt = 256
    while bt > 8 and (r % bt or bt * c * 4 * 7 * 2 > 16 * 1024 * 1024):
        bt //= 2
    if r % bt:
        bt = r
    return _rowwise(name, fn, [(w, 0), (g, 0), (m, 0), (v, 0)], [], [(1, f32)] * 3, [], width=c, bt=bt)


SB_BQ, SB_BK = 512, 256


def _dot(a, b):
    return jnp.dot(a, b, preferred_element_type=f32)


def _dot_nt(a, b):
    return lax.dot_general(a, b, (((1,), (1,)), ((), ())), preferred_element_type=f32)


def _dot_tn(a, b):
    return lax.dot_general(a, b, (((0,), (0,)), ((), ())), preferred_element_type=f32)


def _dot2(x, tri):
    hi = x.astype(bf16)
    lo = (x - hi.astype(f32)).astype(bf16)
    return _dot(hi, tri) + _dot(lo, tri)


def _tri(n, rel):
    r = lax.broadcasted_iota(jnp.int32, (n, n), 0)
    c = lax.broadcasted_iota(jnp.int32, (n, n), 1)
    m = {"ge": r >= c, "lt": r < c, "le": r <= c}[rel]
    return jnp.where(m, 1.0, 0.0).astype(bf16)


def sb_fwd(name, pm, bsz, seq):
    bq = min(SB_BQ, seq)
    bk = min(SB_BK, bq)
    nq, nd = seq // bq, bq // bk
    scale = SB_HD ** -0.5
    qb, kb_, vb_ = OFF_Q // 128, OFF_K // 128, OFF_V // 128

    def body(q_ref, k_ref, v_ref, o_ref, tot_ref):
        i = pl.program_id(2)
        lane = lax.broadcasted_iota(jnp.int32, (1, 128), 1)
        m0 = lane < SB_HD
        q = q_ref[...].astype(f32) * scale
        qs = (jnp.where(m0, q, 0.0).astype(bf16), jnp.where(m0, 0.0, q).astype(bf16))
        neg_tri = -_tri(bk, "ge")
        t_idx = i * bq + lax.broadcasted_iota(jnp.int32, (bq, 1), 0)

        def block(ks, carry, masked):
            o_acc, c0, c1 = carry
            kblk = k_ref[pl.ds(ks, bk), :].astype(bf16)
            vblk = v_ref[pl.ds(ks, bk), :].astype(bf16)
            vs = (jnp.where(m0, vblk, 0).astype(bf16), jnp.where(m0, 0, vblk).astype(bf16))
            if masked:
                valid = (ks + lax.broadcasted_iota(jnp.int32, (1, bk), 1)) < t_idx
            cs = [c0, c1]
            for h in range(2):
                z = _dot_nt(qs[h], kblk)
                sp = _softplus(z)
                if masked:
                    sp = jnp.where(valid, sp, 0.0)
                tl = _dot2(sp, neg_tri)
                w = jnp.exp(z + tl + cs[h])
                if masked:
                    w = jnp.where(valid, w, 0.0)
                o_acc = o_acc + _dot(w.astype(bf16), vs[h])
                cs[h] = cs[h] + tl[:, 0:1]
            return o_acc, cs[0], cs[1]

        zc = jnp.zeros((bq, 1), f32)
        carry = (jnp.zeros((bq, 128), f32), zc, zc)
        for d in range(nd):
            carry = block(pl.multiple_of((i * nd + nd - 1 - d) * bk, bk), carry, True)
        carry = lax.fori_loop(0, i * nd, lambda n, c: block(pl.multiple_of((i * nd - 1 - n) * bk, bk), c, False), carry)
        o, c0, c1 = carry
        o_ref[...] = o.astype(o_ref.dtype)
        tot_ref[0, 0] = jnp.where(m0, c0, c1)

    return pl.pallas_call(
        body, name=name, grid=(bsz, 8, nq),
        in_specs=[pl.BlockSpec((bq, 128), lambda b, p, i: (b * nq + i, qb + p)),
                  pl.BlockSpec((seq, 128), lambda b, p, i: (b, kb_ + p)),
                  pl.BlockSpec((seq, 128), lambda b, p, i: (b, vb_ + p))],
        out_specs=[pl.BlockSpec((bq, 128), lambda b, p, i: (b * nq + i, p)),
                   pl.BlockSpec((1, 1, bq, 128), lambda b, p, i: (b, p, i, 0))],
        out_shape=[jax.ShapeDtypeStruct((bsz * seq, 1024), bf16), jax.ShapeDtypeStruct((bsz, 8, seq, 128), f32)],
        compiler_params=_cparams(("parallel", "parallel", "parallel")),
    )(pm, pm, pm)


def sb_bwd(name, pm, tot, do, bsz, seq):
    bq = min(SB_BQ, seq)
    bk = min(SB_BK, bq)
    nq, nd = seq // bq, bq // bk
    scale = SB_HD ** -0.5
    qb, kb_, vb_ = OFF_Q // 128, OFF_K // 128, OFF_V // 128

    def body(q_ref, k_ref, v_ref, do_ref, tot_ref, dq_ref, dk_ref, dv_ref, dk_acc, dv_acc):
        i = pl.program_id(2)

        @pl.when(i == 0)
        def _():
            dk_acc[...] = jnp.zeros_like(dk_acc)
            dv_acc[...] = jnp.zeros_like(dv_acc)

        lane = lax.broadcasted_iota(jnp.int32, (1, 128), 1)
        m0 = lane < SB_HD
        ms = (m0, jnp.logical_not(m0))
        q = q_ref[...].astype(f32) * scale
        qpair = q.astype(bf16)
        qs = (jnp.where(m0, q, 0.0).astype(bf16), jnp.where(m0, 0.0, q).astype(bf16))
        dout = do_ref[...].astype(f32)
        dos = (jnp.where(m0, dout, 0.0).astype(bf16), jnp.where(m0, 0.0, dout).astype(bf16))
        tot = tot_ref[0, 0]
        tots = (tot[:, 0:1], tot[:, SB_HD:SB_HD + 1])
        tri_lt = _tri(bk, "lt")
        tri_le = _tri(bk, "le")
        t_idx = i * bq + lax.broadcasted_iota(jnp.int32, (bq, 1), 0)

        def block(ks, carry, masked):
            dq_acc, p0, p1, g0, g1 = carry
            kblk = k_ref[pl.ds(ks, bk), :].astype(bf16)
            vblk = v_ref[pl.ds(ks, bk), :].astype(bf16)
            if masked:
                valid = (ks + lax.broadcasted_iota(jnp.int32, (1, bk), 1)) < t_idx
            ps, gs = [p0, p1], [g0, g1]
            dk_blk = jnp.zeros((bk, 128), f32)
            dv_blk = jnp.zeros((bk, 128), f32)
            for h in range(2):
                z = _dot_nt(qs[h], kblk)
                sp = _softplus(z)
                sig = jnp.exp(z - sp)
                if masked:
                    sp = jnp.where(valid, sp, 0.0)
                w = jnp.exp(z + tots[h] + ps[h] + _dot2(sp, tri_lt))
                if masked:
                    w = jnp.where(valid, w, 0.0)
                g = _dot_nt(dos[h], vblk) * w
                dz = g - sig * (gs[h] + _dot2(g, tri_le))
                if masked:
                    dz = jnp.where(valid, dz, 0.0)
                dz = dz.astype(bf16)
                dq_acc = dq_acc + jnp.where(ms[h], _dot(dz, kblk), 0.0)
                dk_blk = dk_blk + jnp.where(ms[h], _dot_tn(dz, qpair), 0.0)
                dv_blk = dv_blk + _dot_tn(w.astype(bf16), dos[h])
                ps[h] = ps[h] + jnp.sum(sp, axis=1, keepdims=True)
                gs[h] = gs[h] + jnp.sum(g, axis=1, keepdims=True)
            dk_acc[pl.ds(ks, bk), :] += dk_blk
            dv_acc[pl.ds(ks, bk), :] += dv_blk
            return dq_acc, ps[0], ps[1], gs[0], gs[1]

        zc = jnp.zeros((bq, 1), f32)
        carry = (jnp.zeros((bq, 128), f32), zc, zc, zc, zc)
        carry = lax.fori_loop(0, i * nd, lambda j, c: block(pl.multiple_of(j * bk, bk), c, False), carry)
        for d in range(nd):
            carry = block(pl.multiple_of((i * nd + d) * bk, bk), carry, True)
        dq_ref[...] = (carry[0] * scale).astype(dq_ref.dtype)

        @pl.when(i == nq - 1)
        def _():
            dk_ref[...] = dk_acc[...].astype(dk_ref.dtype)
            dv_ref[...] = dv_acc[...].astype(dv_ref.dtype)

    return pl.pallas_call(
        body, name=name, grid=(bsz, 8, nq),
        in_specs=[pl.BlockSpec((bq, 128), lambda b, p, i: (b * nq + i, qb + p)),
                  pl.BlockSpec((seq, 128), lambda b, p, i: (b, kb_ + p)),
                  pl.BlockSpec((seq, 128), lambda b, p, i: (b, vb_ + p)),
                  pl.BlockSpec((bq, 128), lambda b, p, i: (b * nq + i, p)),
                  pl.BlockSpec((1, 1, bq, 128), lambda b, p, i: (b, p, i, 0))],
        out_specs=[pl.BlockSpec((bq, 128), lambda b, p, i: (b * nq + i, p)),
                   pl.BlockSpec((seq, 128), lambda b, p, i: (b, p)),
                   pl.BlockSpec((seq, 128), lambda b, p, i: (b, p))],
        out_shape=[jax.ShapeDtypeStruct((bsz * seq, 1024), bf16)] * 3,
        scratch_shapes=[pltpu.VMEM((seq, 128), f32), pltpu.VMEM((seq, 128), f32)],
        compiler_params=_cparams(("parallel", "parallel", "arbitrary")),
    )(pm, pm, pm, do, tot)


CONV_CB = 256


def _shift_down(x, d, rows):
    return x if d == 0 else jnp.where(rows >= d, pltpu.roll(x, d, axis=0), 0.0)


def _shift_up(x, d, rows, n):
    return x if d == 0 else jnp.where(rows < n - d, pltpu.roll(x, n - d, axis=0), 0.0)


def conv_fwd(name, pm, w, b, bsz, seq):
    nc = CONV_DIM // CONV_CB
    off = OFF_XBC // CONV_CB

    def body(x_ref, w_ref, b_ref, o_ref):
        x = x_ref[...].astype(f32)
        rows = lax.broadcasted_iota(jnp.int32, x.shape, 0)
        pre = b_ref[...] + jnp.zeros_like(x)
        for k in range(SSM_CONV):
            pre = pre + w_ref[k:k + 1, :] * _shift_down(x, SSM_CONV - 1 - k, rows)
        o_ref[...] = _silu(pre)

    return pl.pallas_call(
        body, name=name, grid=(nc, bsz),
        in_specs=[pl.BlockSpec((seq, CONV_CB), lambda j, bb: (bb, off + j)),
                  pl.BlockSpec((SSM_CONV, CONV_CB), lambda j, bb: (0, j)),
                  pl.BlockSpec((1, CONV_CB), lambda j, bb: (0, j))],
        out_specs=pl.BlockSpec((seq, CONV_CB), lambda j, bb: (bb, j)),
        out_shape=jax.ShapeDtypeStruct((bsz * seq, CONV_DIM), f32),
        compiler_params=_cparams(("parallel", "parallel")),
    )(pm, w, b)


def conv_bwd(name, pm, w, b, dact, dskipx, bsz, seq):
    nc = CONV_DIM // CONV_CB
    off = OFF_XBC // CONV_CB
    nxs = SSM_INNER // CONV_CB

    def body(x_ref, w_ref, b_ref, da_ref, ds_ref, dx_ref, dw_ref, db_ref):
        j, bb = pl.program_id(0), pl.program_id(1)
        x = x_ref[...].astype(f32)
        rows = lax.broadcasted_iota(jnp.int32, x.shape, 0)
        xsh = [_shift_down(x, SSM_CONV - 1 - k, rows) for k in range(SSM_CONV)]
        pre = b_ref[...] + jnp.zeros_like(x)
        for k in range(SSM_CONV):
            pre = pre + w_ref[k:k + 1, :] * xsh[k]
        sig = jax.nn.sigmoid(pre)
        dout = da_ref[...] + jnp.where(j < nxs, ds_ref[...], 0.0)
        dpre = dout * (sig * (1.0 + pre * (1.0 - sig)))
        dx = jnp.zeros_like(x)
        for k in range(SSM_CONV):
            dx = dx + w_ref[k:k + 1, :] * _shift_up(dpre, SSM_CONV - 1 - k, rows, seq)
        dx_ref[...] = dx.astype(dx_ref.dtype)
        dw = jnp.concatenate([_colsum(dpre * xsh[k]) for k in range(SSM_CONV)], axis=0)
        db = _colsum(dpre)

        @pl.when(bb == 0)
        def _():
            dw_ref[...] = dw
            db_ref[...] = db

        @pl.when(bb > 0)
        def _():
            dw_ref[...] += dw
            db_ref[...] += db

    return pl.pallas_call(
        body, name=name, grid=(nc, bsz),
        in_specs=[pl.BlockSpec((seq, CONV_CB), lambda j, bb: (bb, off + j)),
                  pl.BlockSpec((SSM_CONV, CONV_CB), lambda j, bb: (0, j)),
                  pl.BlockSpec((1, CONV_CB), lambda j, bb: (0, j)),
                  pl.BlockSpec((seq, CONV_CB), lambda j, bb: (bb, j)),
                  pl.BlockSpec((seq, CONV_CB), lambda j, bb: (bb, jnp.minimum(j, nxs - 1)))],
        out_specs=[pl.BlockSpec((seq, CONV_CB), lambda j, bb: (bb, j)),
                   pl.BlockSpec((SSM_CONV, CONV_CB), lambda j, bb: (0, j)),
                   pl.BlockSpec((1, CONV_CB), lambda j, bb: (0, j))],
        out_shape=[jax.ShapeDtypeStruct((bsz * seq, CONV_DIM), bf16), jax.ShapeDtypeStruct((SSM_CONV, CONV_DIM), f32),
                   jax.ShapeDtypeStruct((1, CONV_DIM), f32)],
        compiler_params=_cparams(("parallel", "arbitrary")),
    )(pm, w, b, dact, dskipx)


CL = SSM_CHUNK


def _ssd_head(x, dtc, csc, csr, bm, cm, s_in):
    row = lax.broadcasted_iota(jnp.int32, (CL, CL), 0)
    col = lax.broadcasted_iota(jnp.int32, (CL, CL), 1)
    xd = x * dtc
    cb = _dot_nt(cm.astype(bf16), bm.astype(bf16))
    decay = jnp.exp(jnp.where(row >= col, csc - csr, -1e30))
    y = _dot((cb * decay).astype(bf16), xd.astype(bf16)) + _dot(cm.astype(bf16), s_in.astype(bf16)) * jnp.exp(csc)
    is_last = lax.broadcasted_iota(jnp.int32, (CL, 1), 0) == CL - 1
    cs_last = jnp.sum(jnp.where(is_last, csc, 0.0), axis=0, keepdims=True)
    w = xd * jnp.exp(cs_last - csc)
    s_out = s_in * jnp.exp(cs_last) + _dot_tn(bm.astype(bf16), w.astype(bf16))
    return y, s_out


def _dot3(a, b, split_a):
    x = a if split_a else b
    t1 = x.astype(bf16)
    r1 = x - t1.astype(f32)
    t2 = r1.astype(bf16)
    t3 = (r1 - t2.astype(f32)).astype(bf16)
    if split_a:
        return _dot(t1, b) + _dot(t2, b) + _dot(t3, b)
    return _dot(a, t1) + _dot(a, t2) + _dot(a, t3)


def _ssd_specs(bsz, seq, rev):
    nch = seq // CL

    def ch(c):
        return (nch - 1 - c) if rev else c

    xg = pl.BlockSpec((CL, GW), lambda b, g, c: (b * nch + ch(c), g))
    lane128 = pl.BlockSpec((CL, 128), lambda b, g, c: (b * nch + ch(c), g))
    adt_t = pl.BlockSpec((128, CL), lambda b, g, c: (g, b * nch + ch(c)))
    bspec = pl.BlockSpec((CL, 128), lambda b, g, c: (b * nch + ch(c), SSM_INNER // 128 + g))
    cspec = pl.BlockSpec((CL, 128), lambda b, g, c: (b * nch + ch(c), SSM_INNER // 128 + SSM_GROUPS + g))
    st = pl.BlockSpec((1, 1, 1, HPG, SSM_STATE, SSM_HD), lambda b, g, c: (b, ch(c), g, 0, 0, 0))
    return nch, xg, lane128, adt_t, bspec, cspec, st


def ssd_fwd(name, xbc, dt, adt, adt_t, bsz, seq):
    nch, xg, lane128, adt_t_spec, bspec, cspec, st = _ssd_specs(bsz, seq, False)

    def body(x_ref, dt_ref, adt_ref, adtt_ref, b_ref, c_ref, y_ref, st_ref, s_scr):
        @pl.when(pl.program_id(2) == 0)
        def _():
            s_scr[...] = jnp.zeros_like(s_scr)

        csc_all = _dot3(_tri(CL, "ge"), adt_ref[...], False)
        csr_all = _dot3(adtt_ref[0:HPG, :], _tri(CL, "le"), True)
        bm, cm = b_ref[...], c_ref[...]
        for h in range(HPG):
            s_in = s_scr[h]
            st_ref[0, 0, 0, h] = s_in
            y, s_out = _ssd_head(x_ref[:, h * SSM_HD:(h + 1) * SSM_HD], dt_ref[:, h:h + 1], csc_all[:, h:h + 1],
                                 csr_all[h:h + 1, :], bm, cm, s_in)
            y_ref[:, h * SSM_HD:(h + 1) * SSM_HD] = y
            s_scr[h] = s_out

    return pl.pallas_call(
        body, name=name, grid=(bsz, SSM_GROUPS, nch),
        in_specs=[xg, lane128, lane128, adt_t_spec, bspec, cspec],
        out_specs=[xg, st],
        out_shape=[jax.ShapeDtypeStruct((bsz * seq, SSM_INNER), f32),
                   jax.ShapeDtypeStruct((bsz, nch, SSM_GROUPS, HPG, SSM_STATE, SSM_HD), f32)],
        scratch_shapes=[pltpu.VMEM((HPG, SSM_STATE, SSM_HD), f32)],
        compiler_params=_cparams(("parallel", "parallel", "arbitrary")),
    )(xbc, dt, adt, adt_t, xbc, xbc)


def ssd_bwd(name, xbc, dt, adt, adt_t, states, dy, bsz, seq):
    nch, xg, lane128, adt_t_spec, bspec, cspec, st = _ssd_specs(bsz, seq, True)

    def body(x_ref, dt_ref, adt_ref, adtt_ref, b_ref, c_ref, st_ref, dy_ref,
             dx_ref, db_ref, dc_ref, ddt_ref, dac_ref, dar_ref, ds_scr):
        @pl.when(pl.program_id(2) == 0)
        def _():
            ds_scr[...] = jnp.zeros_like(ds_scr)

        csc_all = _dot3(_tri(CL, "ge"), adt_ref[...], False)
        csr_all = _dot3(adtt_ref[0:HPG, :], _tri(CL, "le"), True)
        bm, cm = b_ref[...], c_ref[...]
        lane = lax.broadcasted_iota(jnp.int32, (CL, 128), 1)
        sub = lax.broadcasted_iota(jnp.int32, (HPG, CL), 0)
        db = jnp.zeros((CL, 128), f32)
        dc = jnp.zeros((CL, 128), f32)
        ddt = jnp.zeros((CL, 128), f32)
        dcsc = jnp.zeros((CL, 128), f32)
        dcsr = jnp.zeros((HPG, CL), f32)
        for h in range(HPG):
            _, vjp = jax.vjp(_ssd_head, x_ref[:, h * SSM_HD:(h + 1) * SSM_HD], dt_ref[:, h:h + 1],
                             csc_all[:, h:h + 1], csr_all[h:h + 1, :], bm, cm, st_ref[0, 0, 0, h])
            dx, ddtc, dcc, dcr, db_h, dc_h, ds_in = vjp((dy_ref[:, h * SSM_HD:(h + 1) * SSM_HD], ds_scr[h]))
            dx_ref[:, h * SSM_HD:(h + 1) * SSM_HD] = dx
            ds_scr[h] = ds_in
            db, dc = db + db_h, dc + dc_h
            ddt = jnp.where(lane == h, ddtc, ddt)
            dcsc = jnp.where(lane == h, dcc, dcsc)
            dcsr = jnp.where(sub == h, dcr, dcsr)
        db_ref[...] = db
        dc_ref[...] = dc
        ddt_ref[...] = ddt
        dac_ref[...] = _dot3(_tri(CL, "le"), dcsc, False)
        dar_ref[...] = jnp.zeros_like(dar_ref)
        dar_ref[0:HPG, :] = _dot3(dcsr, _tri(CL, "ge"), True)

    t = bsz * seq
    return pl.pallas_call(
        body, name=name, grid=(bsz, SSM_GROUPS, nch),
        in_specs=[xg, lane128, lane128, adt_t_spec, bspec, cspec, st, xg],
        out_specs=[xg, lane128, lane128, lane128, lane128, adt_t_spec],
        out_shape=[jax.ShapeDtypeStruct((t, SSM_INNER), f32), jax.ShapeDtypeStruct((t, DT_W), f32),
                   jax.ShapeDtypeStruct((t, DT_W), f32), jax.ShapeDtypeStruct((t, DT_W), f32),
                   jax.ShapeDtypeStruct((t, DT_W), f32), jax.ShapeDtypeStruct((DT_W, t), f32)],
        scratch_shapes=[pltpu.VMEM((HPG, SSM_STATE, SSM_HD), f32)],
        compiler_params=_cparams(("parallel", "parallel", "arbitrary")),
    )(xbc, dt, adt, adt_t, xbc, xbc, states, dy)


XA_BQ = 512


def _xattn(q, k, v):
    s = _dot_nt(q.astype(bf16), k.astype(bf16)) * (XA_HD ** -0.5)
    p = jnp.exp(s - jnp.max(s, axis=-1, keepdims=True))
    p = p / jnp.sum(p, axis=-1, keepdims=True)
    return _dot(p.astype(bf16), v.astype(bf16))


def xattn_fwd(name, q, kv, bsz, seq, mlen):
    bq = min(XA_BQ, seq)
    nq = seq // bq

    def body(q_ref, k_ref, v_ref, o_ref):
        o_ref[...] = _xattn(q_ref[...], k_ref[...], v_ref[...]).astype(o_ref.dtype)

    return pl.pallas_call(
        body, name=name, grid=(bsz, XA_HEADS, nq),
        in_specs=[pl.BlockSpec((bq, XA_HD), lambda b, h, i: (b * nq + i, h)),
                  pl.BlockSpec((mlen, XA_HD), lambda b, h, i: (b, h)),
                  pl.BlockSpec((mlen, XA_HD), lambda b, h, i: (b, XA_HEADS + h))],
        out_specs=pl.BlockSpec((bq, XA_HD), lambda b, h, i: (b * nq + i, h)),
        out_shape=jax.ShapeDtypeStruct((bsz * seq, D), bf16),
        compiler_params=_cparams(("parallel", "parallel", "parallel")),
    )(q, kv, kv)


def xattn_bwd(name, q, kv, do, bsz, seq, mlen):
    bq = min(XA_BQ, seq)
    nq = seq // bq

    def body(q_ref, k_ref, v_ref, do_ref, dq_ref, dk_ref, dv_ref):
        _, vjp = jax.vjp(_xattn, q_ref[...], k_ref[...], v_ref[...])
        dq, dk, dv = vjp(do_ref[...])
        dq_ref[...] = dq.astype(dq_ref.dtype)
        i = pl.program_id(2)

        @pl.when(i == 0)
        def _():
            dk_ref[...] = dk
            dv_ref[...] = dv

        @pl.when(i > 0)
        def _():
            dk_ref[...] += dk
            dv_ref[...] += dv

    kspec = pl.BlockSpec((mlen, XA_HD), lambda b, h, i: (b, h))
    vspec = pl.BlockSpec((mlen, XA_HD), lambda b, h, i: (b, XA_HEADS + h))
    qspec = pl.BlockSpec((bq, XA_HD), lambda b, h, i: (b * nq + i, h))
    return pl.pallas_call(
        body, name=name, grid=(bsz, XA_HEADS, nq),
        in_specs=[qspec, kspec, vspec, qspec],
        out_specs=[qspec, kspec, kspec],
        out_shape=[jax.ShapeDtypeStruct((bsz * seq, D), bf16), jax.ShapeDtypeStruct((bsz * mlen, D), f32),
                   jax.ShapeDtypeStruct((bsz * mlen, D), f32)],
        compiler_params=_cparams(("parallel", "parallel", "arbitrary")),
    )(q, kv, kv, do)


def _layer_fwd(l, x, mem, w, bsz, seq, mlen):
    n = f"l{l}_"
    sv = {"x0": x}
    sv["h1"] = h1 = rms_fwd(n + "rms_mix", x, w["g_pre_mix"])
    sv["pm"] = pm = _mm(n + "in_proj", h1, w["wm"], "nn", bf16)
    sv["pdt"] = pdt = _mm(n + "in_proj_dt", h1, w["wdt"], "nn")
    sv["o_att"], sv["tot"] = o_att, _ = sb_fwd(n + "sb_fwd", pm, bsz, seq)
    sv["xbc"] = xbc = conv_fwd(n + "conv_fwd", pm, w["conv_w"], w["conv_b"], bsz, seq)
    sv["dt"], sv["adt"] = dt, adt = dt_fwd(n + "dt_fwd", pdt, w["dt_bias"], w["a_log"])
    sv["adt_t"] = adt_t = adt.T
    sv["y_ssd"], sv["states"] = y_ssd, _ = ssd_fwd(n + "ssd_fwd", xbc, dt, adt, adt_t, bsz, seq)
    sv["o_ssm"] = o_ssm = gnorm_fwd(n + "gnorm_fwd", y_ssd, xbc, pm, w["d_skip"], w["g_ssm_norm"])
    sv["a"] = a = _mm(n + "br_att", o_att, w["w_br_att"], "nn")
    sv["s"] = s = _mm(n + "br_ssm", o_ssm, w["w_br_ssm"], "nn")
    sv["merged"] = merged = merge_fwd(n + "merge_fwd", pm, a, s)
    sv["u"] = u = _mm(n + "mix_out", merged, w["w_mix_out"], "nn")
    sv["x1"] = x1 = addnorm_fwd(n + "post_mix", x, u, w["g_post_mix"])
    sv["h2"] = h2 = rms_fwd(n + "rms_xa", x1, w["g_pre_xa"])
    sv["memn"] = memn = rms_fwd(n + "rms_mem", mem, w["g_mem"])
    sv["qx"] = qx = _mm(n + "xq", h2, w["w_xq"], "nn")
    sv["kv"] = kv = _mm(n + "xkv", memn, w["w_xkv"], "nn")
    sv["ox"] = ox = xattn_fwd(n + "xattn_fwd", qx, kv, bsz, seq, mlen)
    sv["yx"] = yx = _mm(n + "xo", ox, w["w_xo"], "nn")
    sv["x2"] = x2 = addnorm_fwd(n + "post_xa", x1, yx, w["g_post_xa"])
    sv["h3"] = h3 = rms_fwd(n + "rms_ffn", x2, w["g_pre_ffn"])
    sv["gu"] = gu = _mm(n + "gu", h3, w["w_gu"], "nn", bf16)
    sv["act"] = act = swiglu_fwd(n + "swiglu_fwd", gu)
    sv["d"] = d = _mm(n + "down", act, w["w_down"], "nn")
    x3 = addnorm_fwd(n + "post_ffn", x2, d, w["g_post_ffn"])
    return x3, sv


def _layer_bwd(l, dx, mem, w, sv, bsz, seq, mlen):
    n = f"l{l}_b_"
    g = {}
    dd, g["g_post_ffn"] = addnorm_bwd(n + "post_ffn", sv["d"], w["g_post_ffn"], dx)
    g["w_down"] = _mm(n + "dw_down", sv["act"], dd, "tn")
    dact = _mm(n + "dact", dd, w["w_down"], "nt", bf16)
    dgu = swiglu_bwd(n + "swiglu", sv["gu"], dact)
    g["w_gu"] = _mm(n + "dw_gu", sv["h3"], dgu, "tn")
    dh3 = _mm(n + "dh3", dgu, w["w_gu"], "nt")
    dx, g["g_pre_ffn"] = rms_bwd(n + "rms_ffn", sv["x2"], w["g_pre_ffn"], [dh3], dx)
    dyx, g["g_post_xa"] = addnorm_bwd(n + "post_xa", sv["yx"], w["g_post_xa"], dx)
    g["w_xo"] = _mm(n + "dw_xo", sv["ox"], dyx, "tn")
    dox = _mm(n + "dox", dyx, w["w_xo"], "nt")
    dqx, dk, dv = xattn_bwd(n + "xattn", sv["qx"], sv["kv"], dox, bsz, seq, mlen)
    g["w_xq"] = _mm(n + "dw_xq", sv["h2"], dqx, "tn")
    dh2 = _mm(n + "dh2", dqx, w["w_xq"], "nt")
    dkv = jnp.concatenate([dk, dv], axis=1)
    g["w_xkv"] = _mm(n + "dw_xkv", sv["memn"], dkv, "tn")
    dmemn = _mm(n + "dmemn", dkv, w["w_xkv"], "nt")
    _, g["g_mem"] = rms_bwd(n + "rms_mem", mem, w["g_mem"], [dmemn])
    dx, g["g_pre_xa"] = rms_bwd(n + "rms_xa", sv["x1"], w["g_pre_xa"], [dh2], dx)
    du, g["g_post_mix"] = addnorm_bwd(n + "post_mix", sv["u"], w["g_post_mix"], dx)
    g["w_mix_out"] = _mm(n + "dw_mix", sv["merged"], du, "tn")
    dmerged = _mm(n + "dmerged", du, w["w_mix_out"], "nt")
    dgates, da, ds = merge_bwd(n + "merge", sv["pm"], sv["a"], sv["s"], dmerged)
    g["w_br_att"] = _mm(n + "dw_att", sv["o_att"], da, "tn")
    do_att = _mm(n + "do_att", da, w["w_br_att"], "nt")
    g["w_br_ssm"] = _mm(n + "dw_ssm", sv["o_ssm"], ds, "tn")
    do_ssm = _mm(n + "do_ssm", ds, w["w_br_ssm"], "nt")
    dy_ssd, dxs_skip, dz, g["d_skip"], g["g_ssm_norm"] = gnorm_bwd(
        n + "gnorm", sv["y_ssd"], sv["xbc"], sv["pm"], w["d_skip"], w["g_ssm_norm"], do_ssm)
    dxs, dbm, dcm, ddt, dadt_c, dadt_r = ssd_bwd(n + "ssd", sv["xbc"], sv["dt"], sv["adt"], sv["adt_t"],
                                                   sv["states"], dy_ssd, bsz, seq)
    dxbc_act = jnp.concatenate([dxs, dbm, dcm], axis=1)
    dxbc, g["conv_w"], g["conv_b"] = conv_bwd(n + "conv", sv["pm"], w["conv_w"], w["conv_b"], dxbc_act, dxs_skip, bsz, seq)
    dpdt, g["dt_bias"], g["a_log"] = dt_bwd(n + "dt", sv["pdt"], w["dt_bias"], w["a_log"], ddt, dadt_c, dadt_r.T)
    dq, dk_, dv_ = sb_bwd(n + "sb", sv["pm"], sv["tot"], do_att, bsz, seq)
    dpm = jnp.concatenate([dz, dxbc, dq, dk_, dv_, dgates], axis=1)
    g["wm"] = _mm(n + "dw_in", sv["h1"], dpm, "tn")
    g["wdt"] = _mm(n + "dw_in_dt", sv["h1"], dpdt, "tn")
    dh1 = _mm(n + "dh1", dpm, w["wm"], "nt")
    dh1_dt = _mm(n + "dh1_dt", dpdt, w["wdt"], "nt")
    dx, g["g_pre_mix"] = rms_bwd(n + "rms_mix", sv["x0"], w["g_pre_mix"], [dh1, dh1_dt], dx)
    return dx, g


def _group_pad(v):
    lead = v.shape[:-1]
    v = v.reshape(*lead, SSM_GROUPS, HPG)
    return jnp.pad(v, [(0, 0)] * (len(lead) + 1) + [(0, 128 - HPG)]).reshape(*lead, DT_W)


def _group_unpad(v):
    lead = v.shape[:-1]
    return v.reshape(*lead, SSM_GROUPS, 128)[..., :HPG].reshape(*lead, SSM_HEADS)


BIG = ("w_in", "w_br_att", "w_br_ssm", "w_mix_out", "w_xq", "w_xkv", "w_xo", "w_gu", "w_down")
GAINS = ("g_pre_mix", "g_post_mix", "g_pre_xa", "g_mem", "g_post_xa", "g_pre_ffn", "g_post_ffn")
HEAD_VECS = ("dt_bias", "a_log", "d_skip")
SMALL = GAINS + ("conv_w", "conv_b", "g_ssm_norm") + HEAD_VECS


def _prep_layer(p):
    w_in = p["w_in"]
    w = {k: p[k] for k in BIG[1:]}
    w["wm"] = jnp.concatenate([w_in[:, 3072:8192], w_in[:, 0:3072], w_in[:, 8224:10272]], axis=1)
    w["wdt"] = _group_pad(w_in[:, 8192:8224])
    for k in GAINS + ("conv_b", "g_ssm_norm"):
        w[k] = p[k].reshape(1, -1)
    w["conv_w"] = p["conv_w"]
    w["dt_bias"] = _group_pad(p["dt_bias"]).reshape(1, DT_W)
    w["a_log"] = _group_pad(p["a_log"]).reshape(1, DT_W)
    w["d_skip"] = jnp.repeat(p["d_skip"], SSM_HD).reshape(1, SSM_INNER)
    return w


def _unprep_grads(g):
    out = {k: g[k] for k in BIG[1:]}
    gm = g["wm"]
    out["w_in"] = jnp.concatenate([gm[:, 5120:8192], gm[:, 0:5120], _group_unpad(g["wdt"]), gm[:, 8192:10240]], axis=1)
    for k in GAINS + ("conv_b", "g_ssm_norm"):
        out[k] = g[k].reshape(-1)
    out["conv_w"] = g["conv_w"]
    out["dt_bias"] = _group_unpad(g["dt_bias"]).reshape(-1)
    out["a_log"] = _group_unpad(g["a_log"]).reshape(-1)
    out["d_skip"] = g["d_skip"].reshape(SSM_HEADS, SSM_HD).sum(axis=1)
    return out


def _local_step(x, mem, target, ws, bsz, seq, mlen):
    saved = []
    for l in range(len(ws)):
        x, sv = _layer_fwd(l, x, mem, ws[l], bsz, seq, mlen)
        saved.append(sv)
    dx, loss_lanes = loss_fwd_bwd("loss", x, target)
    grads = [None] * len(ws)
    for l in reversed(range(len(ws))):
        dx, grads[l] = _layer_bwd(l, dx, mem, ws[l], saved[l], bsz, seq, mlen)
    return loss_lanes, dx, grads


HBM_SPEC = pl.BlockSpec(memory_space=pltpu.HBM)
FLIP_C = (0, 0, 1)
FLIPS_CHIP = ((1, 0, 0), (0, 1, 0), (1, 1, 0))
FLIPS_ALL = tuple(((f >> 2) & 1, (f >> 1) & 1, f & 1) for f in range(1, 8))


def _exchange(name, src, transfers, n_slots):
    _, r, w = src.shape
    nt = len(transfers)
    assert sorted(t[2] for t in transfers) == list(range(n_slots))

    def body(src_ref, out_ref, send_sems, recv_sems):
        pos = (lax.axis_index("x"), lax.axis_index("y"), lax.axis_index("c"))
        copies = []
        for t, (flip, index_fn, slot) in enumerate(transfers):
            peer = tuple(1 - p if f else p for p, f in zip(pos, flip))
            cp = pltpu.make_async_remote_copy(
                src_ref=src_ref.at[index_fn(*pos)], dst_ref=out_ref.at[slot],
                send_sem=send_sems.at[t], recv_sem=recv_sems.at[t], device_id=peer, device_id_type=MESH)
            cp.start()
            copies.append(cp)
        for cp in copies:
            cp.wait()

    return pl.pallas_call(
        body, name=name, out_shape=jax.ShapeDtypeStruct((n_slots, r, w), src.dtype),
        in_specs=[HBM_SPEC], out_specs=HBM_SPEC,
        scratch_shapes=[pltpu.SemaphoreType.DMA((nt,)), pltpu.SemaphoreType.DMA((nt,))],
    )(src)


def _const(i):
    return lambda x, y, c: i


def _allgather8(name, v, me):
    got = _exchange(name, v[None], [(fl, _const(0), j) for j, fl in enumerate(FLIPS_ALL)], 7)
    rel = jnp.concatenate([v[None], got], axis=0)
    return jnp.stack([lax.dynamic_index_in_dim(rel, k ^ me, 0, keepdims=False) for k in range(8)])


def _sum8(name, parts):
    def fn(*p):
        acc = p[0]
        for q in p[1:]:
            acc = acc + q
        return acc

    return _rowwise(name, fn, [(parts[k], 0) for k in range(8)], [], [(1, f32)], [], width=128, bt=parts.shape[1])[0]


RED_BT = 464


def _reduce_pairs(name, gf8, recv, ci):
    _, r, w = gf8.shape
    nb = r // RED_BT

    def body(c_ref, a_ref, b_ref, o32_ref, o16_ref):
        acc = a_ref[...] + b_ref[...].astype(f32)
        o32_ref[...] = acc
        o16_ref[...] = acc.astype(bf16)

    blk = (1, RED_BT, w)
    return pl.pallas_call(
        body, name=name,
        grid_spec=pltpu.PrefetchScalarGridSpec(
            num_scalar_prefetch=1, grid=(4, nb),
            in_specs=[pl.BlockSpec(blk, lambda s, i, c_ref: (2 * s + c_ref[0], i, 0)),
                      pl.BlockSpec(blk, lambda s, i, c_ref: (s, i, 0))],
            out_specs=[pl.BlockSpec(blk, lambda s, i, c_ref: (s, i, 0))] * 2),
        out_shape=[jax.ShapeDtypeStruct((4, r, w), f32), jax.ShapeDtypeStruct((4, r, w), bf16)],
        compiler_params=_cparams(("parallel", "parallel")),
    )(ci.reshape(1).astype(jnp.int32), gf8, recv)


def _reduce_final(name, p32, recv, shard):
    _, r, w = p32.shape
    nb = r // RED_BT

    def body(s_ref, a_ref, b_ref, o_ref):
        acc = a_ref[0]
        for j in range(3):
            acc = acc + b_ref[j].astype(f32)
        o_ref[0] = acc

    return pl.pallas_call(
        body, name=name,
        grid_spec=pltpu.PrefetchScalarGridSpec(
            num_scalar_prefetch=1, grid=(nb,),
            in_specs=[pl.BlockSpec((1, RED_BT, w), lambda i, s_ref: (s_ref[0], i, 0)),
                      pl.BlockSpec((3, RED_BT, w), lambda i, s_ref: (0, i, 0))],
            out_specs=pl.BlockSpec((1, RED_BT, w), lambda i, s_ref: (0, i, 0))),
        out_shape=jax.ShapeDtypeStruct((1, r, w), f32),
        compiler_params=_cparams(("parallel",)),
    )(shard.reshape(1).astype(jnp.int32), p32, recv)


COL_SHARDED = ("w_in", "w_xkv", "w_gu")
W = 1024


def _flat_rows(a):
    return a.reshape(-1, W)


def _pack(arrs, rows_multiple=8):
    flat = jnp.concatenate([a.reshape(-1) for a in arrs])
    pad = (-flat.shape[0]) % (128 * rows_multiple)
    return jnp.pad(flat, (0, pad)).reshape(-1, 128)


def _unpack(buf, shapes):
    flat, out, o = buf.reshape(-1), [], 0
    for s in shapes:
        n = math.prod(s)
        out.append(flat[o:o + n].reshape(s))
        o += n
    return out


def kernel(x, mem, g_pre_mix, w_in, conv_w, conv_b, dt_bias, a_log, d_skip, g_ssm_norm, w_br_att, w_br_ssm, w_mix_out, g_post_mix, g_pre_xa, g_mem, w_xq, w_xkv, w_xo, g_post_xa, g_pre_ffn, w_gu, w_down, g_post_ffn, loss_target, m_g_pre_mix, m_w_in, m_conv_w, m_conv_b, m_dt_bias, m_a_log, m_d_skip, m_g_ssm_norm, m_w_br_att, m_w_br_ssm, m_w_mix_out, m_g_post_mix, m_g_pre_xa, m_g_mem, m_w_xq, m_w_xkv, m_w_xo, m_g_post_xa, m_g_pre_ffn, m_w_gu, m_w_down, m_g_post_ffn, v_g_pre_mix, v_w_in, v_conv_w, v_conv_b, v_dt_bias, v_a_log, v_d_skip, v_g_ssm_norm, v_w_br_att, v_w_br_ssm, v_w_mix_out, v_g_post_mix, v_g_pre_xa, v_g_mem, v_w_xq, v_w_xkv, v_w_xo, v_g_post_xa, v_g_pre_ffn, v_w_gu, v_w_down, v_g_post_ffn):
    a = dict(locals())
    names = ("g_pre_mix", "w_in", "conv_w", "conv_b", "dt_bias", "a_log", "d_skip", "g_ssm_norm", "w_br_att", "w_br_ssm",
             "w_mix_out", "g_post_mix", "g_pre_xa", "g_mem", "w_xq", "w_xkv", "w_xo", "g_post_xa", "g_pre_ffn", "w_gu",
             "w_down", "g_post_ffn")
    depth = w_in.shape[0]
    bsz, seq, _ = x.shape
    mlen = mem.shape[1]
    xi, yi, ci = lax.axis_index("x"), lax.axis_index("y"), lax.axis_index("c")
    shard = 2 * xi + yi
    me = 2 * shard + ci

    segs = [(l, k) for l in range(depth) for k in BIG]
    seg_rows = [a[k].shape[1] * a[k].shape[2] // W for _, k in segs]
    wb = jnp.concatenate([_flat_rows(a[k][l].astype(bf16)) for l, k in segs], axis=0)
    rows = wb.shape[0]
    half = rows // 2
    wb2 = wb.reshape(2, half, W)
    got_ici = _exchange("ag_ici", wb2, [(fl, lambda x_, y_, c_: c_, j) for j, fl in enumerate(FLIPS_CHIP)], 3)
    got_d2d = _exchange("ag_d2d", got_ici, [(FLIP_C, _const(j), j) for j in range(3)], 3)
    rel = [None] * 4
    rel[0] = wb
    for j, fl in enumerate(FLIPS_CHIP):
        pair = jnp.stack([got_ici[j], got_d2d[j]])
        rel[2 * fl[0] + fl[1]] = jnp.concatenate([lax.dynamic_index_in_dim(pair, ci, 0, keepdims=False),
                                                  lax.dynamic_index_in_dim(pair, 1 - ci, 0, keepdims=False)], axis=0)
    rel = jnp.stack(rel)
    shards = [lax.dynamic_index_in_dim(rel, s ^ shard, 0, keepdims=False) for s in range(4)]

    cw_all = _allgather8("ag_conv_w", _pack([conv_w]), me)
    cw_shape = conv_w.shape
    conv_w_full = jnp.concatenate([_unpack(cw_all[2 * s], [cw_shape])[0] for s in range(4)], axis=2)

    ws, o = [], 0
    full = {}
    for (l, k), n in zip(segs, seg_rows):
        r, c = a[k].shape[1:]
        pieces = [shards[s][o:o + n].reshape(r, c) for s in range(4)]
        full[(l, k)] = jnp.concatenate(pieces, axis=1 if k in COL_SHARDED else 0)
        o += n
    for l in range(depth):
        p = {k: full[(l, k)] for k in BIG}
        for k in SMALL:
            p[k] = conv_w_full[l] if k == "conv_w" else a[k][l]
        ws.append(_prep_layer(p))

    loss_lanes, gx, grads = _local_step(x.reshape(bsz * seq, D), mem.reshape(bsz * mlen, D),
                                        loss_target.reshape(bsz * seq, D), ws, bsz, seq, mlen)
    grads = [_unprep_grads(g) for g in grads]

    pieces = []
    for (l, k), n in zip(segs, seg_rows):
        g = grads[l][k]
        r, c = a[k].shape[1:]
        g4 = g.reshape(g.shape[0], 4, c).transpose(1, 0, 2) if k in COL_SHARDED else g.reshape(4, r, c)
        pieces.append(g4.reshape(4, n, W))
    gf8 = jnp.concatenate(pieces, axis=1).reshape(8, half, W)
    wire1 = gf8.astype(bf16)
    from_sib = _exchange("rs_d2d", wire1, [(FLIP_C, functools.partial(lambda x_, y_, c_, s: 2 * s + 1 - c_, s=s), s)
                                           for s in range(4)], 4)
    p32, p16 = _reduce_pairs("rs_sum_pair", gf8, from_sib, ci)
    from_chips = _exchange("rs_ici", p16, [(fl, functools.partial(lambda x_, y_, c_, f: (2 * x_ + y_) ^ f, f=2 * fl[0] + fl[1]), j)
                                           for j, fl in enumerate(FLIPS_CHIP)], 3)
    red = _reduce_final("rs_sum_chips", p32, from_chips, shard)
    red_sib = _exchange("rs_swap", red, [(FLIP_C, _const(0), 0)], 1)
    pair = jnp.concatenate([red, red_sib], axis=0)
    gshard = jnp.concatenate([lax.dynamic_index_in_dim(pair, ci, 0, keepdims=False),
                              lax.dynamic_index_in_dim(pair, 1 - ci, 0, keepdims=False)], axis=0)

    out_g, out_d, out_m, out_v = {}, {}, {}, {}
    o, per = 0, {}
    for (l, k), n in zip(segs, seg_rows):
        per[(l, k)] = gshard[o:o + n].reshape(a[k].shape[1:])
        o += n
    for k in BIG:
        g = jnp.stack([per[(l, k)] for l in range(depth)])
        shp = a[k].shape
        two_d = (shp[0] * shp[1], shp[2])
        d_, m_, v_ = adamw("adamw_" + k, a[k].reshape(two_d), g.reshape(two_d), a["m_" + k].reshape(two_d), a["v_" + k].reshape(two_d))
        out_g[k], out_d[k], out_m[k], out_v[k] = g, d_.reshape(shp), m_.reshape(shp), v_.reshape(shp)

    small_shapes = [(depth,) + (conv_w_full.shape[1:] if k == "conv_w" else a[k].shape[1:]) for k in SMALL]
    small = _pack([jnp.stack([grads[l][k] for l in range(depth)]) for k in SMALL] + [loss_lanes])
    total = _sum8("small_sum", _allgather8("ag_small", small, me))
    *gsmall, loss_l = _unpack(total, small_shapes + [loss_lanes.shape])
    gsmall = dict(zip(SMALL, gsmall))
    gsmall["conv_w"] = lax.dynamic_slice_in_dim(gsmall["conv_w"], shard * cw_shape[2], cw_shape[2], axis=2)
    loc_shapes = [a[k].shape for k in SMALL]
    d_, m_, v_ = adamw("adamw_small", _pack([a[k] for k in SMALL]), _pack([gsmall[k] for k in SMALL]),
                       _pack([a["m_" + k] for k in SMALL]), _pack([a["v_" + k] for k in SMALL]))
    for k, dd, mm, vv in zip(SMALL, _unpack(d_, loc_shapes), _unpack(m_, loc_shapes), _unpack(v_, loc_shapes)):
        out_g[k], out_d[k], out_m[k], out_v[k] = gsmall[k], dd, mm, vv

    loss = jnp.sum(loss_l)
    return (loss, gx.reshape(x.shape), *[out_g[k] for k in names], *[out_d[k] for k in names],
            *[out_m[k] for k in names], *[out_v[k] for k in names])
```

```python
import functools
import math

import jax
import jax.numpy as jnp
from jax import lax
from jax.experimental import pallas as pl
from jax.experimental.pallas import tpu as pltpu

f32, bf16 = jnp.float32, jnp.bfloat16

DEPTH = 4
D = 1024
SB_HEADS, SB_HD = 16, 64
SSM_INNER, SSM_HD, SSM_HEADS, SSM_GROUPS, SSM_STATE, SSM_CONV, SSM_CHUNK = 2048, 64, 32, 4, 128, 4, 128
HPG = SSM_HEADS // SSM_GROUPS
CONV_DIM = SSM_INNER + 2 * SSM_GROUPS * SSM_STATE
XA_HEADS, XA_HD = 4, 256
FFN = 2816
IN_WIDTH = 10272
RMS_EPS = 1e-6
LR, B1, B2, EPS, WD, STEP = 0.001, 0.9, 0.999, 1e-08, 0.01, 10

PM_W = 10240
OFF_Z, OFF_XBC, OFF_Q, OFF_K, OFF_V, OFF_GA, OFF_GS = 0, 2048, 5120, 6144, 7168, 8192, 9216
DT_W = SSM_GROUPS * 128

VMEM_LIMIT = 48 * 1024 * 1024
MESH = pl.DeviceIdType.MESH


def _cparams(sem):
    return pltpu.CompilerParams(dimension_semantics=sem, vmem_limit_bytes=VMEM_LIMIT)


def _tile(n):
    for t in (512, 256, 128):
        if n % t == 0:
            return t
    raise ValueError(f"dimension {n} is not a multiple of 128")


MM_VMEM_BUDGET = 34 * 1024 * 1024


def _mm_tiles(m, n, k, sa, sb, so):
    best = None
    for tm in (2048, 1024, 512, 256, 128):
        if m % tm:
            continue
        for tn in (2048, 1024, 512, 256, 128):
            if n % tn:
                continue
            for tk in (k, 2048, 1024, 512):
                if tk > k or k % tk:
                    continue
                vmem = 2 * (tm * tk * sa + tk * tn * sb + tm * tn * so) + tm * tn * 4 * (2 if tk < k else 1)
                if vmem > MM_VMEM_BUDGET:
                    continue
                traffic = m * k * sa * (n // tn) + k * n * sb * (m // tm) + m * n * so
                steps = (m // tm) * (n // tn) * (k // tk)
                accumulate = (k // tk > 1) * (k // tk) * m * n * 2
                key = (traffic + steps * 800_000 + accumulate, steps)
                if best is None or key < best[0]:
                    best = (key, (tm, tn, tk))
    assert best is not None, (m, n, k)
    return best[1]


def _mm(name, a, b, mode, out_dtype=f32):
    if mode == "nn":
        (m, k), (k2, n) = a.shape, b.shape
    elif mode == "nt":
        (m, k), (n, k2) = a.shape, b.shape
    else:
        (k, m), (k2, n) = a.shape, b.shape
    assert k == k2, (name, a.shape, b.shape, mode)
    tm, tn, tk = _mm_tiles(m, n, k, a.dtype.itemsize, b.dtype.itemsize, jnp.dtype(out_dtype).itemsize)
    nk = k // tk
    dn = {"nn": (((1,), (0,)), ((), ())), "nt": (((1,), (1,)), ((), ())), "tn": (((0,), (0,)), ((), ()))}[mode]

    def product(a_ref, b_ref):
        return lax.dot_general(a_ref[...].astype(bf16), b_ref[...].astype(bf16), dn, preferred_element_type=f32)

    def body_whole_k(a_ref, b_ref, o_ref):
        o_ref[...] = product(a_ref, b_ref).astype(o_ref.dtype)

    def body_k_loop(a_ref, b_ref, o_ref, acc_ref):
        kk = pl.program_id(2)

        @pl.when(kk == 0)
        def _():
            acc_ref[...] = product(a_ref, b_ref)

        @pl.when(kk > 0)
        def _():
            acc_ref[...] += product(a_ref, b_ref)

        @pl.when(kk == nk - 1)
        def _():
            o_ref[...] = acc_ref[...].astype(o_ref.dtype)

    a_spec = pl.BlockSpec((tk, tm), lambda i, j, kk: (kk, i)) if mode == "tn" else pl.BlockSpec((tm, tk), lambda i, j, kk: (i, kk))
    b_spec = pl.BlockSpec((tn, tk), lambda i, j, kk: (j, kk)) if mode == "nt" else pl.BlockSpec((tk, tn), lambda i, j, kk: (kk, j))
    return pl.pallas_call(
        body_whole_k if nk == 1 else body_k_loop, name=name, grid=(m // tm, n // tn, nk),
        in_specs=[a_spec, b_spec],
        out_specs=pl.BlockSpec((tm, tn), lambda i, j, kk: (i, j)),
        out_shape=jax.ShapeDtypeStruct((m, n), out_dtype),
        scratch_shapes=[] if nk == 1 else [pltpu.VMEM((tm, tn), f32)],
        compiler_params=_cparams(("parallel", "parallel", "arbitrary")),
    )(a, b)


def _rowwise(name, fn, rows, consts, out_rows, out_accs, *, width, ncol=1, bt=256):
    r = rows[0][0].shape[0]
    bt = min(bt, r)
    assert r % bt == 0, (name, r, bt)
    nrow = r // bt
    n_in = len(rows) + len(consts)
    n_or = len(out_rows)

    def body(*refs):
        ins = [ref[...].astype(f32) for ref in refs[:n_in]]
        outs = fn(*ins)
        if not isinstance(outs, (tuple, list)):
            outs = (outs,)
        o_refs = refs[n_in:]
        for o_ref, val in zip(o_refs[:n_or], outs[:n_or]):
            o_ref[...] = val.astype(o_ref.dtype)
        if out_accs:
            i = pl.program_id(1)
            for o_ref, val in zip(o_refs[n_or:], outs[n_or:]):
                @pl.when(i == 0)
                def _(o_ref=o_ref, val=val):
                    o_ref[...] = val

                @pl.when(i > 0)
                def _(o_ref=o_ref, val=val):
                    o_ref[...] += val

    in_specs = [pl.BlockSpec((bt, width), functools.partial(lambda j, i, off: (i, off + j), off=off)) for _, off in rows]
    in_specs += [pl.BlockSpec((c.shape[0], width), functools.partial(lambda j, i, off: (0, off + j), off=off)) for c, off in consts]
    out_specs = [pl.BlockSpec((bt, mlt * width), lambda j, i: (i, j)) for mlt, _ in out_rows]
    out_specs += [pl.BlockSpec((k, width), lambda j, i: (0, j)) for k in out_accs]
    out_shape = [jax.ShapeDtypeStruct((r, ncol * mlt * width), dt) for mlt, dt in out_rows]
    out_shape += [jax.ShapeDtypeStruct((k, ncol * width), f32) for k in out_accs]
    res = pl.pallas_call(
        body, name=name, grid=(ncol, nrow), in_specs=in_specs, out_specs=out_specs, out_shape=out_shape,
        compiler_params=_cparams(("parallel", "arbitrary" if out_accs else "parallel")),
    )(*[a for a, _ in rows], *[c for c, _ in consts])
    return res


def _rms(x, g):
    return x * lax.rsqrt(jnp.mean(x * x, axis=-1, keepdims=True) + RMS_EPS) * g


def _silu(x):
    return x * jax.nn.sigmoid(x)


def _softplus(x):
    return jnp.maximum(x, 0.0) + jnp.log(1.0 + jnp.exp(-jnp.abs(x)))


def _colsum(x):
    return jnp.sum(x, axis=0, keepdims=True)


def rms_fwd(name, x, g):
    return _rowwise(name, _rms, [(x, 0)], [(g, 0)], [(1, bf16)], [], width=D)[0]


def rms_bwd(name, x, g, dhs, dres=None):
    nd = len(dhs)

    def fn(x, *rest):
        dh = rest[0]
        for extra in rest[1:nd]:
            dh = dh + extra
        g = rest[-1]
        _, vjp = jax.vjp(_rms, x, g)
        dx, dg = vjp(dh.astype(f32))
        if dres is not None:
            dx = dx + rest[nd]
        return dx, dg

    rows = [(x, 0)] + [(d, 0) for d in dhs] + ([(dres, 0)] if dres is not None else [])
    return _rowwise(name, fn, rows, [(g, 0)], [(1, f32)], [1], width=D)


def addnorm_fwd(name, x, u, g):
    return _rowwise(name, lambda x, u, g: x + _rms(u, g), [(x, 0), (u, 0)], [(g, 0)], [(1, f32)], [], width=D)[0]


def addnorm_bwd(name, u, g, dx):
    def fn(u, dx, g):
        _, vjp = jax.vjp(_rms, u, g)
        return vjp(dx)

    return _rowwise(name, fn, [(u, 0), (dx, 0)], [(g, 0)], [(1, bf16)], [1], width=D)


def _merge(ga, gs, a, s):
    return jax.nn.sigmoid(ga) * a + jax.nn.sigmoid(gs) * s


def merge_fwd(name, pm, a, s):
    return _rowwise(name, _merge, [(pm, OFF_GA // D), (pm, OFF_GS // D), (a, 0), (s, 0)], [], [(1, bf16)], [], width=D)[0]


def merge_bwd(name, pm, a, s, dm):
    def fn(ga, gs, a, s, dm):
        _, vjp = jax.vjp(_merge, ga, gs, a, s)
        dga, dgs, da, ds = vjp(dm)
        return jnp.concatenate([dga, dgs], axis=1), da, ds

    return _rowwise(name, fn, [(pm, OFF_GA // D), (pm, OFF_GS // D), (a, 0), (s, 0), (dm, 0)], [],
                    [(2, bf16), (1, bf16), (1, bf16)], [], width=D)


def _swiglu(gate, up):
    return _silu(gate) * up


def swiglu_fwd(name, gu):
    return _rowwise(name, _swiglu, [(gu, 0), (gu, 1)], [], [(1, bf16)], [], width=FFN)[0]


def swiglu_bwd(name, gu, dact):
    def fn(gate, up, dact):
        _, vjp = jax.vjp(_swiglu, gate, up)
        dg, du = vjp(dact.astype(f32))
        return jnp.concatenate([dg, du], axis=1)

    return _rowwise(name, fn, [(gu, 0), (gu, 1), (dact, 0)], [], [(2, bf16)], [], width=FFN, bt=128)[0]


GW = SSM_INNER // SSM_GROUPS


def _gnorm(y, xs, z, dskip, gn):
    yy = (y + dskip * xs) * _silu(z)
    return yy * lax.rsqrt(jnp.mean(yy * yy, axis=-1, keepdims=True) + RMS_EPS) * gn


def gnorm_fwd(name, y, xbc, pm, dskip, gn):
    return _rowwise(name, _gnorm, [(y, 0), (xbc, 0), (pm, OFF_Z // GW)], [(dskip, 0), (gn, 0)], [(1, bf16)], [],
                    width=GW, ncol=SSM_GROUPS)[0]


def gnorm_bwd(name, y, xbc, pm, dskip, gn, do):
    def fn(y, xs, z, do, dskip, gn):
        _, vjp = jax.vjp(_gnorm, y, xs, z, dskip, gn)
        return vjp(do.astype(f32))

    return _rowwise(name, fn, [(y, 0), (xbc, 0), (pm, OFF_Z // GW), (do, 0)], [(dskip, 0), (gn, 0)],
                    [(1, f32), (1, f32), (1, bf16)], [1, 1], width=GW, ncol=SSM_GROUPS)


def _dtfn(pdt, bias, alog):
    dt = _softplus(pdt + bias)
    return dt, -jnp.exp(alog) * dt


def dt_fwd(name, pdt, bias, alog):
    return _rowwise(name, _dtfn, [(pdt, 0)], [(bias, 0), (alog, 0)], [(1, f32), (1, f32)], [], width=DT_W)


def dt_bwd(name, pdt, bias, alog, ddt, dadt_c, dadt_r):
    def fn(pdt, ddt, dac, dar, bias, alog):
        _, vjp = jax.vjp(_dtfn, pdt, bias, alog)
        return vjp((ddt, dac + dar))

    return _rowwise(name, fn, [(pdt, 0), (ddt, 0), (dadt_c, 0), (dadt_r, 0)], [(bias, 0), (alog, 0)],
                    [(1, f32)], [1, 1], width=DT_W)


def loss_fwd_bwd(name, y, target):
    def fn(y, t):
        e = y - t
        return e * (1.0 / D), _colsum(e * e) * (0.5 / D)

    return _rowwise(name, fn, [(y, 0), (target, 0)], [], [(1, f32)], [1], width=D)


def adamw(name, w, g, m, v):
    r, c = w.shape

    def fn(w, g, m, v):
        m = B1 * m + (1.0 - B1) * g
        v = B2 * v + (1.0 - B2) * (g * g)
        m_hat = m / (1.0 - B1 ** STEP)
        v_hat = v / (1.0 - B2 ** STEP)
        return -LR * (m_hat / (jnp.sqrt(v_hat) + EPS) + WD * w), m, v

    bt = 256
    while bt > 8 and (r % bt or bt * c * 4 * 7 * 2 > 16 * 1024 * 1024):
        bt //= 2
    if r % bt:
        bt = r
    return _rowwise(name, fn, [(w, 0), (g, 0), (m, 0), (v, 0)], [], [(1, f32)] * 3, [], width=c, bt=bt)


SB_BQ, SB_BK = 512, 256


def _dot(a, b):
    return jnp.dot(a, b, preferred_element_type=f32)


def _dot_nt(a, b):
    return lax.dot_general(a, b, (((1,), (1,)), ((), ())), preferred_element_type=f32)


def _dot_tn(a, b):
    return lax.dot_general(a, b, (((0,), (0,)), ((), ())), preferred_element_type=f32)


def _dot2(x, tri):
    hi = x.astype(bf16)
    lo = (x - hi.astype(f32)).astype(bf16)
    return _dot(hi, tri) + _dot(lo, tri)


def _tri(n, rel):
    r = lax.broadcasted_iota(jnp.int32, (n, n), 0)
    c = lax.broadcasted_iota(jnp.int32, (n, n), 1)
    m = {"ge": r >= c, "lt": r < c, "le": r <= c}[rel]
    return jnp.where(m, 1.0, 0.0).astype(bf16)


def sb_fwd(name, pm, bsz, seq):
    bq = min(SB_BQ, seq)
    bk = min(SB_BK, bq)
    nq, nd = seq // bq, bq // bk
    scale = SB_HD ** -0.5
    qb, kb_, vb_ = OFF_Q // 128, OFF_K // 128, OFF_V // 128

    def body(q_ref, k_ref, v_ref, o_ref, tot_ref):
        i = pl.program_id(2)
        lane = lax.broadcasted_iota(jnp.int32, (1, 128), 1)
        m0 = lane < SB_HD
        q = q_ref[...].astype(f32) * scale
        qs = (jnp.where(m0, q, 0.0).astype(bf16), jnp.where(m0, 0.0, q).astype(bf16))
        neg_tri = -_tri(bk, "ge")
        t_idx = i * bq + lax.broadcasted_iota(jnp.int32, (bq, 1), 0)

        def block(ks, carry, masked):
            o_acc, c0, c1 = carry
            kblk = k_ref[pl.ds(ks, bk), :].astype(bf16)
            vblk = v_ref[pl.ds(ks, bk), :].astype(bf16)
            vs = (jnp.where(m0, vblk, 0).astype(bf16), jnp.where(m0, 0, vblk).astype(bf16))
            if masked:
                valid = (ks + lax.broadcasted_iota(jnp.int32, (1, bk), 1)) < t_idx
            cs = [c0, c1]
            for h in range(2):
                z = _dot_nt(qs[h], kblk)
                sp = _softplus(z)
                if masked:
                    sp = jnp.where(valid, sp, 0.0)
                tl = _dot2(sp, neg_tri)
                w = jnp.exp(z + tl + cs[h])
                if masked:
                    w = jnp.where(valid, w, 0.0)
                o_acc = o_acc + _dot(w.astype(bf16), vs[h])
                cs[h] = cs[h] + tl[:, 0:1]
            return o_acc, cs[0], cs[1]

        zc = jnp.zeros((bq, 1), f32)
        carry = (jnp.zeros((bq, 128), f32), zc, zc)
        for d in range(nd):
            carry = block(pl.multiple_of((i * nd + nd - 1 - d) * bk, bk), carry, True)
        carry = lax.fori_loop(0, i * nd, lambda n, c: block(pl.multiple_of((i * nd - 1 - n) * bk, bk), c, False), carry)
        o, c0, c1 = carry
        o_ref[...] = o.astype(o_ref.dtype)
        tot_ref[0, 0] = jnp.where(m0, c0, c1)

    return pl.pallas_call(
        body, name=name, grid=(bsz, 8, nq),
        in_specs=[pl.BlockSpec((bq, 128), lambda b, p, i: (b * nq + i, qb + p)),
                  pl.BlockSpec((seq, 128), lambda b, p, i: (b, kb_ + p)),
                  pl.BlockSpec((seq, 128), lambda b, p, i: (b, vb_ + p))],
        out_specs=[pl.BlockSpec((bq, 128), lambda b, p, i: (b * nq + i, p)),
                   pl.BlockSpec((1, 1, bq, 128), lambda b, p, i: (b, p, i, 0))],
        out_shape=[jax.ShapeDtypeStruct((bsz * seq, 1024), bf16), jax.ShapeDtypeStruct((bsz, 8, seq, 128), f32)],
        compiler_params=_cparams(("parallel", "parallel", "parallel")),
    )(pm, pm, pm)


def sb_bwd(name, pm, tot, do, bsz, seq):
    bq = min(SB_BQ, seq)
    bk = min(SB_BK, bq)
    nq, nd = seq // bq, bq // bk
    scale = SB_HD ** -0.5
    qb, kb_, vb_ = OFF_Q // 128, OFF_K // 128, OFF_V // 128

    def body(q_ref, k_ref, v_ref, do_ref, tot_ref, dq_ref, dk_ref, dv_ref, dk_acc, dv_acc):
        i = pl.program_id(2)

        @pl.when(i == 0)
        def _():
            dk_acc[...] = jnp.zeros_like(dk_acc)
            dv_acc[...] = jnp.zeros_like(dv_acc)

        lane = lax.broadcasted_iota(jnp.int32, (1, 128), 1)
        m0 = lane < SB_HD
        ms = (m0, jnp.logical_not(m0))
        q = q_ref[...].astype(f32) * scale
        qpair = q.astype(bf16)
        qs = (jnp.where(m0, q, 0.0).astype(bf16), jnp.where(m0, 0.0, q).astype(bf16))
        dout = do_ref[...].astype(f32)
        dos = (jnp.where(m0, dout, 0.0).astype(bf16), jnp.where(m0, 0.0, dout).astype(bf16))
        tot = tot_ref[0, 0]
        tots = (tot[:, 0:1], tot[:, SB_HD:SB_HD + 1])
        tri_lt = _tri(bk, "lt")
        tri_le = _tri(bk, "le")
        t_idx = i * bq + lax.broadcasted_iota(jnp.int32, (bq, 1), 0)

        def block(ks, carry, masked):
            dq_acc, p0, p1, g0, g1 = carry
            kblk = k_ref[pl.ds(ks, bk), :].astype(bf16)
            vblk = v_ref[pl.ds(ks, bk), :].astype(bf16)
            if masked:
                valid = (ks + lax.broadcasted_iota(jnp.int32, (1, bk), 1)) < t_idx
            ps, gs = [p0, p1], [g0, g1]
            dk_blk = jnp.zeros((bk, 128), f32)
            dv_blk = jnp.zeros((bk, 128), f32)
            for h in range(2):
                z = _dot_nt(qs[h], kblk)
                sp = _softplus(z)
                sig = jnp.exp(z - sp)
                if masked:
                    sp = jnp.where(valid, sp, 0.0)
                w = jnp.exp(z + tots[h] + ps[h] + _dot2(sp, tri_lt))
                if masked:
                    w = jnp.where(valid, w, 0.0)
                g = _dot_nt(dos[h], vblk) * w
                dz = g - sig * (gs[h] + _dot2(g, tri_le))
                if masked:
                    dz = jnp.where(valid, dz, 0.0)
                dz = dz.astype(bf16)
                dq_acc = dq_acc + jnp.where(ms[h], _dot(dz, kblk), 0.0)
                dk_blk = dk_blk + jnp.where(ms[h], _dot_tn(dz, qpair), 0.0)
                dv_blk = dv_blk + _dot_tn(w.astype(bf16), dos[h])
                ps[h] = ps[h] + jnp.sum(sp, axis=1, keepdims=True)
                gs[h] = gs[h] + jnp.sum(g, axis=1, keepdims=True)
            dk_acc[pl.ds(ks, bk), :] += dk_blk
            dv_acc[pl.ds(ks, bk), :] += dv_blk
            return dq_acc, ps[0], ps[1], gs[0], gs[1]

        zc = jnp.zeros((bq, 1), f32)
        carry = (jnp.zeros((bq, 128), f32), zc, zc, zc, zc)
        carry = lax.fori_loop(0, i * nd, lambda j, c: block(pl.multiple_of(j * bk, bk), c, False), carry)
        for d in range(nd):
            carry = block(pl.multiple_of((i * nd + d) * bk, bk), carry, True)
        dq_ref[...] = (carry[0] * scale).astype(dq_ref.dtype)

        @pl.when(i == nq - 1)
        def _():
            dk_ref[...] = dk_acc[...].astype(dk_ref.dtype)
            dv_ref[...] = dv_acc[...].astype(dv_ref.dtype)

    return pl.pallas_call(
        body, name=name, grid=(bsz, 8, nq),
        in_specs=[pl.BlockSpec((bq, 128), lambda b, p, i: (b * nq + i, qb + p)),
                  pl.BlockSpec((seq, 128), lambda b, p, i: (b, kb_ + p)),
                  pl.BlockSpec((seq, 128), lambda b, p, i: (b, vb_ + p)),
                  pl.BlockSpec((bq, 128), lambda b, p, i: (b * nq + i, p)),
                  pl.BlockSpec((1, 1, bq, 128), lambda b, p, i: (b, p, i, 0))],
        out_specs=[pl.BlockSpec((bq, 128), lambda b, p, i: (b * nq + i, p)),
                   pl.BlockSpec((seq, 128), lambda b, p, i: (b, p)),
                   pl.BlockSpec((seq, 128), lambda b, p, i: (b, p))],
        out_shape=[jax.ShapeDtypeStruct((bsz * seq, 1024), bf16)] * 3,
        scratch_shapes=[pltpu.VMEM((seq, 128), f32), pltpu.VMEM((seq, 128), f32)],
        compiler_params=_cparams(("parallel", "parallel", "arbitrary")),
    )(pm, pm, pm, do, tot)


CONV_CB = 256


def _shift_down(x, d, rows):
    return x if d == 0 else jnp.where(rows >= d, pltpu.roll(x, d, axis=0), 0.0)


def _shift_up(x, d, rows, n):
    return x if d == 0 else jnp.where(rows < n - d, pltpu.roll(x, n - d, axis=0), 0.0)


def conv_fwd(name, pm, w, b, bsz, seq):
    nc = CONV_DIM // CONV_CB
    off = OFF_XBC // CONV_CB

    def body(x_ref, w_ref, b_ref, o_ref):
        x = x_ref[...].astype(f32)
        rows = lax.broadcasted_iota(jnp.int32, x.shape, 0)
        pre = b_ref[...] + jnp.zeros_like(x)
        for k in range(SSM_CONV):
            pre = pre + w_ref[k:k + 1, :] * _shift_down(x, SSM_CONV - 1 - k, rows)
        o_ref[...] = _silu(pre)

    return pl.pallas_call(
        body, name=name, grid=(nc, bsz),
        in_specs=[pl.BlockSpec((seq, CONV_CB), lambda j, bb: (bb, off + j)),
                  pl.BlockSpec((SSM_CONV, CONV_CB), lambda j, bb: (0, j)),
                  pl.BlockSpec((1, CONV_CB), lambda j, bb: (0, j))],
        out_specs=pl.BlockSpec((seq, CONV_CB), lambda j, bb: (bb, j)),
        out_shape=jax.ShapeDtypeStruct((bsz * seq, CONV_DIM), f32),
        compiler_params=_cparams(("parallel", "parallel")),
    )(pm, w, b)


def conv_bwd(name, pm, w, b, dact, dskipx, bsz, seq):
    nc = CONV_DIM // CONV_CB
    off = OFF_XBC // CONV_CB
    nxs = SSM_INNER // CONV_CB

    def body(x_ref, w_ref, b_ref, da_ref, ds_ref, dx_ref, dw_ref, db_ref):
        j, bb = pl.program_id(0), pl.program_id(1)
        x = x_ref[...].astype(f32)
        rows = lax.broadcasted_iota(jnp.int32, x.shape, 0)
        xsh = [_shift_down(x, SSM_CONV - 1 - k, rows) for k in range(SSM_CONV)]
        pre = b_ref[...] + jnp.zeros_like(x)
        for k in range(SSM_CONV):
            pre = pre + w_ref[k:k + 1, :] * xsh[k]
        sig = jax.nn.sigmoid(pre)
        dout = da_ref[...] + jnp.where(j < nxs, ds_ref[...], 0.0)
        dpre = dout * (sig * (1.0 + pre * (1.0 - sig)))
        dx = jnp.zeros_like(x)
        for k in range(SSM_CONV):
            dx = dx + w_ref[k:k + 1, :] * _shift_up(dpre, SSM_CONV - 1 - k, rows, seq)
        dx_ref[...] = dx.astype(dx_ref.dtype)
        dw = jnp.concatenate([_colsum(dpre * xsh[k]) for k in range(SSM_CONV)], axis=0)
        db = _colsum(dpre)

        @pl.when(bb == 0)
        def _():
            dw_ref[...] = dw
            db_ref[...] = db

        @pl.when(bb > 0)
        def _():
            dw_ref[...] += dw
            db_ref[...] += db

    return pl.pallas_call(
        body, name=name, grid=(nc, bsz),
        in_specs=[pl.BlockSpec((seq, CONV_CB), lambda j, bb: (bb, off + j)),
                  pl.BlockSpec((SSM_CONV, CONV_CB), lambda j, bb: (0, j)),
                  pl.BlockSpec((1, CONV_CB), lambda j, bb: (0, j)),
                  pl.BlockSpec((seq, CONV_CB), lambda j, bb: (bb, j)),
                  pl.BlockSpec((seq, CONV_CB), lambda j, bb: (bb, jnp.minimum(j, nxs - 1)))],
        out_specs=[pl.BlockSpec((seq, CONV_CB), lambda j, bb: (bb, j)),
                   pl.BlockSpec((SSM_CONV, CONV_CB), lambda j, bb: (0, j)),
                   pl.BlockSpec((1, CONV_CB), lambda j, bb: (0, j))],
        out_shape=[jax.ShapeDtypeStruct((bsz * seq, CONV_DIM), bf16), jax.ShapeDtypeStruct((SSM_CONV, CONV_DIM), f32),
                   jax.ShapeDtypeStruct((1, CONV_DIM), f32)],
        compiler_params=_cparams(("parallel", "arbitrary")),
    )(pm, w, b, dact, dskipx)


CL = SSM_CHUNK


def _dot3(a, b, split_a):
    x = a if split_a else b
    t1 = x.astype(bf16)
    r1 = x - t1.astype(f32)
    t2 = r1.astype(bf16)
    t3 = (r1 - t2.astype(f32)).astype(bf16)
    if split_a:
        return _dot(t1, b) + _dot(t2, b) + _dot(t3, b)
    return _dot(a, t1) + _dot(a, t2) + _dot(a, t3)


def _ssd_specs(bsz, seq, rev):
    nch = seq // CL

    def ch(c):
        return (nch - 1 - c) if rev else c

    xg = pl.BlockSpec((CL, GW), lambda b, g, c: (b * nch + ch(c), g))
    lane128 = pl.BlockSpec((CL, 128), lambda b, g, c: (b * nch + ch(c), g))
    adt_t = pl.BlockSpec((128, CL), lambda b, g, c: (g, b * nch + ch(c)))
    bspec = pl.BlockSpec((CL, 128), lambda b, g, c: (b * nch + ch(c), SSM_INNER // 128 + g))
    cspec = pl.BlockSpec((CL, 128), lambda b, g, c: (b * nch + ch(c), SSM_INNER // 128 + SSM_GROUPS + g))
    st = pl.BlockSpec((1, 1, 1, SSM_STATE, GW), lambda b, g, c: (b, ch(c), g, 0, 0))
    return nch, xg, lane128, adt_t, bspec, cspec, st


def _expand_mat(width):
    r = lax.broadcasted_iota(jnp.int32, (128, HPG * width), 0)
    c = lax.broadcasted_iota(jnp.int32, (128, HPG * width), 1)
    return jnp.where((c >= r * width) & (c < (r + 1) * width), 1.0, 0.0).astype(bf16)


def _head_sums(z, e):
    hi = z.astype(bf16)
    lo = (z - hi.astype(f32)).astype(bf16)
    return _dot_nt(hi, e) + _dot_nt(lo, e)


def _ssd_common(dt_ref, adt_ref, adtt_ref):
    e64, e128 = _expand_mat(SSM_HD), _expand_mat(CL)
    csc = _dot3(_tri(CL, "ge"), adt_ref[...], False)
    csr = _dot3(adtt_ref[0:HPG, :], _tri(CL, "le"), True)
    return e64, csc, csr, _dot3(dt_ref[...], e64, True), _dot3(csc, e64, True), _dot3(csc, e128, True)


def ssd_fwd(name, xbc, dt, adt, adt_t, bsz, seq):
    nch, xg, lane128, adt_t_spec, bspec, cspec, st = _ssd_specs(bsz, seq, False)

    def body(x_ref, dt_ref, adt_ref, adtt_ref, b_ref, c_ref, y_ref, st_ref, s_scr, xd_scr):
        @pl.when(pl.program_id(2) == 0)
        def _():
            s_scr[...] = jnp.zeros_like(s_scr)

        _, _, csr, dt_e, cs_e, cs_b = _ssd_common(dt_ref, adt_ref, adtt_ref)
        cs_last = cs_e[CL - 1:CL, :]
        bm, cm = b_ref[...].astype(bf16), c_ref[...].astype(bf16)
        s_in = s_scr[...]
        st_ref[0, 0, 0] = s_in
        xd = x_ref[...] * dt_e
        xd_scr[...] = xd.astype(bf16)
        y_ref[...] = _dot(cm, s_in.astype(bf16)) * jnp.exp(cs_e)
        w = xd * jnp.exp(cs_last - cs_e)
        s_scr[...] = s_in * jnp.exp(cs_last) + _dot_tn(bm, w.astype(bf16))
        cb = _dot_nt(cm, bm)
        row = lax.broadcasted_iota(jnp.int32, (CL, CL), 0)
        col = lax.broadcasted_iota(jnp.int32, (CL, CL), 1)
        for h in range(HPG):
            hs = slice(h * SSM_HD, (h + 1) * SSM_HD)
            decay = jnp.exp(jnp.where(row >= col, cs_b[:, h * CL:(h + 1) * CL] - csr[h:h + 1, :], -1e30))
            y_ref[:, hs] += _dot((cb * decay).astype(bf16), xd_scr[:, hs])

    return pl.pallas_call(
        body, name=name, grid=(bsz, SSM_GROUPS, nch),
        in_specs=[xg, lane128, lane128, adt_t_spec, bspec, cspec],
        out_specs=[xg, st],
        out_shape=[jax.ShapeDtypeStruct((bsz * seq, SSM_INNER), f32),
                   jax.ShapeDtypeStruct((bsz, nch, SSM_GROUPS, SSM_STATE, GW), f32)],
        scratch_shapes=[pltpu.VMEM((SSM_STATE, GW), f32), pltpu.VMEM((CL, GW), bf16)],
        compiler_params=_cparams(("parallel", "parallel", "arbitrary")),
    )(xbc, dt, adt, adt_t, xbc, xbc)


def ssd_bwd(name, xbc, dt, adt, adt_t, states, dy, bsz, seq):
    nch, xg, lane128, adt_t_spec, bspec, cspec, st = _ssd_specs(bsz, seq, True)

    def body(x_ref, dt_ref, adt_ref, adtt_ref, b_ref, c_ref, st_ref, dy_ref,
             dx_ref, db_ref, dc_ref, ddt_ref, dac_ref, dar_ref, ds_scr, xd_scr, dxd_scr):
        @pl.when(pl.program_id(2) == 0)
        def _():
            ds_scr[...] = jnp.zeros_like(ds_scr)

        e64, _, csr, dt_e, cs_e, cs_b = _ssd_common(dt_ref, adt_ref, adtt_ref)
        cs_last = cs_e[CL - 1:CL, :]
        bm, cm = b_ref[...].astype(bf16), c_ref[...].astype(bf16)
        x, dy, s_in, ds_out = x_ref[...], dy_ref[...], st_ref[0, 0, 0], ds_scr[...]
        e_last = jnp.exp(cs_last)
        d_end = jnp.exp(cs_last - cs_e)
        xd = x * dt_e
        xd_scr[...] = xd.astype(bf16)
        w = xd * d_end
        dq = dy * jnp.exp(cs_e)
        dc = _dot_nt(dq.astype(bf16), s_in.astype(bf16))
        ds_scr[...] = _dot_tn(cm, dq.astype(bf16)) + ds_out * e_last
        dw = _dot(bm, ds_out.astype(bf16))
        db = _dot_nt(w.astype(bf16), ds_out.astype(bf16))
        rw = dw * w
        dcs_e = dq * _dot(cm, s_in.astype(bf16)) - rw
        dcs_last = _colsum(rw) + _colsum(ds_out * s_in) * e_last
        is_last = lax.broadcasted_iota(jnp.int32, (CL, 1), 0) == CL - 1
        dcs_e = dcs_e + jnp.where(is_last, dcs_last, 0.0)
        dxd_scr[...] = dw * d_end
        cb, cbt = _dot_nt(cm, bm), _dot_nt(bm, cm)
        row = lax.broadcasted_iota(jnp.int32, (CL, CL), 0)
        col = lax.broadcasted_iota(jnp.int32, (CL, CL), 1)
        lane = lax.broadcasted_iota(jnp.int32, (CL, 128), 1)
        sub = lax.broadcasted_iota(jnp.int32, (HPG, CL), 0)
        dcb = jnp.zeros((CL, CL), f32)
        r_rows = jnp.zeros((CL, 128), f32)
        r_cols = jnp.zeros((HPG, CL), f32)
        for h in range(HPG):
            hs = slice(h * SSM_HD, (h + 1) * SSM_HD)
            diff = cs_b[:, h * CL:(h + 1) * CL] - csr[h:h + 1, :]
            decay = jnp.exp(jnp.where(row >= col, diff, -1e30))
            decay_t = jnp.exp(jnp.where(col >= row, -diff, -1e30))
            dy_h = dy_ref[:, hs].astype(bf16)
            dm = _dot_nt(dy_h, xd_scr[:, hs])
            dxd_scr[:, hs] += _dot((cbt * decay_t).astype(bf16), dy_h)
            r = dm * (cb * decay)
            dcb = dcb + dm * decay
            r_rows = r_rows + _dot2(r, jnp.where(lane == h, 1.0, 0.0).astype(bf16))
            r_cols = jnp.where(sub == h, _colsum(r), r_cols)
        dc_ref[...] = dc + _dot(dcb.astype(bf16), bm)
        db_ref[...] = db + _dot_tn(dcb.astype(bf16), cm)
        dxd = dxd_scr[...]
        dx_ref[...] = dxd * dt_e
        ddt_ref[...] = _head_sums(dxd * x, e64)
        dac_ref[...] = _dot3(_tri(CL, "le"), r_rows + _head_sums(dcs_e, e64), False)
        dar_ref[...] = jnp.zeros_like(dar_ref)
        dar_ref[0:HPG, :] = _dot3(-r_cols, _tri(CL, "ge"), True)

    t = bsz * seq
    return pl.pallas_call(
        body, name=name, grid=(bsz, SSM_GROUPS, nch),
        in_specs=[xg, lane128, lane128, adt_t_spec, bspec, cspec, st, xg],
        out_specs=[xg, lane128, lane128, lane128, lane128, adt_t_spec],
        out_shape=[jax.ShapeDtypeStruct((t, SSM_INNER), f32), jax.ShapeDtypeStruct((t, DT_W), f32),
                   jax.ShapeDtypeStruct((t, DT_W), f32), jax.ShapeDtypeStruct((t, DT_W), f32),
                   jax.ShapeDtypeStruct((t, DT_W), f32), jax.ShapeDtypeStruct((DT_W, t), f32)],
        scratch_shapes=[pltpu.VMEM((SSM_STATE, GW), f32), pltpu.VMEM((CL, GW), bf16), pltpu.VMEM((CL, GW), f32)],
        compiler_params=_cparams(("parallel", "parallel", "arbitrary")),
    )(xbc, dt, adt, adt_t, xbc, xbc, states, dy)


XA_BQ = 512


def _xattn(q, k, v):
    s = _dot_nt(q.astype(bf16), k.astype(bf16)) * (XA_HD ** -0.5)
    p = jnp.exp(s - jnp.max(s, axis=-1, keepdims=True))
    p = p / jnp.sum(p, axis=-1, keepdims=True)
    return _dot(p.astype(bf16), v.astype(bf16))


def xattn_fwd(name, q, kv, bsz, seq, mlen):
    bq = min(XA_BQ, seq)
    nq = seq // bq

    def body(q_ref, k_ref, v_ref, o_ref):
        o_ref[...] = _xattn(q_ref[...], k_ref[...], v_ref[...]).astype(o_ref.dtype)

    return pl.pallas_call(
        body, name=name, grid=(bsz, XA_HEADS, nq),
        in_specs=[pl.BlockSpec((bq, XA_HD), lambda b, h, i: (b * nq + i, h)),
                  pl.BlockSpec((mlen, XA_HD), lambda b, h, i: (b, h)),
                  pl.BlockSpec((mlen, XA_HD), lambda b, h, i: (b, XA_HEADS + h))],
        out_specs=pl.BlockSpec((bq, XA_HD), lambda b, h, i: (b * nq + i, h)),
        out_shape=jax.ShapeDtypeStruct((bsz * seq, D), bf16),
        compiler_params=_cparams(("parallel", "parallel", "parallel")),
    )(q, kv, kv)


def xattn_bwd(name, q, kv, do, bsz, seq, mlen):
    bq = min(XA_BQ, seq)
    nq = seq // bq

    def body(q_ref, k_ref, v_ref, do_ref, dq_ref, dk_ref, dv_ref):
        _, vjp = jax.vjp(_xattn, q_ref[...], k_ref[...], v_ref[...])
        dq, dk, dv = vjp(do_ref[...])
        dq_ref[...] = dq.astype(dq_ref.dtype)
        i = pl.program_id(2)

        @pl.when(i == 0)
        def _():
            dk_ref[...] = dk
            dv_ref[...] = dv

        @pl.when(i > 0)
        def _():
            dk_ref[...] += dk
            dv_ref[...] += dv

    kspec = pl.BlockSpec((mlen, XA_HD), lambda b, h, i: (b, h))
    vspec = pl.BlockSpec((mlen, XA_HD), lambda b, h, i: (b, XA_HEADS + h))
    qspec = pl.BlockSpec((bq, XA_HD), lambda b, h, i: (b * nq + i, h))
    return pl.pallas_call(
        body, name=name, grid=(bsz, XA_HEADS, nq),
        in_specs=[qspec, kspec, vspec, qspec],
        out_specs=[qspec, kspec, kspec],
        out_shape=[jax.ShapeDtypeStruct((bsz * seq, D), bf16), jax.ShapeDtypeStruct((bsz * mlen, D), f32),
                   jax.ShapeDtypeStruct((bsz * mlen, D), f32)],
        compiler_params=_cparams(("parallel", "parallel", "arbitrary")),
    )(q, kv, kv, do)


def _layer_fwd(l, x, mem, w, bsz, seq, mlen):
    n = f"l{l}_"
    sv = {"x0": x}
    sv["h1"] = h1 = rms_fwd(n + "rms_mix", x, w["g_pre_mix"])
    sv["pm"] = pm = _mm(n + "in_proj", h1, w["wm"], "nn", bf16)
    sv["pdt"] = pdt = _mm(n + "in_proj_dt", h1, w["wdt"], "nn")
    sv["o_att"], sv["tot"] = o_att, _ = sb_fwd(n + "sb_fwd", pm, bsz, seq)
    sv["xbc"] = xbc = conv_fwd(n + "conv_fwd", pm, w["conv_w"], w["conv_b"], bsz, seq)
    sv["dt"], sv["adt"] = dt, adt = dt_fwd(n + "dt_fwd", pdt, w["dt_bias"], w["a_log"])
    sv["adt_t"] = adt_t = adt.T
    sv["y_ssd"], sv["states"] = y_ssd, _ = ssd_fwd(n + "ssd_fwd", xbc, dt, adt, adt_t, bsz, seq)
    sv["o_ssm"] = o_ssm = gnorm_fwd(n + "gnorm_fwd", y_ssd, xbc, pm, w["d_skip"], w["g_ssm_norm"])
    sv["a"] = a = _mm(n + "br_att", o_att, w["w_br_att"], "nn")
    sv["s"] = s = _mm(n + "br_ssm", o_ssm, w["w_br_ssm"], "nn")
    sv["merged"] = merged = merge_fwd(n + "merge_fwd", pm, a, s)
    sv["u"] = u = _mm(n + "mix_out", merged, w["w_mix_out"], "nn")
    sv["x1"] = x1 = addnorm_fwd(n + "post_mix", x, u, w["g_post_mix"])
    sv["h2"] = h2 = rms_fwd(n + "rms_xa", x1, w["g_pre_xa"])
    sv["memn"] = memn = rms_fwd(n + "rms_mem", mem, w["g_mem"])
    sv["qx"] = qx = _mm(n + "xq", h2, w["w_xq"], "nn")
    sv["kv"] = kv = _mm(n + "xkv", memn, w["w_xkv"], "nn")
    sv["ox"] = ox = xattn_fwd(n + "xattn_fwd", qx, kv, bsz, seq, mlen)
    sv["yx"] = yx = _mm(n + "xo", ox, w["w_xo"], "nn")
    sv["x2"] = x2 = addnorm_fwd(n + "post_xa", x1, yx, w["g_post_xa"])
    sv["h3"] = h3 = rms_fwd(n + "rms_ffn", x2, w["g_pre_ffn"])
    sv["gu"] = gu = _mm(n + "gu", h3, w["w_gu"], "nn", bf16)
    sv["act"] = act = swiglu_fwd(n + "swiglu_fwd", gu)
    sv["d"] = d = _mm(n + "down", act, w["w_down"], "nn")
    x3 = addnorm_fwd(n + "post_ffn", x2, d, w["g_post_ffn"])
    return x3, sv


def _layer_bwd(l, dx, mem, w, sv, bsz, seq, mlen):
    n = f"l{l}_b_"
    g = {}
    dd, g["g_post_ffn"] = addnorm_bwd(n + "post_ffn", sv["d"], w["g_post_ffn"], dx)
    g["w_down"] = _mm(n + "dw_down", sv["act"], dd, "tn", bf16)
    dact = _mm(n + "dact", dd, w["w_down"], "nt", bf16)
    dgu = swiglu_bwd(n + "swiglu", sv["gu"], dact)
    g["w_gu"] = _mm(n + "dw_gu", sv["h3"], dgu, "tn", bf16)
    dh3 = _mm(n + "dh3", dgu, w["w_gu"], "nt")
    dx, g["g_pre_ffn"] = rms_bwd(n + "rms_ffn", sv["x2"], w["g_pre_ffn"], [dh3], dx)
    dyx, g["g_post_xa"] = addnorm_bwd(n + "post_xa", sv["yx"], w["g_post_xa"], dx)
    g["w_xo"] = _mm(n + "dw_xo", sv["ox"], dyx, "tn", bf16)
    dox = _mm(n + "dox", dyx, w["w_xo"], "nt")
    dqx, dk, dv = xattn_bwd(n + "xattn", sv["qx"], sv["kv"], dox, bsz, seq, mlen)
    g["w_xq"] = _mm(n + "dw_xq", sv["h2"], dqx, "tn", bf16)
    dh2 = _mm(n + "dh2", dqx, w["w_xq"], "nt")
    dkv = jnp.concatenate([dk, dv], axis=1)
    g["w_xkv"] = _mm(n + "dw_xkv", sv["memn"], dkv, "tn", bf16)
    dmemn = _mm(n + "dmemn", dkv, w["w_xkv"], "nt")
    _, g["g_mem"] = rms_bwd(n + "rms_mem", mem, w["g_mem"], [dmemn])
    dx, g["g_pre_xa"] = rms_bwd(n + "rms_xa", sv["x1"], w["g_pre_xa"], [dh2], dx)
    du, g["g_post_mix"] = addnorm_bwd(n + "post_mix", sv["u"], w["g_post_mix"], dx)
    g["w_mix_out"] = _mm(n + "dw_mix", sv["merged"], du, "tn", bf16)
    dmerged = _mm(n + "dmerged", du, w["w_mix_out"], "nt")
    dgates, da, ds = merge_bwd(n + "merge", sv["pm"], sv["a"], sv["s"], dmerged)
    g["w_br_att"] = _mm(n + "dw_att", sv["o_att"], da, "tn", bf16)
    do_att = _mm(n + "do_att", da, w["w_br_att"], "nt")
    g["w_br_ssm"] = _mm(n + "dw_ssm", sv["o_ssm"], ds, "tn", bf16)
    do_ssm = _mm(n + "do_ssm", ds, w["w_br_ssm"], "nt")
    dy_ssd, dxs_skip, dz, g["d_skip"], g["g_ssm_norm"] = gnorm_bwd(
        n + "gnorm", sv["y_ssd"], sv["xbc"], sv["pm"], w["d_skip"], w["g_ssm_norm"], do_ssm)
    dxs, dbm, dcm, ddt, dadt_c, dadt_r = ssd_bwd(n + "ssd", sv["xbc"], sv["dt"], sv["adt"], sv["adt_t"],
                                                   sv["states"], dy_ssd, bsz, seq)
    dxbc_act = jnp.concatenate([dxs, dbm, dcm], axis=1)
    dxbc, g["conv_w"], g["conv_b"] = conv_bwd(n + "conv", sv["pm"], w["conv_w"], w["conv_b"], dxbc_act, dxs_skip, bsz, seq)
    dpdt, g["dt_bias"], g["a_log"] = dt_bwd(n + "dt", sv["pdt"], w["dt_bias"], w["a_log"], ddt, dadt_c, dadt_r.T)
    dq, dk_, dv_ = sb_bwd(n + "sb", sv["pm"], sv["tot"], do_att, bsz, seq)
    dpm = jnp.concatenate([dz, dxbc, dq, dk_, dv_, dgates], axis=1)
    g["wm"] = _mm(n + "dw_in", sv["h1"], dpm, "tn", bf16)
    g["wdt"] = _mm(n + "dw_in_dt", sv["h1"], dpdt, "tn", bf16)
    dh1 = _mm(n + "dh1", dpm, w["wm"], "nt")
    dh1_dt = _mm(n + "dh1_dt", dpdt, w["wdt"], "nt")
    dx, g["g_pre_mix"] = rms_bwd(n + "rms_mix", sv["x0"], w["g_pre_mix"], [dh1, dh1_dt], dx)
    return dx, g


def _group_pad(v):
    lead = v.shape[:-1]
    v = v.reshape(*lead, SSM_GROUPS, HPG)
    return jnp.pad(v, [(0, 0)] * (len(lead) + 1) + [(0, 128 - HPG)]).reshape(*lead, DT_W)


def _group_unpad(v):
    lead = v.shape[:-1]
    return v.reshape(*lead, SSM_GROUPS, 128)[..., :HPG].reshape(*lead, SSM_HEADS)


BIG = ("w_in", "w_br_att", "w_br_ssm", "w_mix_out", "w_xq", "w_xkv", "w_xo", "w_gu", "w_down")
GAINS = ("g_pre_mix", "g_post_mix", "g_pre_xa", "g_mem", "g_post_xa", "g_pre_ffn", "g_post_ffn")
HEAD_VECS = ("dt_bias", "a_log", "d_skip")
SMALL = GAINS + ("conv_w", "conv_b", "g_ssm_norm") + HEAD_VECS


def _prep_layer(p):
    w = {k: p[k] for k in BIG[1:]}
    if "wm" in p:
        w["wm"], w["wdt"] = p["wm"], p["wdt"]
    else:
        w_in = p["w_in"]
        w["wm"] = jnp.concatenate([w_in[:, 3072:8192], w_in[:, 0:3072], w_in[:, 8224:10272]], axis=1)
        w["wdt"] = _group_pad(w_in[:, 8192:8224])
    for k in GAINS + ("conv_b", "g_ssm_norm"):
        w[k] = p[k].reshape(1, -1)
    w["conv_w"] = p["conv_w"]
    w["dt_bias"] = _group_pad(p["dt_bias"]).reshape(1, DT_W)
    w["a_log"] = _group_pad(p["a_log"]).reshape(1, DT_W)
    w["d_skip"] = jnp.repeat(p["d_skip"], SSM_HD).reshape(1, SSM_INNER)
    return w


def _unprep_grads(g):
    out = {k: g[k] for k in BIG[1:]}
    gm = g["wm"]
    out["w_in"] = jnp.concatenate([gm[:, 5120:8192], gm[:, 0:5120], _group_unpad(g["wdt"]), gm[:, 8192:10240]], axis=1)
    for k in GAINS + ("conv_b", "g_ssm_norm"):
        out[k] = g[k].reshape(-1)
    out["conv_w"] = g["conv_w"]
    out["dt_bias"] = _group_unpad(g["dt_bias"]).reshape(-1)
    out["a_log"] = _group_unpad(g["a_log"]).reshape(-1)
    out["d_skip"] = g["d_skip"].reshape(SSM_HEADS, SSM_HD).sum(axis=1)
    return out


def _local_step(x, mem, target, ws, bsz, seq, mlen):
    saved = []
    for l in range(len(ws)):
        x, sv = _layer_fwd(l, x, mem, ws[l], bsz, seq, mlen)
        saved.append(sv)
    dx, loss_lanes = loss_fwd_bwd("loss", x, target)
    grads = [None] * len(ws)
    for l in reversed(range(len(ws))):
        dx, grads[l] = _layer_bwd(l, dx, mem, ws[l], saved[l], bsz, seq, mlen)
    return loss_lanes, dx, grads


HBM_SPEC = pl.BlockSpec(memory_space=pltpu.HBM)
FLIP_C = (0, 0, 1)
FLIPS_CHIP = ((1, 0, 0), (0, 1, 0), (1, 1, 0))
FLIPS_ALL = tuple(((f >> 2) & 1, (f >> 1) & 1, f & 1) for f in range(1, 8))


def _view(ref, index):
    return ref.at[index] if index != () else ref


def _exchange(name, srcs, out_shapes, transfers, in_place=False):
    na = len(srcs)
    nr = sum(1 for t in transfers if any(t[0]))
    nl = len(transfers) - nr

    def body(*refs):
        out_refs = refs[na:2 * na]
        src_refs = out_refs if in_place else refs[:na]
        send_sems, recv_sems, local_sems = refs[2 * na:]
        pos = (lax.axis_index("x"), lax.axis_index("y"), lax.axis_index("c"))
        copies = []
        for a in range(na):
            ri = li = 0
            for flip, src_index, dst_index in transfers:
                src_view, dst_view = _view(src_refs[a], src_index(*pos)), _view(out_refs[a], dst_index(*pos))
                if any(flip):
                    peer = tuple(1 - p if f else p for p, f in zip(pos, flip))
                    cp = pltpu.make_async_remote_copy(
                        src_ref=src_view, dst_ref=dst_view, send_sem=send_sems.at[a * nr + ri],
                        recv_sem=recv_sems.at[a * nr + ri], device_id=peer, device_id_type=MESH)
                    ri += 1
                else:
                    cp = pltpu.make_async_copy(src_view, dst_view, local_sems.at[a * nl + li])
                    li += 1
                cp.start()
                copies.append(cp)
        for cp in copies:
            cp.wait()

    if in_place:
        out_shape = [jax.ShapeDtypeStruct(s.shape, s.dtype) for s in srcs]
    else:
        out_shape = [jax.ShapeDtypeStruct(shape, dtype) for shape, dtype in out_shapes]
    return pl.pallas_call(
        body, name=name, out_shape=out_shape, in_specs=[HBM_SPEC] * na, out_specs=[HBM_SPEC] * na,
        input_output_aliases={a: a for a in range(na)} if in_place else {},
        scratch_shapes=[pltpu.SemaphoreType.DMA((max(na * nr, 1),)), pltpu.SemaphoreType.DMA((max(na * nr, 1),)),
                        pltpu.SemaphoreType.DMA((max(na * nl, 1),))],
    )(*srcs)


def _at(*index):
    return lambda x, y, c: index


def _allgather8(name, v, me):
    got = _exchange(name, [v], [((7,) + v.shape, v.dtype)], [(fl, _at(), _at(j)) for j, fl in enumerate(FLIPS_ALL)])[0]
    rel = jnp.concatenate([v[None], got], axis=0)
    return jnp.stack([lax.dynamic_index_in_dim(rel, k ^ me, 0, keepdims=False) for k in range(8)])


def _sum8(name, parts):
    def fn(*p):
        acc = p[0]
        for q in p[1:]:
            acc = acc + q
        return acc

    return _rowwise(name, fn, [(parts[k], 0) for k in range(8)], [], [(1, f32)], [], width=128, bt=parts.shape[1])[0]


def _rows_block(r, w, bytes_per_row_elem):
    for bt in (512, 256, 128, 64, 32, 16, 8):
        if r % bt == 0 and bt * w * bytes_per_row_elem * 2 <= 16 * 1024 * 1024:
            return bt
    raise ValueError((r, w))


def _reduce_pair(name, g, recv, ci):
    _, m, w = g.shape
    bt = _rows_block(m, w, 2 + 2 + 4 + 2)

    def body(c_ref, a_ref, b_ref, o32_ref, o16_ref):
        acc = a_ref[0].astype(f32) + b_ref[...].astype(f32)
        o32_ref[...] = acc
        o16_ref[...] = acc.astype(bf16)

    return pl.pallas_call(
        body, name=name,
        grid_spec=pltpu.PrefetchScalarGridSpec(
            num_scalar_prefetch=1, grid=(m // bt,),
            in_specs=[pl.BlockSpec((1, bt, w), lambda i, c_ref: (c_ref[0], i, 0)),
                      pl.BlockSpec((bt, w), lambda i, c_ref: (i, 0))],
            out_specs=[pl.BlockSpec((bt, w), lambda i, c_ref: (i, 0))] * 2),
        out_shape=[jax.ShapeDtypeStruct((m, w), f32), jax.ShapeDtypeStruct((m, w), bf16)],
        compiler_params=_cparams(("parallel",)),
    )(ci.reshape(1).astype(jnp.int32), g, recv)


def _reduce_chips(name, p32, recv, shard):
    _, _, r, w = p32.shape
    bt = _rows_block(r, w, 4 + 3 * 2 + 4)

    def body(s_ref, a_ref, b_ref, o_ref):
        acc = a_ref[0, 0]
        for j in range(3):
            acc = acc + b_ref[j, 0].astype(f32)
        o_ref[0] = acc

    return pl.pallas_call(
        body, name=name,
        grid_spec=pltpu.PrefetchScalarGridSpec(
            num_scalar_prefetch=1, grid=(2, r // bt),
            in_specs=[pl.BlockSpec((1, 1, bt, w), lambda j, i, s_ref: (j, s_ref[0], i, 0)),
                      pl.BlockSpec((3, 1, bt, w), lambda j, i, s_ref: (0, j, i, 0))],
            out_specs=pl.BlockSpec((1, bt, w), lambda j, i, s_ref: (j, i, 0))),
        out_shape=jax.ShapeDtypeStruct((2, r, w), f32),
        compiler_params=_cparams(("parallel", "parallel")),
    )(shard.reshape(1).astype(jnp.int32), p32, recv)


COL_SHARDED = ("w_in", "w_xkv", "w_gu")


def _ref_cols(pieces, lo, hi):
    c, out = pieces[0].shape[1], []
    for s, p in enumerate(pieces):
        a0, a1 = max(lo, s * c), min(hi, (s + 1) * c)
        if a0 < a1:
            out.append(p[:, a0 - s * c:a1 - s * c])
    return out


def _my_cols(gm, g32, lo, hi):
    out = []
    for r0, r1, src, shift in ((0, 3072, gm, 5120), (3072, 8192, gm, -3072), (8192, 8224, g32, -8192), (8224, IN_WIDTH, gm, -32)):
        a0, a1 = max(lo, r0), min(hi, r1)
        if a0 < a1:
            out.append(src[:, a0 + shift:a1 + shift])
    return out


def _pack(arrs, rows_multiple=8):
    flat = jnp.concatenate([a.reshape(-1) for a in arrs])
    pad = (-flat.shape[0]) % (128 * rows_multiple)
    return jnp.pad(flat, (0, pad)).reshape(-1, 128)


def _unpack(buf, shapes):
    flat, out, o = buf.reshape(-1), [], 0
    for s in shapes:
        n = math.prod(s)
        out.append(flat[o:o + n].reshape(s))
        o += n
    return out


def kernel(x, mem, g_pre_mix, w_in, conv_w, conv_b, dt_bias, a_log, d_skip, g_ssm_norm, w_br_att, w_br_ssm, w_mix_out, g_post_mix, g_pre_xa, g_mem, w_xq, w_xkv, w_xo, g_post_xa, g_pre_ffn, w_gu, w_down, g_post_ffn, loss_target, m_g_pre_mix, m_w_in, m_conv_w, m_conv_b, m_dt_bias, m_a_log, m_d_skip, m_g_ssm_norm, m_w_br_att, m_w_br_ssm, m_w_mix_out, m_g_post_mix, m_g_pre_xa, m_g_mem, m_w_xq, m_w_xkv, m_w_xo, m_g_post_xa, m_g_pre_ffn, m_w_gu, m_w_down, m_g_post_ffn, v_g_pre_mix, v_w_in, v_conv_w, v_conv_b, v_dt_bias, v_a_log, v_d_skip, v_g_ssm_norm, v_w_br_att, v_w_br_ssm, v_w_mix_out, v_g_post_mix, v_g_pre_xa, v_g_mem, v_w_xq, v_w_xkv, v_w_xo, v_g_post_xa, v_g_pre_ffn, v_w_gu, v_w_down, v_g_post_ffn):
    a = dict(locals())
    names = ("g_pre_mix", "w_in", "conv_w", "conv_b", "dt_bias", "a_log", "d_skip", "g_ssm_norm", "w_br_att", "w_br_ssm",
             "w_mix_out", "g_post_mix", "g_pre_xa", "g_mem", "w_xq", "w_xkv", "w_xo", "g_post_xa", "g_pre_ffn", "w_gu",
             "w_down", "g_post_ffn")
    depth = w_in.shape[0]
    bsz, seq, _ = x.shape
    mlen = mem.shape[1]
    xi, yi, ci = lax.axis_index("x"), lax.axis_index("y"), lax.axis_index("c")
    shard = 2 * xi + yi
    me = 2 * shard + ci

    assert depth == 4
    by_shard = lambda x_, y_, c_: (slice(None), slice(None), 2 * x_ + y_)
    srcs = [a[k].astype(bf16).reshape(2, 2, *a[k].shape[1:]) for k in BIG]
    gathered = _exchange(
        "ag_ici", srcs, [((2, 2, 4) + s.shape[2:], bf16) for s in srcs],
        [(fl, lambda x_, y_, c_: (c_,), lambda x_, y_, c_: (c_, slice(None), 2 * x_ + y_)) for fl in FLIPS_CHIP]
        + [((0, 0, 0), _at(), by_shard)])
    fetched = [functools.partial(lambda x_, y_, c_, f: (c_, slice(None), (2 * x_ + y_) ^ f), f=2 * fl[0] + fl[1])
               for fl in FLIPS_CHIP]
    gathered = _exchange("ag_d2d", gathered, None, [(FLIP_C, fn, fn) for fn in fetched], in_place=True)
    full = {k: g.reshape(4, 4, *g.shape[3:]) for k, g in zip(BIG, gathered)}

    cw_all = _allgather8("ag_conv_w", _pack([conv_w]), me)
    cw_shape = conv_w.shape
    conv_w_full = jnp.concatenate([_unpack(cw_all[2 * s], [cw_shape])[0] for s in range(4)], axis=2)

    ws = []
    for l in range(depth):
        p = {}
        for k in BIG[1:]:
            sh = full[k][l]
            p[k] = sh.transpose(1, 0, 2).reshape(sh.shape[1], -1) if k in COL_SHARDED else sh.reshape(-1, sh.shape[2])
        pieces = [full["w_in"][l, s] for s in range(4)]
        p["wm"] = jnp.concatenate(_ref_cols(pieces, 3072, 8192) + _ref_cols(pieces, 0, 3072)
                                  + _ref_cols(pieces, 8224, 10272), axis=1)
        p["wdt"] = _group_pad(jnp.concatenate(_ref_cols(pieces, 8192, 8224), axis=1))
        for k in SMALL:
            p[k] = conv_w_full[l] if k == "conv_w" else a[k][l]
        ws.append(_prep_layer(p))

    loss_lanes, gx, grads = _local_step(x.reshape(bsz * seq, D), mem.reshape(bsz * mlen, D),
                                        loss_target.reshape(bsz * seq, D), ws, bsz, seq, mlen)
    wires = []
    for k in BIG:
        r, c = a[k].shape[1:]
        per_layer = []
        for l in range(depth):
            g = grads[l]
            if k == "w_in":
                g32 = _group_unpad(g["wdt"])
                per_layer.append(jnp.stack([jnp.concatenate(_my_cols(g["wm"], g32, s * c, (s + 1) * c), axis=1)
                                            for s in range(4)]))
            elif k in COL_SHARDED:
                per_layer.append(g[k].reshape(r, 4, c).transpose(1, 0, 2))
            else:
                per_layer.append(g[k].reshape(4, r, c))
        wires.append(jnp.stack(per_layer).astype(bf16).reshape(2, 2 * 4 * r, c))
    from_sib = _exchange("rs_d2d", wires, [(w.shape[1:], bf16) for w in wires], [(FLIP_C, lambda x_, y_, c_: (1 - c_,), _at())])
    sums = [_reduce_pair("rs_pair_" + k, w, got, ci) for k, w, got in zip(BIG, wires, from_sib)]
    by_shard4 = [(2, 4) + a[k].shape[1:] for k in BIG]
    from_chips = _exchange(
        "rs_ici", [s[1].reshape(shp) for s, shp in zip(sums, by_shard4)], [((3, 2) + a[k].shape[1:], bf16) for k in BIG],
        [(fl, functools.partial(lambda x_, y_, c_, f: (slice(None), (2 * x_ + y_) ^ f), f=2 * fl[0] + fl[1]), _at(j))
         for j, fl in enumerate(FLIPS_CHIP)])
    red = [_reduce_chips("rs_chips_" + k, s[0].reshape(shp), got, shard) for k, s, shp, got in zip(BIG, sums, by_shard4, from_chips)]
    to_half = lambda x_, y_, c_: (c_,)
    gshards = _exchange("rs_swap", red, [((2, 2) + a[k].shape[1:], f32) for k in BIG],
                        [(FLIP_C, _at(), to_half), ((0, 0, 0), _at(), to_half)])
    grads = [_unprep_grads(g) for g in grads]

    out_g, out_d, out_m, out_v = {}, {}, {}, {}
    for k, g in zip(BIG, gshards):
        shp = a[k].shape
        g = g.reshape(shp)
        two_d = (shp[0] * shp[1], shp[2])
        d_, m_, v_ = adamw("adamw_" + k, a[k].reshape(two_d), g.reshape(two_d), a["m_" + k].reshape(two_d), a["v_" + k].reshape(two_d))
        out_g[k], out_d[k], out_m[k], out_v[k] = g, d_.reshape(shp), m_.reshape(shp), v_.reshape(shp)

    small_shapes = [(depth,) + (conv_w_full.shape[1:] if k == "conv_w" else a[k].shape[1:]) for k in SMALL]
    small = _pack([jnp.stack([grads[l][k] for l in range(depth)]) for k in SMALL] + [loss_lanes])
    total = _sum8("small_sum", _allgather8("ag_small", small, me))
    *gsmall, loss_l = _unpack(total, small_shapes + [loss_lanes.shape])
    gsmall = dict(zip(SMALL, gsmall))
    gsmall["conv_w"] = lax.dynamic_slice_in_dim(gsmall["conv_w"], shard * cw_shape[2], cw_shape[2], axis=2)
    loc_shapes = [a[k].shape for k in SMALL]
    d_, m_, v_ = adamw("adamw_small", _pack([a[k] for k in SMALL]), _pack([gsmall[k] for k in SMALL]),
                       _pack([a["m_" + k] for k in SMALL]), _pack([a["v_" + k] for k in SMALL]))
    for k, dd, mm, vv in zip(SMALL, _unpack(d_, loc_shapes), _unpack(m_, loc_shapes), _unpack(v_, loc_shapes)):
        out_g[k], out_d[k], out_m[k], out_v[k] = gsmall[k], dd, mm, vv

    loss = jnp.sum(loss_l)
    return (loss, gx.reshape(x.shape), *[out_g[k] for k in names], *[out_d[k] for k in names],
            *[out_m[k] for k in names], *[out_v[k] for k in names])
```

```python
import functools
import math

import jax
import jax.numpy as jnp
from jax import lax
from jax.experimental import pallas as pl
from jax.experimental.pallas import tpu as pltpu

f32, bf16 = jnp.float32, jnp.bfloat16

DEPTH = 4
D = 1024
SB_HEADS, SB_HD = 16, 64
SSM_INNER, SSM_HD, SSM_HEADS, SSM_GROUPS, SSM_STATE, SSM_CONV, SSM_CHUNK = 2048, 64, 32, 4, 128, 4, 128
HPG = SSM_HEADS // SSM_GROUPS
CONV_DIM = SSM_INNER + 2 * SSM_GROUPS * SSM_STATE
XA_HEADS, XA_HD = 4, 256
FFN = 2816
IN_WIDTH = 10272
RMS_EPS = 1e-6
LR, B1, B2, EPS, WD, STEP = 0.001, 0.9, 0.999, 1e-08, 0.01, 10

PM_W = 10240
OFF_Z, OFF_XBC, OFF_Q, OFF_K, OFF_V, OFF_GA, OFF_GS = 0, 2048, 5120, 6144, 7168, 8192, 9216
DT_W = SSM_GROUPS * 128

VMEM_LIMIT = 48 * 1024 * 1024
MESH = pl.DeviceIdType.MESH


def _cparams(sem):
    return pltpu.CompilerParams(dimension_semantics=sem, vmem_limit_bytes=VMEM_LIMIT)


def _tile(n):
    for t in (512, 256, 128):
        if n % t == 0:
            return t
    raise ValueError(f"dimension {n} is not a multiple of 128")


MM_VMEM_BUDGET = 34 * 1024 * 1024


def _mm_tiles(m, n, k, sa, sb, so):
    best = None
    for tm in (2048, 1024, 512, 256, 128):
        if m % tm:
            continue
        for tn in (2048, 1024, 512, 256, 128):
            if n % tn:
                continue
            for tk in (k, 2048, 1024, 512):
                if tk > k or k % tk:
                    continue
                vmem = 2 * (tm * tk * sa + tk * tn * sb + tm * tn * so) + tm * tn * 4 * (2 if tk < k else 1)
                if vmem > MM_VMEM_BUDGET:
                    continue
                traffic = m * k * sa * (n // tn) + k * n * sb * (m // tm) + m * n * so
                steps = (m // tm) * (n // tn) * (k // tk)
                accumulate = (k // tk > 1) * (k // tk) * m * n * 2
                key = (traffic + steps * 800_000 + accumulate, steps)
                if best is None or key < best[0]:
                    best = (key, (tm, tn, tk))
    assert best is not None, (m, n, k)
    return best[1]


def _mm(name, a, b, mode, out_dtype=f32):
    if mode == "nn":
        (m, k), (k2, n) = a.shape, b.shape
    elif mode == "nt":
        (m, k), (n, k2) = a.shape, b.shape
    else:
        (k, m), (k2, n) = a.shape, b.shape
    assert k == k2, (name, a.shape, b.shape, mode)
    tm, tn, tk = _mm_tiles(m, n, k, a.dtype.itemsize, b.dtype.itemsize, jnp.dtype(out_dtype).itemsize)
    nk = k // tk
    dn = {"nn": (((1,), (0,)), ((), ())), "nt": (((1,), (1,)), ((), ())), "tn": (((0,), (0,)), ((), ()))}[mode]

    def product(a_ref, b_ref):
        return lax.dot_general(a_ref[...].astype(bf16), b_ref[...].astype(bf16), dn, preferred_element_type=f32)

    def body_whole_k(a_ref, b_ref, o_ref):
        o_ref[...] = product(a_ref, b_ref).astype(o_ref.dtype)

    def body_k_loop(a_ref, b_ref, o_ref, acc_ref):
        kk = pl.program_id(2)

        @pl.when(kk == 0)
        def _():
            acc_ref[...] = product(a_ref, b_ref)

        @pl.when(kk > 0)
        def _():
            acc_ref[...] += product(a_ref, b_ref)

        @pl.when(kk == nk - 1)
        def _():
            o_ref[...] = acc_ref[...].astype(o_ref.dtype)

    a_spec = pl.BlockSpec((tk, tm), lambda i, j, kk: (kk, i)) if mode == "tn" else pl.BlockSpec((tm, tk), lambda i, j, kk: (i, kk))
    b_spec = pl.BlockSpec((tn, tk), lambda i, j, kk: (j, kk)) if mode == "nt" else pl.BlockSpec((tk, tn), lambda i, j, kk: (kk, j))
    return pl.pallas_call(
        body_whole_k if nk == 1 else body_k_loop, name=name, grid=(m // tm, n // tn, nk),
        in_specs=[a_spec, b_spec],
        out_specs=pl.BlockSpec((tm, tn), lambda i, j, kk: (i, j)),
        out_shape=jax.ShapeDtypeStruct((m, n), out_dtype),
        scratch_shapes=[] if nk == 1 else [pltpu.VMEM((tm, tn), f32)],
        compiler_params=_cparams(("parallel", "parallel", "arbitrary")),
    )(a, b)


def _rowwise(name, fn, rows, consts, out_rows, out_accs, *, width, ncol=1, bt=256):
    r = rows[0][0].shape[0]
    bt = min(bt, r)
    assert r % bt == 0, (name, r, bt)
    nrow = r // bt
    n_in = len(rows) + len(consts)
    n_or = len(out_rows)

    def body(*refs):
        ins = [ref[...].astype(f32) for ref in refs[:n_in]]
        outs = fn(*ins)
        if not isinstance(outs, (tuple, list)):
            outs = (outs,)
        o_refs = refs[n_in:]
        for o_ref, val in zip(o_refs[:n_or], outs[:n_or]):
            o_ref[...] = val.astype(o_ref.dtype)
        if out_accs:
            i = pl.program_id(1)
            for o_ref, val in zip(o_refs[n_or:], outs[n_or:]):
                @pl.when(i == 0)
                def _(o_ref=o_ref, val=val):
                    o_ref[...] = val

                @pl.when(i > 0)
                def _(o_ref=o_ref, val=val):
                    o_ref[...] += val

    in_specs = [pl.BlockSpec((bt, width), functools.partial(lambda j, i, off: (i, off + j), off=off)) for _, off in rows]
    in_specs += [pl.BlockSpec((c.shape[0], width), functools.partial(lambda j, i, off: (0, off + j), off=off)) for c, off in consts]
    out_specs = [pl.BlockSpec((bt, mlt * width), lambda j, i: (i, j)) for mlt, _ in out_rows]
    out_specs += [pl.BlockSpec((k, width), lambda j, i: (0, j)) for k in out_accs]
    out_shape = [jax.ShapeDtypeStruct((r, ncol * mlt * width), dt) for mlt, dt in out_rows]
    out_shape += [jax.ShapeDtypeStruct((k, ncol * width), f32) for k in out_accs]
    res = pl.pallas_call(
        body, name=name, grid=(ncol, nrow), in_specs=in_specs, out_specs=out_specs, out_shape=out_shape,
        compiler_params=_cparams(("parallel", "arbitrary" if out_accs else "parallel")),
    )(*[a for a, _ in rows], *[c for c, _ in consts])
    return res


def _rms(x, g):
    return x * lax.rsqrt(jnp.mean(x * x, axis=-1, keepdims=True) + RMS_EPS) * g


def _silu(x):
    return x * jax.nn.sigmoid(x)


def _softplus(x):
    return jnp.maximum(x, 0.0) + jnp.log(1.0 + jnp.exp(-jnp.abs(x)))


def _colsum(x):
    return jnp.sum(x, axis=0, keepdims=True)


def rms_fwd(name, x, g):
    return _rowwise(name, _rms, [(x, 0)], [(g, 0)], [(1, bf16)], [], width=D)[0]


def rms_bwd(name, x, g, dhs, dres=None):
    nd = len(dhs)

    def fn(x, *rest):
        dh = rest[0]
        for extra in rest[1:nd]:
            dh = dh + extra
        g = rest[-1]
        _, vjp = jax.vjp(_rms, x, g)
        dx, dg = vjp(dh.astype(f32))
        if dres is not None:
            dx = dx + rest[nd]
        return dx, dg

    rows = [(x, 0)] + [(d, 0) for d in dhs] + ([(dres, 0)] if dres is not None else [])
    return _rowwise(name, fn, rows, [(g, 0)], [(1, f32)], [1], width=D)


def addnorm_fwd(name, x, u, g):
    return _rowwise(name, lambda x, u, g: x + _rms(u, g), [(x, 0), (u, 0)], [(g, 0)], [(1, f32)], [], width=D)[0]


def addnorm_bwd(name, u, g, dx):
    def fn(u, dx, g):
        _, vjp = jax.vjp(_rms, u, g)
        return vjp(dx)

    return _rowwise(name, fn, [(u, 0), (dx, 0)], [(g, 0)], [(1, bf16)], [1], width=D)


def _merge(ga, gs, a, s):
    return jax.nn.sigmoid(ga) * a + jax.nn.sigmoid(gs) * s


def merge_fwd(name, pm, a, s):
    return _rowwise(name, _merge, [(pm, OFF_GA // D), (pm, OFF_GS // D), (a, 0), (s, 0)], [], [(1, bf16)], [], width=D)[0]


def merge_bwd(name, pm, a, s, dm):
    def fn(ga, gs, a, s, dm):
        _, vjp = jax.vjp(_merge, ga, gs, a, s)
        dga, dgs, da, ds = vjp(dm)
        return jnp.concatenate([dga, dgs], axis=1), da, ds

    return _rowwise(name, fn, [(pm, OFF_GA // D), (pm, OFF_GS // D), (a, 0), (s, 0), (dm, 0)], [],
                    [(2, bf16), (1, bf16), (1, bf16)], [], width=D)


def _swiglu(gate, up):
    return _silu(gate) * up


def swiglu_fwd(name, gu):
    return _rowwise(name, _swiglu, [(gu, 0), (gu, 1)], [], [(1, bf16)], [], width=FFN)[0]


def swiglu_bwd(name, gu, dact):
    def fn(gate, up, dact):
        _, vjp = jax.vjp(_swiglu, gate, up)
        dg, du = vjp(dact.astype(f32))
        return jnp.concatenate([dg, du], axis=1)

    return _rowwise(name, fn, [(gu, 0), (gu, 1), (dact, 0)], [], [(2, bf16)], [], width=FFN, bt=128)[0]


GW = SSM_INNER // SSM_GROUPS


def _gnorm(y, xs, z, dskip, gn):
    yy = (y + dskip * xs) * _silu(z)
    return yy * lax.rsqrt(jnp.mean(yy * yy, axis=-1, keepdims=True) + RMS_EPS) * gn


def gnorm_fwd(name, y, xbc, pm, dskip, gn):
    return _rowwise(name, _gnorm, [(y, 0), (xbc, 0), (pm, OFF_Z // GW)], [(dskip, 0), (gn, 0)], [(1, bf16)], [],
                    width=GW, ncol=SSM_GROUPS)[0]


def gnorm_bwd(name, y, xbc, pm, dskip, gn, do):
    def fn(y, xs, z, do, dskip, gn):
        _, vjp = jax.vjp(_gnorm, y, xs, z, dskip, gn)
        return vjp(do.astype(f32))

    return _rowwise(name, fn, [(y, 0), (xbc, 0), (pm, OFF_Z // GW), (do, 0)], [(dskip, 0), (gn, 0)],
                    [(1, f32), (1, f32), (1, bf16)], [1, 1], width=GW, ncol=SSM_GROUPS)


def _dtfn(pdt, bias, alog):
    dt = _softplus(pdt + bias)
    return dt, -jnp.exp(alog) * dt


def dt_fwd(name, pdt, bias, alog):
    return _rowwise(name, _dtfn, [(pdt, 0)], [(bias, 0), (alog, 0)], [(1, f32), (1, f32)], [], width=DT_W)


def dt_bwd(name, pdt, bias, alog, ddt, dadt_c, dadt_r):
    def fn(pdt, ddt, dac, dar, bias, alog):
        _, vjp = jax.vjp(_dtfn, pdt, bias, alog)
        return vjp((ddt, dac + dar))

    return _rowwise(name, fn, [(pdt, 0), (ddt, 0), (dadt_c, 0), (dadt_r, 0)], [(bias, 0), (alog, 0)],
                    [(1, f32)], [1, 1], width=DT_W)


def loss_fwd_bwd(name, y, target):
    def fn(y, t):
        e = y - t
        return e * (1.0 / D), _colsum(e * e) * (0.5 / D)

    return _rowwise(name, fn, [(y, 0), (target, 0)], [], [(1, f32)], [1], width=D)


def adamw(name, w, g, m, v):
    r, c = w.shape

    def fn(w, g, m, v):
        m = B1 * m + (1.0 - B1) * g
        v = B2 * v + (1.0 - B2) * (g * g)
        m_hat = m / (1.0 - B1 ** STEP)
        v_hat = v / (1.0 - B2 ** STEP)
        return -LR * (m_hat / (jnp.sqrt(v_hat) + EPS) + WD * w), m, v

    bt = 256
    while bt > 8 and (r % bt or bt * c * 4 * 7 * 2 > 16 * 1024 * 1024):
        bt //= 2
    if r % bt:
        bt = r
    return _rowwise(name, fn, [(w, 0), (g, 0), (m, 0), (v, 0)], [], [(1, f32)] * 3, [], width=c, bt=bt)


SB_BQ, SB_BK = 512, 256
SB_UNROLL = 2


def _dot(a, b):
    return jnp.dot(a, b, preferred_element_type=f32)


def _dot_nt(a, b):
    return lax.dot_general(a, b, (((1,), (1,)), ((), ())), preferred_element_type=f32)


def _dot_tn(a, b):
    return lax.dot_general(a, b, (((0,), (0,)), ((), ())), preferred_element_type=f32)


def _dot2(x, tri):
    hi = x.astype(bf16)
    lo = (x - hi.astype(f32)).astype(bf16)
    return _dot(hi, tri) + _dot(lo, tri)


def _tri(n, rel):
    r = lax.broadcasted_iota(jnp.int32, (n, n), 0)
    c = lax.broadcasted_iota(jnp.int32, (n, n), 1)
    m = {"ge": r >= c, "lt": r < c, "le": r <= c}[rel]
    return jnp.where(m, 1.0, 0.0).astype(bf16)


def sb_fwd(name, pm, bsz, seq):
    bq = min(SB_BQ, seq)
    bk = min(SB_BK, bq)
    nq, nd = seq // bq, bq // bk
    step = SB_UNROLL if nd % SB_UNROLL == 0 else 1
    scale = SB_HD ** -0.5
    qb, kb_, vb_ = OFF_Q // 128, OFF_K // 128, OFF_V // 128

    def body(q_ref, k_ref, v_ref, o_ref, tot_ref):
        i = pl.program_id(2)
        lane = lax.broadcasted_iota(jnp.int32, (1, 128), 1)
        m0 = lane < SB_HD
        q = q_ref[...].astype(f32) * scale
        qs = (jnp.where(m0, q, 0.0).astype(bf16), jnp.where(m0, 0.0, q).astype(bf16))
        neg_tri = -_tri(bk, "ge")
        t_idx = i * bq + lax.broadcasted_iota(jnp.int32, (bq, 1), 0)

        def block(ks, carry, masked):
            o_acc, c0, c1 = carry
            kblk = k_ref[pl.ds(ks, bk), :].astype(bf16)
            vblk = v_ref[pl.ds(ks, bk), :].astype(bf16)
            vs = (jnp.where(m0, vblk, 0).astype(bf16), jnp.where(m0, 0, vblk).astype(bf16))
            if masked:
                valid = (ks + lax.broadcasted_iota(jnp.int32, (1, bk), 1)) < t_idx
            cs = [c0, c1]
            for h in range(2):
                z = _dot_nt(qs[h], kblk)
                sp = _softplus(z)
                if masked:
                    sp = jnp.where(valid, sp, 0.0)
                tl = _dot2(sp, neg_tri)
                w = jnp.exp(z + tl + cs[h])
                if masked:
                    w = jnp.where(valid, w, 0.0)
                o_acc = o_acc + _dot(w.astype(bf16), vs[h])
                cs[h] = cs[h] + tl[:, 0:1]
            return o_acc, cs[0], cs[1]

        zc = jnp.zeros((bq, 1), f32)
        carry = (jnp.zeros((bq, 128), f32), zc, zc)
        for d in range(nd):
            carry = block(pl.multiple_of((i * nd + nd - 1 - d) * bk, bk), carry, True)
        def far(n, c):
            for u in range(step):
                c = block(pl.multiple_of((i * nd - 1 - (step * n + u)) * bk, bk), c, False)
            return c

        carry = lax.fori_loop(0, i * (nd // step), far, carry)
        o, c0, c1 = carry
        o_ref[...] = o.astype(o_ref.dtype)
        tot_ref[0, 0] = jnp.where(m0, c0, c1)

    return pl.pallas_call(
        body, name=name, grid=(bsz, 8, nq),
        in_specs=[pl.BlockSpec((bq, 128), lambda b, p, i: (b * nq + i, qb + p)),
                  pl.BlockSpec((seq, 128), lambda b, p, i: (b, kb_ + p)),
                  pl.BlockSpec((seq, 128), lambda b, p, i: (b, vb_ + p))],
        out_specs=[pl.BlockSpec((bq, 128), lambda b, p, i: (b * nq + i, p)),
                   pl.BlockSpec((1, 1, bq, 128), lambda b, p, i: (b, p, i, 0))],
        out_shape=[jax.ShapeDtypeStruct((bsz * seq, 1024), bf16), jax.ShapeDtypeStruct((bsz, 8, seq, 128), f32)],
        compiler_params=_cparams(("parallel", "parallel", "parallel")),
    )(pm, pm, pm)


def sb_bwd(name, pm, tot, do, bsz, seq):
    bq = min(SB_BQ, seq)
    bk = min(SB_BK, bq)
    nq, nd = seq // bq, bq // bk
    step = SB_UNROLL if nd % SB_UNROLL == 0 else 1
    scale = SB_HD ** -0.5
    qb, kb_, vb_ = OFF_Q // 128, OFF_K // 128, OFF_V // 128

    def body(q_ref, k_ref, v_ref, do_ref, tot_ref, dq_ref, dk_ref, dv_ref, dk_acc, dv_acc):
        i = pl.program_id(2)

        @pl.when(i == 0)
        def _():
            dk_acc[...] = jnp.zeros_like(dk_acc)
            dv_acc[...] = jnp.zeros_like(dv_acc)

        lane = lax.broadcasted_iota(jnp.int32, (1, 128), 1)
        m0 = lane < SB_HD
        ms = (m0, jnp.logical_not(m0))
        q = q_ref[...].astype(f32) * scale
        qpair = q.astype(bf16)
        qs = (jnp.where(m0, q, 0.0).astype(bf16), jnp.where(m0, 0.0, q).astype(bf16))
        dout = do_ref[...].astype(f32)
        dos = (jnp.where(m0, dout, 0.0).astype(bf16), jnp.where(m0, 0.0, dout).astype(bf16))
        tot = tot_ref[0, 0]
        tots = (tot[:, 0:1], tot[:, SB_HD:SB_HD + 1])
        tri_lt = _tri(bk, "lt")
        tri_le = _tri(bk, "le")
        t_idx = i * bq + lax.broadcasted_iota(jnp.int32, (bq, 1), 0)

        def block(ks, carry, masked):
            dq_acc, p0, p1, g0, g1 = carry
            kblk = k_ref[pl.ds(ks, bk), :].astype(bf16)
            vblk = v_ref[pl.ds(ks, bk), :].astype(bf16)
            if masked:
                valid = (ks + lax.broadcasted_iota(jnp.int32, (1, bk), 1)) < t_idx
            ps, gs = [p0, p1], [g0, g1]
            dk_blk = jnp.zeros((bk, 128), f32)
            dv_blk = jnp.zeros((bk, 128), f32)
            for h in range(2):
                z = _dot_nt(qs[h], kblk)
                sp = _softplus(z)
                sig = jnp.exp(z - sp)
                if masked:
                    sp = jnp.where(valid, sp, 0.0)
                w = jnp.exp(z + tots[h] + ps[h] + _dot2(sp, tri_lt))
                if masked:
                    w = jnp.where(valid, w, 0.0)
                g = _dot_nt(dos[h], vblk) * w
                dz = g - sig * (gs[h] + _dot(g.astype(bf16), tri_le))
                if masked:
                    dz = jnp.where(valid, dz, 0.0)
                dz = dz.astype(bf16)
                dq_acc = dq_acc + jnp.where(ms[h], _dot(dz, kblk), 0.0)
                dk_blk = dk_blk + jnp.where(ms[h], _dot_tn(dz, qpair), 0.0)
                dv_blk = dv_blk + _dot_tn(w.astype(bf16), dos[h])
                ps[h] = ps[h] + jnp.sum(sp, axis=1, keepdims=True)
                gs[h] = gs[h] + jnp.sum(g, axis=1, keepdims=True)
            dk_acc[pl.ds(ks, bk), :] += dk_blk
            dv_acc[pl.ds(ks, bk), :] += dv_blk
            return dq_acc, ps[0], ps[1], gs[0], gs[1]

        zc = jnp.zeros((bq, 1), f32)
        carry = (jnp.zeros((bq, 128), f32), zc, zc, zc, zc)
        def far(n, c):
            for u in range(step):
                c = block(pl.multiple_of((step * n + u) * bk, bk), c, False)
            return c

        carry = lax.fori_loop(0, i * (nd // step), far, carry)
        for d in range(nd):
            carry = block(pl.multiple_of((i * nd + d) * bk, bk), carry, True)
        dq_ref[...] = (carry[0] * scale).astype(dq_ref.dtype)

        @pl.when(i == nq - 1)
        def _():
            dk_ref[...] = dk_acc[...].astype(dk_ref.dtype)
            dv_ref[...] = dv_acc[...].astype(dv_ref.dtype)

    return pl.pallas_call(
        body, name=name, grid=(bsz, 8, nq),
        in_specs=[pl.BlockSpec((bq, 128), lambda b, p, i: (b * nq + i, qb + p)),
                  pl.BlockSpec((seq, 128), lambda b, p, i: (b, kb_ + p)),
                  pl.BlockSpec((seq, 128), lambda b, p, i: (b, vb_ + p)),
                  pl.BlockSpec((bq, 128), lambda b, p, i: (b * nq + i, p)),
                  pl.BlockSpec((1, 1, bq, 128), lambda b, p, i: (b, p, i, 0))],
        out_specs=[pl.BlockSpec((bq, 128), lambda b, p, i: (b * nq + i, p)),
                   pl.BlockSpec((seq, 128), lambda b, p, i: (b, p)),
                   pl.BlockSpec((seq, 128), lambda b, p, i: (b, p))],
        out_shape=[jax.ShapeDtypeStruct((bsz * seq, 1024), bf16)] * 3,
        scratch_shapes=[pltpu.VMEM((seq, 128), f32), pltpu.VMEM((seq, 128), f32)],
        compiler_params=_cparams(("parallel", "parallel", "arbitrary")),
    )(pm, pm, pm, do, tot)


CONV_CB = 256


def _shift_down(x, d, rows):
    return x if d == 0 else jnp.where(rows >= d, pltpu.roll(x, d, axis=0), 0.0)


def _shift_up(x, d, rows, n):
    return x if d == 0 else jnp.where(rows < n - d, pltpu.roll(x, n - d, axis=0), 0.0)


def conv_fwd(name, pm, w, b, bsz, seq):
    nc = CONV_DIM // CONV_CB
    off = OFF_XBC // CONV_CB

    def body(x_ref, w_ref, b_ref, o_ref):
        x = x_ref[...].astype(f32)
        rows = lax.broadcasted_iota(jnp.int32, x.shape, 0)
        pre = b_ref[...] + jnp.zeros_like(x)
        for k in range(SSM_CONV):
            pre = pre + w_ref[k:k + 1, :] * _shift_down(x, SSM_CONV - 1 - k, rows)
        o_ref[...] = _silu(pre)

    return pl.pallas_call(
        body, name=name, grid=(nc, bsz),
        in_specs=[pl.BlockSpec((seq, CONV_CB), lambda j, bb: (bb, off + j)),
                  pl.BlockSpec((SSM_CONV, CONV_CB), lambda j, bb: (0, j)),
                  pl.BlockSpec((1, CONV_CB), lambda j, bb: (0, j))],
        out_specs=pl.BlockSpec((seq, CONV_CB), lambda j, bb: (bb, j)),
        out_shape=jax.ShapeDtypeStruct((bsz * seq, CONV_DIM), f32),
        compiler_params=_cparams(("parallel", "parallel")),
    )(pm, w, b)


def conv_bwd(name, pm, w, b, dact, dskipx, bsz, seq):
    nc = CONV_DIM // CONV_CB
    off = OFF_XBC // CONV_CB
    nxs = SSM_INNER // CONV_CB

    def body(x_ref, w_ref, b_ref, da_ref, ds_ref, dx_ref, dw_ref, db_ref):
        j, bb = pl.program_id(0), pl.program_id(1)
        x = x_ref[...].astype(f32)
        rows = lax.broadcasted_iota(jnp.int32, x.shape, 0)
        xsh = [_shift_down(x, SSM_CONV - 1 - k, rows) for k in range(SSM_CONV)]
        pre = b_ref[...] + jnp.zeros_like(x)
        for k in range(SSM_CONV):
            pre = pre + w_ref[k:k + 1, :] * xsh[k]
        sig = jax.nn.sigmoid(pre)
        dout = da_ref[...] + jnp.where(j < nxs, ds_ref[...], 0.0)
        dpre = dout * (sig * (1.0 + pre * (1.0 - sig)))
        dx = jnp.zeros_like(x)
        for k in range(SSM_CONV):
            dx = dx + w_ref[k:k + 1, :] * _shift_up(dpre, SSM_CONV - 1 - k, rows, seq)
        dx_ref[...] = dx.astype(dx_ref.dtype)
        dw = jnp.concatenate([_colsum(dpre * xsh[k]) for k in range(SSM_CONV)], axis=0)
        db = _colsum(dpre)

        @pl.when(bb == 0)
        def _():
            dw_ref[...] = dw
            db_ref[...] = db

        @pl.when(bb > 0)
        def _():
            dw_ref[...] += dw
            db_ref[...] += db

    return pl.pallas_call(
        body, name=name, grid=(nc, bsz),
        in_specs=[pl.BlockSpec((seq, CONV_CB), lambda j, bb: (bb, off + j)),
                  pl.BlockSpec((SSM_CONV, CONV_CB), lambda j, bb: (0, j)),
                  pl.BlockSpec((1, CONV_CB), lambda j, bb: (0, j)),
                  pl.BlockSpec((seq, CONV_CB), lambda j, bb: (bb, j)),
                  pl.BlockSpec((seq, CONV_CB), lambda j, bb: (bb, jnp.minimum(j, nxs - 1)))],
        out_specs=[pl.BlockSpec((seq, CONV_CB), lambda j, bb: (bb, j)),
                   pl.BlockSpec((SSM_CONV, CONV_CB), lambda j, bb: (0, j)),
                   pl.BlockSpec((1, CONV_CB), lambda j, bb: (0, j))],
        out_shape=[jax.ShapeDtypeStruct((bsz * seq, CONV_DIM), bf16), jax.ShapeDtypeStruct((SSM_CONV, CONV_DIM), f32),
                   jax.ShapeDtypeStruct((1, CONV_DIM), f32)],
        compiler_params=_cparams(("parallel", "arbitrary")),
    )(pm, w, b, dact, dskipx)


CL = SSM_CHUNK


def _dot3(a, b, split_a):
    x = a if split_a else b
    t1 = x.astype(bf16)
    r1 = x - t1.astype(f32)
    t2 = r1.astype(bf16)
    t3 = (r1 - t2.astype(f32)).astype(bf16)
    if split_a:
        return _dot(t1, b) + _dot(t2, b) + _dot(t3, b)
    return _dot(a, t1) + _dot(a, t2) + _dot(a, t3)


def _ssd_specs(bsz, seq, rev):
    nch = seq // CL

    def ch(c):
        return (nch - 1 - c) if rev else c

    xg = pl.BlockSpec((CL, GW), lambda b, g, c: (b * nch + ch(c), g))
    lane128 = pl.BlockSpec((CL, 128), lambda b, g, c: (b * nch + ch(c), g))
    adt_t = pl.BlockSpec((128, CL), lambda b, g, c: (g, b * nch + ch(c)))
    bspec = pl.BlockSpec((CL, 128), lambda b, g, c: (b * nch + ch(c), SSM_INNER // 128 + g))
    cspec = pl.BlockSpec((CL, 128), lambda b, g, c: (b * nch + ch(c), SSM_INNER // 128 + SSM_GROUPS + g))
    st = pl.BlockSpec((1, 1, 1, SSM_STATE, GW), lambda b, g, c: (b, ch(c), g, 0, 0))
    return nch, xg, lane128, adt_t, bspec, cspec, st


def _expand_mat(width):
    r = lax.broadcasted_iota(jnp.int32, (128, HPG * width), 0)
    c = lax.broadcasted_iota(jnp.int32, (128, HPG * width), 1)
    return jnp.where((c >= r * width) & (c < (r + 1) * width), 1.0, 0.0).astype(bf16)


def _head_sums(z, e):
    hi = z.astype(bf16)
    lo = (z - hi.astype(f32)).astype(bf16)
    return _dot_nt(hi, e) + _dot_nt(lo, e)


def _ssd_common(dt_ref, adt_ref, adtt_ref):
    e64, e128 = _expand_mat(SSM_HD), _expand_mat(CL)
    csc = _dot3(_tri(CL, "ge"), adt_ref[...], False)
    csr = _dot3(adtt_ref[0:HPG, :], _tri(CL, "le"), True)
    return e64, csc, csr, _dot3(dt_ref[...], e64, True), _dot3(csc, e64, True), _dot3(csc, e128, True)


def ssd_fwd(name, xbc, dt, adt, adt_t, bsz, seq):
    nch, xg, lane128, adt_t_spec, bspec, cspec, st = _ssd_specs(bsz, seq, False)

    def body(x_ref, dt_ref, adt_ref, adtt_ref, b_ref, c_ref, y_ref, st_ref, s_scr, xd_scr):
        @pl.when(pl.program_id(2) == 0)
        def _():
            s_scr[...] = jnp.zeros_like(s_scr)

        _, _, csr, dt_e, cs_e, cs_b = _ssd_common(dt_ref, adt_ref, adtt_ref)
        cs_last = cs_e[CL - 1:CL, :]
        bm, cm = b_ref[...].astype(bf16), c_ref[...].astype(bf16)
        s_in = s_scr[...]
        st_ref[0, 0, 0] = s_in
        xd = x_ref[...] * dt_e
        xd_scr[...] = xd.astype(bf16)
        y_ref[...] = _dot(cm, s_in.astype(bf16)) * jnp.exp(cs_e)
        w = xd * jnp.exp(cs_last - cs_e)
        s_scr[...] = s_in * jnp.exp(cs_last) + _dot_tn(bm, w.astype(bf16))
        cb = _dot_nt(cm, bm)
        row = lax.broadcasted_iota(jnp.int32, (CL, CL), 0)
        col = lax.broadcasted_iota(jnp.int32, (CL, CL), 1)
        for h in range(HPG):
            hs = slice(h * SSM_HD, (h + 1) * SSM_HD)
            decay = jnp.exp(jnp.where(row >= col, cs_b[:, h * CL:(h + 1) * CL] - csr[h:h + 1, :], -1e30))
            y_ref[:, hs] += _dot((cb * decay).astype(bf16), xd_scr[:, hs])

    return pl.pallas_call(
        body, name=name, grid=(bsz, SSM_GROUPS, nch),
        in_specs=[xg, lane128, lane128, adt_t_spec, bspec, cspec],
        out_specs=[xg, st],
        out_shape=[jax.ShapeDtypeStruct((bsz * seq, SSM_INNER), f32),
                   jax.ShapeDtypeStruct((bsz, nch, SSM_GROUPS, SSM_STATE, GW), f32)],
        scratch_shapes=[pltpu.VMEM((SSM_STATE, GW), f32), pltpu.VMEM((CL, GW), bf16)],
        compiler_params=_cparams(("parallel", "parallel", "arbitrary")),
    )(xbc, dt, adt, adt_t, xbc, xbc)


def ssd_bwd(name, xbc, dt, adt, adt_t, states, dy, bsz, seq):
    nch, xg, lane128, adt_t_spec, bspec, cspec, st = _ssd_specs(bsz, seq, True)

    def body(x_ref, dt_ref, adt_ref, adtt_ref, b_ref, c_ref, st_ref, dy_ref,
             dx_ref, db_ref, dc_ref, ddt_ref, dac_ref, dar_ref, ds_scr, xd_scr, dxd_scr):
        @pl.when(pl.program_id(2) == 0)
        def _():
            ds_scr[...] = jnp.zeros_like(ds_scr)

        e64, _, csr, dt_e, cs_e, cs_b = _ssd_common(dt_ref, adt_ref, adtt_ref)
        cs_last = cs_e[CL - 1:CL, :]
        bm, cm = b_ref[...].astype(bf16), c_ref[...].astype(bf16)
        x, dy, s_in, ds_out = x_ref[...], dy_ref[...], st_ref[0, 0, 0], ds_scr[...]
        e_last = jnp.exp(cs_last)
        d_end = jnp.exp(cs_last - cs_e)
        xd = x * dt_e
        xd_scr[...] = xd.astype(bf16)
        w = xd * d_end
        dq = dy * jnp.exp(cs_e)
        dc = _dot_nt(dq.astype(bf16), s_in.astype(bf16))
        ds_scr[...] = _dot_tn(cm, dq.astype(bf16)) + ds_out * e_last
        dw = _dot(bm, ds_out.astype(bf16))
        db = _dot_nt(w.astype(bf16), ds_out.astype(bf16))
        rw = dw * w
        dcs_e = dq * _dot(cm, s_in.astype(bf16)) - rw
        dcs_last = _colsum(rw) + _colsum(ds_out * s_in) * e_last
        is_last = lax.broadcasted_iota(jnp.int32, (CL, 1), 0) == CL - 1
        dcs_e = dcs_e + jnp.where(is_last, dcs_last, 0.0)
        dxd_scr[...] = dw * d_end
        cb, cbt = _dot_nt(cm, bm), _dot_nt(bm, cm)
        row = lax.broadcasted_iota(jnp.int32, (CL, CL), 0)
        col = lax.broadcasted_iota(jnp.int32, (CL, CL), 1)
        lane = lax.broadcasted_iota(jnp.int32, (CL, 128), 1)
        sub = lax.broadcasted_iota(jnp.int32, (HPG, CL), 0)
        dcb = jnp.zeros((CL, CL), f32)
        r_rows = jnp.zeros((CL, 128), f32)
        r_cols = jnp.zeros((HPG, CL), f32)
        for h in range(HPG):
            hs = slice(h * SSM_HD, (h + 1) * SSM_HD)
            diff = cs_b[:, h * CL:(h + 1) * CL] - csr[h:h + 1, :]
            decay = jnp.exp(jnp.where(row >= col, diff, -1e30))
            decay_t = jnp.exp(jnp.where(col >= row, -diff, -1e30))
            dy_h = dy_ref[:, hs].astype(bf16)
            dm = _dot_nt(dy_h, xd_scr[:, hs])
            dxd_scr[:, hs] += _dot((cbt * decay_t).astype(bf16), dy_h)
            r = dm * (cb * decay)
            dcb = dcb + dm * decay
            r_rows = r_rows + _dot2(r, jnp.where(lane == h, 1.0, 0.0).astype(bf16))
            r_cols = jnp.where(sub == h, _colsum(r), r_cols)
        dc_ref[...] = dc + _dot(dcb.astype(bf16), bm)
        db_ref[...] = db + _dot_tn(dcb.astype(bf16), cm)
        dxd = dxd_scr[...]
        dx_ref[...] = dxd * dt_e
        ddt_ref[...] = _head_sums(dxd * x, e64)
        dac_ref[...] = _dot3(_tri(CL, "le"), r_rows + _head_sums(dcs_e, e64), False)
        dar_ref[...] = jnp.zeros_like(dar_ref)
        dar_ref[0:HPG, :] = _dot3(-r_cols, _tri(CL, "ge"), True)

    t = bsz * seq
    return pl.pallas_call(
        body, name=name, grid=(bsz, SSM_GROUPS, nch),
        in_specs=[xg, lane128, lane128, adt_t_spec, bspec, cspec, st, xg],
        out_specs=[xg, lane128, lane128, lane128, lane128, adt_t_spec],
        out_shape=[jax.ShapeDtypeStruct((t, SSM_INNER), f32), jax.ShapeDtypeStruct((t, DT_W), f32),
                   jax.ShapeDtypeStruct((t, DT_W), f32), jax.ShapeDtypeStruct((t, DT_W), f32),
                   jax.ShapeDtypeStruct((t, DT_W), f32), jax.ShapeDtypeStruct((DT_W, t), f32)],
        scratch_shapes=[pltpu.VMEM((SSM_STATE, GW), f32), pltpu.VMEM((CL, GW), bf16), pltpu.VMEM((CL, GW), f32)],
        compiler_params=_cparams(("parallel", "parallel", "arbitrary")),
    )(xbc, dt, adt, adt_t, xbc, xbc, states, dy)


XA_BQ = 512


def _xattn(q, k, v):
    s = _dot_nt(q.astype(bf16), k.astype(bf16)) * (XA_HD ** -0.5)
    p = jnp.exp(s - jnp.max(s, axis=-1, keepdims=True))
    p = p / jnp.sum(p, axis=-1, keepdims=True)
    return _dot(p.astype(bf16), v.astype(bf16))


def xattn_fwd(name, q, kv, bsz, seq, mlen):
    bq = min(XA_BQ, seq)
    nq = seq // bq

    def body(q_ref, k_ref, v_ref, o_ref):
        o_ref[...] = _xattn(q_ref[...], k_ref[...], v_ref[...]).astype(o_ref.dtype)

    return pl.pallas_call(
        body, name=name, grid=(bsz, XA_HEADS, nq),
        in_specs=[pl.BlockSpec((bq, XA_HD), lambda b, h, i: (b * nq + i, h)),
                  pl.BlockSpec((mlen, XA_HD), lambda b, h, i: (b, h)),
                  pl.BlockSpec((mlen, XA_HD), lambda b, h, i: (b, XA_HEADS + h))],
        out_specs=pl.BlockSpec((bq, XA_HD), lambda b, h, i: (b * nq + i, h)),
        out_shape=jax.ShapeDtypeStruct((bsz * seq, D), bf16),
        compiler_params=_cparams(("parallel", "parallel", "parallel")),
    )(q, kv, kv)


def xattn_bwd(name, q, kv, do, bsz, seq, mlen):
    bq = min(XA_BQ, seq)
    nq = seq // bq

    def body(q_ref, k_ref, v_ref, do_ref, dq_ref, dk_ref, dv_ref):
        _, vjp = jax.vjp(_xattn, q_ref[...], k_ref[...], v_ref[...])
        dq, dk, dv = vjp(do_ref[...])
        dq_ref[...] = dq.astype(dq_ref.dtype)
        i = pl.program_id(2)

        @pl.when(i == 0)
        def _():
            dk_ref[...] = dk
            dv_ref[...] = dv

        @pl.when(i > 0)
        def _():
            dk_ref[...] += dk
            dv_ref[...] += dv

    kspec = pl.BlockSpec((mlen, XA_HD), lambda b, h, i: (b, h))
    vspec = pl.BlockSpec((mlen, XA_HD), lambda b, h, i: (b, XA_HEADS + h))
    qspec = pl.BlockSpec((bq, XA_HD), lambda b, h, i: (b * nq + i, h))
    return pl.pallas_call(
        body, name=name, grid=(bsz, XA_HEADS, nq),
        in_specs=[qspec, kspec, vspec, qspec],
        out_specs=[qspec, kspec, kspec],
        out_shape=[jax.ShapeDtypeStruct((bsz * seq, D), bf16), jax.ShapeDtypeStruct((bsz * mlen, D), f32),
                   jax.ShapeDtypeStruct((bsz * mlen, D), f32)],
        compiler_params=_cparams(("parallel", "parallel", "arbitrary")),
    )(q, kv, kv, do)


def _layer_fwd(l, x, mem, w, bsz, seq, mlen):
    n = f"l{l}_"
    sv = {"x0": x}
    sv["h1"] = h1 = rms_fwd(n + "rms_mix", x, w["g_pre_mix"])
    sv["pm"] = pm = _mm(n + "in_proj", h1, w["wm"], "nn", bf16)
    sv["pdt"] = pdt = _mm(n + "in_proj_dt", h1, w["wdt"], "nn")
    sv["o_att"], sv["tot"] = o_att, _ = sb_fwd(n + "sb_fwd", pm, bsz, seq)
    sv["xbc"] = xbc = conv_fwd(n + "conv_fwd", pm, w["conv_w"], w["conv_b"], bsz, seq)
    sv["dt"], sv["adt"] = dt, adt = dt_fwd(n + "dt_fwd", pdt, w["dt_bias"], w["a_log"])
    sv["adt_t"] = adt_t = adt.T
    sv["y_ssd"], sv["states"] = y_ssd, _ = ssd_fwd(n + "ssd_fwd", xbc, dt, adt, adt_t, bsz, seq)
    sv["o_ssm"] = o_ssm = gnorm_fwd(n + "gnorm_fwd", y_ssd, xbc, pm, w["d_skip"], w["g_ssm_norm"])
    sv["a"] = a = _mm(n + "br_att", o_att, w["w_br_att"], "nn")
    sv["s"] = s = _mm(n + "br_ssm", o_ssm, w["w_br_ssm"], "nn")
    sv["merged"] = merged = merge_fwd(n + "merge_fwd", pm, a, s)
    sv["u"] = u = _mm(n + "mix_out", merged, w["w_mix_out"], "nn")
    sv["x1"] = x1 = addnorm_fwd(n + "post_mix", x, u, w["g_post_mix"])
    sv["h2"] = h2 = rms_fwd(n + "rms_xa", x1, w["g_pre_xa"])
    sv["memn"] = memn = rms_fwd(n + "rms_mem", mem, w["g_mem"])
    sv["qx"] = qx = _mm(n + "xq", h2, w["w_xq"], "nn")
    sv["kv"] = kv = _mm(n + "xkv", memn, w["w_xkv"], "nn")
    sv["ox"] = ox = xattn_fwd(n + "xattn_fwd", qx, kv, bsz, seq, mlen)
    sv["yx"] = yx = _mm(n + "xo", ox, w["w_xo"], "nn")
    sv["x2"] = x2 = addnorm_fwd(n + "post_xa", x1, yx, w["g_post_xa"])
    sv["h3"] = h3 = rms_fwd(n + "rms_ffn", x2, w["g_pre_ffn"])
    sv["gu"] = gu = _mm(n + "gu", h3, w["w_gu"], "nn", bf16)
    sv["act"] = act = swiglu_fwd(n + "swiglu_fwd", gu)
    sv["d"] = d = _mm(n + "down", act, w["w_down"], "nn")
    x3 = addnorm_fwd(n + "post_ffn", x2, d, w["g_post_ffn"])
    return x3, sv


def _layer_bwd(l, dx, mem, w, sv, bsz, seq, mlen):
    n = f"l{l}_b_"
    g = {}
    dd, g["g_post_ffn"] = addnorm_bwd(n + "post_ffn", sv["d"], w["g_post_ffn"], dx)
    g["w_down"] = _mm(n + "dw_down", sv["act"], dd, "tn", bf16)
    dact = _mm(n + "dact", dd, w["w_down"], "nt", bf16)
    dgu = swiglu_bwd(n + "swiglu", sv["gu"], dact)
    g["w_gu"] = _mm(n + "dw_gu", sv["h3"], dgu, "tn", bf16)
    dh3 = _mm(n + "dh3", dgu, w["w_gu"], "nt")
    dx, g["g_pre_ffn"] = rms_bwd(n + "rms_ffn", sv["x2"], w["g_pre_ffn"], [dh3], dx)
    dyx, g["g_post_xa"] = addnorm_bwd(n + "post_xa", sv["yx"], w["g_post_xa"], dx)
    g["w_xo"] = _mm(n + "dw_xo", sv["ox"], dyx, "tn", bf16)
    dox = _mm(n + "dox", dyx, w["w_xo"], "nt")
    dqx, dk, dv = xattn_bwd(n + "xattn", sv["qx"], sv["kv"], dox, bsz, seq, mlen)
    g["w_xq"] = _mm(n + "dw_xq", sv["h2"], dqx, "tn", bf16)
    dh2 = _mm(n + "dh2", dqx, w["w_xq"], "nt")
    dkv = jnp.concatenate([dk, dv], axis=1)
    g["w_xkv"] = _mm(n + "dw_xkv", sv["memn"], dkv, "tn", bf16)
    dmemn = _mm(n + "dmemn", dkv, w["w_xkv"], "nt")
    _, g["g_mem"] = rms_bwd(n + "rms_mem", mem, w["g_mem"], [dmemn])
    dx, g["g_pre_xa"] = rms_bwd(n + "rms_xa", sv["x1"], w["g_pre_xa"], [dh2], dx)
    du, g["g_post_mix"] = addnorm_bwd(n + "post_mix", sv["u"], w["g_post_mix"], dx)
    g["w_mix_out"] = _mm(n + "dw_mix", sv["merged"], du, "tn", bf16)
    dmerged = _mm(n + "dmerged", du, w["w_mix_out"], "nt")
    dgates, da, ds = merge_bwd(n + "merge", sv["pm"], sv["a"], sv["s"], dmerged)
    g["w_br_att"] = _mm(n + "dw_att", sv["o_att"], da, "tn", bf16)
    do_att = _mm(n + "do_att", da, w["w_br_att"], "nt")
    g["w_br_ssm"] = _mm(n + "dw_ssm", sv["o_ssm"], ds, "tn", bf16)
    do_ssm = _mm(n + "do_ssm", ds, w["w_br_ssm"], "nt")
    dy_ssd, dxs_skip, dz, g["d_skip"], g["g_ssm_norm"] = gnorm_bwd(
        n + "gnorm", sv["y_ssd"], sv["xbc"], sv["pm"], w["d_skip"], w["g_ssm_norm"], do_ssm)
    dxs, dbm, dcm, ddt, dadt_c, dadt_r = ssd_bwd(n + "ssd", sv["xbc"], sv["dt"], sv["adt"], sv["adt_t"],
                                                   sv["states"], dy_ssd, bsz, seq)
    dxbc_act = jnp.concatenate([dxs, dbm, dcm], axis=1)
    dxbc, g["conv_w"], g["conv_b"] = conv_bwd(n + "conv", sv["pm"], w["conv_w"], w["conv_b"], dxbc_act, dxs_skip, bsz, seq)
    dpdt, g["dt_bias"], g["a_log"] = dt_bwd(n + "dt", sv["pdt"], w["dt_bias"], w["a_log"], ddt, dadt_c, dadt_r.T)
    dq, dk_, dv_ = sb_bwd(n + "sb", sv["pm"], sv["tot"], do_att, bsz, seq)
    dpm = jnp.concatenate([dz, dxbc, dq, dk_, dv_, dgates], axis=1)
    g["wm"] = _mm(n + "dw_in", sv["h1"], dpm, "tn", bf16)
    g["wdt"] = _mm(n + "dw_in_dt", sv["h1"], dpdt, "tn", bf16)
    dh1 = _mm(n + "dh1", dpm, w["wm"], "nt")
    dh1_dt = _mm(n + "dh1_dt", dpdt, w["wdt"], "nt")
    dx, g["g_pre_mix"] = rms_bwd(n + "rms_mix", sv["x0"], w["g_pre_mix"], [dh1, dh1_dt], dx)
    return dx, g


def _group_pad(v):
    lead = v.shape[:-1]
    v = v.reshape(*lead, SSM_GROUPS, HPG)
    return jnp.pad(v, [(0, 0)] * (len(lead) + 1) + [(0, 128 - HPG)]).reshape(*lead, DT_W)


def _group_unpad(v):
    lead = v.shape[:-1]
    return v.reshape(*lead, SSM_GROUPS, 128)[..., :HPG].reshape(*lead, SSM_HEADS)


BIG = ("w_in", "w_br_att", "w_br_ssm", "w_mix_out", "w_xq", "w_xkv", "w_xo", "w_gu", "w_down")
GAINS = ("g_pre_mix", "g_post_mix", "g_pre_xa", "g_mem", "g_post_xa", "g_pre_ffn", "g_post_ffn")
HEAD_VECS = ("dt_bias", "a_log", "d_skip")
SMALL = GAINS + ("conv_w", "conv_b", "g_ssm_norm") + HEAD_VECS


def _prep_layer(p):
    w = {k: p[k] for k in BIG[1:]}
    if "wm" in p:
        w["wm"], w["wdt"] = p["wm"], p["wdt"]
    else:
        w_in = p["w_in"]
        w["wm"] = jnp.concatenate([w_in[:, 3072:8192], w_in[:, 0:3072], w_in[:, 8224:10272]], axis=1)
        w["wdt"] = _group_pad(w_in[:, 8192:8224])
    for k in GAINS + ("conv_b", "g_ssm_norm"):
        w[k] = p[k].reshape(1, -1)
    w["conv_w"] = p["conv_w"]
    w["dt_bias"] = _group_pad(p["dt_bias"]).reshape(1, DT_W)
    w["a_log"] = _group_pad(p["a_log"]).reshape(1, DT_W)
    w["d_skip"] = jnp.repeat(p["d_skip"], SSM_HD).reshape(1, SSM_INNER)
    return w


def _unprep_grads(g):
    out = {k: g[k] for k in BIG[1:]}
    gm = g["wm"]
    out["w_in"] = jnp.concatenate([gm[:, 5120:8192], gm[:, 0:5120], _group_unpad(g["wdt"]), gm[:, 8192:10240]], axis=1)
    for k in GAINS + ("conv_b", "g_ssm_norm"):
        out[k] = g[k].reshape(-1)
    out["conv_w"] = g["conv_w"]
    out["dt_bias"] = _group_unpad(g["dt_bias"]).reshape(-1)
    out["a_log"] = _group_unpad(g["a_log"]).reshape(-1)
    out["d_skip"] = g["d_skip"].reshape(SSM_HEADS, SSM_HD).sum(axis=1)
    return out


def _local_step(x, mem, target, ws, bsz, seq, mlen):
    saved = []
    for l in range(len(ws)):
        x, sv = _layer_fwd(l, x, mem, ws[l], bsz, seq, mlen)
        saved.append(sv)
    dx, loss_lanes = loss_fwd_bwd("loss", x, target)
    grads = [None] * len(ws)
    for l in reversed(range(len(ws))):
        dx, grads[l] = _layer_bwd(l, dx, mem, ws[l], saved[l], bsz, seq, mlen)
    return loss_lanes, dx, grads


HBM_SPEC = pl.BlockSpec(memory_space=pltpu.HBM)
FLIP_C = (0, 0, 1)
FLIPS_CHIP = ((1, 0, 0), (0, 1, 0), (1, 1, 0))
FLIPS_ALL = tuple(((f >> 2) & 1, (f >> 1) & 1, f & 1) for f in range(1, 8))


def _view(ref, index):
    return ref.at[index] if index != () else ref


def _exchange(name, srcs, out_shapes, transfers, in_place=False):
    na = len(srcs)
    nr = sum(1 for t in transfers if any(t[0]))
    nl = len(transfers) - nr

    def body(*refs):
        out_refs = refs[na:2 * na]
        src_refs = out_refs if in_place else refs[:na]
        send_sems, recv_sems, local_sems = refs[2 * na:]
        pos = (lax.axis_index("x"), lax.axis_index("y"), lax.axis_index("c"))
        copies = []
        for a in range(na):
            ri = li = 0
            for flip, src_index, dst_index in transfers:
                src_view, dst_view = _view(src_refs[a], src_index(*pos)), _view(out_refs[a], dst_index(*pos))
                if any(flip):
                    peer = tuple(1 - p if f else p for p, f in zip(pos, flip))
                    cp = pltpu.make_async_remote_copy(
                        src_ref=src_view, dst_ref=dst_view, send_sem=send_sems.at[a * nr + ri],
                        recv_sem=recv_sems.at[a * nr + ri], device_id=peer, device_id_type=MESH)
                    ri += 1
                else:
                    cp = pltpu.make_async_copy(src_view, dst_view, local_sems.at[a * nl + li])
                    li += 1
                cp.start()
                copies.append(cp)
        for cp in copies:
            cp.wait()

    if in_place:
        out_shape = [jax.ShapeDtypeStruct(s.shape, s.dtype) for s in srcs]
    else:
        out_shape = [jax.ShapeDtypeStruct(shape, dtype) for shape, dtype in out_shapes]
    return pl.pallas_call(
        body, name=name, out_shape=out_shape, in_specs=[HBM_SPEC] * na, out_specs=[HBM_SPEC] * na,
        input_output_aliases={a: a for a in range(na)} if in_place else {},
        scratch_shapes=[pltpu.SemaphoreType.DMA((max(na * nr, 1),)), pltpu.SemaphoreType.DMA((max(na * nr, 1),)),
                        pltpu.SemaphoreType.DMA((max(na * nl, 1),))],
    )(*srcs)


def _at(*index):
    return lambda x, y, c: index


def _allgather8(name, v, me):
    got = _exchange(name, [v], [((7,) + v.shape, v.dtype)], [(fl, _at(), _at(j)) for j, fl in enumerate(FLIPS_ALL)])[0]
    rel = jnp.concatenate([v[None], got], axis=0)
    return jnp.stack([lax.dynamic_index_in_dim(rel, k ^ me, 0, keepdims=False) for k in range(8)])


def _sum8(name, parts):
    def fn(*p):
        acc = p[0]
        for q in p[1:]:
            acc = acc + q
        return acc

    return _rowwise(name, fn, [(parts[k], 0) for k in range(8)], [], [(1, f32)], [], width=128, bt=parts.shape[1])[0]


def _rows_block(r, w, bytes_per_row_elem):
    for bt in (512, 256, 128, 64, 32, 16, 8):
        if r % bt == 0 and bt * w * bytes_per_row_elem * 2 <= 16 * 1024 * 1024:
            return bt
    raise ValueError((r, w))


def _reduce_pair(name, g, recv, ci):
    _, m, w = g.shape
    bt = _rows_block(m, w, 2 + 2 + 4 + 2)

    def body(c_ref, a_ref, b_ref, o32_ref, o16_ref):
        acc = a_ref[0].astype(f32) + b_ref[...].astype(f32)
        o32_ref[...] = acc
        o16_ref[...] = acc.astype(bf16)

    return pl.pallas_call(
        body, name=name,
        grid_spec=pltpu.PrefetchScalarGridSpec(
            num_scalar_prefetch=1, grid=(m // bt,),
            in_specs=[pl.BlockSpec((1, bt, w), lambda i, c_ref: (c_ref[0], i, 0)),
                      pl.BlockSpec((bt, w), lambda i, c_ref: (i, 0))],
            out_specs=[pl.BlockSpec((bt, w), lambda i, c_ref: (i, 0))] * 2),
        out_shape=[jax.ShapeDtypeStruct((m, w), f32), jax.ShapeDtypeStruct((m, w), bf16)],
        compiler_params=_cparams(("parallel",)),
    )(ci.reshape(1).astype(jnp.int32), g, recv)


def _reduce_chips(name, p32, recv, shard, ci):
    _, _, r, w = p32.shape
    bt = _rows_block(r, w, 4 + 3 * 2 + 4)

    def body(s_ref, a_ref, b_ref, o_ref):
        acc = a_ref[0, 0]
        for j in range(3):
            acc = acc + b_ref[j, 0].astype(f32)
        o_ref[0, 0] = acc

    return pl.pallas_call(
        body, name=name,
        grid_spec=pltpu.PrefetchScalarGridSpec(
            num_scalar_prefetch=1, grid=(2, r // bt),
            in_specs=[pl.BlockSpec((1, 1, bt, w), lambda j, i, s_ref: (j, s_ref[0], i, 0)),
                      pl.BlockSpec((3, 1, bt, w), lambda j, i, s_ref: (0, j, i, 0))],
            out_specs=pl.BlockSpec((1, 1, bt, w), lambda j, i, s_ref: (s_ref[1], j, i, 0))),
        out_shape=jax.ShapeDtypeStruct((2, 2, r, w), f32),
        compiler_params=_cparams(("parallel", "parallel")),
    )(jnp.stack([shard, ci]).astype(jnp.int32), p32, recv)


COL_SHARDED = ("w_in", "w_xkv", "w_gu")


def _ref_cols(pieces, lo, hi):
    c, out = pieces[0].shape[1], []
    for s, p in enumerate(pieces):
        a0, a1 = max(lo, s * c), min(hi, (s + 1) * c)
        if a0 < a1:
            out.append(p[:, a0 - s * c:a1 - s * c])
    return out


def _my_cols(gm, g32, lo, hi):
    out = []
    for r0, r1, src, shift in ((0, 3072, gm, 5120), (3072, 8192, gm, -3072), (8192, 8224, g32, -8192), (8224, IN_WIDTH, gm, -32)):
        a0, a1 = max(lo, r0), min(hi, r1)
        if a0 < a1:
            out.append(src[:, a0 + shift:a1 + shift])
    return out


def _pack(arrs, rows_multiple=8):
    flat = jnp.concatenate([a.reshape(-1) for a in arrs])
    pad = (-flat.shape[0]) % (128 * rows_multiple)
    return jnp.pad(flat, (0, pad)).reshape(-1, 128)


def _unpack(buf, shapes):
    flat, out, o = buf.reshape(-1), [], 0
    for s in shapes:
        n = math.prod(s)
        out.append(flat[o:o + n].reshape(s))
        o += n
    return out


def kernel(x, mem, g_pre_mix, w_in, conv_w, conv_b, dt_bias, a_log, d_skip, g_ssm_norm, w_br_att, w_br_ssm, w_mix_out, g_post_mix, g_pre_xa, g_mem, w_xq, w_xkv, w_xo, g_post_xa, g_pre_ffn, w_gu, w_down, g_post_ffn, loss_target, m_g_pre_mix, m_w_in, m_conv_w, m_conv_b, m_dt_bias, m_a_log, m_d_skip, m_g_ssm_norm, m_w_br_att, m_w_br_ssm, m_w_mix_out, m_g_post_mix, m_g_pre_xa, m_g_mem, m_w_xq, m_w_xkv, m_w_xo, m_g_post_xa, m_g_pre_ffn, m_w_gu, m_w_down, m_g_post_ffn, v_g_pre_mix, v_w_in, v_conv_w, v_conv_b, v_dt_bias, v_a_log, v_d_skip, v_g_ssm_norm, v_w_br_att, v_w_br_ssm, v_w_mix_out, v_g_post_mix, v_g_pre_xa, v_g_mem, v_w_xq, v_w_xkv, v_w_xo, v_g_post_xa, v_g_pre_ffn, v_w_gu, v_w_down, v_g_post_ffn):
    a = dict(locals())
    names = ("g_pre_mix", "w_in", "conv_w", "conv_b", "dt_bias", "a_log", "d_skip", "g_ssm_norm", "w_br_att", "w_br_ssm",
             "w_mix_out", "g_post_mix", "g_pre_xa", "g_mem", "w_xq", "w_xkv", "w_xo", "g_post_xa", "g_pre_ffn", "w_gu",
             "w_down", "g_post_ffn")
    depth = w_in.shape[0]
    bsz, seq, _ = x.shape
    mlen = mem.shape[1]
    xi, yi, ci = lax.axis_index("x"), lax.axis_index("y"), lax.axis_index("c")
    shard = 2 * xi + yi
    me = 2 * shard + ci

    assert depth == 4
    srcs = [a[k].astype(bf16).reshape(2, 2, *a[k].shape[1:]) for k in BIG]
    gathered = _exchange(
        "ag_ici", srcs, [((2, 2, 4) + s.shape[2:], bf16) for s in srcs],
        [(fl, lambda x_, y_, c_: (c_,), lambda x_, y_, c_: (c_, slice(None), 2 * x_ + y_)) for fl in FLIPS_CHIP])
    gathered = [lax.dynamic_update_slice(g, s[:, :, None], (0, 0, shard, 0, 0)) for g, s in zip(gathered, srcs)]
    fetched = [functools.partial(lambda x_, y_, c_, f: (c_, slice(None), (2 * x_ + y_) ^ f), f=2 * fl[0] + fl[1])
               for fl in FLIPS_CHIP]
    gathered = _exchange("ag_d2d", gathered, None, [(FLIP_C, fn, fn) for fn in fetched], in_place=True)
    full = {k: g.reshape(4, 4, *g.shape[3:]) for k, g in zip(BIG, gathered)}

    cw_all = _allgather8("ag_conv_w", _pack([conv_w]), me)
    cw_shape = conv_w.shape
    conv_w_full = jnp.concatenate([_unpack(cw_all[2 * s], [cw_shape])[0] for s in range(4)], axis=2)

    ws = []
    for l in range(depth):
        p = {}
        for k in BIG[1:]:
            sh = full[k][l]
            p[k] = sh.transpose(1, 0, 2).reshape(sh.shape[1], -1) if k in COL_SHARDED else sh.reshape(-1, sh.shape[2])
        pieces = [full["w_in"][l, s] for s in range(4)]
        p["wm"] = jnp.concatenate(_ref_cols(pieces, 3072, 8192) + _ref_cols(pieces, 0, 3072)
                                  + _ref_cols(pieces, 8224, 10272), axis=1)
        p["wdt"] = _group_pad(jnp.concatenate(_ref_cols(pieces, 8192, 8224), axis=1))
        for k in SMALL:
            p[k] = conv_w_full[l] if k == "conv_w" else a[k][l]
        ws.append(_prep_layer(p))

    loss_lanes, gx, grads = _local_step(x.reshape(bsz * seq, D), mem.reshape(bsz * mlen, D),
                                        loss_target.reshape(bsz * seq, D), ws, bsz, seq, mlen)
    wires = []
    for k in BIG:
        r, c = a[k].shape[1:]
        per_layer = []
        for l in range(depth):
            g = grads[l]
            if k == "w_in":
                g32 = _group_unpad(g["wdt"])
                per_layer.append(jnp.stack([jnp.concatenate(_my_cols(g["wm"], g32, s * c, (s + 1) * c), axis=1)
                                            for s in range(4)]))
            elif k in COL_SHARDED:
                per_layer.append(g[k].reshape(r, 4, c).transpose(1, 0, 2))
            else:
                per_layer.append(g[k].reshape(4, r, c))
        wires.append(jnp.stack(per_layer).astype(bf16).reshape(2, 2 * 4 * r, c))
    from_sib = _exchange("rs_d2d", wires, [(w.shape[1:], bf16) for w in wires], [(FLIP_C, lambda x_, y_, c_: (1 - c_,), _at())])
    sums = [_reduce_pair("rs_pair_" + k, w, got, ci) for k, w, got in zip(BIG, wires, from_sib)]
    by_shard4 = [(2, 4) + a[k].shape[1:] for k in BIG]
    from_chips = _exchange(
        "rs_ici", [s[1].reshape(shp) for s, shp in zip(sums, by_shard4)], [((3, 2) + a[k].shape[1:], bf16) for k in BIG],
        [(fl, functools.partial(lambda x_, y_, c_, f: (slice(None), (2 * x_ + y_) ^ f), f=2 * fl[0] + fl[1]), _at(j))
         for j, fl in enumerate(FLIPS_CHIP)])
    red = [_reduce_chips("rs_chips_" + k, s[0].reshape(shp), got, shard, ci) for k, s, shp, got in zip(BIG, sums, by_shard4, from_chips)]
    my_half = lambda x_, y_, c_: (c_,)
    gshards = _exchange("rs_swap", red, None, [(FLIP_C, my_half, my_half)], in_place=True)
    grads = [_unprep_grads(g) for g in grads]

    out_g, out_d, out_m, out_v = {}, {}, {}, {}
    for k, g in zip(BIG, gshards):
        shp = a[k].shape
        g = g.reshape(shp)
        two_d = (shp[0] * shp[1], shp[2])
        d_, m_, v_ = adamw("adamw_" + k, a[k].reshape(two_d), g.reshape(two_d), a["m_" + k].reshape(two_d), a["v_" + k].reshape(two_d))
        out_g[k], out_d[k], out_m[k], out_v[k] = g, d_.reshape(shp), m_.reshape(shp), v_.reshape(shp)

    small_shapes = [(depth,) + (conv_w_full.shape[1:] if k == "conv_w" else a[k].shape[1:]) for k in SMALL]
    small = _pack([jnp.stack([grads[l][k] for l in range(depth)]) for k in SMALL] + [loss_lanes])
    total = _sum8("small_sum", _allgather8("ag_small", small, me))
    *gsmall, loss_l = _unpack(total, small_shapes + [loss_lanes.shape])
    gsmall = dict(zip(SMALL, gsmall))
    gsmall["conv_w"] = lax.dynamic_slice_in_dim(gsmall["conv_w"], shard * cw_shape[2], cw_shape[2], axis=2)
    loc_shapes = [a[k].shape for k in SMALL]
    d_, m_, v_ = adamw("adamw_small", _pack([a[k] for k in SMALL]), _pack([gsmall[k] for k in SMALL]),
                       _pack([a["m_" + k] for k in SMALL]), _pack([a["v_" + k] for k in SMALL]))
    for k, dd, mm, vv in zip(SMALL, _unpack(d_, loc_shapes), _unpack(m_, loc_shapes), _unpack(v_, loc_shapes)):
        out_g[k], out_d[k], out_m[k], out_v[k] = gsmall[k], dd, mm, vv

    loss = jnp.sum(loss_l)
    return (loss, gx.reshape(x.shape), *[out_g[k] for k in names], *[out_d[k] for k in names],
            *[out_m[k] for k in names], *[out_v[k] for k in names])
```

```python
import functools
import math

import jax
import jax.numpy as jnp
from jax import lax
from jax.experimental import pallas as pl
from jax.experimental.pallas import tpu as pltpu

f32, bf16 = jnp.float32, jnp.bfloat16

DEPTH = 4
D = 1024
SB_HEADS, SB_HD = 16, 64
SSM_INNER, SSM_HD, SSM_HEADS, SSM_GROUPS, SSM_STATE, SSM_CONV, SSM_CHUNK = 2048, 64, 32, 4, 128, 4, 128
HPG = SSM_HEADS // SSM_GROUPS
CONV_DIM = SSM_INNER + 2 * SSM_GROUPS * SSM_STATE
XA_HEADS, XA_HD = 4, 256
FFN = 2816
IN_WIDTH = 10272
RMS_EPS = 1e-6
LR, B1, B2, EPS, WD, STEP = 0.001, 0.9, 0.999, 1e-08, 0.01, 10

PM_W = 10240
OFF_Z, OFF_XBC, OFF_Q, OFF_K, OFF_V, OFF_GA, OFF_GS = 0, 2048, 5120, 6144, 7168, 8192, 9216
DT_W = SSM_GROUPS * 128

VMEM_LIMIT = 48 * 1024 * 1024
MESH = pl.DeviceIdType.MESH


def _cparams(sem):
    return pltpu.CompilerParams(dimension_semantics=sem, vmem_limit_bytes=VMEM_LIMIT)


def _tile(n):
    for t in (512, 256, 128):
        if n % t == 0:
            return t
    raise ValueError(f"dimension {n} is not a multiple of 128")


MM_VMEM_BUDGET = 34 * 1024 * 1024


def _mm_tiles(m, n, k, sa, sb, so):
    best = None
    for tm in (2048, 1024, 512, 256, 128):
        if m % tm:
            continue
        for tn in (2048, 1024, 512, 256, 128):
            if n % tn:
                continue
            for tk in (k, 2048, 1024, 512):
                if tk > k or k % tk:
                    continue
                vmem = 2 * (tm * tk * sa + tk * tn * sb + tm * tn * so) + tm * tn * 4 * (2 if tk < k else 1)
                if vmem > MM_VMEM_BUDGET:
                    continue
                traffic = m * k * sa * (n // tn) + k * n * sb * (m // tm) + m * n * so
                steps = (m // tm) * (n // tn) * (k // tk)
                accumulate = (k // tk > 1) * (k // tk) * m * n * 2
                key = (traffic + steps * 800_000 + accumulate, steps)
                if best is None or key < best[0]:
                    best = (key, (tm, tn, tk))
    assert best is not None, (m, n, k)
    return best[1]


def _mm(name, a, b, mode, out_dtype=f32):
    if mode == "nn":
        (m, k), (k2, n) = a.shape, b.shape
    elif mode == "nt":
        (m, k), (n, k2) = a.shape, b.shape
    else:
        (k, m), (k2, n) = a.shape, b.shape
    assert k == k2, (name, a.shape, b.shape, mode)
    tm, tn, tk = _mm_tiles(m, n, k, a.dtype.itemsize, b.dtype.itemsize, jnp.dtype(out_dtype).itemsize)
    nk = k // tk
    dn = {"nn": (((1,), (0,)), ((), ())), "nt": (((1,), (1,)), ((), ())), "tn": (((0,), (0,)), ((), ()))}[mode]

    def product(a_ref, b_ref):
        return lax.dot_general(a_ref[...].astype(bf16), b_ref[...].astype(bf16), dn, preferred_element_type=f32)

    def body_whole_k(a_ref, b_ref, o_ref):
        o_ref[...] = product(a_ref, b_ref).astype(o_ref.dtype)

    def body_k_loop(a_ref, b_ref, o_ref, acc_ref):
        kk = pl.program_id(2)

        @pl.when(kk == 0)
        def _():
            acc_ref[...] = product(a_ref, b_ref)

        @pl.when(kk > 0)
        def _():
            acc_ref[...] += product(a_ref, b_ref)

        @pl.when(kk == nk - 1)
        def _():
            o_ref[...] = acc_ref[...].astype(o_ref.dtype)

    a_spec = pl.BlockSpec((tk, tm), lambda i, j, kk: (kk, i)) if mode == "tn" else pl.BlockSpec((tm, tk), lambda i, j, kk: (i, kk))
    b_spec = pl.BlockSpec((tn, tk), lambda i, j, kk: (j, kk)) if mode == "nt" else pl.BlockSpec((tk, tn), lambda i, j, kk: (kk, j))
    return pl.pallas_call(
        body_whole_k if nk == 1 else body_k_loop, name=name, grid=(m // tm, n // tn, nk),
        in_specs=[a_spec, b_spec],
        out_specs=pl.BlockSpec((tm, tn), lambda i, j, kk: (i, j)),
        out_shape=jax.ShapeDtypeStruct((m, n), out_dtype),
        scratch_shapes=[] if nk == 1 else [pltpu.VMEM((tm, tn), f32)],
        compiler_params=_cparams(("parallel", "parallel", "arbitrary")),
    )(a, b)


def _rowwise(name, fn, rows, consts, out_rows, out_accs, *, width, ncol=1, bt=256):
    r = rows[0][0].shape[0]
    bt = min(bt, r)
    assert r % bt == 0, (name, r, bt)
    nrow = r // bt
    n_in = len(rows) + len(consts)
    n_or = len(out_rows)

    def body(*refs):
        ins = [ref[...].astype(f32) for ref in refs[:n_in]]
        outs = fn(*ins)
        if not isinstance(outs, (tuple, list)):
            outs = (outs,)
        o_refs = refs[n_in:]
        for o_ref, val in zip(o_refs[:n_or], outs[:n_or]):
            o_ref[...] = val.astype(o_ref.dtype)
        if out_accs:
            i = pl.program_id(1)
            for o_ref, val in zip(o_refs[n_or:], outs[n_or:]):
                @pl.when(i == 0)
                def _(o_ref=o_ref, val=val):
                    o_ref[...] = val

                @pl.when(i > 0)
                def _(o_ref=o_ref, val=val):
                    o_ref[...] += val

    in_specs = [pl.BlockSpec((bt, width), functools.partial(lambda j, i, off: (i, off + j), off=off)) for _, off in rows]
    in_specs += [pl.BlockSpec((c.shape[0], width), functools.partial(lambda j, i, off: (0, off + j), off=off)) for c, off in consts]
    out_specs = [pl.BlockSpec((bt, mlt * width), lambda j, i: (i, j)) for mlt, _ in out_rows]
    out_specs += [pl.BlockSpec((k, width), lambda j, i: (0, j)) for k in out_accs]
    out_shape = [jax.ShapeDtypeStruct((r, ncol * mlt * width), dt) for mlt, dt in out_rows]
    out_shape += [jax.ShapeDtypeStruct((k, ncol * width), f32) for k in out_accs]
    res = pl.pallas_call(
        body, name=name, grid=(ncol, nrow), in_specs=in_specs, out_specs=out_specs, out_shape=out_shape,
        compiler_params=_cparams(("parallel", "arbitrary" if out_accs else "parallel")),
    )(*[a for a, _ in rows], *[c for c, _ in consts])
    return res


def _rms(x, g):
    return x * lax.rsqrt(jnp.mean(x * x, axis=-1, keepdims=True) + RMS_EPS) * g


def _silu(x):
    return x * jax.nn.sigmoid(x)


def _softplus(x):
    return jnp.maximum(x, 0.0) + jnp.log(1.0 + jnp.exp(-jnp.abs(x)))


def _colsum(x):
    return jnp.sum(x, axis=0, keepdims=True)


def rms_fwd(name, x, g):
    return _rowwise(name, _rms, [(x, 0)], [(g, 0)], [(1, bf16)], [], width=D)[0]


def rms_bwd(name, x, g, dhs, dres=None):
    nd = len(dhs)

    def fn(x, *rest):
        dh = rest[0]
        for extra in rest[1:nd]:
            dh = dh + extra
        g = rest[-1]
        _, vjp = jax.vjp(_rms, x, g)
        dx, dg = vjp(dh.astype(f32))
        if dres is not None:
            dx = dx + rest[nd]
        return dx, dg

    rows = [(x, 0)] + [(d, 0) for d in dhs] + ([(dres, 0)] if dres is not None else [])
    return _rowwise(name, fn, rows, [(g, 0)], [(1, f32)], [1], width=D)


def addnorm_fwd(name, x, u, g):
    return _rowwise(name, lambda x, u, g: x + _rms(u, g), [(x, 0), (u, 0)], [(g, 0)], [(1, f32)], [], width=D)[0]


def addnorm_bwd(name, u, g, dx):
    def fn(u, dx, g):
        _, vjp = jax.vjp(_rms, u, g)
        return vjp(dx)

    return _rowwise(name, fn, [(u, 0), (dx, 0)], [(g, 0)], [(1, bf16)], [1], width=D)


def _merge(ga, gs, a, s):
    return jax.nn.sigmoid(ga) * a + jax.nn.sigmoid(gs) * s


def merge_fwd(name, pm, a, s):
    return _rowwise(name, _merge, [(pm, OFF_GA // D), (pm, OFF_GS // D), (a, 0), (s, 0)], [], [(1, bf16)], [], width=D)[0]


def merge_bwd(name, pm, a, s, dm):
    def fn(ga, gs, a, s, dm):
        _, vjp = jax.vjp(_merge, ga, gs, a, s)
        dga, dgs, da, ds = vjp(dm)
        return jnp.concatenate([dga, dgs], axis=1), da, ds

    return _rowwise(name, fn, [(pm, OFF_GA // D), (pm, OFF_GS // D), (a, 0), (s, 0), (dm, 0)], [],
                    [(2, bf16), (1, bf16), (1, bf16)], [], width=D)


def _swiglu(gate, up):
    return _silu(gate) * up


def swiglu_fwd(name, gu):
    return _rowwise(name, _swiglu, [(gu, 0), (gu, 1)], [], [(1, bf16)], [], width=FFN)[0]


def swiglu_bwd(name, gu, dact):
    def fn(gate, up, dact):
        _, vjp = jax.vjp(_swiglu, gate, up)
        dg, du = vjp(dact.astype(f32))
        return jnp.concatenate([dg, du], axis=1)

    return _rowwise(name, fn, [(gu, 0), (gu, 1), (dact, 0)], [], [(2, bf16)], [], width=FFN, bt=128)[0]


GW = SSM_INNER // SSM_GROUPS


def _gnorm(y, xs, z, dskip, gn):
    yy = (y + dskip * xs) * _silu(z)
    return yy * lax.rsqrt(jnp.mean(yy * yy, axis=-1, keepdims=True) + RMS_EPS) * gn


def gnorm_fwd(name, y, xbc, pm, dskip, gn):
    return _rowwise(name, _gnorm, [(y, 0), (xbc, 0), (pm, OFF_Z // GW)], [(dskip, 0), (gn, 0)], [(1, bf16)], [],
                    width=GW, ncol=SSM_GROUPS)[0]


def gnorm_bwd(name, y, xbc, pm, dskip, gn, do):
    def fn(y, xs, z, do, dskip, gn):
        _, vjp = jax.vjp(_gnorm, y, xs, z, dskip, gn)
        return vjp(do.astype(f32))

    return _rowwise(name, fn, [(y, 0), (xbc, 0), (pm, OFF_Z // GW), (do, 0)], [(dskip, 0), (gn, 0)],
                    [(1, f32), (1, f32), (1, bf16)], [1, 1], width=GW, ncol=SSM_GROUPS)


def _dtfn(pdt, bias, alog):
    dt = _softplus(pdt + bias)
    return dt, -jnp.exp(alog) * dt


def dt_fwd(name, pdt, bias, alog):
    return _rowwise(name, _dtfn, [(pdt, 0)], [(bias, 0), (alog, 0)], [(1, f32), (1, f32)], [], width=DT_W)


def dt_bwd(name, pdt, bias, alog, ddt, dadt_c, dadt_r):
    def fn(pdt, ddt, dac, dar, bias, alog):
        _, vjp = jax.vjp(_dtfn, pdt, bias, alog)
        return vjp((ddt, dac + dar))

    return _rowwise(name, fn, [(pdt, 0), (ddt, 0), (dadt_c, 0), (dadt_r, 0)], [(bias, 0), (alog, 0)],
                    [(1, f32)], [1, 1], width=DT_W)


def loss_fwd_bwd(name, y, target):
    def fn(y, t):
        e = y - t
        return e * (1.0 / D), _colsum(e * e) * (0.5 / D)

    return _rowwise(name, fn, [(y, 0), (target, 0)], [], [(1, f32)], [1], width=D)


def adamw(name, w, g, m, v):
    r, c = w.shape

    def fn(w, g, m, v):
        m = B1 * m + (1.0 - B1) * g
        v = B2 * v + (1.0 - B2) * (g * g)
        m_hat = m / (1.0 - B1 ** STEP)
        v_hat = v / (1.0 - B2 ** STEP)
        return -LR * (m_hat / (jnp.sqrt(v_hat) + EPS) + WD * w), m, v

    bt = 256
    while bt > 8 and (r % bt or bt * c * 4 * 7 * 2 > 16 * 1024 * 1024):
        bt //= 2
    if r % bt:
        bt = r
    return _rowwise(name, fn, [(w, 0), (g, 0), (m, 0), (v, 0)], [], [(1, f32)] * 3, [], width=c, bt=bt)


SB_BQ, SB_BK = 512, 256
SB_UNROLL = 2


def _dot(a, b):
    return jnp.dot(a, b, preferred_element_type=f32)


def _dot_nt(a, b):
    return lax.dot_general(a, b, (((1,), (1,)), ((), ())), preferred_element_type=f32)


def _dot_tn(a, b):
    return lax.dot_general(a, b, (((0,), (0,)), ((), ())), preferred_element_type=f32)


def _dot2(x, tri):
    hi = x.astype(bf16)
    lo = (x - hi.astype(f32)).astype(bf16)
    return _dot(hi, tri) + _dot(lo, tri)


def _tri(n, rel):
    r = lax.broadcasted_iota(jnp.int32, (n, n), 0)
    c = lax.broadcasted_iota(jnp.int32, (n, n), 1)
    m = {"ge": r >= c, "lt": r < c, "le": r <= c}[rel]
    return jnp.where(m, 1.0, 0.0).astype(bf16)


def _grid_ends(grid):
    ids = [pl.program_id(d) for d in range(len(grid))]
    first = functools.reduce(jnp.logical_and, [i == 0 for i in ids])
    last = functools.reduce(jnp.logical_and, [i == n - 1 for i, n in zip(ids, grid)])
    return first, last


def sb_fwd(name, pm, bsz, seq, exchange=None):
    bq = min(SB_BQ, seq)
    bk = min(SB_BK, bq)
    nq, nd = seq // bq, bq // bk
    step = SB_UNROLL if nd % SB_UNROLL == 0 else 1
    scale = SB_HD ** -0.5
    qb, kb_, vb_ = OFF_Q // 128, OFF_K // 128, OFF_V // 128
    na = len(exchange[0]) if exchange else 0
    grid = (bsz, 8, nq)

    def body(*refs):
        q_ref, k_ref, v_ref = refs[:3]
        o_ref, tot_ref = refs[3 + na:5 + na]
        if exchange:
            _exchange_hook(exchange, refs[3:3 + na], refs[5 + na:5 + 2 * na], refs[5 + 2 * na:], *_grid_ends(grid))
        i = pl.program_id(2)
        lane = lax.broadcasted_iota(jnp.int32, (1, 128), 1)
        m0 = lane < SB_HD
        q = q_ref[...].astype(f32) * scale
        qs = (jnp.where(m0, q, 0.0).astype(bf16), jnp.where(m0, 0.0, q).astype(bf16))
        neg_tri = -_tri(bk, "ge")
        t_idx = i * bq + lax.broadcasted_iota(jnp.int32, (bq, 1), 0)

        def block(ks, carry, masked):
            o_acc, c0, c1 = carry
            kblk = k_ref[pl.ds(ks, bk), :].astype(bf16)
            vblk = v_ref[pl.ds(ks, bk), :].astype(bf16)
            vs = (jnp.where(m0, vblk, 0).astype(bf16), jnp.where(m0, 0, vblk).astype(bf16))
            if masked:
                valid = (ks + lax.broadcasted_iota(jnp.int32, (1, bk), 1)) < t_idx
            cs = [c0, c1]
            for h in range(2):
                z = _dot_nt(qs[h], kblk)
                sp = _softplus(z)
                if masked:
                    sp = jnp.where(valid, sp, 0.0)
                tl = _dot2(sp, neg_tri)
                w = jnp.exp(z + tl + cs[h])
                if masked:
                    w = jnp.where(valid, w, 0.0)
                o_acc = o_acc + _dot(w.astype(bf16), vs[h])
                cs[h] = cs[h] + tl[:, 0:1]
            return o_acc, cs[0], cs[1]

        zc = jnp.zeros((bq, 1), f32)
        carry = (jnp.zeros((bq, 128), f32), zc, zc)
        for d in range(nd):
            carry = block(pl.multiple_of((i * nd + nd - 1 - d) * bk, bk), carry, True)
        def far(n, c):
            for u in range(step):
                c = block(pl.multiple_of((i * nd - 1 - (step * n + u)) * bk, bk), c, False)
            return c

        carry = lax.fori_loop(0, i * (nd // step), far, carry)
        o, c0, c1 = carry
        o_ref[...] = o.astype(o_ref.dtype)
        tot_ref[0, 0] = jnp.where(m0, c0, c1)

    return pl.pallas_call(
        body, name=name, grid=grid,
        in_specs=[pl.BlockSpec((bq, 128), lambda b, p, i: (b * nq + i, qb + p)),
                  pl.BlockSpec((seq, 128), lambda b, p, i: (b, kb_ + p)),
                  pl.BlockSpec((seq, 128), lambda b, p, i: (b, vb_ + p))] + [HBM_SPEC] * na,
        out_specs=[pl.BlockSpec((bq, 128), lambda b, p, i: (b * nq + i, p)),
                   pl.BlockSpec((1, 1, bq, 128), lambda b, p, i: (b, p, i, 0))] + [HBM_SPEC] * na,
        out_shape=[jax.ShapeDtypeStruct((bsz * seq, 1024), bf16), jax.ShapeDtypeStruct((bsz, 8, seq, 128), f32)]
        + ([jax.ShapeDtypeStruct(shape, dtype) for shape, dtype in exchange[1]] if exchange else []),
        scratch_shapes=_copy_semaphores(na, exchange[2]) if exchange else [],
        compiler_params=_cparams(("arbitrary",) * 3 if exchange else ("parallel",) * 3),
    )(pm, pm, pm, *(exchange[0] if exchange else []))


def sb_bwd(name, pm, tot, do, bsz, seq, exchange=None):
    bq = min(SB_BQ, seq)
    bk = min(SB_BK, bq)
    nq, nd = seq // bq, bq // bk
    step = SB_UNROLL if nd % SB_UNROLL == 0 else 1
    scale = SB_HD ** -0.5
    qb, kb_, vb_ = OFF_Q // 128, OFF_K // 128, OFF_V // 128
    na = len(exchange[0]) if exchange else 0
    grid = (bsz, 8, nq)

    def body(*refs):
        q_ref, k_ref, v_ref, do_ref, tot_ref = refs[:5]
        dq_ref, dk_ref, dv_ref = refs[5 + na:8 + na]
        dk_acc, dv_acc = refs[8 + 2 * na:10 + 2 * na]
        if exchange:
            _exchange_hook(exchange, refs[5:5 + na], refs[8 + na:8 + 2 * na], refs[10 + 2 * na:], *_grid_ends(grid))
        i = pl.program_id(2)

        @pl.when(i == 0)
        def _():
            dk_acc[...] = jnp.zeros_like(dk_acc)
            dv_acc[...] = jnp.zeros_like(dv_acc)

        lane = lax.broadcasted_iota(jnp.int32, (1, 128), 1)
        m0 = lane < SB_HD
        ms = (m0, jnp.logical_not(m0))
        q = q_ref[...].astype(f32) * scale
        qpair = q.astype(bf16)
        qs = (jnp.where(m0, q, 0.0).astype(bf16), jnp.where(m0, 0.0, q).astype(bf16))
        dout = do_ref[...].astype(f32)
        dos = (jnp.where(m0, dout, 0.0).astype(bf16), jnp.where(m0, 0.0, dout).astype(bf16))
        tot = tot_ref[0, 0]
        tots = (tot[:, 0:1], tot[:, SB_HD:SB_HD + 1])
        tri_lt = _tri(bk, "lt")
        tri_le = _tri(bk, "le")
        t_idx = i * bq + lax.broadcasted_iota(jnp.int32, (bq, 1), 0)

        def block(ks, carry, masked):
            dq_acc, p0, p1, g0, g1 = carry
            kblk = k_ref[pl.ds(ks, bk), :].astype(bf16)
            vblk = v_ref[pl.ds(ks, bk), :].astype(bf16)
            if masked:
                valid = (ks + lax.broadcasted_iota(jnp.int32, (1, bk), 1)) < t_idx
            ps, gs = [p0, p1], [g0, g1]
            dk_blk = jnp.zeros((bk, 128), f32)
            dv_blk = jnp.zeros((bk, 128), f32)
            for h in range(2):
                z = _dot_nt(qs[h], kblk)
                sp = _softplus(z)
                sig = jnp.exp(z - sp)
                if masked:
                    sp = jnp.where(valid, sp, 0.0)
                w = jnp.exp(z + tots[h] + ps[h] + _dot2(sp, tri_lt))
                if masked:
                    w = jnp.where(valid, w, 0.0)
                g = _dot_nt(dos[h], vblk) * w
                dz = g - sig * (gs[h] + _dot(g.astype(bf16), tri_le))
                if masked:
                    dz = jnp.where(valid, dz, 0.0)
                dz = dz.astype(bf16)
                dq_acc = dq_acc + jnp.where(ms[h], _dot(dz, kblk), 0.0)
                dk_blk = dk_blk + jnp.where(ms[h], _dot_tn(dz, qpair), 0.0)
                dv_blk = dv_blk + _dot_tn(w.astype(bf16), dos[h])
                ps[h] = ps[h] + jnp.sum(sp, axis=1, keepdims=True)
                gs[h] = gs[h] + jnp.sum(g, axis=1, keepdims=True)
            dk_acc[pl.ds(ks, bk), :] += dk_blk
            dv_acc[pl.ds(ks, bk), :] += dv_blk
            return dq_acc, ps[0], ps[1], gs[0], gs[1]

        zc = jnp.zeros((bq, 1), f32)
        carry = (jnp.zeros((bq, 128), f32), zc, zc, zc, zc)
        def far(n, c):
            for u in range(step):
                c = block(pl.multiple_of((step * n + u) * bk, bk), c, False)
            return c

        carry = lax.fori_loop(0, i * (nd // step), far, carry)
        for d in range(nd):
            carry = block(pl.multiple_of((i * nd + d) * bk, bk), carry, True)
        dq_ref[...] = (carry[0] * scale).astype(dq_ref.dtype)

        @pl.when(i == nq - 1)
        def _():
            dk_ref[...] = dk_acc[...].astype(dk_ref.dtype)
            dv_ref[...] = dv_acc[...].astype(dv_ref.dtype)

    return pl.pallas_call(
        body, name=name, grid=grid,
        in_specs=[pl.BlockSpec((bq, 128), lambda b, p, i: (b * nq + i, qb + p)),
                  pl.BlockSpec((seq, 128), lambda b, p, i: (b, kb_ + p)),
                  pl.BlockSpec((seq, 128), lambda b, p, i: (b, vb_ + p)),
                  pl.BlockSpec((bq, 128), lambda b, p, i: (b * nq + i, p)),
                  pl.BlockSpec((1, 1, bq, 128), lambda b, p, i: (b, p, i, 0))] + [HBM_SPEC] * na,
        out_specs=[pl.BlockSpec((bq, 128), lambda b, p, i: (b * nq + i, p)),
                   pl.BlockSpec((seq, 128), lambda b, p, i: (b, p)),
                   pl.BlockSpec((seq, 128), lambda b, p, i: (b, p))] + [HBM_SPEC] * na,
        out_shape=[jax.ShapeDtypeStruct((bsz * seq, 1024), bf16)] * 3
        + ([jax.ShapeDtypeStruct(shape, dtype) for shape, dtype in exchange[1]] if exchange else []),
        scratch_shapes=[pltpu.VMEM((seq, 128), f32), pltpu.VMEM((seq, 128), f32)]
        + (_copy_semaphores(na, exchange[2]) if exchange else []),
        compiler_params=_cparams(("arbitrary",) * 3 if exchange else ("parallel", "parallel", "arbitrary")),
    )(pm, pm, pm, do, tot, *(exchange[0] if exchange else []))


CONV_CB = 256


def _shift_down(x, d, rows):
    return x if d == 0 else jnp.where(rows >= d, pltpu.roll(x, d, axis=0), 0.0)


def _shift_up(x, d, rows, n):
    return x if d == 0 else jnp.where(rows < n - d, pltpu.roll(x, n - d, axis=0), 0.0)


def conv_fwd(name, pm, w, b, bsz, seq):
    nc = CONV_DIM // CONV_CB
    off = OFF_XBC // CONV_CB

    def body(x_ref, w_ref, b_ref, o_ref):
        x = x_ref[...].astype(f32)
        rows = lax.broadcasted_iota(jnp.int32, x.shape, 0)
        pre = b_ref[...] + jnp.zeros_like(x)
        for k in range(SSM_CONV):
            pre = pre + w_ref[k:k + 1, :] * _shift_down(x, SSM_CONV - 1 - k, rows)
        o_ref[...] = _silu(pre)

    return pl.pallas_call(
        body, name=name, grid=(nc, bsz),
        in_specs=[pl.BlockSpec((seq, CONV_CB), lambda j, bb: (bb, off + j)),
                  pl.BlockSpec((SSM_CONV, CONV_CB), lambda j, bb: (0, j)),
                  pl.BlockSpec((1, CONV_CB), lambda j, bb: (0, j))],
        out_specs=pl.BlockSpec((seq, CONV_CB), lambda j, bb: (bb, j)),
        out_shape=jax.ShapeDtypeStruct((bsz * seq, CONV_DIM), f32),
        compiler_params=_cparams(("parallel", "parallel")),
    )(pm, w, b)


def conv_bwd(name, pm, w, b, dact, dskipx, bsz, seq):
    nc = CONV_DIM // CONV_CB
    off = OFF_XBC // CONV_CB
    nxs = SSM_INNER // CONV_CB

    def body(x_ref, w_ref, b_ref, da_ref, ds_ref, dx_ref, dw_ref, db_ref):
        j, bb = pl.program_id(0), pl.program_id(1)
        x = x_ref[...].astype(f32)
        rows = lax.broadcasted_iota(jnp.int32, x.shape, 0)
        xsh = [_shift_down(x, SSM_CONV - 1 - k, rows) for k in range(SSM_CONV)]
        pre = b_ref[...] + jnp.zeros_like(x)
        for k in range(SSM_CONV):
            pre = pre + w_ref[k:k + 1, :] * xsh[k]
        sig = jax.nn.sigmoid(pre)
        dout = da_ref[...] + jnp.where(j < nxs, ds_ref[...], 0.0)
        dpre = dout * (sig * (1.0 + pre * (1.0 - sig)))
        dx = jnp.zeros_like(x)
        for k in range(SSM_CONV):
            dx = dx + w_ref[k:k + 1, :] * _shift_up(dpre, SSM_CONV - 1 - k, rows, seq)
        dx_ref[...] = dx.astype(dx_ref.dtype)
        dw = jnp.concatenate([_colsum(dpre * xsh[k]) for k in range(SSM_CONV)], axis=0)
        db = _colsum(dpre)

        @pl.when(bb == 0)
        def _():
            dw_ref[...] = dw
            db_ref[...] = db

        @pl.when(bb > 0)
        def _():
            dw_ref[...] += dw
            db_ref[...] += db

    return pl.pallas_call(
        body, name=name, grid=(nc, bsz),
        in_specs=[pl.BlockSpec((seq, CONV_CB), lambda j, bb: (bb, off + j)),
                  pl.BlockSpec((SSM_CONV, CONV_CB), lambda j, bb: (0, j)),
                  pl.BlockSpec((1, CONV_CB), lambda j, bb: (0, j)),
                  pl.BlockSpec((seq, CONV_CB), lambda j, bb: (bb, j)),
                  pl.BlockSpec((seq, CONV_CB), lambda j, bb: (bb, jnp.minimum(j, nxs - 1)))],
        out_specs=[pl.BlockSpec((seq, CONV_CB), lambda j, bb: (bb, j)),
                   pl.BlockSpec((SSM_CONV, CONV_CB), lambda j, bb: (0, j)),
                   pl.BlockSpec((1, CONV_CB), lambda j, bb: (0, j))],
        out_shape=[jax.ShapeDtypeStruct((bsz * seq, CONV_DIM), bf16), jax.ShapeDtypeStruct((SSM_CONV, CONV_DIM), f32),
                   jax.ShapeDtypeStruct((1, CONV_DIM), f32)],
        compiler_params=_cparams(("parallel", "arbitrary")),
    )(pm, w, b, dact, dskipx)


CL = SSM_CHUNK


def _dot3(a, b, split_a):
    x = a if split_a else b
    t1 = x.astype(bf16)
    r1 = x - t1.astype(f32)
    t2 = r1.astype(bf16)
    t3 = (r1 - t2.astype(f32)).astype(bf16)
    if split_a:
        return _dot(t1, b) + _dot(t2, b) + _dot(t3, b)
    return _dot(a, t1) + _dot(a, t2) + _dot(a, t3)


def _ssd_specs(bsz, seq, rev):
    nch = seq // CL

    def ch(c):
        return (nch - 1 - c) if rev else c

    xg = pl.BlockSpec((CL, GW), lambda b, g, c: (b * nch + ch(c), g))
    lane128 = pl.BlockSpec((CL, 128), lambda b, g, c: (b * nch + ch(c), g))
    adt_t = pl.BlockSpec((128, CL), lambda b, g, c: (g, b * nch + ch(c)))
    bspec = pl.BlockSpec((CL, 128), lambda b, g, c: (b * nch + ch(c), SSM_INNER // 128 + g))
    cspec = pl.BlockSpec((CL, 128), lambda b, g, c: (b * nch + ch(c), SSM_INNER // 128 + SSM_GROUPS + g))
    st = pl.BlockSpec((1, 1, 1, SSM_STATE, GW), lambda b, g, c: (b, ch(c), g, 0, 0))
    return nch, xg, lane128, adt_t, bspec, cspec, st


def _expand_mat(width):
    r = lax.broadcasted_iota(jnp.int32, (128, HPG * width), 0)
    c = lax.broadcasted_iota(jnp.int32, (128, HPG * width), 1)
    return jnp.where((c >= r * width) & (c < (r + 1) * width), 1.0, 0.0).astype(bf16)


def _head_sums(z, e):
    hi = z.astype(bf16)
    lo = (z - hi.astype(f32)).astype(bf16)
    return _dot_nt(hi, e) + _dot_nt(lo, e)


def _ssd_common(dt_ref, adt_ref, adtt_ref):
    e64, e128 = _expand_mat(SSM_HD), _expand_mat(CL)
    csc = _dot3(_tri(CL, "ge"), adt_ref[...], False)
    csr = _dot3(adtt_ref[0:HPG, :], _tri(CL, "le"), True)
    return e64, csc, csr, _dot3(dt_ref[...], e64, True), _dot3(csc, e64, True), _dot3(csc, e128, True)


def ssd_fwd(name, xbc, dt, adt, adt_t, bsz, seq):
    nch, xg, lane128, adt_t_spec, bspec, cspec, st = _ssd_specs(bsz, seq, False)

    def body(x_ref, dt_ref, adt_ref, adtt_ref, b_ref, c_ref, y_ref, st_ref, s_scr, xd_scr):
        @pl.when(pl.program_id(2) == 0)
        def _():
            s_scr[...] = jnp.zeros_like(s_scr)

        _, _, csr, dt_e, cs_e, cs_b = _ssd_common(dt_ref, adt_ref, adtt_ref)
        cs_last = cs_e[CL - 1:CL, :]
        bm, cm = b_ref[...].astype(bf16), c_ref[...].astype(bf16)
        s_in = s_scr[...]
        st_ref[0, 0, 0] = s_in
        xd = x_ref[...] * dt_e
        xd_scr[...] = xd.astype(bf16)
        y_ref[...] = _dot(cm, s_in.astype(bf16)) * jnp.exp(cs_e)
        w = xd * jnp.exp(cs_last - cs_e)
        s_scr[...] = s_in * jnp.exp(cs_last) + _dot_tn(bm, w.astype(bf16))
        cb = _dot_nt(cm, bm)
        row = lax.broadcasted_iota(jnp.int32, (CL, CL), 0)
        col = lax.broadcasted_iota(jnp.int32, (CL, CL), 1)
        for h in range(HPG):
            hs = slice(h * SSM_HD, (h + 1) * SSM_HD)
            decay = jnp.exp(jnp.where(row >= col, cs_b[:, h * CL:(h + 1) * CL] - csr[h:h + 1, :], -1e30))
            y_ref[:, hs] += _dot((cb * decay).astype(bf16), xd_scr[:, hs])

    return pl.pallas_call(
        body, name=name, grid=(bsz, SSM_GROUPS, nch),
        in_specs=[xg, lane128, lane128, adt_t_spec, bspec, cspec],
        out_specs=[xg, st],
        out_shape=[jax.ShapeDtypeStruct((bsz * seq, SSM_INNER), f32),
                   jax.ShapeDtypeStruct((bsz, nch, SSM_GROUPS, SSM_STATE, GW), f32)],
        scratch_shapes=[pltpu.VMEM((SSM_STATE, GW), f32), pltpu.VMEM((CL, GW), bf16)],
        compiler_params=_cparams(("parallel", "parallel", "arbitrary")),
    )(xbc, dt, adt, adt_t, xbc, xbc)


def ssd_bwd(name, xbc, dt, adt, adt_t, states, dy, bsz, seq):
    nch, xg, lane128, adt_t_spec, bspec, cspec, st = _ssd_specs(bsz, seq, True)

    def body(x_ref, dt_ref, adt_ref, adtt_ref, b_ref, c_ref, st_ref, dy_ref,
             dx_ref, db_ref, dc_ref, ddt_ref, dac_ref, dar_ref, ds_scr, xd_scr, dxd_scr):
        @pl.when(pl.program_id(2) == 0)
        def _():
            ds_scr[...] = jnp.zeros_like(ds_scr)

        e64, _, csr, dt_e, cs_e, cs_b = _ssd_common(dt_ref, adt_ref, adtt_ref)
        cs_last = cs_e[CL - 1:CL, :]
        bm, cm = b_ref[...].astype(bf16), c_ref[...].astype(bf16)
        x, dy, s_in, ds_out = x_ref[...], dy_ref[...], st_ref[0, 0, 0], ds_scr[...]
        e_last = jnp.exp(cs_last)
        d_end = jnp.exp(cs_last - cs_e)
        xd = x * dt_e
        xd_scr[...] = xd.astype(bf16)
        w = xd * d_end
        dq = dy * jnp.exp(cs_e)
        dc = _dot_nt(dq.astype(bf16), s_in.astype(bf16))
        ds_scr[...] = _dot_tn(cm, dq.astype(bf16)) + ds_out * e_last
        dw = _dot(bm, ds_out.astype(bf16))
        db = _dot_nt(w.astype(bf16), ds_out.astype(bf16))
        rw = dw * w
        dcs_e = dq * _dot(cm, s_in.astype(bf16)) - rw
        dcs_last = _colsum(rw) + _colsum(ds_out * s_in) * e_last
        is_last = lax.broadcasted_iota(jnp.int32, (CL, 1), 0) == CL - 1
        dcs_e = dcs_e + jnp.where(is_last, dcs_last, 0.0)
        dxd_scr[...] = dw * d_end
        cb, cbt = _dot_nt(cm, bm), _dot_nt(bm, cm)
        row = lax.broadcasted_iota(jnp.int32, (CL, CL), 0)
        col = lax.broadcasted_iota(jnp.int32, (CL, CL), 1)
        lane = lax.broadcasted_iota(jnp.int32, (CL, 128), 1)
        sub = lax.broadcasted_iota(jnp.int32, (HPG, CL), 0)
        dcb = jnp.zeros((CL, CL), f32)
        r_rows = jnp.zeros((CL, 128), f32)
        r_cols = jnp.zeros((HPG, CL), f32)
        for h in range(HPG):
            hs = slice(h * SSM_HD, (h + 1) * SSM_HD)
            diff = cs_b[:, h * CL:(h + 1) * CL] - csr[h:h + 1, :]
            decay = jnp.exp(jnp.where(row >= col, diff, -1e30))
            decay_t = jnp.exp(jnp.where(col >= row, -diff, -1e30))
            dy_h = dy_ref[:, hs].astype(bf16)
            dm = _dot_nt(dy_h, xd_scr[:, hs])
            dxd_scr[:, hs] += _dot((cbt * decay_t).astype(bf16), dy_h)
            r = dm * (cb * decay)
            dcb = dcb + dm * decay
            r_rows = r_rows + _dot2(r, jnp.where(lane == h, 1.0, 0.0).astype(bf16))
            r_cols = jnp.where(sub == h, _colsum(r), r_cols)
        dc_ref[...] = dc + _dot(dcb.astype(bf16), bm)
        db_ref[...] = db + _dot_tn(dcb.astype(bf16), cm)
        dxd = dxd_scr[...]
        dx_ref[...] = dxd * dt_e
        ddt_ref[...] = _head_sums(dxd * x, e64)
        dac_ref[...] = _dot3(_tri(CL, "le"), r_rows + _head_sums(dcs_e, e64), False)
        dar_ref[...] = jnp.zeros_like(dar_ref)
        dar_ref[0:HPG, :] = _dot3(-r_cols, _tri(CL, "ge"), True)

    t = bsz * seq
    return pl.pallas_call(
        body, name=name, grid=(bsz, SSM_GROUPS, nch),
        in_specs=[xg, lane128, lane128, adt_t_spec, bspec, cspec, st, xg],
        out_specs=[xg, lane128, lane128, lane128, lane128, adt_t_spec],
        out_shape=[jax.ShapeDtypeStruct((t, SSM_INNER), f32), jax.ShapeDtypeStruct((t, DT_W), f32),
                   jax.ShapeDtypeStruct((t, DT_W), f32), jax.ShapeDtypeStruct((t, DT_W), f32),
                   jax.ShapeDtypeStruct((t, DT_W), f32), jax.ShapeDtypeStruct((DT_W, t), f32)],
        scratch_shapes=[pltpu.VMEM((SSM_STATE, GW), f32), pltpu.VMEM((CL, GW), bf16), pltpu.VMEM((CL, GW), f32)],
        compiler_params=_cparams(("parallel", "parallel", "arbitrary")),
    )(xbc, dt, adt, adt_t, xbc, xbc, states, dy)


XA_BQ = 512


def _xattn(q, k, v):
    s = _dot_nt(q.astype(bf16), k.astype(bf16)) * (XA_HD ** -0.5)
    p = jnp.exp(s - jnp.max(s, axis=-1, keepdims=True))
    p = p / jnp.sum(p, axis=-1, keepdims=True)
    return _dot(p.astype(bf16), v.astype(bf16))


def xattn_fwd(name, q, kv, bsz, seq, mlen):
    bq = min(XA_BQ, seq)
    nq = seq // bq

    def body(q_ref, k_ref, v_ref, o_ref):
        o_ref[...] = _xattn(q_ref[...], k_ref[...], v_ref[...]).astype(o_ref.dtype)

    return pl.pallas_call(
        body, name=name, grid=(bsz, XA_HEADS, nq),
        in_specs=[pl.BlockSpec((bq, XA_HD), lambda b, h, i: (b * nq + i, h)),
                  pl.BlockSpec((mlen, XA_HD), lambda b, h, i: (b, h)),
                  pl.BlockSpec((mlen, XA_HD), lambda b, h, i: (b, XA_HEADS + h))],
        out_specs=pl.BlockSpec((bq, XA_HD), lambda b, h, i: (b * nq + i, h)),
        out_shape=jax.ShapeDtypeStruct((bsz * seq, D), bf16),
        compiler_params=_cparams(("parallel", "parallel", "parallel")),
    )(q, kv, kv)


def xattn_bwd(name, q, kv, do, bsz, seq, mlen):
    bq = min(XA_BQ, seq)
    nq = seq // bq

    def body(q_ref, k_ref, v_ref, do_ref, dq_ref, dk_ref, dv_ref):
        _, vjp = jax.vjp(_xattn, q_ref[...], k_ref[...], v_ref[...])
        dq, dk, dv = vjp(do_ref[...])
        dq_ref[...] = dq.astype(dq_ref.dtype)
        i = pl.program_id(2)

        @pl.when(i == 0)
        def _():
            dk_ref[...] = dk
            dv_ref[...] = dv

        @pl.when(i > 0)
        def _():
            dk_ref[...] += dk
            dv_ref[...] += dv

    kspec = pl.BlockSpec((mlen, XA_HD), lambda b, h, i: (b, h))
    vspec = pl.BlockSpec((mlen, XA_HD), lambda b, h, i: (b, XA_HEADS + h))
    qspec = pl.BlockSpec((bq, XA_HD), lambda b, h, i: (b * nq + i, h))
    return pl.pallas_call(
        body, name=name, grid=(bsz, XA_HEADS, nq),
        in_specs=[qspec, kspec, vspec, qspec],
        out_specs=[qspec, kspec, kspec],
        out_shape=[jax.ShapeDtypeStruct((bsz * seq, D), bf16), jax.ShapeDtypeStruct((bsz * mlen, D), f32),
                   jax.ShapeDtypeStruct((bsz * mlen, D), f32)],
        compiler_params=_cparams(("parallel", "parallel", "arbitrary")),
    )(q, kv, kv, do)


def _layer_fwd(l, x, mem, w, bsz, seq, mlen, exchange=None):
    n = f"l{l}_"
    sv = {"x0": x}
    sv["h1"] = h1 = rms_fwd(n + "rms_mix", x, w["g_pre_mix"])
    sv["pm"] = pm = _mm(n + "in_proj", h1, w["wm"], "nn", bf16)
    sv["pdt"] = pdt = _mm(n + "in_proj_dt", h1, w["wdt"], "nn")
    sv["o_att"], sv["tot"], *exchanged = sb_fwd(n + "sb_fwd", pm, bsz, seq, exchange)
    o_att = sv["o_att"]
    sv["xbc"] = xbc = conv_fwd(n + "conv_fwd", pm, w["conv_w"], w["conv_b"], bsz, seq)
    sv["dt"], sv["adt"] = dt, adt = dt_fwd(n + "dt_fwd", pdt, w["dt_bias"], w["a_log"])
    sv["adt_t"] = adt_t = adt.T
    sv["y_ssd"], sv["states"] = y_ssd, _ = ssd_fwd(n + "ssd_fwd", xbc, dt, adt, adt_t, bsz, seq)
    sv["o_ssm"] = o_ssm = gnorm_fwd(n + "gnorm_fwd", y_ssd, xbc, pm, w["d_skip"], w["g_ssm_norm"])
    sv["a"] = a = _mm(n + "br_att", o_att, w["w_br_att"], "nn")
    sv["s"] = s = _mm(n + "br_ssm", o_ssm, w["w_br_ssm"], "nn")
    sv["merged"] = merged = merge_fwd(n + "merge_fwd", pm, a, s)
    sv["u"] = u = _mm(n + "mix_out", merged, w["w_mix_out"], "nn")
    sv["x1"] = x1 = addnorm_fwd(n + "post_mix", x, u, w["g_post_mix"])
    sv["h2"] = h2 = rms_fwd(n + "rms_xa", x1, w["g_pre_xa"])
    sv["memn"] = memn = rms_fwd(n + "rms_mem", mem, w["g_mem"])
    sv["qx"] = qx = _mm(n + "xq", h2, w["w_xq"], "nn")
    sv["kv"] = kv = _mm(n + "xkv", memn, w["w_xkv"], "nn")
    sv["ox"] = ox = xattn_fwd(n + "xattn_fwd", qx, kv, bsz, seq, mlen)
    sv["yx"] = yx = _mm(n + "xo", ox, w["w_xo"], "nn")
    sv["x2"] = x2 = addnorm_fwd(n + "post_xa", x1, yx, w["g_post_xa"])
    sv["h3"] = h3 = rms_fwd(n + "rms_ffn", x2, w["g_pre_ffn"])
    sv["gu"] = gu = _mm(n + "gu", h3, w["w_gu"], "nn", bf16)
    sv["act"] = act = swiglu_fwd(n + "swiglu_fwd", gu)
    sv["d"] = d = _mm(n + "down", act, w["w_down"], "nn")
    x3 = addnorm_fwd(n + "post_ffn", x2, d, w["g_post_ffn"])
    return x3, sv, exchanged


def _layer_bwd(l, dx, mem, w, sv, bsz, seq, mlen, exchange=None):
    n = f"l{l}_b_"
    g = {}
    dd, g["g_post_ffn"] = addnorm_bwd(n + "post_ffn", sv["d"], w["g_post_ffn"], dx)
    g["w_down"] = _mm(n + "dw_down", sv["act"], dd, "tn", bf16)
    dact = _mm(n + "dact", dd, w["w_down"], "nt", bf16)
    dgu = swiglu_bwd(n + "swiglu", sv["gu"], dact)
    g["w_gu"] = _mm(n + "dw_gu", sv["h3"], dgu, "tn", bf16)
    dh3 = _mm(n + "dh3", dgu, w["w_gu"], "nt")
    dx, g["g_pre_ffn"] = rms_bwd(n + "rms_ffn", sv["x2"], w["g_pre_ffn"], [dh3], dx)
    dyx, g["g_post_xa"] = addnorm_bwd(n + "post_xa", sv["yx"], w["g_post_xa"], dx)
    g["w_xo"] = _mm(n + "dw_xo", sv["ox"], dyx, "tn", bf16)
    dox = _mm(n + "dox", dyx, w["w_xo"], "nt")
    dqx, dk, dv = xattn_bwd(n + "xattn", sv["qx"], sv["kv"], dox, bsz, seq, mlen)
    g["w_xq"] = _mm(n + "dw_xq", sv["h2"], dqx, "tn", bf16)
    dh2 = _mm(n + "dh2", dqx, w["w_xq"], "nt")
    dkv = jnp.concatenate([dk, dv], axis=1)
    g["w_xkv"] = _mm(n + "dw_xkv", sv["memn"], dkv, "tn", bf16)
    dmemn = _mm(n + "dmemn", dkv, w["w_xkv"], "nt")
    _, g["g_mem"] = rms_bwd(n + "rms_mem", mem, w["g_mem"], [dmemn])
    dx, g["g_pre_xa"] = rms_bwd(n + "rms_xa", sv["x1"], w["g_pre_xa"], [dh2], dx)
    du, g["g_post_mix"] = addnorm_bwd(n + "post_mix", sv["u"], w["g_post_mix"], dx)
    g["w_mix_out"] = _mm(n + "dw_mix", sv["merged"], du, "tn", bf16)
    dmerged = _mm(n + "dmerged", du, w["w_mix_out"], "nt")
    dgates, da, ds = merge_bwd(n + "merge", sv["pm"], sv["a"], sv["s"], dmerged)
    g["w_br_att"] = _mm(n + "dw_att", sv["o_att"], da, "tn", bf16)
    do_att = _mm(n + "do_att", da, w["w_br_att"], "nt")
    g["w_br_ssm"] = _mm(n + "dw_ssm", sv["o_ssm"], ds, "tn", bf16)
    do_ssm = _mm(n + "do_ssm", ds, w["w_br_ssm"], "nt")
    dy_ssd, dxs_skip, dz, g["d_skip"], g["g_ssm_norm"] = gnorm_bwd(
        n + "gnorm", sv["y_ssd"], sv["xbc"], sv["pm"], w["d_skip"], w["g_ssm_norm"], do_ssm)
    dxs, dbm, dcm, ddt, dadt_c, dadt_r = ssd_bwd(n + "ssd", sv["xbc"], sv["dt"], sv["adt"], sv["adt_t"],
                                                   sv["states"], dy_ssd, bsz, seq)
    dxbc_act = jnp.concatenate([dxs, dbm, dcm], axis=1)
    dxbc, g["conv_w"], g["conv_b"] = conv_bwd(n + "conv", sv["pm"], w["conv_w"], w["conv_b"], dxbc_act, dxs_skip, bsz, seq)
    dpdt, g["dt_bias"], g["a_log"] = dt_bwd(n + "dt", sv["pdt"], w["dt_bias"], w["a_log"], ddt, dadt_c, dadt_r.T)
    dq, dk_, dv_, *exchanged = sb_bwd(n + "sb", sv["pm"], sv["tot"], do_att, bsz, seq, exchange)
    dpm = jnp.concatenate([dz, dxbc, dq, dk_, dv_, dgates], axis=1)
    g["wm"] = _mm(n + "dw_in", sv["h1"], dpm, "tn", bf16)
    g["wdt"] = _mm(n + "dw_in_dt", sv["h1"], dpdt, "tn", bf16)
    dh1 = _mm(n + "dh1", dpm, w["wm"], "nt")
    dh1_dt = _mm(n + "dh1_dt", dpdt, w["wdt"], "nt")
    dx, g["g_pre_mix"] = rms_bwd(n + "rms_mix", sv["x0"], w["g_pre_mix"], [dh1, dh1_dt], dx)
    return dx, g, exchanged


def _group_pad(v):
    lead = v.shape[:-1]
    v = v.reshape(*lead, SSM_GROUPS, HPG)
    return jnp.pad(v, [(0, 0)] * (len(lead) + 1) + [(0, 128 - HPG)]).reshape(*lead, DT_W)


def _group_unpad(v):
    lead = v.shape[:-1]
    return v.reshape(*lead, SSM_GROUPS, 128)[..., :HPG].reshape(*lead, SSM_HEADS)


BIG = ("w_in", "w_br_att", "w_br_ssm", "w_mix_out", "w_xq", "w_xkv", "w_xo", "w_gu", "w_down")
GAINS = ("g_pre_mix", "g_post_mix", "g_pre_xa", "g_mem", "g_post_xa", "g_pre_ffn", "g_post_ffn")
HEAD_VECS = ("dt_bias", "a_log", "d_skip")
SMALL = GAINS + ("conv_w", "conv_b", "g_ssm_norm") + HEAD_VECS


def _prep_layer(p):
    w = {k: p[k] for k in BIG[1:]}
    if "wm" in p:
        w["wm"], w["wdt"] = p["wm"], p["wdt"]
    else:
        w_in = p["w_in"]
        w["wm"] = jnp.concatenate([w_in[:, 3072:8192], w_in[:, 0:3072], w_in[:, 8224:10272]], axis=1)
        w["wdt"] = _group_pad(w_in[:, 8192:8224])
    for k in GAINS + ("conv_b", "g_ssm_norm"):
        w[k] = p[k].reshape(1, -1)
    w["conv_w"] = p["conv_w"]
    w["dt_bias"] = _group_pad(p["dt_bias"]).reshape(1, DT_W)
    w["a_log"] = _group_pad(p["a_log"]).reshape(1, DT_W)
    w["d_skip"] = jnp.repeat(p["d_skip"], SSM_HD).reshape(1, SSM_INNER)
    return w


def _unprep_grads(g):
    out = {k: g[k] for k in BIG[1:]}
    gm = g["wm"]
    out["w_in"] = jnp.concatenate([gm[:, 5120:8192], gm[:, 0:5120], _group_unpad(g["wdt"]), gm[:, 8192:10240]], axis=1)
    for k in GAINS + ("conv_b", "g_ssm_norm"):
        out[k] = g[k].reshape(-1)
    out["conv_w"] = g["conv_w"]
    out["dt_bias"] = _group_unpad(g["dt_bias"]).reshape(-1)
    out["a_log"] = _group_unpad(g["a_log"]).reshape(-1)
    out["d_skip"] = g["d_skip"].reshape(SSM_HEADS, SSM_HD).sum(axis=1)
    return out


def _local_step(x, mem, target, ws, bsz, seq, mlen):
    saved = []
    for l in range(len(ws)):
        x, sv, _ = _layer_fwd(l, x, mem, ws[l], bsz, seq, mlen)
        saved.append(sv)
    dx, loss_lanes = loss_fwd_bwd("loss", x, target)
    grads = [None] * len(ws)
    for l in reversed(range(len(ws))):
        dx, grads[l], _ = _layer_bwd(l, dx, mem, ws[l], saved[l], bsz, seq, mlen)
    return loss_lanes, dx, grads


HBM_SPEC = pl.BlockSpec(memory_space=pltpu.HBM)
FLIP_C = (0, 0, 1)
FLIPS_CHIP = ((1, 0, 0), (0, 1, 0), (1, 1, 0))
FLIPS_ALL = tuple(((f >> 2) & 1, (f >> 1) & 1, f & 1) for f in range(1, 8))


def _view(ref, index):
    return ref.at[index] if index != () else ref


def _exchange(name, srcs, out_shapes, transfers, in_place=False):
    na = len(srcs)

    def body(*refs):
        out_refs = refs[na:2 * na]
        copies = _remote_copies(out_refs if in_place else refs[:na], out_refs, transfers, *refs[2 * na:])
        for cp in copies:
            cp.start()
        for cp in copies:
            cp.wait()

    if in_place:
        out_shape = [jax.ShapeDtypeStruct(s.shape, s.dtype) for s in srcs]
    else:
        out_shape = [jax.ShapeDtypeStruct(shape, dtype) for shape, dtype in out_shapes]
    return pl.pallas_call(
        body, name=name, out_shape=out_shape, in_specs=[HBM_SPEC] * na, out_specs=[HBM_SPEC] * na,
        input_output_aliases={a: a for a in range(na)} if in_place else {},
        scratch_shapes=_copy_semaphores(na, transfers),
    )(*srcs)


def _copy_semaphores(na, transfers):
    return [pltpu.SemaphoreType.DMA((na * len(transfers),)), pltpu.SemaphoreType.DMA((na * len(transfers),))]


def _remote_copies(src_refs, out_refs, transfers, send_sems, recv_sems):
    pos = (lax.axis_index("x"), lax.axis_index("y"), lax.axis_index("c"))
    nt, copies = len(transfers), []
    for a, (src_ref, out_ref) in enumerate(zip(src_refs, out_refs)):
        for t, (flip, src_index, dst_index) in enumerate(transfers):
            assert any(flip)
            peer = tuple(1 - p if f else p for p, f in zip(pos, flip))
            copies.append(pltpu.make_async_remote_copy(
                src_ref=_view(src_ref, src_index(*pos)), dst_ref=_view(out_ref, dst_index(*pos)),
                send_sem=send_sems.at[a * nt + t], recv_sem=recv_sems.at[a * nt + t],
                device_id=peer, device_id_type=MESH))
    return copies


def _exchange_hook(exchange, refs_in, refs_out, sems, first, last):
    copies = _remote_copies(refs_in, refs_out, exchange[2], *sems)

    @pl.when(first)
    def _():
        for cp in copies:
            cp.start()

    @pl.when(last)
    def _():
        for cp in copies:
            cp.wait()


def _at(*index):
    return lambda x, y, c: index


def _allgather8(name, v, me):
    got = _exchange(name, [v], [((7,) + v.shape, v.dtype)], [(fl, _at(), _at(j)) for j, fl in enumerate(FLIPS_ALL)])[0]
    rel = jnp.concatenate([v[None], got], axis=0)
    return jnp.stack([lax.dynamic_index_in_dim(rel, k ^ me, 0, keepdims=False) for k in range(8)])


def _sum8(name, parts):
    def fn(*p):
        acc = p[0]
        for q in p[1:]:
            acc = acc + q
        return acc

    return _rowwise(name, fn, [(parts[k], 0) for k in range(8)], [], [(1, f32)], [], width=128, bt=parts.shape[1])[0]


def _rows_block(r, w, bytes_per_row_elem):
    for bt in (512, 256, 128, 64, 32, 16, 8):
        if r % bt == 0 and bt * w * bytes_per_row_elem * 2 <= 16 * 1024 * 1024:
            return bt
    raise ValueError((r, w))


def _reduce8(name, wire, recv, shard, ci):
    _, _, h, w = wire.shape
    bt = _rows_block(h, w, 2 + 7 * 2 + 4)

    def body(s_ref, a_ref, b_ref, o_ref):
        acc = a_ref[0, 0].astype(f32)
        for j in range(7):
            acc = acc + b_ref[j].astype(f32)
        o_ref[0] = acc

    return pl.pallas_call(
        body, name=name,
        grid_spec=pltpu.PrefetchScalarGridSpec(
            num_scalar_prefetch=1, grid=(h // bt,),
            in_specs=[pl.BlockSpec((1, 1, bt, w), lambda i, s_ref: (s_ref[0], s_ref[1], i, 0)),
                      pl.BlockSpec((7, bt, w), lambda i, s_ref: (0, i, 0))],
            out_specs=pl.BlockSpec((1, bt, w), lambda i, s_ref: (s_ref[1], i, 0))),
        out_shape=jax.ShapeDtypeStruct((2, h, w), f32),
        compiler_params=_cparams(("parallel",)),
    )(jnp.stack([shard, ci]).astype(jnp.int32), wire, recv)


COL_SHARDED = ("w_in", "w_xkv", "w_gu")


def _ref_cols(pieces, lo, hi):
    c, out = pieces[0].shape[1], []
    for s, p in enumerate(pieces):
        a0, a1 = max(lo, s * c), min(hi, (s + 1) * c)
        if a0 < a1:
            out.append(p[:, a0 - s * c:a1 - s * c])
    return out


def _my_cols(gm, g32, lo, hi):
    out = []
    for r0, r1, src, shift in ((0, 3072, gm, 5120), (3072, 8192, gm, -3072), (8192, 8224, g32, -8192), (8224, IN_WIDTH, gm, -32)):
        a0, a1 = max(lo, r0), min(hi, r1)
        if a0 < a1:
            out.append(src[:, a0 + shift:a1 + shift])
    return out


def _pack(arrs, rows_multiple=8):
    flat = jnp.concatenate([a.reshape(-1) for a in arrs])
    pad = (-flat.shape[0]) % (128 * rows_multiple)
    return jnp.pad(flat, (0, pad)).reshape(-1, 128)


def _unpack(buf, shapes):
    flat, out, o = buf.reshape(-1), [], 0
    for s in shapes:
        n = math.prod(s)
        out.append(flat[o:o + n].reshape(s))
        o += n
    return out


def kernel(x, mem, g_pre_mix, w_in, conv_w, conv_b, dt_bias, a_log, d_skip, g_ssm_norm, w_br_att, w_br_ssm, w_mix_out, g_post_mix, g_pre_xa, g_mem, w_xq, w_xkv, w_xo, g_post_xa, g_pre_ffn, w_gu, w_down, g_post_ffn, loss_target, m_g_pre_mix, m_w_in, m_conv_w, m_conv_b, m_dt_bias, m_a_log, m_d_skip, m_g_ssm_norm, m_w_br_att, m_w_br_ssm, m_w_mix_out, m_g_post_mix, m_g_pre_xa, m_g_mem, m_w_xq, m_w_xkv, m_w_xo, m_g_post_xa, m_g_pre_ffn, m_w_gu, m_w_down, m_g_post_ffn, v_g_pre_mix, v_w_in, v_conv_w, v_conv_b, v_dt_bias, v_a_log, v_d_skip, v_g_ssm_norm, v_w_br_att, v_w_br_ssm, v_w_mix_out, v_g_post_mix, v_g_pre_xa, v_g_mem, v_w_xq, v_w_xkv, v_w_xo, v_g_post_xa, v_g_pre_ffn, v_w_gu, v_w_down, v_g_post_ffn):
    a = dict(locals())
    names = ("g_pre_mix", "w_in", "conv_w", "conv_b", "dt_bias", "a_log", "d_skip", "g_ssm_norm", "w_br_att", "w_br_ssm",
             "w_mix_out", "g_post_mix", "g_pre_xa", "g_mem", "w_xq", "w_xkv", "w_xo", "g_post_xa", "g_pre_ffn", "w_gu",
             "w_down", "g_post_ffn")
    depth = w_in.shape[0]
    bsz, seq, _ = x.shape
    mlen = mem.shape[1]
    xi, yi, ci = lax.axis_index("x"), lax.axis_index("y"), lax.axis_index("c")
    shard = 2 * xi + yi
    me = 2 * shard + ci

    cw_all = _allgather8("ag_conv_w", _pack([conv_w]), me)
    cw_shape = conv_w.shape
    conv_w_full = jnp.concatenate([_unpack(cw_all[2 * s], [cw_shape])[0] for s in range(4)], axis=2)

    halves = {k: (a[k].shape[1] // 2, a[k].shape[2]) for k in BIG}
    wbf = {k: a[k].astype(bf16) for k in BIG}
    ag_shapes = [((4, 2) + halves[k], bf16) for k in BIG]
    ag_transfers = [(fl, lambda x_, y_, c_: (c_,), lambda x_, y_, c_: (2 * x_ + y_, c_)) for fl in FLIPS_CHIP]
    fetched = [functools.partial(lambda x_, y_, c_, f: ((2 * x_ + y_) ^ f, c_), f=2 * fl[0] + fl[1]) for fl in FLIPS_CHIP]

    def ag_sources(l):
        return [wbf[k][l].reshape((2,) + halves[k]) for k in BIG]

    def layer_weights(l, got):
        got = [lax.dynamic_update_slice(g, s[None], (shard, 0, 0, 0)) for g, s in zip(got, ag_sources(l))]
        got = _exchange(f"ag_d2d_l{l}", got, None, [(FLIP_C, fn, fn) for fn in fetched], in_place=True)
        full = {k: g.reshape(4, 2 * halves[k][0], halves[k][1]) for k, g in zip(BIG, got)}
        p = {}
        for k in BIG[1:]:
            sh = full[k]
            p[k] = sh.transpose(1, 0, 2).reshape(sh.shape[1], -1) if k in COL_SHARDED else sh.reshape(-1, sh.shape[2])
        pieces = [full["w_in"][s] for s in range(4)]
        p["wm"] = jnp.concatenate(_ref_cols(pieces, 3072, 8192) + _ref_cols(pieces, 0, 3072)
                                  + _ref_cols(pieces, 8224, 10272), axis=1)
        p["wdt"] = _group_pad(jnp.concatenate(_ref_cols(pieces, 8192, 8224), axis=1))
        for k in SMALL:
            p[k] = conv_w_full[l] if k == "conv_w" else a[k][l]
        return _prep_layer(p)

    rs_shapes = [((7,) + halves[k], bf16) for k in BIG]
    rs_transfers = [(fl, functools.partial(lambda x_, y_, c_, fs, fc: ((2 * x_ + y_) ^ fs, c_ ^ fc), fs=2 * fl[0] + fl[1], fc=fl[2]),
                     _at(j)) for j, fl in enumerate(FLIPS_ALL)]

    def rs_sources(g):
        out = []
        for k in BIG:
            r, c = a[k].shape[1:]
            if k == "w_in":
                g32 = _group_unpad(g["wdt"])
                gk = jnp.stack([jnp.concatenate(_my_cols(g["wm"], g32, s * c, (s + 1) * c), axis=1) for s in range(4)])
            elif k in COL_SHARDED:
                gk = g[k].reshape(r, 4, c).transpose(1, 0, 2)
            else:
                gk = g[k]
            out.append(gk.astype(bf16).reshape((4, 2) + halves[k]))
        return out

    def layer_grads(l, wires, got):
        red = [_reduce8(f"rs_sum_l{l}_{k}", w, r_, shard, ci) for k, w, r_ in zip(BIG, wires, got)]
        my_half = lambda x_, y_, c_: (c_,)
        red = _exchange(f"rs_swap_l{l}", red, None, [(FLIP_C, my_half, my_half)], in_place=True)
        return {k: r_.reshape(a[k].shape[1:]) for k, r_ in zip(BIG, red)}

    assert depth >= 2
    xt, memt = x.reshape(bsz * seq, D), mem.reshape(bsz * mlen, D)
    ws, saved = [], []
    got = _exchange("ag_ici_l0", ag_sources(0), ag_shapes, ag_transfers)
    for l in range(depth):
        ws.append(layer_weights(l, got))
        nxt = (ag_sources(l + 1), ag_shapes, ag_transfers) if l + 1 < depth else None
        xt, sv, got = _layer_fwd(l, xt, memt, ws[l], bsz, seq, mlen, nxt)
        saved.append(sv)
    gx, loss_lanes = loss_fwd_bwd("loss", xt, loss_target.reshape(bsz * seq, D))
    grads, gshard = [None] * depth, [None] * depth
    wires = None
    for l in reversed(range(depth)):
        riding = (wires, rs_shapes, rs_transfers) if wires is not None else None
        gx, grads[l], got = _layer_bwd(l, gx, memt, ws[l], saved[l], bsz, seq, mlen, riding)
        if wires is not None:
            gshard[l + 1] = layer_grads(l + 1, wires, got)
        wires = rs_sources(grads[l])
    gshard[0] = layer_grads(0, wires, _exchange("rs_all_l0", wires, rs_shapes, rs_transfers))
    grads = [_unprep_grads(g) for g in grads]

    out_g, out_d, out_m, out_v = {}, {}, {}, {}
    for k in BIG:
        shp = a[k].shape
        g = jnp.stack([gshard[l][k] for l in range(depth)])
        two_d = (shp[0] * shp[1], shp[2])
        d_, m_, v_ = adamw("adamw_" + k, a[k].reshape(two_d), g.reshape(two_d), a["m_" + k].reshape(two_d), a["v_" + k].reshape(two_d))
        out_g[k], out_d[k], out_m[k], out_v[k] = g, d_.reshape(shp), m_.reshape(shp), v_.reshape(shp)

    small_shapes = [(depth,) + (conv_w_full.shape[1:] if k == "conv_w" else a[k].shape[1:]) for k in SMALL]
    small = _pack([jnp.stack([grads[l][k] for l in range(depth)]) for k in SMALL] + [loss_lanes])
    total = _sum8("small_sum", _allgather8("ag_small", small, me))
    *gsmall, loss_l = _unpack(total, small_shapes + [loss_lanes.shape])
    gsmall = dict(zip(SMALL, gsmall))
    gsmall["conv_w"] = lax.dynamic_slice_in_dim(gsmall["conv_w"], shard * cw_shape[2], cw_shape[2], axis=2)
    loc_shapes = [a[k].shape for k in SMALL]
    d_, m_, v_ = adamw("adamw_small", _pack([a[k] for k in SMALL]), _pack([gsmall[k] for k in SMALL]),
                       _pack([a["m_" + k] for k in SMALL]), _pack([a["v_" + k] for k in SMALL]))
    for k, dd, mm, vv in zip(SMALL, _unpack(d_, loc_shapes), _unpack(m_, loc_shapes), _unpack(v_, loc_shapes)):
        out_g[k], out_d[k], out_m[k], out_v[k] = gsmall[k], dd, mm, vv

    loss = jnp.sum(loss_l)
    return (loss, gx.reshape(x.shape), *[out_g[k] for k in names], *[out_d[k] for k in names],
            *[out_m[k] for k in names], *[out_v[k] for k in names])
```

```python
import functools
import math

import jax
import jax.numpy as jnp
from jax import lax
from jax.experimental import pallas as pl
from jax.experimental.pallas import tpu as pltpu

f32, bf16 = jnp.float32, jnp.bfloat16

DEPTH = 4
D = 1024
SB_HEADS, SB_HD = 16, 64
SSM_INNER, SSM_HD, SSM_HEADS, SSM_GROUPS, SSM_STATE, SSM_CONV, SSM_CHUNK = 2048, 64, 32, 4, 128, 4, 128
HPG = SSM_HEADS // SSM_GROUPS
CONV_DIM = SSM_INNER + 2 * SSM_GROUPS * SSM_STATE
XA_HEADS, XA_HD = 4, 256
FFN = 2816
IN_WIDTH = 10272
RMS_EPS = 1e-6
LR, B1, B2, EPS, WD, STEP = 0.001, 0.9, 0.999, 1e-08, 0.01, 10

PM_W = 10240
OFF_Z, OFF_XBC, OFF_Q, OFF_K, OFF_V, OFF_GA, OFF_GS = 0, 2048, 5120, 6144, 7168, 8192, 9216
DT_W = SSM_GROUPS * 128

VMEM_LIMIT = 48 * 1024 * 1024
MESH = pl.DeviceIdType.MESH


def _cparams(sem):
    return pltpu.CompilerParams(dimension_semantics=sem, vmem_limit_bytes=VMEM_LIMIT)


def _tile(n):
    for t in (512, 256, 128):
        if n % t == 0:
            return t
    raise ValueError(f"dimension {n} is not a multiple of 128")


MM_VMEM_BUDGET = 34 * 1024 * 1024


def _mm_tiles(m, n, k, sa, sb, so):
    best = None
    for tm in (2048, 1024, 512, 256, 128):
        if m % tm:
            continue
        for tn in (2048, 1024, 512, 256, 128):
            if n % tn:
                continue
            for tk in (k, 2048, 1024, 512):
                if tk > k or k % tk:
                    continue
                vmem = 2 * (tm * tk * sa + tk * tn * sb + tm * tn * so) + tm * tn * 4 * (2 if tk < k else 1)
                if vmem > MM_VMEM_BUDGET:
                    continue
                traffic = m * k * sa * (n // tn) + k * n * sb * (m // tm) + m * n * so
                steps = (m // tm) * (n // tn) * (k // tk)
                accumulate = (k // tk > 1) * (k // tk) * m * n * 2
                key = (traffic + steps * 800_000 + accumulate, steps)
                if best is None or key < best[0]:
                    best = (key, (tm, tn, tk))
    assert best is not None, (m, n, k)
    return best[1]


def _mm(name, a, b, mode, out_dtype=f32):
    if mode == "nn":
        (m, k), (k2, n) = a.shape, b.shape
    elif mode == "nt":
        (m, k), (n, k2) = a.shape, b.shape
    else:
        (k, m), (k2, n) = a.shape, b.shape
    assert k == k2, (name, a.shape, b.shape, mode)
    tm, tn, tk = _mm_tiles(m, n, k, a.dtype.itemsize, b.dtype.itemsize, jnp.dtype(out_dtype).itemsize)
    nk = k // tk
    dn = {"nn": (((1,), (0,)), ((), ())), "nt": (((1,), (1,)), ((), ())), "tn": (((0,), (0,)), ((), ()))}[mode]

    def product(a_ref, b_ref):
        return lax.dot_general(a_ref[...].astype(bf16), b_ref[...].astype(bf16), dn, preferred_element_type=f32)

    def body_whole_k(a_ref, b_ref, o_ref):
        o_ref[...] = product(a_ref, b_ref).astype(o_ref.dtype)

    def body_k_loop(a_ref, b_ref, o_ref, acc_ref):
        kk = pl.program_id(2)

        @pl.when(kk == 0)
        def _():
            acc_ref[...] = product(a_ref, b_ref)

        @pl.when(kk > 0)
        def _():
            acc_ref[...] += product(a_ref, b_ref)

        @pl.when(kk == nk - 1)
        def _():
            o_ref[...] = acc_ref[...].astype(o_ref.dtype)

    a_spec = pl.BlockSpec((tk, tm), lambda i, j, kk: (kk, i)) if mode == "tn" else pl.BlockSpec((tm, tk), lambda i, j, kk: (i, kk))
    b_spec = pl.BlockSpec((tn, tk), lambda i, j, kk: (j, kk)) if mode == "nt" else pl.BlockSpec((tk, tn), lambda i, j, kk: (kk, j))
    return pl.pallas_call(
        body_whole_k if nk == 1 else body_k_loop, name=name, grid=(m // tm, n // tn, nk),
        in_specs=[a_spec, b_spec],
        out_specs=pl.BlockSpec((tm, tn), lambda i, j, kk: (i, j)),
        out_shape=jax.ShapeDtypeStruct((m, n), out_dtype),
        scratch_shapes=[] if nk == 1 else [pltpu.VMEM((tm, tn), f32)],
        compiler_params=_cparams(("parallel", "parallel", "arbitrary")),
    )(a, b)


def _rowwise(name, fn, rows, consts, out_rows, out_accs, *, width, ncol=1, bt=256):
    r = rows[0][0].shape[0]
    bt = min(bt, r)
    assert r % bt == 0, (name, r, bt)
    nrow = r // bt
    n_in = len(rows) + len(consts)
    n_or = len(out_rows)

    def body(*refs):
        ins = [ref[...].astype(f32) for ref in refs[:n_in]]
        outs = fn(*ins)
        if not isinstance(outs, (tuple, list)):
            outs = (outs,)
        o_refs = refs[n_in:]
        for o_ref, val in zip(o_refs[:n_or], outs[:n_or]):
            o_ref[...] = val.astype(o_ref.dtype)
        if out_accs:
            i = pl.program_id(1)
            for o_ref, val in zip(o_refs[n_or:], outs[n_or:]):
                @pl.when(i == 0)
                def _(o_ref=o_ref, val=val):
                    o_ref[...] = val

                @pl.when(i > 0)
                def _(o_ref=o_ref, val=val):
                    o_ref[...] += val

    in_specs = [pl.BlockSpec((bt, width), functools.partial(lambda j, i, off: (i, off + j), off=off)) for _, off in rows]
    in_specs += [pl.BlockSpec((c.shape[0], width), functools.partial(lambda j, i, off: (0, off + j), off=off)) for c, off in consts]
    out_specs = [pl.BlockSpec((bt, mlt * width), lambda j, i: (i, j)) for mlt, _ in out_rows]
    out_specs += [pl.BlockSpec((k, width), lambda j, i: (0, j)) for k in out_accs]
    out_shape = [jax.ShapeDtypeStruct((r, ncol * mlt * width), dt) for mlt, dt in out_rows]
    out_shape += [jax.ShapeDtypeStruct((k, ncol * width), f32) for k in out_accs]
    res = pl.pallas_call(
        body, name=name, grid=(ncol, nrow), in_specs=in_specs, out_specs=out_specs, out_shape=out_shape,
        compiler_params=_cparams(("parallel", "arbitrary" if out_accs else "parallel")),
    )(*[a for a, _ in rows], *[c for c, _ in consts])
    return res


def _rms(x, g):
    return x * lax.rsqrt(jnp.mean(x * x, axis=-1, keepdims=True) + RMS_EPS) * g


def _silu(x):
    return x * jax.nn.sigmoid(x)


def _softplus(x):
    return jnp.maximum(x, 0.0) + jnp.log(1.0 + jnp.exp(-jnp.abs(x)))


def _colsum(x):
    return jnp.sum(x, axis=0, keepdims=True)


def rms_fwd(name, x, g):
    return _rowwise(name, _rms, [(x, 0)], [(g, 0)], [(1, bf16)], [], width=D)[0]


def rms_bwd(name, x, g, dhs, dres=None):
    nd = len(dhs)

    def fn(x, *rest):
        dh = rest[0]
        for extra in rest[1:nd]:
            dh = dh + extra
        g = rest[-1]
        _, vjp = jax.vjp(_rms, x, g)
        dx, dg = vjp(dh.astype(f32))
        if dres is not None:
            dx = dx + rest[nd]
        return dx, dg

    rows = [(x, 0)] + [(d, 0) for d in dhs] + ([(dres, 0)] if dres is not None else [])
    return _rowwise(name, fn, rows, [(g, 0)], [(1, f32)], [1], width=D)


def addnorm_fwd(name, x, u, g):
    return _rowwise(name, lambda x, u, g: x + _rms(u, g), [(x, 0), (u, 0)], [(g, 0)], [(1, f32)], [], width=D)[0]


def addnorm_bwd(name, u, g, dx):
    def fn(u, dx, g):
        _, vjp = jax.vjp(_rms, u, g)
        return vjp(dx)

    return _rowwise(name, fn, [(u, 0), (dx, 0)], [(g, 0)], [(1, bf16)], [1], width=D)


def _merge(ga, gs, a, s):
    return jax.nn.sigmoid(ga) * a + jax.nn.sigmoid(gs) * s


def merge_fwd(name, pm, a, s):
    return _rowwise(name, _merge, [(pm, OFF_GA // D), (pm, OFF_GS // D), (a, 0), (s, 0)], [], [(1, bf16)], [], width=D)[0]


def merge_bwd(name, pm, a, s, dm):
    def fn(ga, gs, a, s, dm):
        _, vjp = jax.vjp(_merge, ga, gs, a, s)
        dga, dgs, da, ds = vjp(dm)
        return jnp.concatenate([dga, dgs], axis=1), da, ds

    return _rowwise(name, fn, [(pm, OFF_GA // D), (pm, OFF_GS // D), (a, 0), (s, 0), (dm, 0)], [],
                    [(2, bf16), (1, bf16), (1, bf16)], [], width=D)


def _swiglu(gate, up):
    return _silu(gate) * up


def swiglu_fwd(name, gu):
    return _rowwise(name, _swiglu, [(gu, 0), (gu, 1)], [], [(1, bf16)], [], width=FFN)[0]


def swiglu_bwd(name, gu, dact):
    def fn(gate, up, dact):
        _, vjp = jax.vjp(_swiglu, gate, up)
        dg, du = vjp(dact.astype(f32))
        return jnp.concatenate([dg, du], axis=1)

    return _rowwise(name, fn, [(gu, 0), (gu, 1), (dact, 0)], [], [(2, bf16)], [], width=FFN, bt=128)[0]


GW = SSM_INNER // SSM_GROUPS


def _gnorm(y, xs, z, dskip, gn):
    yy = (y + dskip * xs) * _silu(z)
    return yy * lax.rsqrt(jnp.mean(yy * yy, axis=-1, keepdims=True) + RMS_EPS) * gn


def gnorm_fwd(name, y, xbc, pm, dskip, gn):
    return _rowwise(name, _gnorm, [(y, 0), (xbc, 0), (pm, OFF_Z // GW)], [(dskip, 0), (gn, 0)], [(1, bf16)], [],
                    width=GW, ncol=SSM_GROUPS)[0]


def gnorm_bwd(name, y, xbc, pm, dskip, gn, do):
    def fn(y, xs, z, do, dskip, gn):
        _, vjp = jax.vjp(_gnorm, y, xs, z, dskip, gn)
        return vjp(do.astype(f32))

    return _rowwise(name, fn, [(y, 0), (xbc, 0), (pm, OFF_Z // GW), (do, 0)], [(dskip, 0), (gn, 0)],
                    [(1, f32), (1, f32), (1, bf16)], [1, 1], width=GW, ncol=SSM_GROUPS)


def _dtfn(pdt, bias, alog):
    dt = _softplus(pdt + bias)
    return dt, -jnp.exp(alog) * dt


def dt_fwd(name, pdt, bias, alog):
    return _rowwise(name, _dtfn, [(pdt, 0)], [(bias, 0), (alog, 0)], [(1, f32), (1, f32)], [], width=DT_W)


def dt_bwd(name, pdt, bias, alog, ddt, dadt_c, dadt_r):
    def fn(pdt, ddt, dac, dar, bias, alog):
        _, vjp = jax.vjp(_dtfn, pdt, bias, alog)
        return vjp((ddt, dac + dar))

    return _rowwise(name, fn, [(pdt, 0), (ddt, 0), (dadt_c, 0), (dadt_r, 0)], [(bias, 0), (alog, 0)],
                    [(1, f32)], [1, 1], width=DT_W)


def loss_fwd_bwd(name, y, target):
    def fn(y, t):
        e = y - t
        return e * (1.0 / D), _colsum(e * e) * (0.5 / D)

    return _rowwise(name, fn, [(y, 0), (target, 0)], [], [(1, f32)], [1], width=D)


def adamw(name, w, g, m, v):
    r, c = w.shape

    def fn(w, g, m, v):
        m = B1 * m + (1.0 - B1) * g
        v = B2 * v + (1.0 - B2) * (g * g)
        m_hat = m / (1.0 - B1 ** STEP)
        v_hat = v / (1.0 - B2 ** STEP)
        return -LR * (m_hat / (jnp.sqrt(v_hat) + EPS) + WD * w), m, v

    bt = 256
    while bt > 8 and (r % bt or bt * c * 4 * 7 * 2 > 16 * 1024 * 1024):
        bt //= 2
    if r % bt:
        bt = r
    return _rowwise(name, fn, [(w, 0), (g, 0), (m, 0), (v, 0)], [], [(1, f32)] * 3, [], width=c, bt=bt)


SB_BQ, SB_BK = 512, 256
SB_UNROLL = 2


def _dot(a, b):
    return jnp.dot(a, b, preferred_element_type=f32)


def _dot_nt(a, b):
    return lax.dot_general(a, b, (((1,), (1,)), ((), ())), preferred_element_type=f32)


def _dot_tn(a, b):
    return lax.dot_general(a, b, (((0,), (0,)), ((), ())), preferred_element_type=f32)


def _dot2(x, tri):
    hi = x.astype(bf16)
    lo = (x - hi.astype(f32)).astype(bf16)
    return _dot(hi, tri) + _dot(lo, tri)


def _tri(n, rel):
    r = lax.broadcasted_iota(jnp.int32, (n, n), 0)
    c = lax.broadcasted_iota(jnp.int32, (n, n), 1)
    m = {"ge": r >= c, "lt": r < c, "le": r <= c}[rel]
    return jnp.where(m, 1.0, 0.0).astype(bf16)


def _grid_ends(grid):
    ids = [pl.program_id(d) for d in range(len(grid))]
    first = functools.reduce(jnp.logical_and, [i == 0 for i in ids])
    last = functools.reduce(jnp.logical_and, [i == n - 1 for i, n in zip(ids, grid)])
    return first, last


def sb_fwd(name, pm, bsz, seq, exchange=None):
    bq = min(SB_BQ, seq)
    bk = min(SB_BK, bq)
    nq, nd = seq // bq, bq // bk
    step = SB_UNROLL if nd % SB_UNROLL == 0 else 1
    scale = SB_HD ** -0.5
    qb, kb_, vb_ = OFF_Q // 128, OFF_K // 128, OFF_V // 128
    na = len(exchange[0]) if exchange else 0
    grid = (bsz, 8, nq)

    def body(*refs):
        q_ref, k_ref, v_ref = refs[:3]
        o_ref, tot_ref = refs[3 + na:5 + na]
        if exchange:
            _exchange_hook(exchange, refs[3:3 + na], refs[5 + na:5 + 2 * na], refs[5 + 2 * na:], *_grid_ends(grid))
        i = pl.program_id(2)
        lane = lax.broadcasted_iota(jnp.int32, (1, 128), 1)
        m0 = lane < SB_HD
        q = q_ref[...].astype(f32) * scale
        qs = (jnp.where(m0, q, 0.0).astype(bf16), jnp.where(m0, 0.0, q).astype(bf16))
        neg_tri = -_tri(bk, "ge")
        t_idx = i * bq + lax.broadcasted_iota(jnp.int32, (bq, 1), 0)

        def block(ks, carry, masked):
            o_acc, c0, c1 = carry
            kblk = k_ref[pl.ds(ks, bk), :].astype(bf16)
            vblk = v_ref[pl.ds(ks, bk), :].astype(bf16)
            vs = (jnp.where(m0, vblk, 0).astype(bf16), jnp.where(m0, 0, vblk).astype(bf16))
            if masked:
                valid = (ks + lax.broadcasted_iota(jnp.int32, (1, bk), 1)) < t_idx
            cs = [c0, c1]
            for h in range(2):
                z = _dot_nt(qs[h], kblk)
                sp = _softplus(z)
                if masked:
                    sp = jnp.where(valid, sp, 0.0)
                tl = _dot2(sp, neg_tri)
                w = jnp.exp(z + tl + cs[h])
                if masked:
                    w = jnp.where(valid, w, 0.0)
                o_acc = o_acc + _dot(w.astype(bf16), vs[h])
                cs[h] = cs[h] + tl[:, 0:1]
            return o_acc, cs[0], cs[1]

        zc = jnp.zeros((bq, 1), f32)
        carry = (jnp.zeros((bq, 128), f32), zc, zc)
        for d in range(nd):
            carry = block(pl.multiple_of((i * nd + nd - 1 - d) * bk, bk), carry, True)
        def far(n, c):
            for u in range(step):
                c = block(pl.multiple_of((i * nd - 1 - (step * n + u)) * bk, bk), c, False)
            return c

        carry = lax.fori_loop(0, i * (nd // step), far, carry)
        o, c0, c1 = carry
        o_ref[...] = o.astype(o_ref.dtype)
        tot_ref[0, 0] = jnp.where(m0, c0, c1)

    return pl.pallas_call(
        body, name=name, grid=grid,
        in_specs=[pl.BlockSpec((bq, 128), lambda b, p, i: (b * nq + i, qb + p)),
                  pl.BlockSpec((seq, 128), lambda b, p, i: (b, kb_ + p)),
                  pl.BlockSpec((seq, 128), lambda b, p, i: (b, vb_ + p))] + [HBM_SPEC] * na,
        out_specs=[pl.BlockSpec((bq, 128), lambda b, p, i: (b * nq + i, p)),
                   pl.BlockSpec((1, 1, bq, 128), lambda b, p, i: (b, p, i, 0))] + [HBM_SPEC] * na,
        out_shape=[jax.ShapeDtypeStruct((bsz * seq, 1024), bf16), jax.ShapeDtypeStruct((bsz, 8, seq, 128), f32)]
        + ([jax.ShapeDtypeStruct(shape, dtype) for shape, dtype in exchange[1]] if exchange else []),
        scratch_shapes=_copy_semaphores(na, exchange[2]) if exchange else [],
        compiler_params=_cparams(("arbitrary",) * 3 if exchange else ("parallel",) * 3),
    )(pm, pm, pm, *(exchange[0] if exchange else []))


def sb_bwd(name, pm, tot, do, bsz, seq, exchange=None):
    bq = min(SB_BQ, seq)
    bk = min(SB_BK, bq)
    nq, nd = seq // bq, bq // bk
    step = SB_UNROLL if nd % SB_UNROLL == 0 else 1
    scale = SB_HD ** -0.5
    qb, kb_, vb_ = OFF_Q // 128, OFF_K // 128, OFF_V // 128
    na = len(exchange[0]) if exchange else 0
    grid = (bsz, 8, nq)

    def body(*refs):
        q_ref, k_ref, v_ref, do_ref, tot_ref = refs[:5]
        dq_ref, dk_ref, dv_ref = refs[5 + na:8 + na]
        dk_acc, dv_acc = refs[8 + 2 * na:10 + 2 * na]
        if exchange:
            _exchange_hook(exchange, refs[5:5 + na], refs[8 + na:8 + 2 * na], refs[10 + 2 * na:], *_grid_ends(grid))
        i = pl.program_id(2)

        @pl.when(i == 0)
        def _():
            dk_acc[...] = jnp.zeros_like(dk_acc)
            dv_acc[...] = jnp.zeros_like(dv_acc)

        lane = lax.broadcasted_iota(jnp.int32, (1, 128), 1)
        m0 = lane < SB_HD
        ms = (m0, jnp.logical_not(m0))
        q = q_ref[...].astype(f32) * scale
        qpair = q.astype(bf16)
        qs = (jnp.where(m0, q, 0.0).astype(bf16), jnp.where(m0, 0.0, q).astype(bf16))
        dout = do_ref[...].astype(f32)
        dos = (jnp.where(m0, dout, 0.0).astype(bf16), jnp.where(m0, 0.0, dout).astype(bf16))
        tot = tot_ref[0, 0]
        tots = (tot[:, 0:1], tot[:, SB_HD:SB_HD + 1])
        tri_lt = _tri(bk, "lt")
        tri_le = _tri(bk, "le")
        t_idx = i * bq + lax.broadcasted_iota(jnp.int32, (bq, 1), 0)

        def block(ks, carry, masked):
            dq_acc, p0, p1, g0, g1 = carry
            kblk = k_ref[pl.ds(ks, bk), :].astype(bf16)
            vblk = v_ref[pl.ds(ks, bk), :].astype(bf16)
            if masked:
                valid = (ks + lax.broadcasted_iota(jnp.int32, (1, bk), 1)) < t_idx
            ps, gs = [p0, p1], [g0, g1]
            dk_blk = jnp.zeros((bk, 128), f32)
            dv_blk = jnp.zeros((bk, 128), f32)
            for h in range(2):
                z = _dot_nt(qs[h], kblk)
                sp = _softplus(z)
                sig = jnp.exp(z - sp)
                if masked:
                    sp = jnp.where(valid, sp, 0.0)
                w = jnp.exp(z + tots[h] + ps[h] + _dot2(sp, tri_lt))
                if masked:
                    w = jnp.where(valid, w, 0.0)
                g = _dot_nt(dos[h], vblk) * w
                dz = g - sig * (gs[h] + _dot(g.astype(bf16), tri_le))
                if masked:
                    dz = jnp.where(valid, dz, 0.0)
                dz = dz.astype(bf16)
                dq_acc = dq_acc + jnp.where(ms[h], _dot(dz, kblk), 0.0)
                dk_blk = dk_blk + jnp.where(ms[h], _dot_tn(dz, qpair), 0.0)
                dv_blk = dv_blk + _dot_tn(w.astype(bf16), dos[h])
                ps[h] = ps[h] + jnp.sum(sp, axis=1, keepdims=True)
                gs[h] = gs[h] + jnp.sum(g, axis=1, keepdims=True)
            dk_acc[pl.ds(ks, bk), :] += dk_blk
            dv_acc[pl.ds(ks, bk), :] += dv_blk
            return dq_acc, ps[0], ps[1], gs[0], gs[1]

        zc = jnp.zeros((bq, 1), f32)
        carry = (jnp.zeros((bq, 128), f32), zc, zc, zc, zc)
        def far(n, c):
            for u in range(step):
                c = block(pl.multiple_of((step * n + u) * bk, bk), c, False)
            return c

        carry = lax.fori_loop(0, i * (nd // step), far, carry)
        for d in range(nd):
            carry = block(pl.multiple_of((i * nd + d) * bk, bk), carry, True)
        dq_ref[...] = (carry[0] * scale).astype(dq_ref.dtype)

        @pl.when(i == nq - 1)
        def _():
            dk_ref[...] = dk_acc[...].astype(dk_ref.dtype)
            dv_ref[...] = dv_acc[...].astype(dv_ref.dtype)

    return pl.pallas_call(
        body, name=name, grid=grid,
        in_specs=[pl.BlockSpec((bq, 128), lambda b, p, i: (b * nq + i, qb + p)),
                  pl.BlockSpec((seq, 128), lambda b, p, i: (b, kb_ + p)),
                  pl.BlockSpec((seq, 128), lambda b, p, i: (b, vb_ + p)),
                  pl.BlockSpec((bq, 128), lambda b, p, i: (b * nq + i, p)),
                  pl.BlockSpec((1, 1, bq, 128), lambda b, p, i: (b, p, i, 0))] + [HBM_SPEC] * na,
        out_specs=[pl.BlockSpec((bq, 128), lambda b, p, i: (b * nq + i, p)),
                   pl.BlockSpec((seq, 128), lambda b, p, i: (b, p)),
                   pl.BlockSpec((seq, 128), lambda b, p, i: (b, p))] + [HBM_SPEC] * na,
        out_shape=[jax.ShapeDtypeStruct((bsz * seq, 1024), bf16)] * 3
        + ([jax.ShapeDtypeStruct(shape, dtype) for shape, dtype in exchange[1]] if exchange else []),
        scratch_shapes=[pltpu.VMEM((seq, 128), f32), pltpu.VMEM((seq, 128), f32)]
        + (_copy_semaphores(na, exchange[2]) if exchange else []),
        compiler_params=_cparams(("arbitrary",) * 3 if exchange else ("parallel", "parallel", "arbitrary")),
    )(pm, pm, pm, do, tot, *(exchange[0] if exchange else []))


CONV_CB = 256


def _shift_down(x, d, rows):
    return x if d == 0 else jnp.where(rows >= d, pltpu.roll(x, d, axis=0), 0.0)


def _shift_up(x, d, rows, n):
    return x if d == 0 else jnp.where(rows < n - d, pltpu.roll(x, n - d, axis=0), 0.0)


def conv_fwd(name, pm, w, b, bsz, seq):
    nc = CONV_DIM // CONV_CB
    off = OFF_XBC // CONV_CB

    def body(x_ref, w_ref, b_ref, o_ref):
        x = x_ref[...].astype(f32)
        rows = lax.broadcasted_iota(jnp.int32, x.shape, 0)
        pre = b_ref[...] + jnp.zeros_like(x)
        for k in range(SSM_CONV):
            pre = pre + w_ref[k:k + 1, :] * _shift_down(x, SSM_CONV - 1 - k, rows)
        o_ref[...] = _silu(pre)

    return pl.pallas_call(
        body, name=name, grid=(nc, bsz),
        in_specs=[pl.BlockSpec((seq, CONV_CB), lambda j, bb: (bb, off + j)),
                  pl.BlockSpec((SSM_CONV, CONV_CB), lambda j, bb: (0, j)),
                  pl.BlockSpec((1, CONV_CB), lambda j, bb: (0, j))],
        out_specs=pl.BlockSpec((seq, CONV_CB), lambda j, bb: (bb, j)),
        out_shape=jax.ShapeDtypeStruct((bsz * seq, CONV_DIM), f32),
        compiler_params=_cparams(("parallel", "parallel")),
    )(pm, w, b)


def conv_bwd(name, pm, w, b, dxs, dbm, dcm, dskipx, bsz, seq):
    nc = CONV_DIM // CONV_CB
    off = OFF_XBC // CONV_CB
    nxs = SSM_INNER // CONV_CB
    nbc = SSM_GROUPS * SSM_STATE // CONV_CB

    def body(x_ref, w_ref, b_ref, dxs_ref, dbm_ref, dcm_ref, ds_ref, dx_ref, dw_ref, db_ref):
        j, bb = pl.program_id(0), pl.program_id(1)
        x = x_ref[...].astype(f32)
        rows = lax.broadcasted_iota(jnp.int32, x.shape, 0)
        xsh = [_shift_down(x, SSM_CONV - 1 - k, rows) for k in range(SSM_CONV)]
        pre = b_ref[...] + jnp.zeros_like(x)
        for k in range(SSM_CONV):
            pre = pre + w_ref[k:k + 1, :] * xsh[k]
        sig = jax.nn.sigmoid(pre)
        dout = jnp.where(j < nxs, dxs_ref[...] + ds_ref[...], jnp.where(j < nxs + nbc, dbm_ref[...], dcm_ref[...]))
        dpre = dout * (sig * (1.0 + pre * (1.0 - sig)))
        dx = jnp.zeros_like(x)
        for k in range(SSM_CONV):
            dx = dx + w_ref[k:k + 1, :] * _shift_up(dpre, SSM_CONV - 1 - k, rows, seq)
        dx_ref[...] = dx.astype(dx_ref.dtype)
        dw = jnp.concatenate([_colsum(dpre * xsh[k]) for k in range(SSM_CONV)], axis=0)
        db = _colsum(dpre)

        @pl.when(bb == 0)
        def _():
            dw_ref[...] = dw
            db_ref[...] = db

        @pl.when(bb > 0)
        def _():
            dw_ref[...] += dw
            db_ref[...] += db

    return pl.pallas_call(
        body, name=name, grid=(nc, bsz),
        in_specs=[pl.BlockSpec((seq, CONV_CB), lambda j, bb: (bb, off + j)),
                  pl.BlockSpec((SSM_CONV, CONV_CB), lambda j, bb: (0, j)),
                  pl.BlockSpec((1, CONV_CB), lambda j, bb: (0, j)),
                  pl.BlockSpec((seq, CONV_CB), lambda j, bb: (bb, jnp.minimum(j, nxs - 1))),
                  pl.BlockSpec((seq, CONV_CB), lambda j, bb: (bb, jnp.clip(j - nxs, 0, nbc - 1))),
                  pl.BlockSpec((seq, CONV_CB), lambda j, bb: (bb, jnp.clip(j - nxs - nbc, 0, nbc - 1))),
                  pl.BlockSpec((seq, CONV_CB), lambda j, bb: (bb, jnp.minimum(j, nxs - 1)))],
        out_specs=[pl.BlockSpec((seq, CONV_CB), lambda j, bb: (bb, j)),
                   pl.BlockSpec((SSM_CONV, CONV_CB), lambda j, bb: (0, j)),
                   pl.BlockSpec((1, CONV_CB), lambda j, bb: (0, j))],
        out_shape=[jax.ShapeDtypeStruct((bsz * seq, CONV_DIM), bf16), jax.ShapeDtypeStruct((SSM_CONV, CONV_DIM), f32),
                   jax.ShapeDtypeStruct((1, CONV_DIM), f32)],
        compiler_params=_cparams(("parallel", "arbitrary")),
    )(pm, w, b, dxs, dbm, dcm, dskipx)


CL = SSM_CHUNK


def _dot3(a, b, split_a):
    x = a if split_a else b
    t1 = x.astype(bf16)
    r1 = x - t1.astype(f32)
    t2 = r1.astype(bf16)
    t3 = (r1 - t2.astype(f32)).astype(bf16)
    if split_a:
        return _dot(t1, b) + _dot(t2, b) + _dot(t3, b)
    return _dot(a, t1) + _dot(a, t2) + _dot(a, t3)


def _ssd_specs(bsz, seq, rev):
    nch = seq // CL

    def ch(c):
        return (nch - 1 - c) if rev else c

    xg = pl.BlockSpec((CL, GW), lambda b, g, c: (b * nch + ch(c), g))
    lane128 = pl.BlockSpec((CL, 128), lambda b, g, c: (b * nch + ch(c), g))
    adt_t = pl.BlockSpec((128, CL), lambda b, g, c: (g, b * nch + ch(c)))
    bspec = pl.BlockSpec((CL, 128), lambda b, g, c: (b * nch + ch(c), SSM_INNER // 128 + g))
    cspec = pl.BlockSpec((CL, 128), lambda b, g, c: (b * nch + ch(c), SSM_INNER // 128 + SSM_GROUPS + g))
    st = pl.BlockSpec((1, 1, 1, SSM_STATE, GW), lambda b, g, c: (b, ch(c), g, 0, 0))
    return nch, xg, lane128, adt_t, bspec, cspec, st


def _expand_mat(width):
    r = lax.broadcasted_iota(jnp.int32, (128, HPG * width), 0)
    c = lax.broadcasted_iota(jnp.int32, (128, HPG * width), 1)
    return jnp.where((c >= r * width) & (c < (r + 1) * width), 1.0, 0.0).astype(bf16)


def _head_sums(z, e):
    hi = z.astype(bf16)
    lo = (z - hi.astype(f32)).astype(bf16)
    return _dot_nt(hi, e) + _dot_nt(lo, e)


def _ssd_common(dt_ref, adt_ref, adtt_ref):
    e64, e128 = _expand_mat(SSM_HD), _expand_mat(CL)
    csc = _dot3(_tri(CL, "ge"), adt_ref[...], False)
    csr = _dot3(adtt_ref[0:HPG, :], _tri(CL, "le"), True)
    return e64, csc, csr, _dot3(dt_ref[...], e64, True), _dot3(csc, e64, True), _dot3(csc, e128, True)


def ssd_fwd(name, xbc, dt, adt, adt_t, bsz, seq):
    nch, xg, lane128, adt_t_spec, bspec, cspec, st = _ssd_specs(bsz, seq, False)

    def body(x_ref, dt_ref, adt_ref, adtt_ref, b_ref, c_ref, y_ref, st_ref, s_scr, xd_scr):
        @pl.when(pl.program_id(2) == 0)
        def _():
            s_scr[...] = jnp.zeros_like(s_scr)

        _, _, csr, dt_e, cs_e, cs_b = _ssd_common(dt_ref, adt_ref, adtt_ref)
        cs_last = cs_e[CL - 1:CL, :]
        bm, cm = b_ref[...].astype(bf16), c_ref[...].astype(bf16)
        s_in = s_scr[...]
        st_ref[0, 0, 0] = s_in
        xd = x_ref[...] * dt_e
        xd_scr[...] = xd.astype(bf16)
        y_ref[...] = _dot(cm, s_in.astype(bf16)) * jnp.exp(cs_e)
        w = xd * jnp.exp(cs_last - cs_e)
        s_scr[...] = s_in * jnp.exp(cs_last) + _dot_tn(bm, w.astype(bf16))
        cb = _dot_nt(cm, bm)
        row = lax.broadcasted_iota(jnp.int32, (CL, CL), 0)
        col = lax.broadcasted_iota(jnp.int32, (CL, CL), 1)
        for h in range(HPG):
            hs = slice(h * SSM_HD, (h + 1) * SSM_HD)
            decay = jnp.exp(jnp.where(row >= col, cs_b[:, h * CL:(h + 1) * CL] - csr[h:h + 1, :], -1e30))
            y_ref[:, hs] += _dot((cb * decay).astype(bf16), xd_scr[:, hs])

    return pl.pallas_call(
        body, name=name, grid=(bsz, SSM_GROUPS, nch),
        in_specs=[xg, lane128, lane128, adt_t_spec, bspec, cspec],
        out_specs=[xg, st],
        out_shape=[jax.ShapeDtypeStruct((bsz * seq, SSM_INNER), f32),
                   jax.ShapeDtypeStruct((bsz, nch, SSM_GROUPS, SSM_STATE, GW), f32)],
        scratch_shapes=[pltpu.VMEM((SSM_STATE, GW), f32), pltpu.VMEM((CL, GW), bf16)],
        compiler_params=_cparams(("parallel", "parallel", "arbitrary")),
    )(xbc, dt, adt, adt_t, xbc, xbc)


def ssd_bwd(name, xbc, dt, adt, adt_t, states, dy, bsz, seq, exchange=None):
    nch, xg, lane128, adt_t_spec, bspec, cspec, st = _ssd_specs(bsz, seq, True)
    na = len(exchange[0]) if exchange else 0
    grid = (bsz, SSM_GROUPS, nch)

    def body(*refs):
        x_ref, dt_ref, adt_ref, adtt_ref, b_ref, c_ref, st_ref, dy_ref = refs[:8]
        dx_ref, db_ref, dc_ref, ddt_ref, dac_ref, dar_ref = refs[8 + na:14 + na]
        ds_scr, xd_scr, dxd_scr = refs[14 + 2 * na:17 + 2 * na]
        if exchange:
            _exchange_hook(exchange, refs[8:8 + na], refs[14 + na:14 + 2 * na], refs[17 + 2 * na:], *_grid_ends(grid))

        @pl.when(pl.program_id(2) == 0)
        def _():
            ds_scr[...] = jnp.zeros_like(ds_scr)

        e64, _, csr, dt_e, cs_e, cs_b = _ssd_common(dt_ref, adt_ref, adtt_ref)
        cs_last = cs_e[CL - 1:CL, :]
        bm, cm = b_ref[...].astype(bf16), c_ref[...].astype(bf16)
        x, dy, s_in, ds_out = x_ref[...], dy_ref[...], st_ref[0, 0, 0], ds_scr[...]
        e_last = jnp.exp(cs_last)
        d_end = jnp.exp(cs_last - cs_e)
        xd = x * dt_e
        xd_scr[...] = xd.astype(bf16)
        w = xd * d_end
        dq = dy * jnp.exp(cs_e)
        dc = _dot_nt(dq.astype(bf16), s_in.astype(bf16))
        ds_scr[...] = _dot_tn(cm, dq.astype(bf16)) + ds_out * e_last
        dw = _dot(bm, ds_out.astype(bf16))
        db = _dot_nt(w.astype(bf16), ds_out.astype(bf16))
        rw = dw * w
        dcs_e = dq * _dot(cm, s_in.astype(bf16)) - rw
        dcs_last = _colsum(rw) + _colsum(ds_out * s_in) * e_last
        is_last = lax.broadcasted_iota(jnp.int32, (CL, 1), 0) == CL - 1
        dcs_e = dcs_e + jnp.where(is_last, dcs_last, 0.0)
        dxd_scr[...] = dw * d_end
        cb, cbt = _dot_nt(cm, bm), _dot_nt(bm, cm)
        row = lax.broadcasted_iota(jnp.int32, (CL, CL), 0)
        col = lax.broadcasted_iota(jnp.int32, (CL, CL), 1)
        lane = lax.broadcasted_iota(jnp.int32, (CL, 128), 1)
        sub = lax.broadcasted_iota(jnp.int32, (HPG, CL), 0)
        dcb = jnp.zeros((CL, CL), f32)
        r_rows = jnp.zeros((CL, 128), f32)
        r_cols = jnp.zeros((HPG, CL), f32)
        for h in range(HPG):
            hs = slice(h * SSM_HD, (h + 1) * SSM_HD)
            diff = cs_b[:, h * CL:(h + 1) * CL] - csr[h:h + 1, :]
            decay = jnp.exp(jnp.where(row >= col, diff, -1e30))
            decay_t = jnp.exp(jnp.where(col >= row, -diff, -1e30))
            dy_h = dy_ref[:, hs].astype(bf16)
            dm = _dot_nt(dy_h, xd_scr[:, hs])
            dxd_scr[:, hs] += _dot((cbt * decay_t).astype(bf16), dy_h)
            r = dm * (cb * decay)
            dcb = dcb + dm * decay
            r_rows = r_rows + _dot2(r, jnp.where(lane == h, 1.0, 0.0).astype(bf16))
            r_cols = jnp.where(sub == h, _colsum(r), r_cols)
        dc_ref[...] = dc + _dot(dcb.astype(bf16), bm)
        db_ref[...] = db + _dot_tn(dcb.astype(bf16), cm)
        dxd = dxd_scr[...]
        dx_ref[...] = dxd * dt_e
        ddt_ref[...] = _head_sums(dxd * x, e64)
        dac_ref[...] = _dot3(_tri(CL, "le"), r_rows + _head_sums(dcs_e, e64), False)
        dar_ref[...] = jnp.zeros_like(dar_ref)
        dar_ref[0:HPG, :] = _dot3(-r_cols, _tri(CL, "ge"), True)

    t = bsz * seq
    return pl.pallas_call(
        body, name=name, grid=grid,
        in_specs=[xg, lane128, lane128, adt_t_spec, bspec, cspec, st, xg] + [HBM_SPEC] * na,
        out_specs=[xg, lane128, lane128, lane128, lane128, adt_t_spec] + [HBM_SPEC] * na,
        out_shape=[jax.ShapeDtypeStruct((t, SSM_INNER), f32), jax.ShapeDtypeStruct((t, DT_W), f32),
                   jax.ShapeDtypeStruct((t, DT_W), f32), jax.ShapeDtypeStruct((t, DT_W), f32),
                   jax.ShapeDtypeStruct((t, DT_W), f32), jax.ShapeDtypeStruct((DT_W, t), f32)]
        + ([jax.ShapeDtypeStruct(shape, dtype) for shape, dtype in exchange[1]] if exchange else []),
        scratch_shapes=[pltpu.VMEM((SSM_STATE, GW), f32), pltpu.VMEM((CL, GW), bf16), pltpu.VMEM((CL, GW), f32)]
        + (_copy_semaphores(na, exchange[2]) if exchange else []),
        compiler_params=_cparams(("arbitrary",) * 3 if exchange else ("parallel", "parallel", "arbitrary")),
    )(xbc, dt, adt, adt_t, xbc, xbc, states, dy, *(exchange[0] if exchange else []))


XA_BQ = 512


def _xattn(q, k, v):
    s = _dot_nt(q.astype(bf16), k.astype(bf16)) * (XA_HD ** -0.5)
    p = jnp.exp(s - jnp.max(s, axis=-1, keepdims=True))
    p = p / jnp.sum(p, axis=-1, keepdims=True)
    return _dot(p.astype(bf16), v.astype(bf16))


def xattn_fwd(name, q, kv, bsz, seq, mlen):
    bq = min(XA_BQ, seq)
    nq = seq // bq

    def body(q_ref, k_ref, v_ref, o_ref):
        o_ref[...] = _xattn(q_ref[...].astype(f32), k_ref[...].astype(f32), v_ref[...].astype(f32)).astype(o_ref.dtype)

    return pl.pallas_call(
        body, name=name, grid=(bsz, XA_HEADS, nq),
        in_specs=[pl.BlockSpec((bq, XA_HD), lambda b, h, i: (b * nq + i, h)),
                  pl.BlockSpec((mlen, XA_HD), lambda b, h, i: (b, h)),
                  pl.BlockSpec((mlen, XA_HD), lambda b, h, i: (b, XA_HEADS + h))],
        out_specs=pl.BlockSpec((bq, XA_HD), lambda b, h, i: (b * nq + i, h)),
        out_shape=jax.ShapeDtypeStruct((bsz * seq, D), bf16),
        compiler_params=_cparams(("parallel", "parallel", "parallel")),
    )(q, kv, kv)


def xattn_bwd(name, q, kv, do, bsz, seq, mlen):
    bq = min(XA_BQ, seq)
    nq = seq // bq

    def body(q_ref, k_ref, v_ref, do_ref, dq_ref, dk_ref, dv_ref):
        _, vjp = jax.vjp(_xattn, q_ref[...].astype(f32), k_ref[...].astype(f32), v_ref[...].astype(f32))
        dq, dk, dv = vjp(do_ref[...].astype(f32))
        dq_ref[...] = dq.astype(dq_ref.dtype)
        i = pl.program_id(2)

        @pl.when(i == 0)
        def _():
            dk_ref[...] = dk
            dv_ref[...] = dv

        @pl.when(i > 0)
        def _():
            dk_ref[...] += dk
            dv_ref[...] += dv

    kspec = pl.BlockSpec((mlen, XA_HD), lambda b, h, i: (b, h))
    vspec = pl.BlockSpec((mlen, XA_HD), lambda b, h, i: (b, XA_HEADS + h))
    qspec = pl.BlockSpec((bq, XA_HD), lambda b, h, i: (b * nq + i, h))
    return pl.pallas_call(
        body, name=name, grid=(bsz, XA_HEADS, nq),
        in_specs=[qspec, kspec, vspec, qspec],
        out_specs=[qspec, kspec, kspec],
        out_shape=[jax.ShapeDtypeStruct((bsz * seq, D), bf16), jax.ShapeDtypeStruct((bsz * mlen, D), f32),
                   jax.ShapeDtypeStruct((bsz * mlen, D), f32)],
        compiler_params=_cparams(("parallel", "parallel", "arbitrary")),
    )(q, kv, kv, do)


def _layer_fwd(l, x, mem, w, bsz, seq, mlen, exchange=None):
    n = f"l{l}_"
    sv = {"x0": x}
    sv["h1"] = h1 = rms_fwd(n + "rms_mix", x, w["g_pre_mix"])
    sv["pm"] = pm = _mm(n + "in_proj", h1, w["wm"], "nn", bf16)
    sv["pdt"] = pdt = _mm(n + "in_proj_dt", h1, w["wdt"], "nn")
    sv["o_att"], sv["tot"], *exchanged = sb_fwd(n + "sb_fwd", pm, bsz, seq, exchange)
    o_att = sv["o_att"]
    sv["xbc"] = xbc = conv_fwd(n + "conv_fwd", pm, w["conv_w"], w["conv_b"], bsz, seq)
    sv["dt"], sv["adt"] = dt, adt = dt_fwd(n + "dt_fwd", pdt, w["dt_bias"], w["a_log"])
    sv["adt_t"] = adt_t = adt.T
    sv["y_ssd"], sv["states"] = y_ssd, _ = ssd_fwd(n + "ssd_fwd", xbc, dt, adt, adt_t, bsz, seq)
    sv["o_ssm"] = o_ssm = gnorm_fwd(n + "gnorm_fwd", y_ssd, xbc, pm, w["d_skip"], w["g_ssm_norm"])
    sv["a"] = a = _mm(n + "br_att", o_att, w["w_br_att"], "nn", bf16)
    sv["s"] = s = _mm(n + "br_ssm", o_ssm, w["w_br_ssm"], "nn", bf16)
    sv["merged"] = merged = merge_fwd(n + "merge_fwd", pm, a, s)
    sv["u"] = u = _mm(n + "mix_out", merged, w["w_mix_out"], "nn", bf16)
    sv["x1"] = x1 = addnorm_fwd(n + "post_mix", x, u, w["g_post_mix"])
    sv["h2"] = h2 = rms_fwd(n + "rms_xa", x1, w["g_pre_xa"])
    sv["memn"] = memn = rms_fwd(n + "rms_mem", mem, w["g_mem"])
    sv["qx"] = qx = _mm(n + "xq", h2, w["w_xq"], "nn", bf16)
    sv["kv"] = kv = _mm(n + "xkv", memn, w["w_xkv"], "nn", bf16)
    sv["ox"] = ox = xattn_fwd(n + "xattn_fwd", qx, kv, bsz, seq, mlen)
    sv["yx"] = yx = _mm(n + "xo", ox, w["w_xo"], "nn", bf16)
    sv["x2"] = x2 = addnorm_fwd(n + "post_xa", x1, yx, w["g_post_xa"])
    sv["h3"] = h3 = rms_fwd(n + "rms_ffn", x2, w["g_pre_ffn"])
    sv["gu"] = gu = _mm(n + "gu", h3, w["w_gu"], "nn", bf16)
    sv["act"] = act = swiglu_fwd(n + "swiglu_fwd", gu)
    sv["d"] = d = _mm(n + "down", act, w["w_down"], "nn", bf16)
    x3 = addnorm_fwd(n + "post_ffn", x2, d, w["g_post_ffn"])
    return x3, sv, exchanged


def _layer_bwd(l, dx, mem, w, sv, bsz, seq, mlen, exchange=None, early_exchange=None):
    n = f"l{l}_b_"
    g = {}
    dd, g["g_post_ffn"] = addnorm_bwd(n + "post_ffn", sv["d"], w["g_post_ffn"], dx)
    g["w_down"] = _mm(n + "dw_down", sv["act"], dd, "tn", bf16)
    dact = _mm(n + "dact", dd, w["w_down"], "nt", bf16)
    dgu = swiglu_bwd(n + "swiglu", sv["gu"], dact)
    g["w_gu"] = _mm(n + "dw_gu", sv["h3"], dgu, "tn", bf16)
    dh3 = _mm(n + "dh3", dgu, w["w_gu"], "nt", bf16)
    dx, g["g_pre_ffn"] = rms_bwd(n + "rms_ffn", sv["x2"], w["g_pre_ffn"], [dh3], dx)
    dyx, g["g_post_xa"] = addnorm_bwd(n + "post_xa", sv["yx"], w["g_post_xa"], dx)
    g["w_xo"] = _mm(n + "dw_xo", sv["ox"], dyx, "tn", bf16)
    dox = _mm(n + "dox", dyx, w["w_xo"], "nt", bf16)
    dqx, dk, dv = xattn_bwd(n + "xattn", sv["qx"], sv["kv"], dox, bsz, seq, mlen)
    g["w_xq"] = _mm(n + "dw_xq", sv["h2"], dqx, "tn", bf16)
    dh2 = _mm(n + "dh2", dqx, w["w_xq"], "nt", bf16)
    dkv = jnp.concatenate([dk, dv], axis=1)
    g["w_xkv"] = _mm(n + "dw_xkv", sv["memn"], dkv, "tn", bf16)
    dmemn = _mm(n + "dmemn", dkv, w["w_xkv"], "nt", bf16)
    _, g["g_mem"] = rms_bwd(n + "rms_mem", mem, w["g_mem"], [dmemn])
    dx, g["g_pre_xa"] = rms_bwd(n + "rms_xa", sv["x1"], w["g_pre_xa"], [dh2], dx)
    du, g["g_post_mix"] = addnorm_bwd(n + "post_mix", sv["u"], w["g_post_mix"], dx)
    g["w_mix_out"] = _mm(n + "dw_mix", sv["merged"], du, "tn", bf16)
    dmerged = _mm(n + "dmerged", du, w["w_mix_out"], "nt", bf16)
    dgates, da, ds = merge_bwd(n + "merge", sv["pm"], sv["a"], sv["s"], dmerged)
    g["w_br_att"] = _mm(n + "dw_att", sv["o_att"], da, "tn", bf16)
    do_att = _mm(n + "do_att", da, w["w_br_att"], "nt", bf16)
    g["w_br_ssm"] = _mm(n + "dw_ssm", sv["o_ssm"], ds, "tn", bf16)
    do_ssm = _mm(n + "do_ssm", ds, w["w_br_ssm"], "nt", bf16)
    dy_ssd, dxs_skip, dz, g["d_skip"], g["g_ssm_norm"] = gnorm_bwd(
        n + "gnorm", sv["y_ssd"], sv["xbc"], sv["pm"], w["d_skip"], w["g_ssm_norm"], do_ssm)
    dxs, dbm, dcm, ddt, dadt_c, dadt_r, *exchanged_early = ssd_bwd(
        n + "ssd", sv["xbc"], sv["dt"], sv["adt"], sv["adt_t"], sv["states"], dy_ssd, bsz, seq,
        early_exchange(g) if early_exchange else None)
    dxbc, g["conv_w"], g["conv_b"] = conv_bwd(n + "conv", sv["pm"], w["conv_w"], w["conv_b"], dxs, dbm, dcm, dxs_skip, bsz, seq)
    dpdt, g["dt_bias"], g["a_log"] = dt_bwd(n + "dt", sv["pdt"], w["dt_bias"], w["a_log"], ddt, dadt_c, dadt_r.T)
    dq, dk_, dv_, *exchanged = sb_bwd(n + "sb", sv["pm"], sv["tot"], do_att, bsz, seq, exchange)
    dpm = jnp.concatenate([dz, dxbc, dq, dk_, dv_, dgates], axis=1)
    g["wm"] = _mm(n + "dw_in", sv["h1"], dpm, "tn", bf16)
    g["wdt"] = _mm(n + "dw_in_dt", sv["h1"], dpdt, "tn", bf16)
    dh1 = _mm(n + "dh1", dpm, w["wm"], "nt", bf16)
    dh1_dt = _mm(n + "dh1_dt", dpdt, w["wdt"], "nt", bf16)
    dx, g["g_pre_mix"] = rms_bwd(n + "rms_mix", sv["x0"], w["g_pre_mix"], [dh1, dh1_dt], dx)
    return dx, g, exchanged, exchanged_early


def _group_pad(v):
    lead = v.shape[:-1]
    v = v.reshape(*lead, SSM_GROUPS, HPG)
    return jnp.pad(v, [(0, 0)] * (len(lead) + 1) + [(0, 128 - HPG)]).reshape(*lead, DT_W)


def _group_unpad(v):
    lead = v.shape[:-1]
    return v.reshape(*lead, SSM_GROUPS, 128)[..., :HPG].reshape(*lead, SSM_HEADS)


BIG = ("w_in", "w_br_att", "w_br_ssm", "w_mix_out", "w_xq", "w_xkv", "w_xo", "w_gu", "w_down")
GAINS = ("g_pre_mix", "g_post_mix", "g_pre_xa", "g_mem", "g_post_xa", "g_pre_ffn", "g_post_ffn")
HEAD_VECS = ("dt_bias", "a_log", "d_skip")
SMALL = GAINS + ("conv_w", "conv_b", "g_ssm_norm") + HEAD_VECS


def _prep_layer(p):
    w = {k: p[k] for k in BIG[1:]}
    if "wm" in p:
        w["wm"], w["wdt"] = p["wm"], p["wdt"]
    else:
        w_in = p["w_in"]
        w["wm"] = jnp.concatenate([w_in[:, 3072:8192], w_in[:, 0:3072], w_in[:, 8224:10272]], axis=1)
        w["wdt"] = _group_pad(w_in[:, 8192:8224])
    for k in GAINS + ("conv_b", "g_ssm_norm"):
        w[k] = p[k].reshape(1, -1)
    w["conv_w"] = p["conv_w"]
    w["dt_bias"] = _group_pad(p["dt_bias"]).reshape(1, DT_W)
    w["a_log"] = _group_pad(p["a_log"]).reshape(1, DT_W)
    w["d_skip"] = jnp.repeat(p["d_skip"], SSM_HD).reshape(1, SSM_INNER)
    return w


def _unprep_grads(g):
    out = {k: g[k] for k in BIG[1:]}
    gm = g["wm"]
    out["w_in"] = jnp.concatenate([gm[:, 5120:8192], gm[:, 0:5120], _group_unpad(g["wdt"]), gm[:, 8192:10240]], axis=1)
    for k in GAINS + ("conv_b", "g_ssm_norm"):
        out[k] = g[k].reshape(-1)
    out["conv_w"] = g["conv_w"]
    out["dt_bias"] = _group_unpad(g["dt_bias"]).reshape(-1)
    out["a_log"] = _group_unpad(g["a_log"]).reshape(-1)
    out["d_skip"] = g["d_skip"].reshape(SSM_HEADS, SSM_HD).sum(axis=1)
    return out


def _local_step(x, mem, target, ws, bsz, seq, mlen):
    saved = []
    for l in range(len(ws)):
        x, sv, _ = _layer_fwd(l, x, mem, ws[l], bsz, seq, mlen)
        saved.append(sv)
    dx, loss_lanes = loss_fwd_bwd("loss", x, target)
    grads = [None] * len(ws)
    for l in reversed(range(len(ws))):
        dx, grads[l], _, _ = _layer_bwd(l, dx, mem, ws[l], saved[l], bsz, seq, mlen)
    return loss_lanes, dx, grads


HBM_SPEC = pl.BlockSpec(memory_space=pltpu.HBM)
FLIP_C = (0, 0, 1)
FLIPS_CHIP = ((1, 0, 0), (0, 1, 0), (1, 1, 0))
FLIPS_ALL = tuple(((f >> 2) & 1, (f >> 1) & 1, f & 1) for f in range(1, 8))


def _view(ref, index):
    return ref.at[index] if index != () else ref


def _exchange(name, srcs, out_shapes, transfers, in_place=False):
    na = len(srcs)

    def body(*refs):
        out_refs = refs[na:2 * na]
        copies = _remote_copies(out_refs if in_place else refs[:na], out_refs, transfers, *refs[2 * na:])
        for cp in copies:
            cp.start()
        for cp in copies:
            cp.wait()

    if in_place:
        out_shape = [jax.ShapeDtypeStruct(s.shape, s.dtype) for s in srcs]
    else:
        out_shape = [jax.ShapeDtypeStruct(shape, dtype) for shape, dtype in out_shapes]
    return pl.pallas_call(
        body, name=name, out_shape=out_shape, in_specs=[HBM_SPEC] * na, out_specs=[HBM_SPEC] * na,
        input_output_aliases={a: a for a in range(na)} if in_place else {},
        scratch_shapes=_copy_semaphores(na, transfers),
    )(*srcs)


def _copy_semaphores(na, transfers):
    return [pltpu.SemaphoreType.DMA((na * len(transfers),)), pltpu.SemaphoreType.DMA((na * len(transfers),))]


def _remote_copies(src_refs, out_refs, transfers, send_sems, recv_sems):
    pos = (lax.axis_index("x"), lax.axis_index("y"), lax.axis_index("c"))
    nt, copies = len(transfers), []
    for a, (src_ref, out_ref) in enumerate(zip(src_refs, out_refs)):
        for t, (flip, src_index, dst_index) in enumerate(transfers):
            assert any(flip)
            peer = tuple(1 - p if f else p for p, f in zip(pos, flip))
            copies.append(pltpu.make_async_remote_copy(
                src_ref=_view(src_ref, src_index(*pos)), dst_ref=_view(out_ref, dst_index(*pos)),
                send_sem=send_sems.at[a * nt + t], recv_sem=recv_sems.at[a * nt + t],
                device_id=peer, device_id_type=MESH))
    return copies


def _exchange_hook(exchange, refs_in, refs_out, sems, first, last):
    copies = _remote_copies(refs_in, refs_out, exchange[2], *sems)

    @pl.when(first)
    def _():
        for cp in copies:
            cp.start()

    @pl.when(last)
    def _():
        for cp in copies:
            cp.wait()


def _at(*index):
    return lambda x, y, c: index


def _allgather8(name, v, me):
    got = _exchange(name, [v], [((7,) + v.shape, v.dtype)], [(fl, _at(), _at(j)) for j, fl in enumerate(FLIPS_ALL)])[0]
    rel = jnp.concatenate([v[None], got], axis=0)
    return jnp.stack([lax.dynamic_index_in_dim(rel, k ^ me, 0, keepdims=False) for k in range(8)])


def _sum8(name, parts):
    def fn(*p):
        acc = p[0]
        for q in p[1:]:
            acc = acc + q
        return acc

    return _rowwise(name, fn, [(parts[k], 0) for k in range(8)], [], [(1, f32)], [], width=128, bt=parts.shape[1])[0]


def _rows_block(r, w, bytes_per_row_elem):
    for bt in (512, 256, 128, 64, 32, 16, 8):
        if r % bt == 0 and bt * w * bytes_per_row_elem * 2 <= 16 * 1024 * 1024:
            return bt
    raise ValueError((r, w))


def _reduce8(name, wire, recv, shard, ci):
    _, _, h, w = wire.shape
    bt = _rows_block(h, w, 2 + 7 * 2 + 4)

    def body(s_ref, a_ref, b_ref, o_ref):
        acc = a_ref[0, 0].astype(f32)
        for j in range(7):
            acc = acc + b_ref[j].astype(f32)
        o_ref[0] = acc

    return pl.pallas_call(
        body, name=name,
        grid_spec=pltpu.PrefetchScalarGridSpec(
            num_scalar_prefetch=1, grid=(h // bt,),
            in_specs=[pl.BlockSpec((1, 1, bt, w), lambda i, s_ref: (s_ref[0], s_ref[1], i, 0)),
                      pl.BlockSpec((7, bt, w), lambda i, s_ref: (0, i, 0))],
            out_specs=pl.BlockSpec((1, bt, w), lambda i, s_ref: (s_ref[1], i, 0))),
        out_shape=jax.ShapeDtypeStruct((2, h, w), f32),
        compiler_params=_cparams(("parallel",)),
    )(jnp.stack([shard, ci]).astype(jnp.int32), wire, recv)


COL_SHARDED = ("w_in", "w_xkv", "w_gu")


def _ref_cols(pieces, lo, hi):
    c, out = pieces[0].shape[1], []
    for s, p in enumerate(pieces):
        a0, a1 = max(lo, s * c), min(hi, (s + 1) * c)
        if a0 < a1:
            out.append(p[:, a0 - s * c:a1 - s * c])
    return out


def _my_cols(gm, g32, lo, hi):
    out = []
    for r0, r1, src, shift in ((0, 3072, gm, 5120), (3072, 8192, gm, -3072), (8192, 8224, g32, -8192), (8224, IN_WIDTH, gm, -32)):
        a0, a1 = max(lo, r0), min(hi, r1)
        if a0 < a1:
            out.append(src[:, a0 + shift:a1 + shift])
    return out


def _pack(arrs, rows_multiple=8):
    flat = jnp.concatenate([a.reshape(-1) for a in arrs])
    pad = (-flat.shape[0]) % (128 * rows_multiple)
    return jnp.pad(flat, (0, pad)).reshape(-1, 128)


def _unpack(buf, shapes):
    flat, out, o = buf.reshape(-1), [], 0
    for s in shapes:
        n = math.prod(s)
        out.append(flat[o:o + n].reshape(s))
        o += n
    return out


def kernel(x, mem, g_pre_mix, w_in, conv_w, conv_b, dt_bias, a_log, d_skip, g_ssm_norm, w_br_att, w_br_ssm, w_mix_out, g_post_mix, g_pre_xa, g_mem, w_xq, w_xkv, w_xo, g_post_xa, g_pre_ffn, w_gu, w_down, g_post_ffn, loss_target, m_g_pre_mix, m_w_in, m_conv_w, m_conv_b, m_dt_bias, m_a_log, m_d_skip, m_g_ssm_norm, m_w_br_att, m_w_br_ssm, m_w_mix_out, m_g_post_mix, m_g_pre_xa, m_g_mem, m_w_xq, m_w_xkv, m_w_xo, m_g_post_xa, m_g_pre_ffn, m_w_gu, m_w_down, m_g_post_ffn, v_g_pre_mix, v_w_in, v_conv_w, v_conv_b, v_dt_bias, v_a_log, v_d_skip, v_g_ssm_norm, v_w_br_att, v_w_br_ssm, v_w_mix_out, v_g_post_mix, v_g_pre_xa, v_g_mem, v_w_xq, v_w_xkv, v_w_xo, v_g_post_xa, v_g_pre_ffn, v_w_gu, v_w_down, v_g_post_ffn):
    a = dict(locals())
    names = ("g_pre_mix", "w_in", "conv_w", "conv_b", "dt_bias", "a_log", "d_skip", "g_ssm_norm", "w_br_att", "w_br_ssm",
             "w_mix_out", "g_post_mix", "g_pre_xa", "g_mem", "w_xq", "w_xkv", "w_xo", "g_post_xa", "g_pre_ffn", "w_gu",
             "w_down", "g_post_ffn")
    depth = w_in.shape[0]
    bsz, seq, _ = x.shape
    mlen = mem.shape[1]
    xi, yi, ci = lax.axis_index("x"), lax.axis_index("y"), lax.axis_index("c")
    shard = 2 * xi + yi
    me = 2 * shard + ci

    cw_all = _allgather8("ag_conv_w", _pack([conv_w]), me)
    cw_shape = conv_w.shape
    conv_w_full = jnp.concatenate([_unpack(cw_all[2 * s], [cw_shape])[0] for s in range(4)], axis=2)

    halves = {k: (a[k].shape[1] // 2, a[k].shape[2]) for k in BIG}
    wbf = {k: a[k].astype(bf16) for k in BIG}
    ag_shapes = [((4, 2) + halves[k], bf16) for k in BIG]
    ag_transfers = [(fl, lambda x_, y_, c_: (c_,), lambda x_, y_, c_: (2 * x_ + y_, c_)) for fl in FLIPS_CHIP]
    fetched = [functools.partial(lambda x_, y_, c_, f: ((2 * x_ + y_) ^ f, c_), f=2 * fl[0] + fl[1]) for fl in FLIPS_CHIP]

    def ag_sources(l):
        return [wbf[k][l].reshape((2,) + halves[k]) for k in BIG]

    def layer_weights(l, got):
        got = [lax.dynamic_update_slice(g, s[None], (shard, 0, 0, 0)) for g, s in zip(got, ag_sources(l))]
        got = _exchange(f"ag_d2d_l{l}", got, None, [(FLIP_C, fn, fn) for fn in fetched], in_place=True)
        full = {k: g.reshape(4, 2 * halves[k][0], halves[k][1]) for k, g in zip(BIG, got)}
        p = {}
        for k in BIG[1:]:
            sh = full[k]
            p[k] = sh.transpose(1, 0, 2).reshape(sh.shape[1], -1) if k in COL_SHARDED else sh.reshape(-1, sh.shape[2])
        pieces = [full["w_in"][s] for s in range(4)]
        p["wm"] = jnp.concatenate(_ref_cols(pieces, 3072, 8192) + _ref_cols(pieces, 0, 3072)
                                  + _ref_cols(pieces, 8224, 10272), axis=1)
        p["wdt"] = _group_pad(jnp.concatenate(_ref_cols(pieces, 8192, 8224), axis=1))
        for k in SMALL:
            p[k] = conv_w_full[l] if k == "conv_w" else a[k][l]
        return _prep_layer(p)

    rs_transfers = [(fl, functools.partial(lambda x_, y_, c_, fs, fc: ((2 * x_ + y_) ^ fs, c_ ^ fc), fs=2 * fl[0] + fl[1], fc=fl[2]),
                     _at(j)) for j, fl in enumerate(FLIPS_ALL)]

    def rs_exchange(g, keys):
        srcs = []
        for k in keys:
            r, c = a[k].shape[1:]
            if k == "w_in":
                g32 = _group_unpad(g["wdt"])
                gk = jnp.stack([jnp.concatenate(_my_cols(g["wm"], g32, s * c, (s + 1) * c), axis=1) for s in range(4)])
            elif k in COL_SHARDED:
                gk = g[k].reshape(r, 4, c).transpose(1, 0, 2)
            else:
                gk = g[k]
            srcs.append(gk.astype(bf16).reshape((4, 2) + halves[k]))
        return srcs, [((7,) + halves[k], bf16) for k in keys], rs_transfers

    def layer_grads(l, keys, wires, got):
        red = [_reduce8(f"rs_sum_l{l}_{k}", w, r_, shard, ci) for k, w, r_ in zip(keys, wires, got)]
        my_half = lambda x_, y_, c_: (c_,)
        red = _exchange(f"rs_swap_l{l}_{len(keys)}", red, None, [(FLIP_C, my_half, my_half)], in_place=True)
        return {k: r_.reshape(a[k].shape[1:]) for k, r_ in zip(keys, red)}

    assert depth >= 2
    xt, memt = x.reshape(bsz * seq, D), mem.reshape(bsz * mlen, D)
    ws, saved = [], []
    got = _exchange("ag_ici_l0", ag_sources(0), ag_shapes, ag_transfers)
    for l in range(depth):
        ws.append(layer_weights(l, got))
        nxt = (ag_sources(l + 1), ag_shapes, ag_transfers) if l + 1 < depth else None
        xt, sv, got = _layer_fwd(l, xt, memt, ws[l], bsz, seq, mlen, nxt)
        saved.append(sv)
    gx, loss_lanes = loss_fwd_bwd("loss", xt, loss_target.reshape(bsz * seq, D))
    grads, gshard = [None] * depth, [None] * depth
    riding, early = None, {}
    for l in reversed(range(depth)):
        early_fn = (lambda g: early.setdefault("ex", rs_exchange(g, BIG[1:]))) if l == 0 else None
        gx, grads[l], got, got_early = _layer_bwd(l, gx, memt, ws[l], saved[l], bsz, seq, mlen, riding, early_fn)
        if riding is not None:
            gshard[l + 1] = layer_grads(l + 1, BIG, riding[0], got)
        riding = rs_exchange(grads[l], BIG) if l > 0 else None
    last = rs_exchange(grads[0], BIG[:1])
    gshard[0] = {**layer_grads(0, BIG[1:], early["ex"][0], got_early),
                 **layer_grads(0, BIG[:1], last[0], _exchange("rs_all_l0", *last))}
    grads = [_unprep_grads(g) for g in grads]

    out_g, out_d, out_m, out_v = {}, {}, {}, {}
    for k in BIG:
        shp = a[k].shape
        g = jnp.stack([gshard[l][k] for l in range(depth)])
        two_d = (shp[0] * shp[1], shp[2])
        d_, m_, v_ = adamw("adamw_" + k, a[k].reshape(two_d), g.reshape(two_d), a["m_" + k].reshape(two_d), a["v_" + k].reshape(two_d))
        out_g[k], out_d[k], out_m[k], out_v[k] = g, d_.reshape(shp), m_.reshape(shp), v_.reshape(shp)

    small_shapes = [(depth,) + (conv_w_full.shape[1:] if k == "conv_w" else a[k].shape[1:]) for k in SMALL]
    small = _pack([jnp.stack([grads[l][k] for l in range(depth)]) for k in SMALL] + [loss_lanes])
    total = _sum8("small_sum", _allgather8("ag_small", small, me))
    *gsmall, loss_l = _unpack(total, small_shapes + [loss_lanes.shape])
    gsmall = dict(zip(SMALL, gsmall))
    gsmall["conv_w"] = lax.dynamic_slice_in_dim(gsmall["conv_w"], shard * cw_shape[2], cw_shape[2], axis=2)
    loc_shapes = [a[k].shape for k in SMALL]
    d_, m_, v_ = adamw("adamw_small", _pack([a[k] for k in SMALL]), _pack([gsmall[k] for k in SMALL]),
                       _pack([a["m_" + k] for k in SMALL]), _pack([a["v_" + k] for k in SMALL]))
    for k, dd, mm, vv in zip(SMALL, _unpack(d_, loc_shapes), _unpack(m_, loc_shapes), _unpack(v_, loc_shapes)):
        out_g[k], out_d[k], out_m[k], out_v[k] = gsmall[k], dd, mm, vv

    loss = jnp.sum(loss_l)
    return (loss, gx.reshape(x.shape), *[out_g[k] for k in names], *[out_d[k] for k in names],
            *[out_m[k] for k in names], *[out_v[k] for k in names])
```

```python
import functools
import math

import jax
import jax.numpy as jnp
from jax import lax
from jax.experimental import pallas as pl
from jax.experimental.pallas import tpu as pltpu

f32, bf16 = jnp.float32, jnp.bfloat16

DEPTH = 4
D = 1024
SB_HEADS, SB_HD = 16, 64
SSM_INNER, SSM_HD, SSM_HEADS, SSM_GROUPS, SSM_STATE, SSM_CONV, SSM_CHUNK = 2048, 64, 32, 4, 128, 4, 128
HPG = SSM_HEADS // SSM_GROUPS
CONV_DIM = SSM_INNER + 2 * SSM_GROUPS * SSM_STATE
XA_HEADS, XA_HD = 4, 256
FFN = 2816
IN_WIDTH = 10272
RMS_EPS = 1e-6
LR, B1, B2, EPS, WD, STEP = 0.001, 0.9, 0.999, 1e-08, 0.01, 10

PM_W = 10240
OFF_Z, OFF_XBC, OFF_Q, OFF_K, OFF_V, OFF_GA, OFF_GS = 0, 2048, 5120, 6144, 7168, 8192, 9216
DT_W = SSM_GROUPS * 128

VMEM_LIMIT = 48 * 1024 * 1024
MESH = pl.DeviceIdType.MESH


def _cparams(sem):
    return pltpu.CompilerParams(dimension_semantics=sem, vmem_limit_bytes=VMEM_LIMIT)


def _tile(n):
    for t in (512, 256, 128):
        if n % t == 0:
            return t
    raise ValueError(f"dimension {n} is not a multiple of 128")


MM_VMEM_BUDGET = 34 * 1024 * 1024


def _mm_tiles(m, n, k, sa, sb, so):
    best = None
    for tm in (2048, 1024, 512, 256, 128):
        if m % tm:
            continue
        for tn in (2048, 1024, 512, 256, 128):
            if n % tn:
                continue
            for tk in (k, 2048, 1024, 512):
                if tk > k or k % tk:
                    continue
                vmem = 2 * (tm * tk * sa + tk * tn * sb + tm * tn * so) + tm * tn * 4 * (2 if tk < k else 1)
                if vmem > MM_VMEM_BUDGET:
                    continue
                traffic = m * k * sa * (n // tn) + k * n * sb * (m // tm) + m * n * so
                steps = (m // tm) * (n // tn) * (k // tk)
                accumulate = (k // tk > 1) * (k // tk) * m * n * 2
                key = (traffic + steps * 800_000 + accumulate, steps)
                if best is None or key < best[0]:
                    best = (key, (tm, tn, tk))
    assert best is not None, (m, n, k)
    return best[1]


def _mm(name, a, b, mode, out_dtype=f32):
    if mode == "nn":
        (m, k), (k2, n) = a.shape, b.shape
    elif mode == "nt":
        (m, k), (n, k2) = a.shape, b.shape
    else:
        (k, m), (k2, n) = a.shape, b.shape
    assert k == k2, (name, a.shape, b.shape, mode)
    tm, tn, tk = _mm_tiles(m, n, k, a.dtype.itemsize, b.dtype.itemsize, jnp.dtype(out_dtype).itemsize)
    nk = k // tk
    dn = {"nn": (((1,), (0,)), ((), ())), "nt": (((1,), (1,)), ((), ())), "tn": (((0,), (0,)), ((), ()))}[mode]

    def product(a_ref, b_ref):
        return lax.dot_general(a_ref[...].astype(bf16), b_ref[...].astype(bf16), dn, preferred_element_type=f32)

    def body_whole_k(a_ref, b_ref, o_ref):
        o_ref[...] = product(a_ref, b_ref).astype(o_ref.dtype)

    def body_k_loop(a_ref, b_ref, o_ref, acc_ref):
        kk = pl.program_id(2)

        @pl.when(kk == 0)
        def _():
            acc_ref[...] = product(a_ref, b_ref)

        @pl.when(kk > 0)
        def _():
            acc_ref[...] += product(a_ref, b_ref)

        @pl.when(kk == nk - 1)
        def _():
            o_ref[...] = acc_ref[...].astype(o_ref.dtype)

    a_spec = pl.BlockSpec((tk, tm), lambda i, j, kk: (kk, i)) if mode == "tn" else pl.BlockSpec((tm, tk), lambda i, j, kk: (i, kk))
    b_spec = pl.BlockSpec((tn, tk), lambda i, j, kk: (j, kk)) if mode == "nt" else pl.BlockSpec((tk, tn), lambda i, j, kk: (kk, j))
    return pl.pallas_call(
        body_whole_k if nk == 1 else body_k_loop, name=name, grid=(m // tm, n // tn, nk),
        in_specs=[a_spec, b_spec],
        out_specs=pl.BlockSpec((tm, tn), lambda i, j, kk: (i, j)),
        out_shape=jax.ShapeDtypeStruct((m, n), out_dtype),
        scratch_shapes=[] if nk == 1 else [pltpu.VMEM((tm, tn), f32)],
        compiler_params=_cparams(("parallel", "parallel", "arbitrary")),
    )(a, b)


def _rowwise(name, fn, rows, consts, out_rows, out_accs, *, width, ncol=1, bt=256):
    r = rows[0][0].shape[0]
    bt = min(bt, r)
    assert r % bt == 0, (name, r, bt)
    nrow = r // bt
    n_in = len(rows) + len(consts)
    n_or = len(out_rows)

    def body(*refs):
        ins = [ref[...].astype(f32) for ref in refs[:n_in]]
        outs = fn(*ins)
        if not isinstance(outs, (tuple, list)):
            outs = (outs,)
        o_refs = refs[n_in:]
        for o_ref, val in zip(o_refs[:n_or], outs[:n_or]):
            o_ref[...] = val.astype(o_ref.dtype)
        if out_accs:
            i = pl.program_id(1)
            for o_ref, val in zip(o_refs[n_or:], outs[n_or:]):
                @pl.when(i == 0)
                def _(o_ref=o_ref, val=val):
                    o_ref[...] = val

                @pl.when(i > 0)
                def _(o_ref=o_ref, val=val):
                    o_ref[...] += val

    in_specs = [pl.BlockSpec((bt, width), functools.partial(lambda j, i, off: (i, off + j), off=off)) for _, off in rows]
    in_specs += [pl.BlockSpec((c.shape[0], width), functools.partial(lambda j, i, off: (0, off + j), off=off)) for c, off in consts]
    out_specs = [pl.BlockSpec((bt, mlt * width), lambda j, i: (i, j)) for mlt, _ in out_rows]
    out_specs += [pl.BlockSpec((k, width), lambda j, i: (0, j)) for k in out_accs]
    out_shape = [jax.ShapeDtypeStruct((r, ncol * mlt * width), dt) for mlt, dt in out_rows]
    out_shape += [jax.ShapeDtypeStruct((k, ncol * width), f32) for k in out_accs]
    res = pl.pallas_call(
        body, name=name, grid=(ncol, nrow), in_specs=in_specs, out_specs=out_specs, out_shape=out_shape,
        compiler_params=_cparams(("parallel", "arbitrary" if out_accs else "parallel")),
    )(*[a for a, _ in rows], *[c for c, _ in consts])
    return res


def _rms(x, g):
    return x * lax.rsqrt(jnp.mean(x * x, axis=-1, keepdims=True) + RMS_EPS) * g


def _silu(x):
    return x * jax.nn.sigmoid(x)


def _softplus(x):
    return jnp.maximum(x, 0.0) + jnp.log(1.0 + jnp.exp(-jnp.abs(x)))


def _colsum(x):
    return jnp.sum(x, axis=0, keepdims=True)


def rms_fwd(name, x, g):
    return _rowwise(name, _rms, [(x, 0)], [(g, 0)], [(1, bf16)], [], width=D)[0]


def rms_bwd(name, x, g, dhs, dres=None):
    nd = len(dhs)

    def fn(x, *rest):
        dh = rest[0]
        for extra in rest[1:nd]:
            dh = dh + extra
        g = rest[-1]
        _, vjp = jax.vjp(_rms, x, g)
        dx, dg = vjp(dh.astype(f32))
        if dres is not None:
            dx = dx + rest[nd]
        return dx, dg

    rows = [(x, 0)] + [(d, 0) for d in dhs] + ([(dres, 0)] if dres is not None else [])
    return _rowwise(name, fn, rows, [(g, 0)], [(1, f32)], [1], width=D)


def addnorm_fwd(name, x, u, g):
    return _rowwise(name, lambda x, u, g: x + _rms(u, g), [(x, 0), (u, 0)], [(g, 0)], [(1, f32)], [], width=D)[0]


def addnorm_rms_fwd(name, x, u, g, g_next):
    def fn(x, u, g, g_next):
        x_new = x + _rms(u, g)
        return x_new, _rms(x_new, g_next)

    return _rowwise(name, fn, [(x, 0), (u, 0)], [(g, 0), (g_next, 0)], [(1, f32), (1, bf16)], [], width=D)


def addnorm_bwd(name, u, g, dx):
    def fn(u, dx, g):
        _, vjp = jax.vjp(_rms, u, g)
        return vjp(dx)

    return _rowwise(name, fn, [(u, 0), (dx, 0)], [(g, 0)], [(1, bf16)], [1], width=D)


def _merge(ga, gs, a, s):
    return jax.nn.sigmoid(ga) * a + jax.nn.sigmoid(gs) * s


def merge_fwd(name, pm, a, s):
    return _rowwise(name, _merge, [(pm, OFF_GA // D), (pm, OFF_GS // D), (a, 0), (s, 0)], [], [(1, bf16)], [], width=D)[0]


def merge_bwd(name, pm, a, s, dm):
    def fn(ga, gs, a, s, dm):
        _, vjp = jax.vjp(_merge, ga, gs, a, s)
        dga, dgs, da, ds = vjp(dm)
        return jnp.concatenate([dga, dgs], axis=1), da, ds

    return _rowwise(name, fn, [(pm, OFF_GA // D), (pm, OFF_GS // D), (a, 0), (s, 0), (dm, 0)], [],
                    [(2, bf16), (1, bf16), (1, bf16)], [], width=D)


def _swiglu(gate, up):
    return _silu(gate) * up


def swiglu_fwd(name, gu):
    return _rowwise(name, _swiglu, [(gu, 0), (gu, 1)], [], [(1, bf16)], [], width=FFN)[0]


def swiglu_bwd(name, gu, dact):
    def fn(gate, up, dact):
        _, vjp = jax.vjp(_swiglu, gate, up)
        dg, du = vjp(dact.astype(f32))
        return jnp.concatenate([dg, du], axis=1)

    return _rowwise(name, fn, [(gu, 0), (gu, 1), (dact, 0)], [], [(2, bf16)], [], width=FFN, bt=128)[0]


GW = SSM_INNER // SSM_GROUPS


def _gnorm(y, xs, z, dskip, gn):
    yy = (y + dskip * xs) * _silu(z)
    return yy * lax.rsqrt(jnp.mean(yy * yy, axis=-1, keepdims=True) + RMS_EPS) * gn


def gnorm_fwd(name, y, xbc, pm, dskip, gn):
    return _rowwise(name, _gnorm, [(y, 0), (xbc, 0), (pm, OFF_Z // GW)], [(dskip, 0), (gn, 0)], [(1, bf16)], [],
                    width=GW, ncol=SSM_GROUPS)[0]


def gnorm_bwd(name, y, xbc, pm, dskip, gn, do):
    def fn(y, xs, z, do, dskip, gn):
        _, vjp = jax.vjp(_gnorm, y, xs, z, dskip, gn)
        return vjp(do.astype(f32))

    return _rowwise(name, fn, [(y, 0), (xbc, 0), (pm, OFF_Z // GW), (do, 0)], [(dskip, 0), (gn, 0)],
                    [(1, f32), (1, f32), (1, bf16)], [1, 1], width=GW, ncol=SSM_GROUPS)


def _dtfn(pdt, bias, alog):
    dt = _softplus(pdt + bias)
    return dt, -jnp.exp(alog) * dt


def dt_fwd(name, pdt, bias, alog):
    return _rowwise(name, _dtfn, [(pdt, 0)], [(bias, 0), (alog, 0)], [(1, f32), (1, f32)], [], width=DT_W)


def dt_bwd(name, pdt, bias, alog, ddt, dadt_c, dadt_r):
    def fn(pdt, ddt, dac, dar, bias, alog):
        _, vjp = jax.vjp(_dtfn, pdt, bias, alog)
        return vjp((ddt, dac + dar))

    return _rowwise(name, fn, [(pdt, 0), (ddt, 0), (dadt_c, 0), (dadt_r, 0)], [(bias, 0), (alog, 0)],
                    [(1, f32)], [1, 1], width=DT_W)


def loss_fwd_bwd(name, y, target):
    def fn(y, t):
        e = y - t
        return e * (1.0 / D), _colsum(e * e) * (0.5 / D)

    return _rowwise(name, fn, [(y, 0), (target, 0)], [], [(1, f32)], [1], width=D)


def adamw(name, w, g, m, v):
    r, c = w.shape

    def fn(w, g, m, v):
        m = B1 * m + (1.0 - B1) * g
        v = B2 * v + (1.0 - B2) * (g * g)
        m_hat = m / (1.0 - B1 ** STEP)
        v_hat = v / (1.0 - B2 ** STEP)
        return -LR * (m_hat / (jnp.sqrt(v_hat) + EPS) + WD * w), m, v

    bt = 256
    while bt > 8 and (r % bt or bt * c * 4 * 7 * 2 > 16 * 1024 * 1024):
        bt //= 2
    if r % bt:
        bt = r
    return _rowwise(name, fn, [(w, 0), (g, 0), (m, 0), (v, 0)], [], [(1, f32)] * 3, [], width=c, bt=bt)


SB_BQ, SB_BK = 512, 256
SB_UNROLL = 2


def _dot(a, b):
    return jnp.dot(a, b, preferred_element_type=f32)


def _dot_nt(a, b):
    return lax.dot_general(a, b, (((1,), (1,)), ((), ())), preferred_element_type=f32)


def _dot_tn(a, b):
    return lax.dot_general(a, b, (((0,), (0,)), ((), ())), preferred_element_type=f32)


def _dot2(x, tri):
    hi = x.astype(bf16)
    lo = (x - hi.astype(f32)).astype(bf16)
    return _dot(hi, tri) + _dot(lo, tri)


def _tri(n, rel):
    r = lax.broadcasted_iota(jnp.int32, (n, n), 0)
    c = lax.broadcasted_iota(jnp.int32, (n, n), 1)
    m = {"ge": r >= c, "lt": r < c, "le": r <= c}[rel]
    return jnp.where(m, 1.0, 0.0).astype(bf16)


def _add_rows(full, r0, upd):
    return full + upd if r0 == 0 else jnp.concatenate([full[:r0], full[r0:] + upd], axis=0)


def _grid_ends(grid):
    ids = [pl.program_id(d) for d in range(len(grid))]
    first = functools.reduce(jnp.logical_and, [i == 0 for i in ids])
    last = functools.reduce(jnp.logical_and, [i == n - 1 for i, n in zip(ids, grid)])
    return first, last


def sb_fwd(name, pm, bsz, seq, exchange=None):
    bq = min(SB_BQ, seq)
    bk = min(SB_BK, bq)
    nq, nd = seq // bq, bq // bk
    step = SB_UNROLL if nd % SB_UNROLL == 0 else 1
    scale = SB_HD ** -0.5
    qb, kb_, vb_ = OFF_Q // 128, OFF_K // 128, OFF_V // 128
    na = len(exchange[0]) if exchange else 0
    grid = (bsz, 8, nq)

    def body(*refs):
        q_ref, k_ref, v_ref = refs[:3]
        o_ref, tot_ref = refs[3 + na:5 + na]
        if exchange:
            _exchange_hook(exchange, refs[3:3 + na], refs[5 + na:5 + 2 * na], refs[5 + 2 * na:], *_grid_ends(grid))
        i = pl.program_id(2)
        lane = lax.broadcasted_iota(jnp.int32, (1, 128), 1)
        m0 = lane < SB_HD
        q = q_ref[...].astype(f32) * scale
        qs = (jnp.where(m0, q, 0.0).astype(bf16), jnp.where(m0, 0.0, q).astype(bf16))
        wide = step * bk
        neg_tri = {bk: -_tri(bk, "ge"), wide: -_tri(wide, "ge")}
        t_idx = i * bq + lax.broadcasted_iota(jnp.int32, (bq, 1), 0)

        def block(ks, carry, masked, width):
            o_acc, c0, c1 = carry
            kblk = k_ref[pl.ds(ks, width), :].astype(bf16)
            vblk = v_ref[pl.ds(ks, width), :].astype(bf16)
            vs = (jnp.where(m0, vblk, 0).astype(bf16), jnp.where(m0, 0, vblk).astype(bf16))
            if masked:
                valid = (ks + lax.broadcasted_iota(jnp.int32, (1, width), 1)) < t_idx
            cs = [c0, c1]
            for h in range(2):
                z = _dot_nt(qs[h], kblk)
                sp = _softplus(z)
                if masked:
                    sp = jnp.where(valid, sp, 0.0)
                tl = _dot2(sp, neg_tri[width])
                w = jnp.exp(z + tl + cs[h])
                if masked:
                    w = jnp.where(valid, w, 0.0)
                o_acc = o_acc + _dot(w.astype(bf16), vs[h])
                cs[h] = cs[h] + tl[:, 0:1]
            return o_acc, cs[0], cs[1]

        zc = jnp.zeros((bq, 1), f32)
        carry = (jnp.zeros((bq, 128), f32), zc, zc)
        for d in range(nd):
            carry = block(pl.multiple_of((i * nd + nd - 1 - d) * bk, bk), carry, True, bk)
        carry = lax.fori_loop(0, i * (nd // step),
                              lambda n, c: block(pl.multiple_of((i * nd - step * (n + 1)) * bk, wide), c, False, wide), carry)
        o, c0, c1 = carry
        o_ref[...] = o.astype(o_ref.dtype)
        tot_ref[0, 0] = jnp.where(m0, c0, c1)

    return pl.pallas_call(
        body, name=name, grid=grid,
        in_specs=[pl.BlockSpec((bq, 128), lambda b, p, i: (b * nq + i, qb + p)),
                  pl.BlockSpec((seq, 128), lambda b, p, i: (b, kb_ + p)),
                  pl.BlockSpec((seq, 128), lambda b, p, i: (b, vb_ + p))] + [HBM_SPEC] * na,
        out_specs=[pl.BlockSpec((bq, 128), lambda b, p, i: (b * nq + i, p)),
                   pl.BlockSpec((1, 1, bq, 128), lambda b, p, i: (b, p, i, 0))] + [HBM_SPEC] * na,
        out_shape=[jax.ShapeDtypeStruct((bsz * seq, 1024), bf16), jax.ShapeDtypeStruct((bsz, 8, seq, 128), f32)]
        + ([jax.ShapeDtypeStruct(shape, dtype) for shape, dtype in exchange[1]] if exchange else []),
        scratch_shapes=_copy_semaphores(na, exchange[2]) if exchange else [],
        compiler_params=_cparams(("arbitrary",) * 3 if exchange else ("parallel",) * 3),
    )(pm, pm, pm, *(exchange[0] if exchange else []))


def sb_bwd(name, pm, tot, do, bsz, seq, exchange=None):
    bq = min(SB_BQ, seq)
    bk = min(SB_BK, bq)
    nq, nd = seq // bq, bq // bk
    step = SB_UNROLL if nd % SB_UNROLL == 0 else 1
    scale = SB_HD ** -0.5
    qb, kb_, vb_ = OFF_Q // 128, OFF_K // 128, OFF_V // 128
    na = len(exchange[0]) if exchange else 0
    grid = (bsz, 8, nq)

    def body(*refs):
        q_ref, k_ref, v_ref, do_ref, tot_ref = refs[:5]
        dq_ref, dk_ref, dv_ref = refs[5 + na:8 + na]
        dk_acc, dv_acc = refs[8 + 2 * na:10 + 2 * na]
        if exchange:
            _exchange_hook(exchange, refs[5:5 + na], refs[8 + na:8 + 2 * na], refs[10 + 2 * na:], *_grid_ends(grid))
        i = pl.program_id(2)

        @pl.when(i == 0)
        def _():
            dk_acc[...] = jnp.zeros_like(dk_acc)
            dv_acc[...] = jnp.zeros_like(dv_acc)

        lane = lax.broadcasted_iota(jnp.int32, (1, 128), 1)
        m0 = lane < SB_HD
        ms = (m0, jnp.logical_not(m0))
        q = q_ref[...].astype(f32) * scale
        qpair = q.astype(bf16)
        qs = (jnp.where(m0, q, 0.0).astype(bf16), jnp.where(m0, 0.0, q).astype(bf16))
        dout = do_ref[...].astype(f32)
        dos = (jnp.where(m0, dout, 0.0).astype(bf16), jnp.where(m0, 0.0, dout).astype(bf16))
        tot = tot_ref[0, 0]
        tots = (tot[:, 0:1], tot[:, SB_HD:SB_HD + 1])
        tri_lt = _tri(bk, "lt")
        tri_le = _tri(bk, "le")
        t_idx = i * bq + lax.broadcasted_iota(jnp.int32, (bq, 1), 0)

        def block(ks, carry, masked, r0=0):
            dq_acc, p0, p1, g0, g1 = carry
            kblk = k_ref[pl.ds(ks, bk), :].astype(bf16)
            vblk = v_ref[pl.ds(ks, bk), :].astype(bf16)
            if masked:
                valid = (ks + lax.broadcasted_iota(jnp.int32, (1, bk), 1)) < t_idx[r0:]
            ps, gs = [p0, p1], [g0, g1]
            dk_blk = jnp.zeros((bk, 128), f32)
            dv_blk = jnp.zeros((bk, 128), f32)
            for h in range(2):
                z = _dot_nt(qs[h][r0:], kblk)
                sp = _softplus(z)
                sig = jnp.exp(z - sp)
                if masked:
                    sp = jnp.where(valid, sp, 0.0)
                w = jnp.exp(z + tots[h][r0:] + ps[h][r0:] + _dot2(sp, tri_lt))
                if masked:
                    w = jnp.where(valid, w, 0.0)
                g = _dot_nt(dos[h][r0:], vblk) * w
                dz = g - sig * (gs[h][r0:] + _dot(g.astype(bf16), tri_le))
                if masked:
                    dz = jnp.where(valid, dz, 0.0)
                dz = dz.astype(bf16)
                dq_acc = _add_rows(dq_acc, r0, jnp.where(ms[h], _dot(dz, kblk), 0.0))
                dk_blk = dk_blk + jnp.where(ms[h], _dot_tn(dz, qpair[r0:]), 0.0)
                dv_blk = dv_blk + _dot_tn(w.astype(bf16), dos[h][r0:])
                ps[h] = _add_rows(ps[h], r0, jnp.sum(sp, axis=1, keepdims=True))
                gs[h] = _add_rows(gs[h], r0, jnp.sum(g, axis=1, keepdims=True))
            dk_acc[pl.ds(ks, bk), :] += dk_blk
            dv_acc[pl.ds(ks, bk), :] += dv_blk
            return dq_acc, ps[0], ps[1], gs[0], gs[1]

        zc = jnp.zeros((bq, 1), f32)
        carry = (jnp.zeros((bq, 128), f32), zc, zc, zc, zc)
        def far(n, c):
            for u in range(step):
                c = block(pl.multiple_of((step * n + u) * bk, bk), c, False)
            return c

        carry = lax.fori_loop(0, i * (nd // step), far, carry)
        for d in range(nd):
            carry = block(pl.multiple_of((i * nd + d) * bk, bk), carry, True, d * bk)
        dq_ref[...] = (carry[0] * scale).astype(dq_ref.dtype)

        @pl.when(i == nq - 1)
        def _():
            dk_ref[...] = dk_acc[...].astype(dk_ref.dtype)
            dv_ref[...] = dv_acc[...].astype(dv_ref.dtype)

    return pl.pallas_call(
        body, name=name, grid=grid,
        in_specs=[pl.BlockSpec((bq, 128), lambda b, p, i: (b * nq + i, qb + p)),
                  pl.BlockSpec((seq, 128), lambda b, p, i: (b, kb_ + p)),
                  pl.BlockSpec((seq, 128), lambda b, p, i: (b, vb_ + p)),
                  pl.BlockSpec((bq, 128), lambda b, p, i: (b * nq + i, p)),
                  pl.BlockSpec((1, 1, bq, 128), lambda b, p, i: (b, p, i, 0))] + [HBM_SPEC] * na,
        out_specs=[pl.BlockSpec((bq, 128), lambda b, p, i: (b * nq + i, p)),
                   pl.BlockSpec((seq, 128), lambda b, p, i: (b, p)),
                   pl.BlockSpec((seq, 128), lambda b, p, i: (b, p))] + [HBM_SPEC] * na,
        out_shape=[jax.ShapeDtypeStruct((bsz * seq, 1024), bf16)] * 3
        + ([jax.ShapeDtypeStruct(shape, dtype) for shape, dtype in exchange[1]] if exchange else []),
        scratch_shapes=[pltpu.VMEM((seq, 128), f32), pltpu.VMEM((seq, 128), f32)]
        + (_copy_semaphores(na, exchange[2]) if exchange else []),
        compiler_params=_cparams(("arbitrary",) * 3 if exchange else ("parallel", "parallel", "arbitrary")),
    )(pm, pm, pm, do, tot, *(exchange[0] if exchange else []))


CONV_CB = 256


def _shift_down(x, d, rows):
    return x if d == 0 else jnp.where(rows >= d, pltpu.roll(x, d, axis=0), 0.0)


def _shift_up(x, d, rows, n):
    return x if d == 0 else jnp.where(rows < n - d, pltpu.roll(x, n - d, axis=0), 0.0)


def conv_fwd(name, pm, w, b, bsz, seq):
    nc = CONV_DIM // CONV_CB
    off = OFF_XBC // CONV_CB

    def body(x_ref, w_ref, b_ref, o_ref):
        x = x_ref[...].astype(f32)
        rows = lax.broadcasted_iota(jnp.int32, x.shape, 0)
        pre = b_ref[...] + jnp.zeros_like(x)
        for k in range(SSM_CONV):
            pre = pre + w_ref[k:k + 1, :] * _shift_down(x, SSM_CONV - 1 - k, rows)
        o_ref[...] = _silu(pre)

    return pl.pallas_call(
        body, name=name, grid=(nc, bsz),
        in_specs=[pl.BlockSpec((seq, CONV_CB), lambda j, bb: (bb, off + j)),
                  pl.BlockSpec((SSM_CONV, CONV_CB), lambda j, bb: (0, j)),
                  pl.BlockSpec((1, CONV_CB), lambda j, bb: (0, j))],
        out_specs=pl.BlockSpec((seq, CONV_CB), lambda j, bb: (bb, j)),
        out_shape=jax.ShapeDtypeStruct((bsz * seq, CONV_DIM), f32),
        compiler_params=_cparams(("parallel", "parallel")),
    )(pm, w, b)


def conv_bwd(name, pm, w, b, dxs, dbm, dcm, dskipx, bsz, seq):
    nc = CONV_DIM // CONV_CB
    off = OFF_XBC // CONV_CB
    nxs = SSM_INNER // CONV_CB
    nbc = SSM_GROUPS * SSM_STATE // CONV_CB

    def body(x_ref, w_ref, b_ref, dxs_ref, dbm_ref, dcm_ref, ds_ref, dx_ref, dw_ref, db_ref):
        j, bb = pl.program_id(0), pl.program_id(1)
        x = x_ref[...].astype(f32)
        rows = lax.broadcasted_iota(jnp.int32, x.shape, 0)
        xsh = [_shift_down(x, SSM_CONV - 1 - k, rows) for k in range(SSM_CONV)]
        pre = b_ref[...] + jnp.zeros_like(x)
        for k in range(SSM_CONV):
            pre = pre + w_ref[k:k + 1, :] * xsh[k]
        sig = jax.nn.sigmoid(pre)
        dout = jnp.where(j < nxs, dxs_ref[...] + ds_ref[...], jnp.where(j < nxs + nbc, dbm_ref[...], dcm_ref[...]))
        dpre = dout * (sig * (1.0 + pre * (1.0 - sig)))
        dx = jnp.zeros_like(x)
        for k in range(SSM_CONV):
            dx = dx + w_ref[k:k + 1, :] * _shift_up(dpre, SSM_CONV - 1 - k, rows, seq)
        dx_ref[...] = dx.astype(dx_ref.dtype)
        dw = jnp.concatenate([_colsum(dpre * xsh[k]) for k in range(SSM_CONV)], axis=0)
        db = _colsum(dpre)

        @pl.when(bb == 0)
        def _():
            dw_ref[...] = dw
            db_ref[...] = db

        @pl.when(bb > 0)
        def _():
            dw_ref[...] += dw
            db_ref[...] += db

    return pl.pallas_call(
        body, name=name, grid=(nc, bsz),
        in_specs=[pl.BlockSpec((seq, CONV_CB), lambda j, bb: (bb, off + j)),
                  pl.BlockSpec((SSM_CONV, CONV_CB), lambda j, bb: (0, j)),
                  pl.BlockSpec((1, CONV_CB), lambda j, bb: (0, j)),
                  pl.BlockSpec((seq, CONV_CB), lambda j, bb: (bb, jnp.minimum(j, nxs - 1))),
                  pl.BlockSpec((seq, CONV_CB), lambda j, bb: (bb, jnp.clip(j - nxs, 0, nbc - 1))),
                  pl.BlockSpec((seq, CONV_CB), lambda j, bb: (bb, jnp.clip(j - nxs - nbc, 0, nbc - 1))),
                  pl.BlockSpec((seq, CONV_CB), lambda j, bb: (bb, jnp.minimum(j, nxs - 1)))],
        out_specs=[pl.BlockSpec((seq, CONV_CB), lambda j, bb: (bb, j)),
                   pl.BlockSpec((SSM_CONV, CONV_CB), lambda j, bb: (0, j)),
                   pl.BlockSpec((1, CONV_CB), lambda j, bb: (0, j))],
        out_shape=[jax.ShapeDtypeStruct((bsz * seq, CONV_DIM), bf16), jax.ShapeDtypeStruct((SSM_CONV, CONV_DIM), f32),
                   jax.ShapeDtypeStruct((1, CONV_DIM), f32)],
        compiler_params=_cparams(("parallel", "arbitrary")),
    )(pm, w, b, dxs, dbm, dcm, dskipx)


CL = SSM_CHUNK


def _dot3(a, b, split_a):
    x = a if split_a else b
    t1 = x.astype(bf16)
    r1 = x - t1.astype(f32)
    t2 = r1.astype(bf16)
    t3 = (r1 - t2.astype(f32)).astype(bf16)
    if split_a:
        return _dot(t1, b) + _dot(t2, b) + _dot(t3, b)
    return _dot(a, t1) + _dot(a, t2) + _dot(a, t3)


def _ssd_specs(bsz, seq, rev):
    nch = seq // CL

    def ch(c):
        return (nch - 1 - c) if rev else c

    xg = pl.BlockSpec((CL, GW), lambda b, g, c: (b * nch + ch(c), g))
    lane128 = pl.BlockSpec((CL, 128), lambda b, g, c: (b * nch + ch(c), g))
    adt_t = pl.BlockSpec((128, CL), lambda b, g, c: (g, b * nch + ch(c)))
    bspec = pl.BlockSpec((CL, 128), lambda b, g, c: (b * nch + ch(c), SSM_INNER // 128 + g))
    cspec = pl.BlockSpec((CL, 128), lambda b, g, c: (b * nch + ch(c), SSM_INNER // 128 + SSM_GROUPS + g))
    st = pl.BlockSpec((1, 1, 1, SSM_STATE, GW), lambda b, g, c: (b, ch(c), g, 0, 0))
    return nch, xg, lane128, adt_t, bspec, cspec, st


def _expand_mat(width):
    r = lax.broadcasted_iota(jnp.int32, (128, HPG * width), 0)
    c = lax.broadcasted_iota(jnp.int32, (128, HPG * width), 1)
    return jnp.where((c >= r * width) & (c < (r + 1) * width), 1.0, 0.0).astype(bf16)


def _head_sums(z, e):
    hi = z.astype(bf16)
    lo = (z - hi.astype(f32)).astype(bf16)
    return _dot_nt(hi, e) + _dot_nt(lo, e)


def _ssd_common(dt_ref, adt_ref, adtt_ref):
    e64, e128 = _expand_mat(SSM_HD), _expand_mat(CL)
    csc = _dot3(_tri(CL, "ge"), adt_ref[...], False)
    csr = _dot3(adtt_ref[0:HPG, :], _tri(CL, "le"), True)
    return e64, csc, csr, _dot3(dt_ref[...], e64, True), _dot3(csc, e64, True), _dot3(csc, e128, True)


def ssd_fwd(name, xbc, dt, adt, adt_t, bsz, seq):
    nch, xg, lane128, adt_t_spec, bspec, cspec, st = _ssd_specs(bsz, seq, False)

    def body(x_ref, dt_ref, adt_ref, adtt_ref, b_ref, c_ref, y_ref, st_ref, s_scr, xd_scr):
        @pl.when(pl.program_id(2) == 0)
        def _():
            s_scr[...] = jnp.zeros_like(s_scr)

        _, _, csr, dt_e, cs_e, cs_b = _ssd_common(dt_ref, adt_ref, adtt_ref)
        cs_last = cs_e[CL - 1:CL, :]
        bm, cm = b_ref[...].astype(bf16), c_ref[...].astype(bf16)
        s_in = s_scr[...]
        st_ref[0, 0, 0] = s_in
        xd = x_ref[...] * dt_e
        xd_scr[...] = xd.astype(bf16)
        y_ref[...] = _dot(cm, s_in.astype(bf16)) * jnp.exp(cs_e)
        w = xd * jnp.exp(cs_last - cs_e)
        s_scr[...] = s_in * jnp.exp(cs_last) + _dot_tn(bm, w.astype(bf16))
        cb = _dot_nt(cm, bm)
        row = lax.broadcasted_iota(jnp.int32, (CL, CL), 0)
        col = lax.broadcasted_iota(jnp.int32, (CL, CL), 1)
        for h in range(HPG):
            hs = slice(h * SSM_HD, (h + 1) * SSM_HD)
            decay = jnp.exp(jnp.where(row >= col, cs_b[:, h * CL:(h + 1) * CL] - csr[h:h + 1, :], -1e30))
            y_ref[:, hs] += _dot((cb * decay).astype(bf16), xd_scr[:, hs])

    return pl.pallas_call(
        body, name=name, grid=(bsz, SSM_GROUPS, nch),
        in_specs=[xg, lane128, lane128, adt_t_spec, bspec, cspec],
        out_specs=[xg, st],
        out_shape=[jax.ShapeDtypeStruct((bsz * seq, SSM_INNER), f32),
                   jax.ShapeDtypeStruct((bsz, nch, SSM_GROUPS, SSM_STATE, GW), f32)],
        scratch_shapes=[pltpu.VMEM((SSM_STATE, GW), f32), pltpu.VMEM((CL, GW), bf16)],
        compiler_params=_cparams(("parallel", "parallel", "arbitrary")),
    )(xbc, dt, adt, adt_t, xbc, xbc)


def ssd_bwd(name, xbc, dt, adt, adt_t, states, dy, bsz, seq, exchange=None):
    nch, xg, lane128, adt_t_spec, bspec, cspec, st = _ssd_specs(bsz, seq, True)
    na = len(exchange[0]) if exchange else 0
    grid = (bsz, SSM_GROUPS, nch)

    def body(*refs):
        x_ref, dt_ref, adt_ref, adtt_ref, b_ref, c_ref, st_ref, dy_ref = refs[:8]
        dx_ref, db_ref, dc_ref, ddt_ref, dac_ref, dar_ref = refs[8 + na:14 + na]
        ds_scr, xd_scr, dxd_scr = refs[14 + 2 * na:17 + 2 * na]
        if exchange:
            _exchange_hook(exchange, refs[8:8 + na], refs[14 + na:14 + 2 * na], refs[17 + 2 * na:], *_grid_ends(grid))

        @pl.when(pl.program_id(2) == 0)
        def _():
            ds_scr[...] = jnp.zeros_like(ds_scr)

        e64, _, csr, dt_e, cs_e, cs_b = _ssd_common(dt_ref, adt_ref, adtt_ref)
        cs_last = cs_e[CL - 1:CL, :]
        bm, cm = b_ref[...].astype(bf16), c_ref[...].astype(bf16)
        x, dy, s_in, ds_out = x_ref[...], dy_ref[...], st_ref[0, 0, 0], ds_scr[...]
        e_last = jnp.exp(cs_last)
        d_end = jnp.exp(cs_last - cs_e)
        xd = x * dt_e
        xd_scr[...] = xd.astype(bf16)
        w = xd * d_end
        dq = dy * jnp.exp(cs_e)
        dc = _dot_nt(dq.astype(bf16), s_in.astype(bf16))
        ds_scr[...] = _dot_tn(cm, dq.astype(bf16)) + ds_out * e_last
        dw = _dot(bm, ds_out.astype(bf16))
        db = _dot_nt(w.astype(bf16), ds_out.astype(bf16))
        rw = dw * w
        dcs_e = dq * _dot(cm, s_in.astype(bf16)) - rw
        dcs_last = _colsum(rw) + _colsum(ds_out * s_in) * e_last
        is_last = lax.broadcasted_iota(jnp.int32, (CL, 1), 0) == CL - 1
        dcs_e = dcs_e + jnp.where(is_last, dcs_last, 0.0)
        dxd_scr[...] = dw * d_end
        cb, cbt = _dot_nt(cm, bm), _dot_nt(bm, cm)
        row = lax.broadcasted_iota(jnp.int32, (CL, CL), 0)
        col = lax.broadcasted_iota(jnp.int32, (CL, CL), 1)
        lane = lax.broadcasted_iota(jnp.int32, (CL, 128), 1)
        sub = lax.broadcasted_iota(jnp.int32, (HPG, CL), 0)
        dcb = jnp.zeros((CL, CL), f32)
        r_rows = jnp.zeros((CL, 128), f32)
        r_cols = jnp.zeros((HPG, CL), f32)
        for h in range(HPG):
            hs = slice(h * SSM_HD, (h + 1) * SSM_HD)
            diff = cs_b[:, h * CL:(h + 1) * CL] - csr[h:h + 1, :]
            decay = jnp.exp(jnp.where(row >= col, diff, -1e30))
            decay_t = jnp.exp(jnp.where(col >= row, -diff, -1e30))
            dy_h = dy_ref[:, hs].astype(bf16)
            dm = _dot_nt(dy_h, xd_scr[:, hs])
            dxd_scr[:, hs] += _dot((cbt * decay_t).astype(bf16), dy_h)
            r = dm * (cb * decay)
            dcb = dcb + dm * decay
            r_rows = r_rows + _dot2(r, jnp.where(lane == h, 1.0, 0.0).astype(bf16))
            r_cols = jnp.where(sub == h, _colsum(r), r_cols)
        dc_ref[...] = dc + _dot(dcb.astype(bf16), bm)
        db_ref[...] = db + _dot_tn(dcb.astype(bf16), cm)
        dxd = dxd_scr[...]
        dx_ref[...] = dxd * dt_e
        ddt_ref[...] = _head_sums(dxd * x, e64)
        dac_ref[...] = _dot3(_tri(CL, "le"), r_rows + _head_sums(dcs_e, e64), False)
        dar_ref[...] = jnp.zeros_like(dar_ref)
        dar_ref[0:HPG, :] = _dot3(-r_cols, _tri(CL, "ge"), True)

    t = bsz * seq
    return pl.pallas_call(
        body, name=name, grid=grid,
        in_specs=[xg, lane128, lane128, adt_t_spec, bspec, cspec, st, xg] + [HBM_SPEC] * na,
        out_specs=[xg, lane128, lane128, lane128, lane128, adt_t_spec] + [HBM_SPEC] * na,
        out_shape=[jax.ShapeDtypeStruct((t, SSM_INNER), f32), jax.ShapeDtypeStruct((t, DT_W), f32),
                   jax.ShapeDtypeStruct((t, DT_W), f32), jax.ShapeDtypeStruct((t, DT_W), f32),
                   jax.ShapeDtypeStruct((t, DT_W), f32), jax.ShapeDtypeStruct((DT_W, t), f32)]
        + ([jax.ShapeDtypeStruct(shape, dtype) for shape, dtype in exchange[1]] if exchange else []),
        scratch_shapes=[pltpu.VMEM((SSM_STATE, GW), f32), pltpu.VMEM((CL, GW), bf16), pltpu.VMEM((CL, GW), f32)]
        + (_copy_semaphores(na, exchange[2]) if exchange else []),
        compiler_params=_cparams(("arbitrary",) * 3 if exchange else ("parallel", "parallel", "arbitrary")),
    )(xbc, dt, adt, adt_t, xbc, xbc, states, dy, *(exchange[0] if exchange else []))


XA_BQ = 512


def _xattn(q, k, v):
    s = _dot_nt(q.astype(bf16), k.astype(bf16)) * (XA_HD ** -0.5)
    p = jnp.exp(s - jnp.max(s, axis=-1, keepdims=True))
    p = p / jnp.sum(p, axis=-1, keepdims=True)
    return _dot(p.astype(bf16), v.astype(bf16))


def xattn_fwd(name, q, kv, bsz, seq, mlen):
    bq = min(XA_BQ, seq)
    nq = seq // bq

    def body(q_ref, k_ref, v_ref, o_ref):
        o_ref[...] = _xattn(q_ref[...].astype(f32), k_ref[...].astype(f32), v_ref[...].astype(f32)).astype(o_ref.dtype)

    return pl.pallas_call(
        body, name=name, grid=(bsz, XA_HEADS, nq),
        in_specs=[pl.BlockSpec((bq, XA_HD), lambda b, h, i: (b * nq + i, h)),
                  pl.BlockSpec((mlen, XA_HD), lambda b, h, i: (b, h)),
                  pl.BlockSpec((mlen, XA_HD), lambda b, h, i: (b, XA_HEADS + h))],
        out_specs=pl.BlockSpec((bq, XA_HD), lambda b, h, i: (b * nq + i, h)),
        out_shape=jax.ShapeDtypeStruct((bsz * seq, D), bf16),
        compiler_params=_cparams(("parallel", "parallel", "parallel")),
    )(q, kv, kv)


def xattn_bwd(name, q, kv, do, bsz, seq, mlen):
    bq = min(XA_BQ, seq)
    nq = seq // bq

    def body(q_ref, k_ref, v_ref, do_ref, dq_ref, dk_ref, dv_ref):
        _, vjp = jax.vjp(_xattn, q_ref[...].astype(f32), k_ref[...].astype(f32), v_ref[...].astype(f32))
        dq, dk, dv = vjp(do_ref[...].astype(f32))
        dq_ref[...] = dq.astype(dq_ref.dtype)
        i = pl.program_id(2)

        @pl.when(i == 0)
        def _():
            dk_ref[...] = dk
            dv_ref[...] = dv

        @pl.when(i > 0)
        def _():
            dk_ref[...] += dk
            dv_ref[...] += dv

    kspec = pl.BlockSpec((mlen, XA_HD), lambda b, h, i: (b, h))
    vspec = pl.BlockSpec((mlen, XA_HD), lambda b, h, i: (b, XA_HEADS + h))
    qspec = pl.BlockSpec((bq, XA_HD), lambda b, h, i: (b * nq + i, h))
    return pl.pallas_call(
        body, name=name, grid=(bsz, XA_HEADS, nq),
        in_specs=[qspec, kspec, vspec, qspec],
        out_specs=[qspec, kspec, kspec],
        out_shape=[jax.ShapeDtypeStruct((bsz * seq, D), bf16), jax.ShapeDtypeStruct((bsz * mlen, D), f32),
                   jax.ShapeDtypeStruct((bsz * mlen, D), f32)],
        compiler_params=_cparams(("parallel", "parallel", "arbitrary")),
    )(q, kv, kv, do)


def _layer_fwd(l, x, mem, w, bsz, seq, mlen, exchange=None):
    n = f"l{l}_"
    sv = {"x0": x}
    sv["h1"] = h1 = rms_fwd(n + "rms_mix", x, w["g_pre_mix"])
    sv["pm"] = pm = _mm(n + "in_proj", h1, w["wm"], "nn", bf16)
    sv["pdt"] = pdt = _mm(n + "in_proj_dt", h1, w["wdt"], "nn")
    sv["o_att"], sv["tot"], *exchanged = sb_fwd(n + "sb_fwd", pm, bsz, seq, exchange)
    o_att = sv["o_att"]
    sv["xbc"] = xbc = conv_fwd(n + "conv_fwd", pm, w["conv_w"], w["conv_b"], bsz, seq)
    sv["dt"], sv["adt"] = dt, adt = dt_fwd(n + "dt_fwd", pdt, w["dt_bias"], w["a_log"])
    sv["adt_t"] = adt_t = adt.T
    sv["y_ssd"], sv["states"] = y_ssd, _ = ssd_fwd(n + "ssd_fwd", xbc, dt, adt, adt_t, bsz, seq)
    sv["o_ssm"] = o_ssm = gnorm_fwd(n + "gnorm_fwd", y_ssd, xbc, pm, w["d_skip"], w["g_ssm_norm"])
    sv["a"] = a = _mm(n + "br_att", o_att, w["w_br_att"], "nn", bf16)
    sv["s"] = s = _mm(n + "br_ssm", o_ssm, w["w_br_ssm"], "nn", bf16)
    sv["merged"] = merged = merge_fwd(n + "merge_fwd", pm, a, s)
    sv["u"] = u = _mm(n + "mix_out", merged, w["w_mix_out"], "nn", bf16)
    sv["x1"], sv["h2"] = x1, h2 = addnorm_rms_fwd(n + "post_mix", x, u, w["g_post_mix"], w["g_pre_xa"])
    sv["memn"] = memn = rms_fwd(n + "rms_mem", mem, w["g_mem"])
    sv["qx"] = qx = _mm(n + "xq", h2, w["w_xq"], "nn", bf16)
    sv["kv"] = kv = _mm(n + "xkv", memn, w["w_xkv"], "nn", bf16)
    sv["ox"] = ox = xattn_fwd(n + "xattn_fwd", qx, kv, bsz, seq, mlen)
    sv["yx"] = yx = _mm(n + "xo", ox, w["w_xo"], "nn", bf16)
    sv["x2"], sv["h3"] = x2, h3 = addnorm_rms_fwd(n + "post_xa", x1, yx, w["g_post_xa"], w["g_pre_ffn"])
    sv["gu"] = gu = _mm(n + "gu", h3, w["w_gu"], "nn", bf16)
    sv["act"] = act = swiglu_fwd(n + "swiglu_fwd", gu)
    sv["d"] = d = _mm(n + "down", act, w["w_down"], "nn", bf16)
    x3 = addnorm_fwd(n + "post_ffn", x2, d, w["g_post_ffn"])
    return x3, sv, exchanged


def _layer_bwd(l, dx, mem, w, sv, bsz, seq, mlen, exchange=None, early_exchange=None):
    n = f"l{l}_b_"
    g = {}
    dd, g["g_post_ffn"] = addnorm_bwd(n + "post_ffn", sv["d"], w["g_post_ffn"], dx)
    g["w_down"] = _mm(n + "dw_down", sv["act"], dd, "tn", bf16)
    dact = _mm(n + "dact", dd, w["w_down"], "nt", bf16)
    dgu = swiglu_bwd(n + "swiglu", sv["gu"], dact)
    g["w_gu"] = _mm(n + "dw_gu", sv["h3"], dgu, "tn", bf16)
    dh3 = _mm(n + "dh3", dgu, w["w_gu"], "nt", bf16)
    dx, g["g_pre_ffn"] = rms_bwd(n + "rms_ffn", sv["x2"], w["g_pre_ffn"], [dh3], dx)
    dyx, g["g_post_xa"] = addnorm_bwd(n + "post_xa", sv["yx"], w["g_post_xa"], dx)
    g["w_xo"] = _mm(n + "dw_xo", sv["ox"], dyx, "tn", bf16)
    dox = _mm(n + "dox", dyx, w["w_xo"], "nt", bf16)
    dqx, dk, dv = xattn_bwd(n + "xattn", sv["qx"], sv["kv"], dox, bsz, seq, mlen)
    g["w_xq"] = _mm(n + "dw_xq", sv["h2"], dqx, "tn", bf16)
    dh2 = _mm(n + "dh2", dqx, w["w_xq"], "nt", bf16)
    dkv = jnp.concatenate([dk, dv], axis=1)
    g["w_xkv"] = _mm(n + "dw_xkv", sv["memn"], dkv, "tn", bf16)
    dmemn = _mm(n + "dmemn", dkv, w["w_xkv"], "nt", bf16)
    _, g["g_mem"] = rms_bwd(n + "rms_mem", mem, w["g_mem"], [dmemn])
    dx, g["g_pre_xa"] = rms_bwd(n + "rms_xa", sv["x1"], w["g_pre_xa"], [dh2], dx)
    du, g["g_post_mix"] = addnorm_bwd(n + "post_mix", sv["u"], w["g_post_mix"], dx)
    g["w_mix_out"] = _mm(n + "dw_mix", sv["merged"], du, "tn", bf16)
    dmerged = _mm(n + "dmerged", du, w["w_mix_out"], "nt", bf16)
    dgates, da, ds = merge_bwd(n + "merge", sv["pm"], sv["a"], sv["s"], dmerged)
    g["w_br_att"] = _mm(n + "dw_att", sv["o_att"], da, "tn", bf16)
    do_att = _mm(n + "do_att", da, w["w_br_att"], "nt", bf16)
    g["w_br_ssm"] = _mm(n + "dw_ssm", sv["o_ssm"], ds, "tn", bf16)
    do_ssm = _mm(n + "do_ssm", ds, w["w_br_ssm"], "nt", bf16)
    dy_ssd, dxs_skip, dz, g["d_skip"], g["g_ssm_norm"] = gnorm_bwd(
        n + "gnorm", sv["y_ssd"], sv["xbc"], sv["pm"], w["d_skip"], w["g_ssm_norm"], do_ssm)
    dxs, dbm, dcm, ddt, dadt_c, dadt_r, *exchanged_early = ssd_bwd(
        n + "ssd", sv["xbc"], sv["dt"], sv["adt"], sv["adt_t"], sv["states"], dy_ssd, bsz, seq,
        early_exchange(g) if early_exchange else None)
    dxbc, g["conv_w"], g["conv_b"] = conv_bwd(n + "conv", sv["pm"], w["conv_w"], w["conv_b"], dxs, dbm, dcm, dxs_skip, bsz, seq)
    dpdt, g["dt_bias"], g["a_log"] = dt_bwd(n + "dt", sv["pdt"], w["dt_bias"], w["a_log"], ddt, dadt_c, dadt_r.T)
    dq, dk_, dv_, *exchanged = sb_bwd(n + "sb", sv["pm"], sv["tot"], do_att, bsz, seq, exchange)
    dpm = jnp.concatenate([dz, dxbc, dq, dk_, dv_, dgates], axis=1)
    g["wm"] = _mm(n + "dw_in", sv["h1"], dpm, "tn", bf16)
    g["wdt"] = _mm(n + "dw_in_dt", sv["h1"], dpdt, "tn", bf16)
    dh1 = _mm(n + "dh1", dpm, w["wm"], "nt", bf16)
    dh1_dt = _mm(n + "dh1_dt", dpdt, w["wdt"], "nt", bf16)
    dx, g["g_pre_mix"] = rms_bwd(n + "rms_mix", sv["x0"], w["g_pre_mix"], [dh1, dh1_dt], dx)
    return dx, g, exchanged, exchanged_early


def _group_pad(v):
    lead = v.shape[:-1]
    v = v.reshape(*lead, SSM_GROUPS, HPG)
    return jnp.pad(v, [(0, 0)] * (len(lead) + 1) + [(0, 128 - HPG)]).reshape(*lead, DT_W)


def _group_unpad(v):
    lead = v.shape[:-1]
    return v.reshape(*lead, SSM_GROUPS, 128)[..., :HPG].reshape(*lead, SSM_HEADS)


BIG = ("w_in", "w_br_att", "w_br_ssm", "w_mix_out", "w_xq", "w_xkv", "w_xo", "w_gu", "w_down")
GAINS = ("g_pre_mix", "g_post_mix", "g_pre_xa", "g_mem", "g_post_xa", "g_pre_ffn", "g_post_ffn")
HEAD_VECS = ("dt_bias", "a_log", "d_skip")
SMALL = GAINS + ("conv_w", "conv_b", "g_ssm_norm") + HEAD_VECS


def _prep_layer(p):
    w = {k: p[k] for k in BIG[1:]}
    if "wm" in p:
        w["wm"], w["wdt"] = p["wm"], p["wdt"]
    else:
        w_in = p["w_in"]
        w["wm"] = jnp.concatenate([w_in[:, 3072:8192], w_in[:, 0:3072], w_in[:, 8224:10272]], axis=1)
        w["wdt"] = _group_pad(w_in[:, 8192:8224])
    for k in GAINS + ("conv_b", "g_ssm_norm"):
        w[k] = p[k].reshape(1, -1)
    w["conv_w"] = p["conv_w"]
    w["dt_bias"] = _group_pad(p["dt_bias"]).reshape(1, DT_W)
    w["a_log"] = _group_pad(p["a_log"]).reshape(1, DT_W)
    w["d_skip"] = jnp.repeat(p["d_skip"], SSM_HD).reshape(1, SSM_INNER)
    return w


def _unprep_grads(g):
    out = {k: g[k] for k in BIG[1:]}
    gm = g["wm"]
    out["w_in"] = jnp.concatenate([gm[:, 5120:8192], gm[:, 0:5120], _group_unpad(g["wdt"]), gm[:, 8192:10240]], axis=1)
    for k in GAINS + ("conv_b", "g_ssm_norm"):
        out[k] = g[k].reshape(-1)
    out["conv_w"] = g["conv_w"]
    out["dt_bias"] = _group_unpad(g["dt_bias"]).reshape(-1)
    out["a_log"] = _group_unpad(g["a_log"]).reshape(-1)
    out["d_skip"] = g["d_skip"].reshape(SSM_HEADS, SSM_HD).sum(axis=1)
    return out


def _local_step(x, mem, target, ws, bsz, seq, mlen):
    saved = []
    for l in range(len(ws)):
        x, sv, _ = _layer_fwd(l, x, mem, ws[l], bsz, seq, mlen)
        saved.append(sv)
    dx, loss_lanes = loss_fwd_bwd("loss", x, target)
    grads = [None] * len(ws)
    for l in reversed(range(len(ws))):
        dx, grads[l], _, _ = _layer_bwd(l, dx, mem, ws[l], saved[l], bsz, seq, mlen)
    return loss_lanes, dx, grads


HBM_SPEC = pl.BlockSpec(memory_space=pltpu.HBM)
FLIP_C = (0, 0, 1)
FLIPS_CHIP = ((1, 0, 0), (0, 1, 0), (1, 1, 0))
FLIPS_ALL = tuple(((f >> 2) & 1, (f >> 1) & 1, f & 1) for f in range(1, 8))


def _view(ref, index):
    return ref.at[index] if index != () else ref


def _exchange(name, srcs, out_shapes, transfers, in_place=False):
    na = len(srcs)

    def body(*refs):
        out_refs = refs[na:2 * na]
        copies = _remote_copies(out_refs if in_place else refs[:na], out_refs, transfers, *refs[2 * na:])
        for cp in copies:
            cp.start()
        for cp in copies:
            cp.wait()

    if in_place:
        out_shape = [jax.ShapeDtypeStruct(s.shape, s.dtype) for s in srcs]
    else:
        out_shape = [jax.ShapeDtypeStruct(shape, dtype) for shape, dtype in out_shapes]
    return pl.pallas_call(
        body, name=name, out_shape=out_shape, in_specs=[HBM_SPEC] * na, out_specs=[HBM_SPEC] * na,
        input_output_aliases={a: a for a in range(na)} if in_place else {},
        scratch_shapes=_copy_semaphores(na, transfers),
    )(*srcs)


def _copy_semaphores(na, transfers):
    return [pltpu.SemaphoreType.DMA((na * len(transfers),)), pltpu.SemaphoreType.DMA((na * len(transfers),))]


def _remote_copies(src_refs, out_refs, transfers, send_sems, recv_sems):
    pos = (lax.axis_index("x"), lax.axis_index("y"), lax.axis_index("c"))
    nt, copies = len(transfers), []
    for a, (src_ref, out_ref) in enumerate(zip(src_refs, out_refs)):
        for t, (flip, src_index, dst_index) in enumerate(transfers):
            assert any(flip)
            peer = tuple(1 - p if f else p for p, f in zip(pos, flip))
            copies.append(pltpu.make_async_remote_copy(
                src_ref=_view(src_ref, src_index(*pos)), dst_ref=_view(out_ref, dst_index(*pos)),
                send_sem=send_sems.at[a * nt + t], recv_sem=recv_sems.at[a * nt + t],
                device_id=peer, device_id_type=MESH))
    return copies


def _exchange_hook(exchange, refs_in, refs_out, sems, first, last):
    copies = _remote_copies(refs_in, refs_out, exchange[2], *sems)

    @pl.when(first)
    def _():
        for cp in copies:
            cp.start()

    @pl.when(last)
    def _():
        for cp in copies:
            cp.wait()


def _at(*index):
    return lambda x, y, c: index


def _allgather8(name, v, me):
    got = _exchange(name, [v], [((7,) + v.shape, v.dtype)], [(fl, _at(), _at(j)) for j, fl in enumerate(FLIPS_ALL)])[0]
    rel = jnp.concatenate([v[None], got], axis=0)
    return jnp.stack([lax.dynamic_index_in_dim(rel, k ^ me, 0, keepdims=False) for k in range(8)])


def _sum8(name, parts):
    def fn(*p):
        acc = p[0]
        for q in p[1:]:
            acc = acc + q
        return acc

    return _rowwise(name, fn, [(parts[k], 0) for k in range(8)], [], [(1, f32)], [], width=128, bt=parts.shape[1])[0]


def _rows_block(r, w, bytes_per_row_elem):
    for bt in (512, 256, 128, 64, 32, 16, 8):
        if r % bt == 0 and bt * w * bytes_per_row_elem * 2 <= 16 * 1024 * 1024:
            return bt
    raise ValueError((r, w))


def _reduce8(name, wire, recv, shard, ci):
    _, _, h, w = wire.shape
    bt = _rows_block(h, w, 2 + 7 * 2 + 4)

    def body(s_ref, a_ref, b_ref, o_ref):
        acc = a_ref[0, 0].astype(f32)
        for j in range(7):
            acc = acc + b_ref[j].astype(f32)
        o_ref[0] = acc

    return pl.pallas_call(
        body, name=name,
        grid_spec=pltpu.PrefetchScalarGridSpec(
            num_scalar_prefetch=1, grid=(h // bt,),
            in_specs=[pl.BlockSpec((1, 1, bt, w), lambda i, s_ref: (s_ref[0], s_ref[1], i, 0)),
                      pl.BlockSpec((7, bt, w), lambda i, s_ref: (0, i, 0))],
            out_specs=pl.BlockSpec((1, bt, w), lambda i, s_ref: (s_ref[1], i, 0))),
        out_shape=jax.ShapeDtypeStruct((2, h, w), f32),
        compiler_params=_cparams(("parallel",)),
    )(jnp.stack([shard, ci]).astype(jnp.int32), wire, recv)


COL_SHARDED = ("w_in", "w_xkv", "w_gu")


def _ref_cols(pieces, lo, hi):
    c, out = pieces[0].shape[1], []
    for s, p in enumerate(pieces):
        a0, a1 = max(lo, s * c), min(hi, (s + 1) * c)
        if a0 < a1:
            out.append(p[:, a0 - s * c:a1 - s * c])
    return out


def _my_cols(gm, g32, lo, hi):
    out = []
    for r0, r1, src, shift in ((0, 3072, gm, 5120), (3072, 8192, gm, -3072), (8192, 8224, g32, -8192), (8224, IN_WIDTH, gm, -32)):
        a0, a1 = max(lo, r0), min(hi, r1)
        if a0 < a1:
            out.append(src[:, a0 + shift:a1 + shift])
    return out


def _pack(arrs, rows_multiple=8):
    flat = jnp.concatenate([a.reshape(-1) for a in arrs])
    pad = (-flat.shape[0]) % (128 * rows_multiple)
    return jnp.pad(flat, (0, pad)).reshape(-1, 128)


def _unpack(buf, shapes):
    flat, out, o = buf.reshape(-1), [], 0
    for s in shapes:
        n = math.prod(s)
        out.append(flat[o:o + n].reshape(s))
        o += n
    return out


def kernel(x, mem, g_pre_mix, w_in, conv_w, conv_b, dt_bias, a_log, d_skip, g_ssm_norm, w_br_att, w_br_ssm, w_mix_out, g_post_mix, g_pre_xa, g_mem, w_xq, w_xkv, w_xo, g_post_xa, g_pre_ffn, w_gu, w_down, g_post_ffn, loss_target, m_g_pre_mix, m_w_in, m_conv_w, m_conv_b, m_dt_bias, m_a_log, m_d_skip, m_g_ssm_norm, m_w_br_att, m_w_br_ssm, m_w_mix_out, m_g_post_mix, m_g_pre_xa, m_g_mem, m_w_xq, m_w_xkv, m_w_xo, m_g_post_xa, m_g_pre_ffn, m_w_gu, m_w_down, m_g_post_ffn, v_g_pre_mix, v_w_in, v_conv_w, v_conv_b, v_dt_bias, v_a_log, v_d_skip, v_g_ssm_norm, v_w_br_att, v_w_br_ssm, v_w_mix_out, v_g_post_mix, v_g_pre_xa, v_g_mem, v_w_xq, v_w_xkv, v_w_xo, v_g_post_xa, v_g_pre_ffn, v_w_gu, v_w_down, v_g_post_ffn):
    a = dict(locals())
    names = ("g_pre_mix", "w_in", "conv_w", "conv_b", "dt_bias", "a_log", "d_skip", "g_ssm_norm", "w_br_att", "w_br_ssm",
             "w_mix_out", "g_post_mix", "g_pre_xa", "g_mem", "w_xq", "w_xkv", "w_xo", "g_post_xa", "g_pre_ffn", "w_gu",
             "w_down", "g_post_ffn")
    depth = w_in.shape[0]
    bsz, seq, _ = x.shape
    mlen = mem.shape[1]
    xi, yi, ci = lax.axis_index("x"), lax.axis_index("y"), lax.axis_index("c")
    shard = 2 * xi + yi
    me = 2 * shard + ci

    cw_all = _allgather8("ag_conv_w", _pack([conv_w]), me)
    cw_shape = conv_w.shape
    conv_w_full = jnp.concatenate([_unpack(cw_all[2 * s], [cw_shape])[0] for s in range(4)], axis=2)

    halves = {k: (a[k].shape[1] // 2, a[k].shape[2]) for k in BIG}
    wbf = {k: a[k].astype(bf16) for k in BIG}
    ag_shapes = [((4, 2) + halves[k], bf16) for k in BIG]
    ag_transfers = [(fl, lambda x_, y_, c_: (c_,), lambda x_, y_, c_: (2 * x_ + y_, c_)) for fl in FLIPS_CHIP]
    fetched = [functools.partial(lambda x_, y_, c_, f: ((2 * x_ + y_) ^ f, c_), f=2 * fl[0] + fl[1]) for fl in FLIPS_CHIP]

    def ag_sources(l):
        return [wbf[k][l].reshape((2,) + halves[k]) for k in BIG]

    def layer_weights(l, got):
        got = [lax.dynamic_update_slice(g, s[None], (shard, 0, 0, 0)) for g, s in zip(got, ag_sources(l))]
        got = _exchange(f"ag_d2d_l{l}", got, None, [(FLIP_C, fn, fn) for fn in fetched], in_place=True)
        full = {k: g.reshape(4, 2 * halves[k][0], halves[k][1]) for k, g in zip(BIG, got)}
        p = {}
        for k in BIG[1:]:
            sh = full[k]
            p[k] = sh.transpose(1, 0, 2).reshape(sh.shape[1], -1) if k in COL_SHARDED else sh.reshape(-1, sh.shape[2])
        pieces = [full["w_in"][s] for s in range(4)]
        p["wm"] = jnp.concatenate(_ref_cols(pieces, 3072, 8192) + _ref_cols(pieces, 0, 3072)
                                  + _ref_cols(pieces, 8224, 10272), axis=1)
        p["wdt"] = _group_pad(jnp.concatenate(_ref_cols(pieces, 8192, 8224), axis=1))
        for k in SMALL:
            p[k] = conv_w_full[l] if k == "conv_w" else a[k][l]
        return _prep_layer(p)

    rs_transfers = [(fl, functools.partial(lambda x_, y_, c_, fs, fc: ((2 * x_ + y_) ^ fs, c_ ^ fc), fs=2 * fl[0] + fl[1], fc=fl[2]),
                     _at(j)) for j, fl in enumerate(FLIPS_ALL)]

    def rs_exchange(g, keys):
        srcs = []
        for k in keys:
            r, c = a[k].shape[1:]
            if k == "w_in":
                g32 = _group_unpad(g["wdt"])
                gk = jnp.stack([jnp.concatenate(_my_cols(g["wm"], g32, s * c, (s + 1) * c), axis=1) for s in range(4)])
            elif k in COL_SHARDED:
                gk = g[k].reshape(r, 4, c).transpose(1, 0, 2)
            else:
                gk = g[k]
            srcs.append(gk.astype(bf16).reshape((4, 2) + halves[k]))
        return srcs, [((7,) + halves[k], bf16) for k in keys], rs_transfers

    def layer_grads(l, keys, wires, got):
        red = [_reduce8(f"rs_sum_l{l}_{k}", w, r_, shard, ci) for k, w, r_ in zip(keys, wires, got)]
        my_half = lambda x_, y_, c_: (c_,)
        red = _exchange(f"rs_swap_l{l}_{len(keys)}", red, None, [(FLIP_C, my_half, my_half)], in_place=True)
        return {k: r_.reshape(a[k].shape[1:]) for k, r_ in zip(keys, red)}

    assert depth >= 2
    xt, memt = x.reshape(bsz * seq, D), mem.reshape(bsz * mlen, D)
    ws, saved = [], []
    got = _exchange("ag_ici_l0", ag_sources(0), ag_shapes, ag_transfers)
    for l in range(depth):
        ws.append(layer_weights(l, got))
        nxt = (ag_sources(l + 1), ag_shapes, ag_transfers) if l + 1 < depth else None
        xt, sv, got = _layer_fwd(l, xt, memt, ws[l], bsz, seq, mlen, nxt)
        saved.append(sv)
    gx, loss_lanes = loss_fwd_bwd("loss", xt, loss_target.reshape(bsz * seq, D))
    grads, gshard = [None] * depth, [None] * depth
    riding, early = None, {}
    for l in reversed(range(depth)):
        early_fn = (lambda g: early.setdefault("ex", rs_exchange(g, BIG[1:]))) if l == 0 else None
        gx, grads[l], got, got_early = _layer_bwd(l, gx, memt, ws[l], saved[l], bsz, seq, mlen, riding, early_fn)
        if riding is not None:
            gshard[l + 1] = layer_grads(l + 1, BIG, riding[0], got)
        riding = rs_exchange(grads[l], BIG) if l > 0 else None
    last = rs_exchange(grads[0], BIG[:1])
    gshard[0] = {**layer_grads(0, BIG[1:], early["ex"][0], got_early),
                 **layer_grads(0, BIG[:1], last[0], _exchange("rs_all_l0", *last))}
    grads = [_unprep_grads(g) for g in grads]

    out_g, out_d, out_m, out_v = {}, {}, {}, {}
    for k in BIG:
        shp = a[k].shape
        g = jnp.stack([gshard[l][k] for l in range(depth)])
        two_d = (shp[0] * shp[1], shp[2])
        d_, m_, v_ = adamw("adamw_" + k, a[k].reshape(two_d), g.reshape(two_d), a["m_" + k].reshape(two_d), a["v_" + k].reshape(two_d))
        out_g[k], out_d[k], out_m[k], out_v[k] = g, d_.reshape(shp), m_.reshape(shp), v_.reshape(shp)

    small_shapes = [(depth,) + (conv_w_full.shape[1:] if k == "conv_w" else a[k].shape[1:]) for k in SMALL]
    small = _pack([jnp.stack([grads[l][k] for l in range(depth)]) for k in SMALL] + [loss_lanes])
    total = _sum8("small_sum", _allgather8("ag_small", small, me))
    *gsmall, loss_l = _unpack(total, small_shapes + [loss_lanes.shape])
    gsmall = dict(zip(SMALL, gsmall))
    gsmall["conv_w"] = lax.dynamic_slice_in_dim(gsmall["conv_w"], shard * cw_shape[2], cw_shape[2], axis=2)
    loc_shapes = [a[k].shape for k in SMALL]
    d_, m_, v_ = adamw("adamw_small", _pack([a[k] for k in SMALL]), _pack([gsmall[k] for k in SMALL]),
                       _pack([a["m_" + k] for k in SMALL]), _pack([a["v_" + k] for k in SMALL]))
    for k, dd, mm, vv in zip(SMALL, _unpack(d_, loc_shapes), _unpack(m_, loc_shapes), _unpack(v_, loc_shapes)):
        out_g[k], out_d[k], out_m[k], out_v[k] = gsmall[k], dd, mm, vv

    loss = jnp.sum(loss_l)
    return (loss, gx.reshape(x.shape), *[out_g[k] for k in names], *[out_d[k] for k in names],
            *[out_m[k] for k in names], *[out_v[k] for k in names])
```

```python
import functools
import math

import jax
import jax.numpy as jnp
from jax import lax
from jax.experimental import pallas as pl
from jax.experimental.pallas import tpu as pltpu

f32, bf16 = jnp.float32, jnp.bfloat16

DEPTH = 4
D = 1024
SB_HEADS, SB_HD = 16, 64
SSM_INNER, SSM_HD, SSM_HEADS, SSM_GROUPS, SSM_STATE, SSM_CONV, SSM_CHUNK = 2048, 64, 32, 4, 128, 4, 128
HPG = SSM_HEADS // SSM_GROUPS
CONV_DIM = SSM_INNER + 2 * SSM_GROUPS * SSM_STATE
XA_HEADS, XA_HD = 4, 256
FFN = 2816
IN_WIDTH = 10272
RMS_EPS = 1e-6
LR, B1, B2, EPS, WD, STEP = 0.001, 0.9, 0.999, 1e-08, 0.01, 10

PM_W = 10240
OFF_Z, OFF_XBC, OFF_Q, OFF_K, OFF_V, OFF_GA, OFF_GS = 0, 2048, 5120, 6144, 7168, 8192, 9216
DT_W = SSM_GROUPS * 128

VMEM_LIMIT = 48 * 1024 * 1024
MESH = pl.DeviceIdType.MESH


def _cparams(sem):
    return pltpu.CompilerParams(dimension_semantics=sem, vmem_limit_bytes=VMEM_LIMIT)


def _tile(n):
    for t in (512, 256, 128):
        if n % t == 0:
            return t
    raise ValueError(f"dimension {n} is not a multiple of 128")


MM_VMEM_BUDGET = 34 * 1024 * 1024


def _mm_tiles(m, n, k, sa, sb, so):
    best = None
    for tm in (2048, 1024, 512, 256, 128):
        if m % tm:
            continue
        for tn in (2048, 1024, 512, 256, 128):
            if n % tn:
                continue
            for tk in (k, 2048, 1024, 512):
                if tk > k or k % tk:
                    continue
                vmem = 2 * (tm * tk * sa + tk * tn * sb + tm * tn * so) + tm * tn * 4 * (2 if tk < k else 1)
                if vmem > MM_VMEM_BUDGET:
                    continue
                traffic = m * k * sa * (n // tn) + k * n * sb * (m // tm) + m * n * so
                steps = (m // tm) * (n // tn) * (k // tk)
                accumulate = (k // tk > 1) * (k // tk) * m * n * 2
                key = (traffic + steps * 800_000 + accumulate, steps)
                if best is None or key < best[0]:
                    best = (key, (tm, tn, tk))
    assert best is not None, (m, n, k)
    return best[1]


def _mm(name, a, b, mode, out_dtype=f32):
    if mode == "nn":
        (m, k), (k2, n) = a.shape, b.shape
    elif mode == "nt":
        (m, k), (n, k2) = a.shape, b.shape
    else:
        (k, m), (k2, n) = a.shape, b.shape
    assert k == k2, (name, a.shape, b.shape, mode)
    tm, tn, tk = _mm_tiles(m, n, k, a.dtype.itemsize, b.dtype.itemsize, jnp.dtype(out_dtype).itemsize)
    nk = k // tk
    dn = {"nn": (((1,), (0,)), ((), ())), "nt": (((1,), (1,)), ((), ())), "tn": (((0,), (0,)), ((), ()))}[mode]

    def product(a_ref, b_ref):
        return lax.dot_general(a_ref[...].astype(bf16), b_ref[...].astype(bf16), dn, preferred_element_type=f32)

    def body_whole_k(a_ref, b_ref, o_ref):
        o_ref[...] = product(a_ref, b_ref).astype(o_ref.dtype)

    def body_k_loop(a_ref, b_ref, o_ref, acc_ref):
        kk = pl.program_id(2)

        @pl.when(kk == 0)
        def _():
            acc_ref[...] = product(a_ref, b_ref)

        @pl.when(kk > 0)
        def _():
            acc_ref[...] += product(a_ref, b_ref)

        @pl.when(kk == nk - 1)
        def _():
            o_ref[...] = acc_ref[...].astype(o_ref.dtype)

    a_spec = pl.BlockSpec((tk, tm), lambda i, j, kk: (kk, i)) if mode == "tn" else pl.BlockSpec((tm, tk), lambda i, j, kk: (i, kk))
    b_spec = pl.BlockSpec((tn, tk), lambda i, j, kk: (j, kk)) if mode == "nt" else pl.BlockSpec((tk, tn), lambda i, j, kk: (kk, j))
    return pl.pallas_call(
        body_whole_k if nk == 1 else body_k_loop, name=name, grid=(m // tm, n // tn, nk),
        in_specs=[a_spec, b_spec],
        out_specs=pl.BlockSpec((tm, tn), lambda i, j, kk: (i, j)),
        out_shape=jax.ShapeDtypeStruct((m, n), out_dtype),
        scratch_shapes=[] if nk == 1 else [pltpu.VMEM((tm, tn), f32)],
        compiler_params=_cparams(("parallel", "parallel", "arbitrary")),
    )(a, b)


def _rowwise(name, fn, rows, consts, out_rows, out_accs, *, width, ncol=1, bt=256, exchange=None):
    r = rows[0][0].shape[0]
    bt = min(bt, r)
    assert r % bt == 0, (name, r, bt)
    nrow = r // bt
    n_in = len(rows) + len(consts)
    n_or = len(out_rows)
    na = len(exchange[0]) if exchange else 0
    n_out = n_or + len(out_accs)

    def body(*refs):
        ins = [ref[...].astype(f32) for ref in refs[:n_in]]
        outs = fn(*ins)
        if not isinstance(outs, (tuple, list)):
            outs = (outs,)
        o_refs = refs[n_in + na:n_in + na + n_out]
        if exchange:
            _exchange_hook(exchange, refs[n_in:n_in + na], refs[n_in + na + n_out:n_in + 2 * na + n_out],
                           refs[n_in + 2 * na + n_out:], *_grid_ends((ncol, nrow)))
        for o_ref, val in zip(o_refs[:n_or], outs[:n_or]):
            o_ref[...] = val.astype(o_ref.dtype)
        if out_accs:
            i = pl.program_id(1)
            for o_ref, val in zip(o_refs[n_or:], outs[n_or:]):
                @pl.when(i == 0)
                def _(o_ref=o_ref, val=val):
                    o_ref[...] = val

                @pl.when(i > 0)
                def _(o_ref=o_ref, val=val):
                    o_ref[...] += val

    in_specs = [pl.BlockSpec((bt, width), functools.partial(lambda j, i, off: (i, off + j), off=off)) for _, off in rows]
    in_specs += [pl.BlockSpec((c.shape[0], width), functools.partial(lambda j, i, off: (0, off + j), off=off)) for c, off in consts]
    out_specs = [pl.BlockSpec((bt, mlt * width), lambda j, i: (i, j)) for mlt, _ in out_rows]
    out_specs += [pl.BlockSpec((k, width), lambda j, i: (0, j)) for k in out_accs]
    out_shape = [jax.ShapeDtypeStruct((r, ncol * mlt * width), dt) for mlt, dt in out_rows]
    out_shape += [jax.ShapeDtypeStruct((k, ncol * width), f32) for k in out_accs]
    if exchange:
        in_specs += [HBM_SPEC] * na
        out_specs += [HBM_SPEC] * na
        out_shape += [jax.ShapeDtypeStruct(shape, dtype) for shape, dtype in exchange[1]]
    return pl.pallas_call(
        body, name=name, grid=(ncol, nrow), in_specs=in_specs, out_specs=out_specs, out_shape=out_shape,
        scratch_shapes=_copy_semaphores(na, exchange[2]) if exchange else [],
        compiler_params=_cparams(("arbitrary", "arbitrary") if exchange else
                                 ("parallel", "arbitrary" if out_accs else "parallel")),
    )(*[a for a, _ in rows], *[c for c, _ in consts], *(exchange[0] if exchange else []))


def _rms(x, g):
    return x * lax.rsqrt(jnp.mean(x * x, axis=-1, keepdims=True) + RMS_EPS) * g


def _silu(x):
    return x * jax.nn.sigmoid(x)


def _softplus(x):
    return jnp.maximum(x, 0.0) + jnp.log(1.0 + jnp.exp(-jnp.abs(x)))


def _colsum(x):
    return jnp.sum(x, axis=0, keepdims=True)


def rms_fwd(name, x, g):
    return _rowwise(name, _rms, [(x, 0)], [(g, 0)], [(1, bf16)], [], width=D)[0]


def rms_bwd(name, x, g, dhs, dres=None):
    nd = len(dhs)

    def fn(x, *rest):
        dh = rest[0]
        for extra in rest[1:nd]:
            dh = dh + extra
        g = rest[-1]
        _, vjp = jax.vjp(_rms, x, g)
        dx, dg = vjp(dh.astype(f32))
        if dres is not None:
            dx = dx + rest[nd]
        return dx, dg

    rows = [(x, 0)] + [(d, 0) for d in dhs] + ([(dres, 0)] if dres is not None else [])
    return _rowwise(name, fn, rows, [(g, 0)], [(1, f32)], [1], width=D)


def addnorm_fwd(name, x, u, g):
    return _rowwise(name, lambda x, u, g: x + _rms(u, g), [(x, 0), (u, 0)], [(g, 0)], [(1, f32)], [], width=D)[0]


def addnorm_rms_fwd(name, x, u, g, g_next):
    def fn(x, u, g, g_next):
        x_new = x + _rms(u, g)
        return x_new, _rms(x_new, g_next)

    return _rowwise(name, fn, [(x, 0), (u, 0)], [(g, 0), (g_next, 0)], [(1, f32), (1, bf16)], [], width=D)


def addnorm_bwd(name, u, g, dx):
    def fn(u, dx, g):
        _, vjp = jax.vjp(_rms, u, g)
        return vjp(dx)

    return _rowwise(name, fn, [(u, 0), (dx, 0)], [(g, 0)], [(1, bf16)], [1], width=D)


def _merge(ga, gs, a, s):
    return jax.nn.sigmoid(ga) * a + jax.nn.sigmoid(gs) * s


def merge_fwd(name, pm, a, s):
    return _rowwise(name, _merge, [(pm, OFF_GA // D), (pm, OFF_GS // D), (a, 0), (s, 0)], [], [(1, bf16)], [], width=D)[0]


def merge_bwd(name, pm, a, s, dm):
    def fn(ga, gs, a, s, dm):
        _, vjp = jax.vjp(_merge, ga, gs, a, s)
        dga, dgs, da, ds = vjp(dm)
        return jnp.concatenate([dga, dgs], axis=1), da, ds

    return _rowwise(name, fn, [(pm, OFF_GA // D), (pm, OFF_GS // D), (a, 0), (s, 0), (dm, 0)], [],
                    [(2, bf16), (1, bf16), (1, bf16)], [], width=D)


def _swiglu(gate, up):
    return _silu(gate) * up


def swiglu_fwd(name, gu):
    return _rowwise(name, _swiglu, [(gu, 0), (gu, 1)], [], [(1, bf16)], [], width=FFN)[0]


def swiglu_bwd(name, gu, dact):
    def fn(gate, up, dact):
        _, vjp = jax.vjp(_swiglu, gate, up)
        dg, du = vjp(dact.astype(f32))
        return jnp.concatenate([dg, du], axis=1)

    return _rowwise(name, fn, [(gu, 0), (gu, 1), (dact, 0)], [], [(2, bf16)], [], width=FFN, bt=128)[0]


GW = SSM_INNER // SSM_GROUPS


def _gnorm(y, xs, z, dskip, gn):
    yy = (y + dskip * xs) * _silu(z)
    return yy * lax.rsqrt(jnp.mean(yy * yy, axis=-1, keepdims=True) + RMS_EPS) * gn


def gnorm_fwd(name, y, xbc, pm, dskip, gn):
    return _rowwise(name, _gnorm, [(y, 0), (xbc, 0), (pm, OFF_Z // GW)], [(dskip, 0), (gn, 0)], [(1, bf16)], [],
                    width=GW, ncol=SSM_GROUPS)[0]


def gnorm_bwd(name, y, xbc, pm, dskip, gn, do):
    def fn(y, xs, z, do, dskip, gn):
        _, vjp = jax.vjp(_gnorm, y, xs, z, dskip, gn)
        return vjp(do.astype(f32))

    return _rowwise(name, fn, [(y, 0), (xbc, 0), (pm, OFF_Z // GW), (do, 0)], [(dskip, 0), (gn, 0)],
                    [(1, f32), (1, f32), (1, bf16)], [1, 1], width=GW, ncol=SSM_GROUPS)


def _dtfn(pdt, bias, alog):
    dt = _softplus(pdt + bias)
    return dt, -jnp.exp(alog) * dt


def dt_fwd(name, pdt, bias, alog):
    return _rowwise(name, _dtfn, [(pdt, 0)], [(bias, 0), (alog, 0)], [(1, f32), (1, f32)], [], width=DT_W)


def dt_bwd(name, pdt, bias, alog, ddt, dadt_c, dadt_r):
    def fn(pdt, ddt, dac, dar, bias, alog):
        _, vjp = jax.vjp(_dtfn, pdt, bias, alog)
        return vjp((ddt, dac + dar))

    return _rowwise(name, fn, [(pdt, 0), (ddt, 0), (dadt_c, 0), (dadt_r, 0)], [(bias, 0), (alog, 0)],
                    [(1, f32)], [1, 1], width=DT_W)


def loss_fwd_bwd(name, y, target):
    def fn(y, t):
        e = y - t
        return e * (1.0 / D), _colsum(e * e) * (0.5 / D)

    return _rowwise(name, fn, [(y, 0), (target, 0)], [], [(1, f32)], [1], width=D)


def adamw(name, w, g, m, v, exchange=None):
    r, c = w.shape

    def fn(w, g, m, v):
        m = B1 * m + (1.0 - B1) * g
        v = B2 * v + (1.0 - B2) * (g * g)
        m_hat = m / (1.0 - B1 ** STEP)
        v_hat = v / (1.0 - B2 ** STEP)
        return -LR * (m_hat / (jnp.sqrt(v_hat) + EPS) + WD * w), m, v

    bt = 256
    while bt > 8 and (r % bt or bt * c * 4 * 7 * 2 > 16 * 1024 * 1024):
        bt //= 2
    if r % bt:
        bt = r
    return _rowwise(name, fn, [(w, 0), (g, 0), (m, 0), (v, 0)], [], [(1, f32)] * 3, [], width=c, bt=bt, exchange=exchange)


SB_BQ, SB_BK = 512, 256
SB_UNROLL = 2
SB_SUB = 256


def _dot(a, b):
    return jnp.dot(a, b, preferred_element_type=f32)


def _dot_nt(a, b):
    return lax.dot_general(a, b, (((1,), (1,)), ((), ())), preferred_element_type=f32)


def _dot_tn(a, b):
    return lax.dot_general(a, b, (((0,), (0,)), ((), ())), preferred_element_type=f32)


def _dot2(x, tri):
    hi = x.astype(bf16)
    lo = (x - hi.astype(f32)).astype(bf16)
    return _dot(hi, tri) + _dot(lo, tri)


def _running(x, tri, reverse=False, exclusive=False, two_pass=True):
    sub = tri.shape[0]
    n = x.shape[1] // sub
    out, carry = [None] * n, None
    for j in (range(n - 1, -1, -1) if reverse else range(n)):
        xj = x[:, j * sub:(j + 1) * sub]
        pj = _dot2(xj, tri) if two_pass else _dot(xj.astype(bf16), tri)
        if carry is not None:
            pj = pj + carry
        out[j] = pj
        carry = pj[:, 0:1] if reverse else pj[:, sub - 1:sub]
        if exclusive:
            carry = carry + xj[:, sub - 1:sub]
    return out[0] if n == 1 else jnp.concatenate(out, axis=1)


def _tri(n, rel):
    r = lax.broadcasted_iota(jnp.int32, (n, n), 0)
    c = lax.broadcasted_iota(jnp.int32, (n, n), 1)
    m = {"ge": r >= c, "lt": r < c, "le": r <= c}[rel]
    return jnp.where(m, 1.0, 0.0).astype(bf16)


def _add_rows(full, r0, upd):
    return full + upd if r0 == 0 else jnp.concatenate([full[:r0], full[r0:] + upd], axis=0)


def _grid_ends(grid):
    ids = [pl.program_id(d) for d in range(len(grid))]
    first = functools.reduce(jnp.logical_and, [i == 0 for i in ids])
    last = functools.reduce(jnp.logical_and, [i == n - 1 for i, n in zip(ids, grid)])
    return first, last


def sb_fwd(name, pm, bsz, seq, exchange=None):
    bq = min(SB_BQ, seq)
    bk = min(SB_BK, bq)
    nq, nd = seq // bq, bq // bk
    step = SB_UNROLL if nd % SB_UNROLL == 0 else 1
    scale = SB_HD ** -0.5
    qb, kb_, vb_ = OFF_Q // 128, OFF_K // 128, OFF_V // 128
    na = len(exchange[0]) if exchange else 0
    grid = (bsz, 8, nq)

    def body(*refs):
        q_ref, k_ref, v_ref = refs[:3]
        o_ref, tot_ref = refs[3 + na:5 + na]
        if exchange:
            _exchange_hook(exchange, refs[3:3 + na], refs[5 + na:5 + 2 * na], refs[5 + 2 * na:], *_grid_ends(grid))
        i = pl.program_id(2)
        lane = lax.broadcasted_iota(jnp.int32, (1, 128), 1)
        m0 = lane < SB_HD
        q = q_ref[...].astype(f32) * scale
        qs = (jnp.where(m0, q, 0.0).astype(bf16), jnp.where(m0, 0.0, q).astype(bf16))
        wide = step * bk
        neg_tri = -_tri(min(SB_SUB, bk), "ge")
        t_idx = i * bq + lax.broadcasted_iota(jnp.int32, (bq, 1), 0)

        def block(ks, carry, masked, width):
            o_acc, c0, c1 = carry
            kblk = k_ref[pl.ds(ks, width), :].astype(bf16)
            vblk = v_ref[pl.ds(ks, width), :].astype(bf16)
            vs = (jnp.where(m0, vblk, 0).astype(bf16), jnp.where(m0, 0, vblk).astype(bf16))
            if masked:
                valid = (ks + lax.broadcasted_iota(jnp.int32, (1, width), 1)) < t_idx
            cs = [c0, c1]
            for h in range(2):
                z = _dot_nt(qs[h], kblk)
                sp = _softplus(z)
                if masked:
                    sp = jnp.where(valid, sp, 0.0)
                tl = _running(sp, neg_tri, reverse=True)
                w = jnp.exp(z + tl + cs[h])
                if masked:
                    w = jnp.where(valid, w, 0.0)
                o_acc = o_acc + _dot(w.astype(bf16), vs[h])
                cs[h] = cs[h] + tl[:, 0:1]
            return o_acc, cs[0], cs[1]

        zc = jnp.zeros((bq, 1), f32)
        carry = (jnp.zeros((bq, 128), f32), zc, zc)
        for d in range(nd):
            carry = block(pl.multiple_of((i * nd + nd - 1 - d) * bk, bk), carry, True, bk)
        carry = lax.fori_loop(0, i * (nd // step),
                              lambda n, c: block(pl.multiple_of((i * nd - step * (n + 1)) * bk, wide), c, False, wide), carry)
        o, c0, c1 = carry
        o_ref[...] = o.astype(o_ref.dtype)
        tot_ref[0, 0] = jnp.where(m0, c0, c1)

    return pl.pallas_call(
        body, name=name, grid=grid,
        in_specs=[pl.BlockSpec((bq, 128), lambda b, p, i: (b * nq + i, qb + p)),
                  pl.BlockSpec((seq, 128), lambda b, p, i: (b, kb_ + p)),
                  pl.BlockSpec((seq, 128), lambda b, p, i: (b, vb_ + p))] + [HBM_SPEC] * na,
        out_specs=[pl.BlockSpec((bq, 128), lambda b, p, i: (b * nq + i, p)),
                   pl.BlockSpec((1, 1, bq, 128), lambda b, p, i: (b, p, i, 0))] + [HBM_SPEC] * na,
        out_shape=[jax.ShapeDtypeStruct((bsz * seq, 1024), bf16), jax.ShapeDtypeStruct((bsz, 8, seq, 128), f32)]
        + ([jax.ShapeDtypeStruct(shape, dtype) for shape, dtype in exchange[1]] if exchange else []),
        scratch_shapes=_copy_semaphores(na, exchange[2]) if exchange else [],
        compiler_params=_cparams(("arbitrary",) * 3 if exchange else ("parallel",) * 3),
    )(pm, pm, pm, *(exchange[0] if exchange else []))


def sb_bwd(name, pm, tot, do, bsz, seq, exchange=None):
    bq = min(SB_BQ, seq)
    bk = min(SB_BK, bq)
    nq, nd = seq // bq, bq // bk
    step = SB_UNROLL if nd % SB_UNROLL == 0 else 1
    scale = SB_HD ** -0.5
    qb, kb_, vb_ = OFF_Q // 128, OFF_K // 128, OFF_V // 128
    na = len(exchange[0]) if exchange else 0
    grid = (bsz, 8, nq)

    def body(*refs):
        q_ref, k_ref, v_ref, do_ref, tot_ref = refs[:5]
        dq_ref, dk_ref, dv_ref = refs[5 + na:8 + na]
        dk_acc, dv_acc = refs[8 + 2 * na:10 + 2 * na]
        if exchange:
            _exchange_hook(exchange, refs[5:5 + na], refs[8 + na:8 + 2 * na], refs[10 + 2 * na:], *_grid_ends(grid))
        i = pl.program_id(2)

        @pl.when(i == 0)
        def _():
            dk_acc[...] = jnp.zeros_like(dk_acc)
            dv_acc[...] = jnp.zeros_like(dv_acc)

        lane = lax.broadcasted_iota(jnp.int32, (1, 128), 1)
        m0 = lane < SB_HD
        ms = (m0, jnp.logical_not(m0))
        q = q_ref[...].astype(f32) * scale
        qpair = q.astype(bf16)
        qs = (jnp.where(m0, q, 0.0).astype(bf16), jnp.where(m0, 0.0, q).astype(bf16))
        dout = do_ref[...].astype(f32)
        dos = (jnp.where(m0, dout, 0.0).astype(bf16), jnp.where(m0, 0.0, dout).astype(bf16))
        tot = tot_ref[0, 0]
        tots = (tot[:, 0:1], tot[:, SB_HD:SB_HD + 1])
        tri_lt = _tri(min(SB_SUB, bk), "lt")
        tri_le = _tri(min(SB_SUB, bk), "le")
        t_idx = i * bq + lax.broadcasted_iota(jnp.int32, (bq, 1), 0)

        def block(ks, carry, masked, r0=0):
            dq_acc, p0, p1, g0, g1 = carry
            kblk = k_ref[pl.ds(ks, bk), :].astype(bf16)
            vblk = v_ref[pl.ds(ks, bk), :].astype(bf16)
            if masked:
                valid = (ks + lax.broadcasted_iota(jnp.int32, (1, bk), 1)) < t_idx[r0:]
            ps, gs = [p0, p1], [g0, g1]
            dk_blk = jnp.zeros((bk, 128), f32)
            dv_blk = jnp.zeros((bk, 128), f32)
            for h in range(2):
                z = _dot_nt(qs[h][r0:], kblk)
                sp = _softplus(z)
                sig = jnp.exp(z - sp)
                if masked:
                    sp = jnp.where(valid, sp, 0.0)
                w = jnp.exp(z + tots[h][r0:] + ps[h][r0:] + _running(sp, tri_lt, exclusive=True))
                if masked:
                    w = jnp.where(valid, w, 0.0)
                g = _dot_nt(dos[h][r0:], vblk) * w
                dz = g - sig * (gs[h][r0:] + _running(g, tri_le, two_pass=False))
                if masked:
                    dz = jnp.where(valid, dz, 0.0)
                dz = dz.astype(bf16)
                dq_acc = _add_rows(dq_acc, r0, jnp.where(ms[h], _dot(dz, kblk), 0.0))
                dk_blk = dk_blk + jnp.where(ms[h], _dot_tn(dz, qpair[r0:]), 0.0)
                dv_blk = dv_blk + _dot_tn(w.astype(bf16), dos[h][r0:])
                ps[h] = _add_rows(ps[h], r0, jnp.sum(sp, axis=1, keepdims=True))
                gs[h] = _add_rows(gs[h], r0, jnp.sum(g, axis=1, keepdims=True))
            dk_acc[pl.ds(ks, bk), :] += dk_blk
            dv_acc[pl.ds(ks, bk), :] += dv_blk
            return dq_acc, ps[0], ps[1], gs[0], gs[1]

        zc = jnp.zeros((bq, 1), f32)
        carry = (jnp.zeros((bq, 128), f32), zc, zc, zc, zc)
        def far(n, c):
            for u in range(step):
                c = block(pl.multiple_of((step * n + u) * bk, bk), c, False)
            return c

        carry = lax.fori_loop(0, i * (nd // step), far, carry)
        for d in range(nd):
            carry = block(pl.multiple_of((i * nd + d) * bk, bk), carry, True, d * bk)
        dq_ref[...] = (carry[0] * scale).astype(dq_ref.dtype)

        @pl.when(i == nq - 1)
        def _():
            dk_ref[...] = dk_acc[...].astype(dk_ref.dtype)
            dv_ref[...] = dv_acc[...].astype(dv_ref.dtype)

    return pl.pallas_call(
        body, name=name, grid=grid,
        in_specs=[pl.BlockSpec((bq, 128), lambda b, p, i: (b * nq + i, qb + p)),
                  pl.BlockSpec((seq, 128), lambda b, p, i: (b, kb_ + p)),
                  pl.BlockSpec((seq, 128), lambda b, p, i: (b, vb_ + p)),
                  pl.BlockSpec((bq, 128), lambda b, p, i: (b * nq + i, p)),
                  pl.BlockSpec((1, 1, bq, 128), lambda b, p, i: (b, p, i, 0))] + [HBM_SPEC] * na,
        out_specs=[pl.BlockSpec((bq, 128), lambda b, p, i: (b * nq + i, p)),
                   pl.BlockSpec((seq, 128), lambda b, p, i: (b, p)),
                   pl.BlockSpec((seq, 128), lambda b, p, i: (b, p))] + [HBM_SPEC] * na,
        out_shape=[jax.ShapeDtypeStruct((bsz * seq, 1024), bf16)] * 3
        + ([jax.ShapeDtypeStruct(shape, dtype) for shape, dtype in exchange[1]] if exchange else []),
        scratch_shapes=[pltpu.VMEM((seq, 128), f32), pltpu.VMEM((seq, 128), f32)]
        + (_copy_semaphores(na, exchange[2]) if exchange else []),
        compiler_params=_cparams(("arbitrary",) * 3 if exchange else ("parallel", "parallel", "arbitrary")),
    )(pm, pm, pm, do, tot, *(exchange[0] if exchange else []))


CONV_CB = 256


def _shift_down(x, d, rows):
    return x if d == 0 else jnp.where(rows >= d, pltpu.roll(x, d, axis=0), 0.0)


def _shift_up(x, d, rows, n):
    return x if d == 0 else jnp.where(rows < n - d, pltpu.roll(x, n - d, axis=0), 0.0)


def conv_fwd(name, pm, w, b, bsz, seq):
    nc = CONV_DIM // CONV_CB
    off = OFF_XBC // CONV_CB

    def body(x_ref, w_ref, b_ref, o_ref):
        x = x_ref[...].astype(f32)
        rows = lax.broadcasted_iota(jnp.int32, x.shape, 0)
        pre = b_ref[...] + jnp.zeros_like(x)
        for k in range(SSM_CONV):
            pre = pre + w_ref[k:k + 1, :] * _shift_down(x, SSM_CONV - 1 - k, rows)
        o_ref[...] = _silu(pre)

    return pl.pallas_call(
        body, name=name, grid=(nc, bsz),
        in_specs=[pl.BlockSpec((seq, CONV_CB), lambda j, bb: (bb, off + j)),
                  pl.BlockSpec((SSM_CONV, CONV_CB), lambda j, bb: (0, j)),
                  pl.BlockSpec((1, CONV_CB), lambda j, bb: (0, j))],
        out_specs=pl.BlockSpec((seq, CONV_CB), lambda j, bb: (bb, j)),
        out_shape=jax.ShapeDtypeStruct((bsz * seq, CONV_DIM), f32),
        compiler_params=_cparams(("parallel", "parallel")),
    )(pm, w, b)


def conv_bwd(name, pm, w, b, dxs, dbm, dcm, dskipx, bsz, seq):
    nc = CONV_DIM // CONV_CB
    off = OFF_XBC // CONV_CB
    nxs = SSM_INNER // CONV_CB
    nbc = SSM_GROUPS * SSM_STATE // CONV_CB

    def body(x_ref, w_ref, b_ref, dxs_ref, dbm_ref, dcm_ref, ds_ref, dx_ref, dw_ref, db_ref):
        j, bb = pl.program_id(0), pl.program_id(1)
        x = x_ref[...].astype(f32)
        rows = lax.broadcasted_iota(jnp.int32, x.shape, 0)
        xsh = [_shift_down(x, SSM_CONV - 1 - k, rows) for k in range(SSM_CONV)]
        pre = b_ref[...] + jnp.zeros_like(x)
        for k in range(SSM_CONV):
            pre = pre + w_ref[k:k + 1, :] * xsh[k]
        sig = jax.nn.sigmoid(pre)
        dout = jnp.where(j < nxs, dxs_ref[...] + ds_ref[...], jnp.where(j < nxs + nbc, dbm_ref[...], dcm_ref[...]))
        dpre = dout * (sig * (1.0 + pre * (1.0 - sig)))
        dx = jnp.zeros_like(x)
        for k in range(SSM_CONV):
            dx = dx + w_ref[k:k + 1, :] * _shift_up(dpre, SSM_CONV - 1 - k, rows, seq)
        dx_ref[...] = dx.astype(dx_ref.dtype)
        dw = jnp.concatenate([_colsum(dpre * xsh[k]) for k in range(SSM_CONV)], axis=0)
        db = _colsum(dpre)

        @pl.when(bb == 0)
        def _():
            dw_ref[...] = dw
            db_ref[...] = db

        @pl.when(bb > 0)
        def _():
            dw_ref[...] += dw
            db_ref[...] += db

    return pl.pallas_call(
        body, name=name, grid=(nc, bsz),
        in_specs=[pl.BlockSpec((seq, CONV_CB), lambda j, bb: (bb, off + j)),
                  pl.BlockSpec((SSM_CONV, CONV_CB), lambda j, bb: (0, j)),
                  pl.BlockSpec((1, CONV_CB), lambda j, bb: (0, j)),
                  pl.BlockSpec((seq, CONV_CB), lambda j, bb: (bb, jnp.minimum(j, nxs - 1))),
                  pl.BlockSpec((seq, CONV_CB), lambda j, bb: (bb, jnp.clip(j - nxs, 0, nbc - 1))),
                  pl.BlockSpec((seq, CONV_CB), lambda j, bb: (bb, jnp.clip(j - nxs - nbc, 0, nbc - 1))),
                  pl.BlockSpec((seq, CONV_CB), lambda j, bb: (bb, jnp.minimum(j, nxs - 1)))],
        out_specs=[pl.BlockSpec((seq, CONV_CB), lambda j, bb: (bb, j)),
                   pl.BlockSpec((SSM_CONV, CONV_CB), lambda j, bb: (0, j)),
                   pl.BlockSpec((1, CONV_CB), lambda j, bb: (0, j))],
        out_shape=[jax.ShapeDtypeStruct((bsz * seq, CONV_DIM), bf16), jax.ShapeDtypeStruct((SSM_CONV, CONV_DIM), f32),
                   jax.ShapeDtypeStruct((1, CONV_DIM), f32)],
        compiler_params=_cparams(("parallel", "arbitrary")),
    )(pm, w, b, dxs, dbm, dcm, dskipx)


CL = SSM_CHUNK


def _dot3(a, b, split_a):
    x = a if split_a else b
    t1 = x.astype(bf16)
    r1 = x - t1.astype(f32)
    t2 = r1.astype(bf16)
    t3 = (r1 - t2.astype(f32)).astype(bf16)
    if split_a:
        return _dot(t1, b) + _dot(t2, b) + _dot(t3, b)
    return _dot(a, t1) + _dot(a, t2) + _dot(a, t3)


def _ssd_specs(bsz, seq, rev):
    nch = seq // CL

    def ch(c):
        return (nch - 1 - c) if rev else c

    xg = pl.BlockSpec((CL, GW), lambda b, g, c: (b * nch + ch(c), g))
    lane128 = pl.BlockSpec((CL, 128), lambda b, g, c: (b * nch + ch(c), g))
    adt_t = pl.BlockSpec((128, CL), lambda b, g, c: (g, b * nch + ch(c)))
    bspec = pl.BlockSpec((CL, 128), lambda b, g, c: (b * nch + ch(c), SSM_INNER // 128 + g))
    cspec = pl.BlockSpec((CL, 128), lambda b, g, c: (b * nch + ch(c), SSM_INNER // 128 + SSM_GROUPS + g))
    st = pl.BlockSpec((1, 1, 1, SSM_STATE, GW), lambda b, g, c: (b, ch(c), g, 0, 0))
    return nch, xg, lane128, adt_t, bspec, cspec, st


def _expand_mat(width):
    r = lax.broadcasted_iota(jnp.int32, (128, HPG * width), 0)
    c = lax.broadcasted_iota(jnp.int32, (128, HPG * width), 1)
    return jnp.where((c >= r * width) & (c < (r + 1) * width), 1.0, 0.0).astype(bf16)


def _head_sums(z, e):
    hi = z.astype(bf16)
    lo = (z - hi.astype(f32)).astype(bf16)
    return _dot_nt(hi, e) + _dot_nt(lo, e)


def _ssd_common(dt_ref, adt_ref, adtt_ref):
    e64, e128 = _expand_mat(SSM_HD), _expand_mat(CL)
    csc = _dot3(_tri(CL, "ge"), adt_ref[...], False)
    csr = _dot3(adtt_ref[0:HPG, :], _tri(CL, "le"), True)
    return e64, csc, csr, _dot3(dt_ref[...], e64, True), _dot3(csc, e64, True), _dot3(csc, e128, True)


def ssd_fwd(name, xbc, dt, adt, adt_t, bsz, seq):
    nch, xg, lane128, adt_t_spec, bspec, cspec, st = _ssd_specs(bsz, seq, False)

    def body(x_ref, dt_ref, adt_ref, adtt_ref, b_ref, c_ref, y_ref, st_ref, s_scr, xd_scr):
        @pl.when(pl.program_id(2) == 0)
        def _():
            s_scr[...] = jnp.zeros_like(s_scr)

        _, _, csr, dt_e, cs_e, cs_b = _ssd_common(dt_ref, adt_ref, adtt_ref)
        cs_last = cs_e[CL - 1:CL, :]
        bm, cm = b_ref[...].astype(bf16), c_ref[...].astype(bf16)
        s_in = s_scr[...]
        st_ref[0, 0, 0] = s_in
        xd = x_ref[...] * dt_e
        xd_scr[...] = xd.astype(bf16)
        y_ref[...] = _dot(cm, s_in.astype(bf16)) * jnp.exp(cs_e)
        w = xd * jnp.exp(cs_last - cs_e)
        s_scr[...] = s_in * jnp.exp(cs_last) + _dot_tn(bm, w.astype(bf16))
        cb = _dot_nt(cm, bm)
        row = lax.broadcasted_iota(jnp.int32, (CL, CL), 0)
        col = lax.broadcasted_iota(jnp.int32, (CL, CL), 1)
        for h in range(HPG):
            hs = slice(h * SSM_HD, (h + 1) * SSM_HD)
            decay = jnp.exp(jnp.where(row >= col, cs_b[:, h * CL:(h + 1) * CL] - csr[h:h + 1, :], -1e30))
            y_ref[:, hs] += _dot((cb * decay).astype(bf16), xd_scr[:, hs])

    return pl.pallas_call(
        body, name=name, grid=(bsz, SSM_GROUPS, nch),
        in_specs=[xg, lane128, lane128, adt_t_spec, bspec, cspec],
        out_specs=[xg, st],
        out_shape=[jax.ShapeDtypeStruct((bsz * seq, SSM_INNER), f32),
                   jax.ShapeDtypeStruct((bsz, nch, SSM_GROUPS, SSM_STATE, GW), f32)],
        scratch_shapes=[pltpu.VMEM((SSM_STATE, GW), f32), pltpu.VMEM((CL, GW), bf16)],
        compiler_params=_cparams(("parallel", "parallel", "arbitrary")),
    )(xbc, dt, adt, adt_t, xbc, xbc)


def ssd_bwd(name, xbc, dt, adt, adt_t, states, dy, bsz, seq, exchange=None):
    nch, xg, lane128, adt_t_spec, bspec, cspec, st = _ssd_specs(bsz, seq, True)
    na = len(exchange[0]) if exchange else 0
    grid = (bsz, SSM_GROUPS, nch)

    def body(*refs):
        x_ref, dt_ref, adt_ref, adtt_ref, b_ref, c_ref, st_ref, dy_ref = refs[:8]
        dx_ref, db_ref, dc_ref, ddt_ref, dac_ref, dar_ref = refs[8 + na:14 + na]
        ds_scr, xd_scr, dxd_scr = refs[14 + 2 * na:17 + 2 * na]
        if exchange:
            _exchange_hook(exchange, refs[8:8 + na], refs[14 + na:14 + 2 * na], refs[17 + 2 * na:], *_grid_ends(grid))

        @pl.when(pl.program_id(2) == 0)
        def _():
            ds_scr[...] = jnp.zeros_like(ds_scr)

        e64, _, csr, dt_e, cs_e, cs_b = _ssd_common(dt_ref, adt_ref, adtt_ref)
        cs_last = cs_e[CL - 1:CL, :]
        bm, cm = b_ref[...].astype(bf16), c_ref[...].astype(bf16)
        x, dy, s_in, ds_out = x_ref[...], dy_ref[...], st_ref[0, 0, 0], ds_scr[...]
        e_last = jnp.exp(cs_last)
        d_end = jnp.exp(cs_last - cs_e)
        xd = x * dt_e
        xd_scr[...] = xd.astype(bf16)
        w = xd * d_end
        dq = dy * jnp.exp(cs_e)
        dc = _dot_nt(dq.astype(bf16), s_in.astype(bf16))
        ds_scr[...] = _dot_tn(cm, dq.astype(bf16)) + ds_out * e_last
        dw = _dot(bm, ds_out.astype(bf16))
        db = _dot_nt(w.astype(bf16), ds_out.astype(bf16))
        rw = dw * w
        dcs_e = dq * _dot(cm, s_in.astype(bf16)) - rw
        dcs_last = _colsum(rw) + _colsum(ds_out * s_in) * e_last
        is_last = lax.broadcasted_iota(jnp.int32, (CL, 1), 0) == CL - 1
        dcs_e = dcs_e + jnp.where(is_last, dcs_last, 0.0)
        dxd_scr[...] = dw * d_end
        cb, cbt = _dot_nt(cm, bm), _dot_nt(bm, cm)
        row = lax.broadcasted_iota(jnp.int32, (CL, CL), 0)
        col = lax.broadcasted_iota(jnp.int32, (CL, CL), 1)
        lane = lax.broadcasted_iota(jnp.int32, (CL, 128), 1)
        sub = lax.broadcasted_iota(jnp.int32, (HPG, CL), 0)
        dcb = jnp.zeros((CL, CL), f32)
        r_rows = jnp.zeros((CL, 128), f32)
        r_cols = jnp.zeros((HPG, CL), f32)
        for h in range(HPG):
            hs = slice(h * SSM_HD, (h + 1) * SSM_HD)
            diff = cs_b[:, h * CL:(h + 1) * CL] - csr[h:h + 1, :]
            decay = jnp.exp(jnp.where(row >= col, diff, -1e30))
            decay_t = jnp.exp(jnp.where(col >= row, -diff, -1e30))
            dy_h = dy_ref[:, hs].astype(bf16)
            dm = _dot_nt(dy_h, xd_scr[:, hs])
            dxd_scr[:, hs] += _dot((cbt * decay_t).astype(bf16), dy_h)
            r = dm * (cb * decay)
            dcb = dcb + dm * decay
            r_rows = r_rows + _dot2(r, jnp.where(lane == h, 1.0, 0.0).astype(bf16))
            r_cols = jnp.where(sub == h, _colsum(r), r_cols)
        dc_ref[...] = dc + _dot(dcb.astype(bf16), bm)
        db_ref[...] = db + _dot_tn(dcb.astype(bf16), cm)
        dxd = dxd_scr[...]
        dx_ref[...] = dxd * dt_e
        ddt_ref[...] = _head_sums(dxd * x, e64)
        dac_ref[...] = _dot3(_tri(CL, "le"), r_rows + _head_sums(dcs_e, e64), False)
        dar_ref[...] = jnp.zeros_like(dar_ref)
        dar_ref[0:HPG, :] = _dot3(-r_cols, _tri(CL, "ge"), True)

    t = bsz * seq
    return pl.pallas_call(
        body, name=name, grid=grid,
        in_specs=[xg, lane128, lane128, adt_t_spec, bspec, cspec, st, xg] + [HBM_SPEC] * na,
        out_specs=[xg, lane128, lane128, lane128, lane128, adt_t_spec] + [HBM_SPEC] * na,
        out_shape=[jax.ShapeDtypeStruct((t, SSM_INNER), f32), jax.ShapeDtypeStruct((t, DT_W), f32),
                   jax.ShapeDtypeStruct((t, DT_W), f32), jax.ShapeDtypeStruct((t, DT_W), f32),
                   jax.ShapeDtypeStruct((t, DT_W), f32), jax.ShapeDtypeStruct((DT_W, t), f32)]
        + ([jax.ShapeDtypeStruct(shape, dtype) for shape, dtype in exchange[1]] if exchange else []),
        scratch_shapes=[pltpu.VMEM((SSM_STATE, GW), f32), pltpu.VMEM((CL, GW), bf16), pltpu.VMEM((CL, GW), f32)]
        + (_copy_semaphores(na, exchange[2]) if exchange else []),
        compiler_params=_cparams(("arbitrary",) * 3 if exchange else ("parallel", "parallel", "arbitrary")),
    )(xbc, dt, adt, adt_t, xbc, xbc, states, dy, *(exchange[0] if exchange else []))


XA_BQ = 512


def _xattn(q, k, v):
    s = _dot_nt(q.astype(bf16), k.astype(bf16)) * (XA_HD ** -0.5)
    p = jnp.exp(s - jnp.max(s, axis=-1, keepdims=True))
    p = p / jnp.sum(p, axis=-1, keepdims=True)
    return _dot(p.astype(bf16), v.astype(bf16))


def xattn_fwd(name, q, kv, bsz, seq, mlen):
    bq = min(XA_BQ, seq)
    nq = seq // bq

    def body(q_ref, k_ref, v_ref, o_ref):
        o_ref[...] = _xattn(q_ref[...].astype(f32), k_ref[...].astype(f32), v_ref[...].astype(f32)).astype(o_ref.dtype)

    return pl.pallas_call(
        body, name=name, grid=(bsz, XA_HEADS, nq),
        in_specs=[pl.BlockSpec((bq, XA_HD), lambda b, h, i: (b * nq + i, h)),
                  pl.BlockSpec((mlen, XA_HD), lambda b, h, i: (b, h)),
                  pl.BlockSpec((mlen, XA_HD), lambda b, h, i: (b, XA_HEADS + h))],
        out_specs=pl.BlockSpec((bq, XA_HD), lambda b, h, i: (b * nq + i, h)),
        out_shape=jax.ShapeDtypeStruct((bsz * seq, D), bf16),
        compiler_params=_cparams(("parallel", "parallel", "parallel")),
    )(q, kv, kv)


def xattn_bwd(name, q, kv, do, bsz, seq, mlen):
    bq = min(XA_BQ, seq)
    nq = seq // bq

    def body(q_ref, k_ref, v_ref, do_ref, dq_ref, dk_ref, dv_ref):
        _, vjp = jax.vjp(_xattn, q_ref[...].astype(f32), k_ref[...].astype(f32), v_ref[...].astype(f32))
        dq, dk, dv = vjp(do_ref[...].astype(f32))
        dq_ref[...] = dq.astype(dq_ref.dtype)
        i = pl.program_id(2)

        @pl.when(i == 0)
        def _():
            dk_ref[...] = dk
            dv_ref[...] = dv

        @pl.when(i > 0)
        def _():
            dk_ref[...] += dk
            dv_ref[...] += dv

    kspec = pl.BlockSpec((mlen, XA_HD), lambda b, h, i: (b, h))
    vspec = pl.BlockSpec((mlen, XA_HD), lambda b, h, i: (b, XA_HEADS + h))
    qspec = pl.BlockSpec((bq, XA_HD), lambda b, h, i: (b * nq + i, h))
    return pl.pallas_call(
        body, name=name, grid=(bsz, XA_HEADS, nq),
        in_specs=[qspec, kspec, vspec, qspec],
        out_specs=[qspec, kspec, kspec],
        out_shape=[jax.ShapeDtypeStruct((bsz * seq, D), bf16), jax.ShapeDtypeStruct((bsz * mlen, D), f32),
                   jax.ShapeDtypeStruct((bsz * mlen, D), f32)],
        compiler_params=_cparams(("parallel", "parallel", "arbitrary")),
    )(q, kv, kv, do)


def _layer_fwd(l, x, mem, w, bsz, seq, mlen, exchange=None):
    n = f"l{l}_"
    sv = {"x0": x}
    sv["h1"] = h1 = rms_fwd(n + "rms_mix", x, w["g_pre_mix"])
    sv["pm"] = pm = _mm(n + "in_proj", h1, w["wm"], "nn", bf16)
    sv["pdt"] = pdt = _mm(n + "in_proj_dt", h1, w["wdt"], "nn")
    sv["o_att"], sv["tot"], *exchanged = sb_fwd(n + "sb_fwd", pm, bsz, seq, exchange)
    o_att = sv["o_att"]
    sv["xbc"] = xbc = conv_fwd(n + "conv_fwd", pm, w["conv_w"], w["conv_b"], bsz, seq)
    sv["dt"], sv["adt"] = dt, adt = dt_fwd(n + "dt_fwd", pdt, w["dt_bias"], w["a_log"])
    sv["adt_t"] = adt_t = adt.T
    sv["y_ssd"], sv["states"] = y_ssd, _ = ssd_fwd(n + "ssd_fwd", xbc, dt, adt, adt_t, bsz, seq)
    sv["o_ssm"] = o_ssm = gnorm_fwd(n + "gnorm_fwd", y_ssd, xbc, pm, w["d_skip"], w["g_ssm_norm"])
    sv["a"] = a = _mm(n + "br_att", o_att, w["w_br_att"], "nn", bf16)
    sv["s"] = s = _mm(n + "br_ssm", o_ssm, w["w_br_ssm"], "nn", bf16)
    sv["merged"] = merged = merge_fwd(n + "merge_fwd", pm, a, s)
    sv["u"] = u = _mm(n + "mix_out", merged, w["w_mix_out"], "nn", bf16)
    sv["x1"], sv["h2"] = x1, h2 = addnorm_rms_fwd(n + "post_mix", x, u, w["g_post_mix"], w["g_pre_xa"])
    sv["memn"] = memn = rms_fwd(n + "rms_mem", mem, w["g_mem"])
    sv["qx"] = qx = _mm(n + "xq", h2, w["w_xq"], "nn", bf16)
    sv["kv"] = kv = _mm(n + "xkv", memn, w["w_xkv"], "nn", bf16)
    sv["ox"] = ox = xattn_fwd(n + "xattn_fwd", qx, kv, bsz, seq, mlen)
    sv["yx"] = yx = _mm(n + "xo", ox, w["w_xo"], "nn", bf16)
    sv["x2"], sv["h3"] = x2, h3 = addnorm_rms_fwd(n + "post_xa", x1, yx, w["g_post_xa"], w["g_pre_ffn"])
    sv["gu"] = gu = _mm(n + "gu", h3, w["w_gu"], "nn", bf16)
    sv["act"] = act = swiglu_fwd(n + "swiglu_fwd", gu)
    sv["d"] = d = _mm(n + "down", act, w["w_down"], "nn", bf16)
    x3 = addnorm_fwd(n + "post_ffn", x2, d, w["g_post_ffn"])
    return x3, sv, exchanged


def _layer_bwd(l, dx, mem, w, sv, bsz, seq, mlen, exchange=None, early_exchange=None):
    n = f"l{l}_b_"
    g = {}
    dd, g["g_post_ffn"] = addnorm_bwd(n + "post_ffn", sv["d"], w["g_post_ffn"], dx)
    g["w_down"] = _mm(n + "dw_down", sv["act"], dd, "tn", bf16)
    dact = _mm(n + "dact", dd, w["w_down"], "nt", bf16)
    dgu = swiglu_bwd(n + "swiglu", sv["gu"], dact)
    g["w_gu"] = _mm(n + "dw_gu", sv["h3"], dgu, "tn", bf16)
    dh3 = _mm(n + "dh3", dgu, w["w_gu"], "nt", bf16)
    dx, g["g_pre_ffn"] = rms_bwd(n + "rms_ffn", sv["x2"], w["g_pre_ffn"], [dh3], dx)
    dyx, g["g_post_xa"] = addnorm_bwd(n + "post_xa", sv["yx"], w["g_post_xa"], dx)
    g["w_xo"] = _mm(n + "dw_xo", sv["ox"], dyx, "tn", bf16)
    dox = _mm(n + "dox", dyx, w["w_xo"], "nt", bf16)
    dqx, dk, dv = xattn_bwd(n + "xattn", sv["qx"], sv["kv"], dox, bsz, seq, mlen)
    g["w_xq"] = _mm(n + "dw_xq", sv["h2"], dqx, "tn", bf16)
    dh2 = _mm(n + "dh2", dqx, w["w_xq"], "nt", bf16)
    dkv = jnp.concatenate([dk, dv], axis=1)
    g["w_xkv"] = _mm(n + "dw_xkv", sv["memn"], dkv, "tn", bf16)
    dmemn = _mm(n + "dmemn", dkv, w["w_xkv"], "nt", bf16)
    _, g["g_mem"] = rms_bwd(n + "rms_mem", mem, w["g_mem"], [dmemn])
    dx, g["g_pre_xa"] = rms_bwd(n + "rms_xa", sv["x1"], w["g_pre_xa"], [dh2], dx)
    du, g["g_post_mix"] = addnorm_bwd(n + "post_mix", sv["u"], w["g_post_mix"], dx)
    g["w_mix_out"] = _mm(n + "dw_mix", sv["merged"], du, "tn", bf16)
    dmerged = _mm(n + "dmerged", du, w["w_mix_out"], "nt", bf16)
    dgates, da, ds = merge_bwd(n + "merge", sv["pm"], sv["a"], sv["s"], dmerged)
    g["w_br_att"] = _mm(n + "dw_att", sv["o_att"], da, "tn", bf16)
    do_att = _mm(n + "do_att", da, w["w_br_att"], "nt", bf16)
    g["w_br_ssm"] = _mm(n + "dw_ssm", sv["o_ssm"], ds, "tn", bf16)
    do_ssm = _mm(n + "do_ssm", ds, w["w_br_ssm"], "nt", bf16)
    dy_ssd, dxs_skip, dz, g["d_skip"], g["g_ssm_norm"] = gnorm_bwd(
        n + "gnorm", sv["y_ssd"], sv["xbc"], sv["pm"], w["d_skip"], w["g_ssm_norm"], do_ssm)
    dxs, dbm, dcm, ddt, dadt_c, dadt_r, *exchanged_early = ssd_bwd(
        n + "ssd", sv["xbc"], sv["dt"], sv["adt"], sv["adt_t"], sv["states"], dy_ssd, bsz, seq,
        early_exchange(g) if early_exchange else None)
    dxbc, g["conv_w"], g["conv_b"] = conv_bwd(n + "conv", sv["pm"], w["conv_w"], w["conv_b"], dxs, dbm, dcm, dxs_skip, bsz, seq)
    dpdt, g["dt_bias"], g["a_log"] = dt_bwd(n + "dt", sv["pdt"], w["dt_bias"], w["a_log"], ddt, dadt_c, dadt_r.T)
    dq, dk_, dv_, *exchanged = sb_bwd(n + "sb", sv["pm"], sv["tot"], do_att, bsz, seq, exchange)
    dpm = jnp.concatenate([dz, dxbc, dq, dk_, dv_, dgates], axis=1)
    g["wm"] = _mm(n + "dw_in", sv["h1"], dpm, "tn", bf16)
    g["wdt"] = _mm(n + "dw_in_dt", sv["h1"], dpdt, "tn", bf16)
    dh1 = _mm(n + "dh1", dpm, w["wm"], "nt", bf16)
    dh1_dt = _mm(n + "dh1_dt", dpdt, w["wdt"], "nt", bf16)
    dx, g["g_pre_mix"] = rms_bwd(n + "rms_mix", sv["x0"], w["g_pre_mix"], [dh1, dh1_dt], dx)
    return dx, g, exchanged, exchanged_early


def _group_pad(v):
    lead = v.shape[:-1]
    v = v.reshape(*lead, SSM_GROUPS, HPG)
    return jnp.pad(v, [(0, 0)] * (len(lead) + 1) + [(0, 128 - HPG)]).reshape(*lead, DT_W)


def _group_unpad(v):
    lead = v.shape[:-1]
    return v.reshape(*lead, SSM_GROUPS, 128)[..., :HPG].reshape(*lead, SSM_HEADS)


BIG = ("w_in", "w_br_att", "w_br_ssm", "w_mix_out", "w_xq", "w_xkv", "w_xo", "w_gu", "w_down")
GAINS = ("g_pre_mix", "g_post_mix", "g_pre_xa", "g_mem", "g_post_xa", "g_pre_ffn", "g_post_ffn")
HEAD_VECS = ("dt_bias", "a_log", "d_skip")
SMALL = GAINS + ("conv_w", "conv_b", "g_ssm_norm") + HEAD_VECS


def _prep_layer(p):
    w = {k: p[k] for k in BIG[1:]}
    if "wm" in p:
        w["wm"], w["wdt"] = p["wm"], p["wdt"]
    else:
        w_in = p["w_in"]
        w["wm"] = jnp.concatenate([w_in[:, 3072:8192], w_in[:, 0:3072], w_in[:, 8224:10272]], axis=1)
        w["wdt"] = _group_pad(w_in[:, 8192:8224])
    for k in GAINS + ("conv_b", "g_ssm_norm"):
        w[k] = p[k].reshape(1, -1)
    w["conv_w"] = p["conv_w"]
    w["dt_bias"] = _group_pad(p["dt_bias"]).reshape(1, DT_W)
    w["a_log"] = _group_pad(p["a_log"]).reshape(1, DT_W)
    w["d_skip"] = jnp.repeat(p["d_skip"], SSM_HD).reshape(1, SSM_INNER)
    return w


def _unprep_grads(g):
    out = {k: g[k] for k in BIG[1:]}
    gm = g["wm"]
    out["w_in"] = jnp.concatenate([gm[:, 5120:8192], gm[:, 0:5120], _group_unpad(g["wdt"]), gm[:, 8192:10240]], axis=1)
    for k in GAINS + ("conv_b", "g_ssm_norm"):
        out[k] = g[k].reshape(-1)
    out["conv_w"] = g["conv_w"]
    out["dt_bias"] = _group_unpad(g["dt_bias"]).reshape(-1)
    out["a_log"] = _group_unpad(g["a_log"]).reshape(-1)
    out["d_skip"] = g["d_skip"].reshape(SSM_HEADS, SSM_HD).sum(axis=1)
    return out


def _local_step(x, mem, target, ws, bsz, seq, mlen):
    saved = []
    for l in range(len(ws)):
        x, sv, _ = _layer_fwd(l, x, mem, ws[l], bsz, seq, mlen)
        saved.append(sv)
    dx, loss_lanes = loss_fwd_bwd("loss", x, target)
    grads = [None] * len(ws)
    for l in reversed(range(len(ws))):
        dx, grads[l], _, _ = _layer_bwd(l, dx, mem, ws[l], saved[l], bsz, seq, mlen)
    return loss_lanes, dx, grads


HBM_SPEC = pl.BlockSpec(memory_space=pltpu.HBM)
FLIP_C = (0, 0, 1)
FLIPS_CHIP = ((1, 0, 0), (0, 1, 0), (1, 1, 0))
FLIPS_ALL = tuple(((f >> 2) & 1, (f >> 1) & 1, f & 1) for f in range(1, 8))


def _view(ref, index):
    return ref.at[index] if index != () else ref


def _exchange(name, srcs, out_shapes, transfers, in_place=False):
    na = len(srcs)

    def body(*refs):
        out_refs = refs[na:2 * na]
        copies = _remote_copies(out_refs if in_place else refs[:na], out_refs, transfers, *refs[2 * na:])
        for cp in copies:
            cp.start()
        for cp in copies:
            cp.wait()

    if in_place:
        out_shape = [jax.ShapeDtypeStruct(s.shape, s.dtype) for s in srcs]
    else:
        out_shape = [jax.ShapeDtypeStruct(shape, dtype) for shape, dtype in out_shapes]
    return pl.pallas_call(
        body, name=name, out_shape=out_shape, in_specs=[HBM_SPEC] * na, out_specs=[HBM_SPEC] * na,
        input_output_aliases={a: a for a in range(na)} if in_place else {},
        scratch_shapes=_copy_semaphores(na, transfers),
    )(*srcs)


def _copy_semaphores(na, transfers):
    return [pltpu.SemaphoreType.DMA((na * len(transfers),)), pltpu.SemaphoreType.DMA((na * len(transfers),))]


def _remote_copies(src_refs, out_refs, transfers, send_sems, recv_sems):
    pos = (lax.axis_index("x"), lax.axis_index("y"), lax.axis_index("c"))
    nt, copies = len(transfers), []
    for a, (src_ref, out_ref) in enumerate(zip(src_refs, out_refs)):
        for t, (flip, src_index, dst_index) in enumerate(transfers):
            assert any(flip)
            peer = tuple(1 - p if f else p for p, f in zip(pos, flip))
            copies.append(pltpu.make_async_remote_copy(
                src_ref=_view(src_ref, src_index(*pos)), dst_ref=_view(out_ref, dst_index(*pos)),
                send_sem=send_sems.at[a * nt + t], recv_sem=recv_sems.at[a * nt + t],
                device_id=peer, device_id_type=MESH))
    return copies


def _exchange_hook(exchange, refs_in, refs_out, sems, first, last):
    copies = _remote_copies(refs_in, refs_out, exchange[2], *sems)

    @pl.when(first)
    def _():
        for cp in copies:
            cp.start()

    @pl.when(last)
    def _():
        for cp in copies:
            cp.wait()


def _at(*index):
    return lambda x, y, c: index


def _allgather8(name, v, me):
    got = _exchange(name, [v], [((7,) + v.shape, v.dtype)], [(fl, _at(), _at(j)) for j, fl in enumerate(FLIPS_ALL)])[0]
    rel = jnp.concatenate([v[None], got], axis=0)
    return jnp.stack([lax.dynamic_index_in_dim(rel, k ^ me, 0, keepdims=False) for k in range(8)])


def _sum8(name, parts):
    def fn(*p):
        acc = p[0]
        for q in p[1:]:
            acc = acc + q
        return acc

    return _rowwise(name, fn, [(parts[k], 0) for k in range(8)], [], [(1, f32)], [], width=128, bt=parts.shape[1])[0]


def _rows_block(r, w, bytes_per_row_elem):
    for bt in (512, 256, 128, 64, 32, 16, 8):
        if r % bt == 0 and bt * w * bytes_per_row_elem * 2 <= 16 * 1024 * 1024:
            return bt
    raise ValueError((r, w))


def _reduce8(name, wire, recv, shard, ci):
    _, _, h, w = wire.shape
    bt = _rows_block(h, w, 2 + 7 * 2 + 4)

    def body(s_ref, a_ref, b_ref, o_ref):
        acc = a_ref[0, 0].astype(f32)
        for j in range(7):
            acc = acc + b_ref[j].astype(f32)
        o_ref[0] = acc

    return pl.pallas_call(
        body, name=name,
        grid_spec=pltpu.PrefetchScalarGridSpec(
            num_scalar_prefetch=1, grid=(h // bt,),
            in_specs=[pl.BlockSpec((1, 1, bt, w), lambda i, s_ref: (s_ref[0], s_ref[1], i, 0)),
                      pl.BlockSpec((7, bt, w), lambda i, s_ref: (0, i, 0))],
            out_specs=pl.BlockSpec((1, bt, w), lambda i, s_ref: (s_ref[1], i, 0))),
        out_shape=jax.ShapeDtypeStruct((2, h, w), f32),
        compiler_params=_cparams(("parallel",)),
    )(jnp.stack([shard, ci]).astype(jnp.int32), wire, recv)


COL_SHARDED = ("w_in", "w_xkv", "w_gu")


def _ref_cols(pieces, lo, hi):
    c, out = pieces[0].shape[1], []
    for s, p in enumerate(pieces):
        a0, a1 = max(lo, s * c), min(hi, (s + 1) * c)
        if a0 < a1:
            out.append(p[:, a0 - s * c:a1 - s * c])
    return out


def _my_cols(gm, g32, lo, hi):
    out = []
    for r0, r1, src, shift in ((0, 3072, gm, 5120), (3072, 8192, gm, -3072), (8192, 8224, g32, -8192), (8224, IN_WIDTH, gm, -32)):
        a0, a1 = max(lo, r0), min(hi, r1)
        if a0 < a1:
            out.append(src[:, a0 + shift:a1 + shift])
    return out


def _pack(arrs, rows_multiple=8):
    flat = jnp.concatenate([a.reshape(-1) for a in arrs])
    pad = (-flat.shape[0]) % (128 * rows_multiple)
    return jnp.pad(flat, (0, pad)).reshape(-1, 128)


def _unpack(buf, shapes):
    flat, out, o = buf.reshape(-1), [], 0
    for s in shapes:
        n = math.prod(s)
        out.append(flat[o:o + n].reshape(s))
        o += n
    return out


def kernel(x, mem, g_pre_mix, w_in, conv_w, conv_b, dt_bias, a_log, d_skip, g_ssm_norm, w_br_att, w_br_ssm, w_mix_out, g_post_mix, g_pre_xa, g_mem, w_xq, w_xkv, w_xo, g_post_xa, g_pre_ffn, w_gu, w_down, g_post_ffn, loss_target, m_g_pre_mix, m_w_in, m_conv_w, m_conv_b, m_dt_bias, m_a_log, m_d_skip, m_g_ssm_norm, m_w_br_att, m_w_br_ssm, m_w_mix_out, m_g_post_mix, m_g_pre_xa, m_g_mem, m_w_xq, m_w_xkv, m_w_xo, m_g_post_xa, m_g_pre_ffn, m_w_gu, m_w_down, m_g_post_ffn, v_g_pre_mix, v_w_in, v_conv_w, v_conv_b, v_dt_bias, v_a_log, v_d_skip, v_g_ssm_norm, v_w_br_att, v_w_br_ssm, v_w_mix_out, v_g_post_mix, v_g_pre_xa, v_g_mem, v_w_xq, v_w_xkv, v_w_xo, v_g_post_xa, v_g_pre_ffn, v_w_gu, v_w_down, v_g_post_ffn):
    a = dict(locals())
    names = ("g_pre_mix", "w_in", "conv_w", "conv_b", "dt_bias", "a_log", "d_skip", "g_ssm_norm", "w_br_att", "w_br_ssm",
             "w_mix_out", "g_post_mix", "g_pre_xa", "g_mem", "w_xq", "w_xkv", "w_xo", "g_post_xa", "g_pre_ffn", "w_gu",
             "w_down", "g_post_ffn")
    depth = w_in.shape[0]
    bsz, seq, _ = x.shape
    mlen = mem.shape[1]
    xi, yi, ci = lax.axis_index("x"), lax.axis_index("y"), lax.axis_index("c")
    shard = 2 * xi + yi
    me = 2 * shard + ci

    cw_all = _allgather8("ag_conv_w", _pack([conv_w]), me)
    cw_shape = conv_w.shape
    conv_w_full = jnp.concatenate([_unpack(cw_all[2 * s], [cw_shape])[0] for s in range(4)], axis=2)

    halves = {k: (a[k].shape[1] // 2, a[k].shape[2]) for k in BIG}
    wbf = {k: a[k].astype(bf16) for k in BIG}
    ag_shapes = [((4, 2) + halves[k], bf16) for k in BIG]
    ag_transfers = [(fl, lambda x_, y_, c_: (c_,), lambda x_, y_, c_: (2 * x_ + y_, c_)) for fl in FLIPS_CHIP]
    fetched = [functools.partial(lambda x_, y_, c_, f: ((2 * x_ + y_) ^ f, c_), f=2 * fl[0] + fl[1]) for fl in FLIPS_CHIP]

    def ag_sources(l):
        return [wbf[k][l].reshape((2,) + halves[k]) for k in BIG]

    def layer_weights(l, got):
        got = [lax.dynamic_update_slice(g, s[None], (shard, 0, 0, 0)) for g, s in zip(got, ag_sources(l))]
        got = _exchange(f"ag_d2d_l{l}", got, None, [(FLIP_C, fn, fn) for fn in fetched], in_place=True)
        full = {k: g.reshape(4, 2 * halves[k][0], halves[k][1]) for k, g in zip(BIG, got)}
        p = {}
        for k in BIG[1:]:
            sh = full[k]
            p[k] = sh.transpose(1, 0, 2).reshape(sh.shape[1], -1) if k in COL_SHARDED else sh.reshape(-1, sh.shape[2])
        pieces = [full["w_in"][s] for s in range(4)]
        p["wm"] = jnp.concatenate(_ref_cols(pieces, 3072, 8192) + _ref_cols(pieces, 0, 3072)
                                  + _ref_cols(pieces, 8224, 10272), axis=1)
        p["wdt"] = _group_pad(jnp.concatenate(_ref_cols(pieces, 8192, 8224), axis=1))
        for k in SMALL:
            p[k] = conv_w_full[l] if k == "conv_w" else a[k][l]
        return _prep_layer(p)

    rs_transfers = [(fl, functools.partial(lambda x_, y_, c_, fs, fc: ((2 * x_ + y_) ^ fs, c_ ^ fc), fs=2 * fl[0] + fl[1], fc=fl[2]),
                     _at(j)) for j, fl in enumerate(FLIPS_ALL)]

    def rs_exchange(g, keys):
        srcs = []
        for k in keys:
            r, c = a[k].shape[1:]
            if k == "w_in":
                g32 = _group_unpad(g["wdt"])
                gk = jnp.stack([jnp.concatenate(_my_cols(g["wm"], g32, s * c, (s + 1) * c), axis=1) for s in range(4)])
            elif k in COL_SHARDED:
                gk = g[k].reshape(r, 4, c).transpose(1, 0, 2)
            else:
                gk = g[k]
            srcs.append(gk.astype(bf16).reshape((4, 2) + halves[k]))
        return srcs, [((7,) + halves[k], bf16) for k in keys], rs_transfers

    def layer_grads(l, keys, wires, got):
        red = [_reduce8(f"rs_sum_l{l}_{k}", w, r_, shard, ci) for k, w, r_ in zip(keys, wires, got)]
        my_half = lambda x_, y_, c_: (c_,)
        red = _exchange(f"rs_swap_l{l}_{len(keys)}", red, None, [(FLIP_C, my_half, my_half)], in_place=True)
        return {k: r_.reshape(a[k].shape[1:]) for k, r_ in zip(keys, red)}

    assert depth >= 2
    xt, memt = x.reshape(bsz * seq, D), mem.reshape(bsz * mlen, D)
    ws, saved = [], []
    got = _exchange("ag_ici_l0", ag_sources(0), ag_shapes, ag_transfers)
    for l in range(depth):
        ws.append(layer_weights(l, got))
        nxt = (ag_sources(l + 1), ag_shapes, ag_transfers) if l + 1 < depth else None
        xt, sv, got = _layer_fwd(l, xt, memt, ws[l], bsz, seq, mlen, nxt)
        saved.append(sv)
    gx, loss_lanes = loss_fwd_bwd("loss", xt, loss_target.reshape(bsz * seq, D))
    grads, gshard = [None] * depth, [None] * depth
    riding, early = None, {}
    for l in reversed(range(depth)):
        early_fn = (lambda g: early.setdefault("ex", rs_exchange(g, BIG[1:]))) if l == 0 else None
        gx, grads[l], got, got_early = _layer_bwd(l, gx, memt, ws[l], saved[l], bsz, seq, mlen, riding, early_fn)
        if riding is not None:
            gshard[l + 1] = layer_grads(l + 1, BIG, riding[0], got)
        riding = rs_exchange(grads[l], BIG) if l > 0 else None
    gshard[0] = layer_grads(0, BIG[1:], early["ex"][0], got_early)
    last = rs_exchange(grads[0], BIG[:1])
    grads = [_unprep_grads(g) for g in grads]

    out_g, out_d, out_m, out_v = {}, {}, {}, {}
    longest = max(BIG[1:], key=lambda k: math.prod(a[k].shape))
    for k in BIG[1:] + BIG[:1]:
        if k == "w_in":
            gshard[0].update(layer_grads(0, BIG[:1], last[0], got_last))
        shp = a[k].shape
        g = jnp.stack([gshard[l][k] for l in range(depth)])
        two_d = (shp[0] * shp[1], shp[2])
        d_, m_, v_, *got = adamw("adamw_" + k, a[k].reshape(two_d), g.reshape(two_d), a["m_" + k].reshape(two_d),
                                 a["v_" + k].reshape(two_d), last if k == longest else None)
        if k == longest:
            got_last = got
        out_g[k], out_d[k], out_m[k], out_v[k] = g, d_.reshape(shp), m_.reshape(shp), v_.reshape(shp)

    small_shapes = [(depth,) + (conv_w_full.shape[1:] if k == "conv_w" else a[k].shape[1:]) for k in SMALL]
    small = _pack([jnp.stack([grads[l][k] for l in range(depth)]) for k in SMALL] + [loss_lanes])
    total = _sum8("small_sum", _allgather8("ag_small", small, me))
    *gsmall, loss_l = _unpack(total, small_shapes + [loss_lanes.shape])
    gsmall = dict(zip(SMALL, gsmall))
    gsmall["conv_w"] = lax.dynamic_slice_in_dim(gsmall["conv_w"], shard * cw_shape[2], cw_shape[2], axis=2)
    loc_shapes = [a[k].shape for k in SMALL]
    d_, m_, v_ = adamw("adamw_small", _pack([a[k] for k in SMALL]), _pack([gsmall[k] for k in SMALL]),
                       _pack([a["m_" + k] for k in SMALL]), _pack([a["v_" + k] for k in SMALL]))
    for k, dd, mm, vv in zip(SMALL, _unpack(d_, loc_shapes), _unpack(m_, loc_shapes), _unpack(v_, loc_shapes)):
        out_g[k], out_d[k], out_m[k], out_v[k] = gsmall[k], dd, mm, vv

    loss = jnp.sum(loss_l)
    return (loss, gx.reshape(x.shape), *[out_g[k] for k in names], *[out_d[k] for k in names],
            *[out_m[k] for k in names], *[out_v[k] for k in names])
```

```python
import functools
import math

import jax
import jax.numpy as jnp
from jax import lax
from jax.experimental import pallas as pl
from jax.experimental.pallas import tpu as pltpu

f32, bf16 = jnp.float32, jnp.bfloat16

DEPTH = 4
D = 1024
SB_HEADS, SB_HD = 16, 64
SSM_INNER, SSM_HD, SSM_HEADS, SSM_GROUPS, SSM_STATE, SSM_CONV, SSM_CHUNK = 2048, 64, 32, 4, 128, 4, 128
HPG = SSM_HEADS // SSM_GROUPS
CONV_DIM = SSM_INNER + 2 * SSM_GROUPS * SSM_STATE
XA_HEADS, XA_HD = 4, 256
FFN = 2816
IN_WIDTH = 10272
RMS_EPS = 1e-6
LR, B1, B2, EPS, WD, STEP = 0.001, 0.9, 0.999, 1e-08, 0.01, 10

PM_W = 10240
OFF_Z, OFF_XBC, OFF_Q, OFF_K, OFF_V, OFF_GA, OFF_GS = 0, 2048, 5120, 6144, 7168, 8192, 9216
DT_W = SSM_GROUPS * 128

VMEM_LIMIT = 48 * 1024 * 1024
MESH = pl.DeviceIdType.MESH


def _cparams(sem):
    return pltpu.CompilerParams(dimension_semantics=sem, vmem_limit_bytes=VMEM_LIMIT)


def _tile(n):
    for t in (512, 256, 128):
        if n % t == 0:
            return t
    raise ValueError(f"dimension {n} is not a multiple of 128")


MM_VMEM_BUDGET = 34 * 1024 * 1024


def _mm_tiles(m, n, k, sa, sb, so):
    best = None
    for tm in (2048, 1024, 512, 256, 128):
        if m % tm:
            continue
        for tn in (2048, 1024, 512, 256, 128):
            if n % tn:
                continue
            for tk in (k, 2048, 1024, 512):
                if tk > k or k % tk:
                    continue
                vmem = 2 * (tm * tk * sa + tk * tn * sb + tm * tn * so) + tm * tn * 4 * (2 if tk < k else 1)
                if vmem > MM_VMEM_BUDGET:
                    continue
                traffic = m * k * sa * (n // tn) + k * n * sb * (m // tm) + m * n * so
                steps = (m // tm) * (n // tn) * (k // tk)
                accumulate = (k // tk > 1) * (k // tk) * m * n * 2
                key = (traffic + steps * 800_000 + accumulate, steps)
                if best is None or key < best[0]:
                    best = (key, (tm, tn, tk))
    assert best is not None, (m, n, k)
    return best[1]


def _mm(name, a, b, mode, out_dtype=f32):
    if mode == "nn":
        (m, k), (k2, n) = a.shape, b.shape
    elif mode == "nt":
        (m, k), (n, k2) = a.shape, b.shape
    else:
        (k, m), (k2, n) = a.shape, b.shape
    assert k == k2, (name, a.shape, b.shape, mode)
    tm, tn, tk = _mm_tiles(m, n, k, a.dtype.itemsize, b.dtype.itemsize, jnp.dtype(out_dtype).itemsize)
    nk = k // tk
    dn = {"nn": (((1,), (0,)), ((), ())), "nt": (((1,), (1,)), ((), ())), "tn": (((0,), (0,)), ((), ()))}[mode]

    def product(a_ref, b_ref):
        return lax.dot_general(a_ref[...].astype(bf16), b_ref[...].astype(bf16), dn, preferred_element_type=f32)

    def body_whole_k(a_ref, b_ref, o_ref):
        o_ref[...] = product(a_ref, b_ref).astype(o_ref.dtype)

    def body_k_loop(a_ref, b_ref, o_ref, acc_ref):
        kk = pl.program_id(2)

        @pl.when(kk == 0)
        def _():
            acc_ref[...] = product(a_ref, b_ref)

        @pl.when(kk > 0)
        def _():
            acc_ref[...] += product(a_ref, b_ref)

        @pl.when(kk == nk - 1)
        def _():
            o_ref[...] = acc_ref[...].astype(o_ref.dtype)

    a_spec = pl.BlockSpec((tk, tm), lambda i, j, kk: (kk, i)) if mode == "tn" else pl.BlockSpec((tm, tk), lambda i, j, kk: (i, kk))
    b_spec = pl.BlockSpec((tn, tk), lambda i, j, kk: (j, kk)) if mode == "nt" else pl.BlockSpec((tk, tn), lambda i, j, kk: (kk, j))
    return pl.pallas_call(
        body_whole_k if nk == 1 else body_k_loop, name=name, grid=(m // tm, n // tn, nk),
        in_specs=[a_spec, b_spec],
        out_specs=pl.BlockSpec((tm, tn), lambda i, j, kk: (i, j)),
        out_shape=jax.ShapeDtypeStruct((m, n), out_dtype),
        scratch_shapes=[] if nk == 1 else [pltpu.VMEM((tm, tn), f32)],
        compiler_params=_cparams(("parallel", "parallel", "arbitrary")),
    )(a, b)


def _rowwise(name, fn, rows, consts, out_rows, out_accs, *, width, ncol=1, bt=512, exchange=None):
    r = rows[0][0].shape[0]
    bt = min(bt, r)
    assert r % bt == 0, (name, r, bt)
    nrow = r // bt
    n_in = len(rows) + len(consts)
    n_or = len(out_rows)
    na = len(exchange[0]) if exchange else 0
    n_out = n_or + len(out_accs)

    def body(*refs):
        ins = [ref[...].astype(f32) for ref in refs[:n_in]]
        outs = fn(*ins)
        if not isinstance(outs, (tuple, list)):
            outs = (outs,)
        o_refs = refs[n_in + na:n_in + na + n_out]
        if exchange:
            _exchange_hook(exchange, refs[n_in:n_in + na], refs[n_in + na + n_out:n_in + 2 * na + n_out],
                           refs[n_in + 2 * na + n_out:], *_grid_ends((ncol, nrow)))
        for o_ref, val in zip(o_refs[:n_or], outs[:n_or]):
            o_ref[...] = val.astype(o_ref.dtype)
        if out_accs:
            i = pl.program_id(1)
            for o_ref, val in zip(o_refs[n_or:], outs[n_or:]):
                @pl.when(i == 0)
                def _(o_ref=o_ref, val=val):
                    o_ref[...] = val

                @pl.when(i > 0)
                def _(o_ref=o_ref, val=val):
                    o_ref[...] += val

    in_specs = [pl.BlockSpec((bt, width), functools.partial(lambda j, i, off: (i, off + j), off=off)) for _, off in rows]
    in_specs += [pl.BlockSpec((c.shape[0], width), functools.partial(lambda j, i, off: (0, off + j), off=off)) for c, off in consts]
    out_specs = [pl.BlockSpec((bt, mlt * width), lambda j, i: (i, j)) for mlt, _ in out_rows]
    out_specs += [pl.BlockSpec((k, width), lambda j, i: (0, j)) for k in out_accs]
    out_shape = [jax.ShapeDtypeStruct((r, ncol * mlt * width), dt) for mlt, dt in out_rows]
    out_shape += [jax.ShapeDtypeStruct((k, ncol * width), f32) for k in out_accs]
    if exchange:
        in_specs += [HBM_SPEC] * na
        out_specs += [HBM_SPEC] * na
        out_shape += [jax.ShapeDtypeStruct(shape, dtype) for shape, dtype in exchange[1]]
    return pl.pallas_call(
        body, name=name, grid=(ncol, nrow), in_specs=in_specs, out_specs=out_specs, out_shape=out_shape,
        scratch_shapes=_copy_semaphores(na, exchange[2]) if exchange else [],
        compiler_params=_cparams(("arbitrary", "arbitrary") if exchange else
                                 ("parallel", "arbitrary" if out_accs else "parallel")),
    )(*[a for a, _ in rows], *[c for c, _ in consts], *(exchange[0] if exchange else []))


def _rms(x, g):
    return x * lax.rsqrt(jnp.mean(x * x, axis=-1, keepdims=True) + RMS_EPS) * g


def _silu(x):
    return x * jax.nn.sigmoid(x)


def _softplus(x):
    return jnp.maximum(x, 0.0) + jnp.log(1.0 + jnp.exp(-jnp.abs(x)))


def _colsum(x):
    return jnp.sum(x, axis=0, keepdims=True)


def rms_fwd(name, x, g):
    return _rowwise(name, _rms, [(x, 0)], [(g, 0)], [(1, bf16)], [], width=D)[0]


def rms_bwd(name, x, g, dhs, dres=None):
    nd = len(dhs)

    def fn(x, *rest):
        dh = rest[0]
        for extra in rest[1:nd]:
            dh = dh + extra
        g = rest[-1]
        _, vjp = jax.vjp(_rms, x, g)
        dx, dg = vjp(dh.astype(f32))
        if dres is not None:
            dx = dx + rest[nd]
        return dx, dg

    rows = [(x, 0)] + [(d, 0) for d in dhs] + ([(dres, 0)] if dres is not None else [])
    return _rowwise(name, fn, rows, [(g, 0)], [(1, f32)], [1], width=D)


def addnorm_fwd(name, x, u, g):
    return _rowwise(name, lambda x, u, g: x + _rms(u, g), [(x, 0), (u, 0)], [(g, 0)], [(1, f32)], [], width=D)[0]


def addnorm_rms_fwd(name, x, u, g, g_next):
    def fn(x, u, g, g_next):
        x_new = x + _rms(u, g)
        return x_new, _rms(x_new, g_next)

    return _rowwise(name, fn, [(x, 0), (u, 0)], [(g, 0), (g_next, 0)], [(1, f32), (1, bf16)], [], width=D)


def addnorm_bwd(name, u, g, dx):
    def fn(u, dx, g):
        _, vjp = jax.vjp(_rms, u, g)
        return vjp(dx)

    return _rowwise(name, fn, [(u, 0), (dx, 0)], [(g, 0)], [(1, bf16)], [1], width=D)


def _merge(ga, gs, a, s):
    return jax.nn.sigmoid(ga) * a + jax.nn.sigmoid(gs) * s


def merge_fwd(name, pm, a, s):
    return _rowwise(name, _merge, [(pm, OFF_GA // D), (pm, OFF_GS // D), (a, 0), (s, 0)], [], [(1, bf16)], [], width=D)[0]


def merge_bwd(name, pm, a, s, dm):
    def fn(ga, gs, a, s, dm):
        _, vjp = jax.vjp(_merge, ga, gs, a, s)
        dga, dgs, da, ds = vjp(dm)
        return jnp.concatenate([dga, dgs], axis=1), da, ds

    return _rowwise(name, fn, [(pm, OFF_GA // D), (pm, OFF_GS // D), (a, 0), (s, 0), (dm, 0)], [],
                    [(2, bf16), (1, bf16), (1, bf16)], [], width=D)


def _swiglu(gate, up):
    return _silu(gate) * up


def swiglu_fwd(name, gu):
    return _rowwise(name, _swiglu, [(gu, 0), (gu, 1)], [], [(1, bf16)], [], width=FFN, bt=256)[0]


def swiglu_bwd(name, gu, dact):
    def fn(gate, up, dact):
        _, vjp = jax.vjp(_swiglu, gate, up)
        dg, du = vjp(dact.astype(f32))
        return jnp.concatenate([dg, du], axis=1)

    return _rowwise(name, fn, [(gu, 0), (gu, 1), (dact, 0)], [], [(2, bf16)], [], width=FFN, bt=128)[0]


GW = SSM_INNER // SSM_GROUPS


def _gnorm(y, xs, z, dskip, gn):
    yy = (y + dskip * xs) * _silu(z)
    return yy * lax.rsqrt(jnp.mean(yy * yy, axis=-1, keepdims=True) + RMS_EPS) * gn


def gnorm_fwd(name, y, xbc, pm, dskip, gn):
    return _rowwise(name, _gnorm, [(y, 0), (xbc, 0), (pm, OFF_Z // GW)], [(dskip, 0), (gn, 0)], [(1, bf16)], [],
                    width=GW, ncol=SSM_GROUPS)[0]


def gnorm_bwd(name, y, xbc, pm, dskip, gn, do):
    def fn(y, xs, z, do, dskip, gn):
        _, vjp = jax.vjp(_gnorm, y, xs, z, dskip, gn)
        return vjp(do.astype(f32))

    return _rowwise(name, fn, [(y, 0), (xbc, 0), (pm, OFF_Z // GW), (do, 0)], [(dskip, 0), (gn, 0)],
                    [(1, f32), (1, f32), (1, bf16)], [1, 1], width=GW, ncol=SSM_GROUPS)


def _dtfn(pdt, bias, alog):
    dt = _softplus(pdt + bias)
    return dt, -jnp.exp(alog) * dt


def dt_fwd(name, pdt, bias, alog):
    return _rowwise(name, _dtfn, [(pdt, 0)], [(bias, 0), (alog, 0)], [(1, f32), (1, f32)], [], width=DT_W)


def dt_bwd(name, pdt, bias, alog, ddt, dadt_c, dadt_r):
    def fn(pdt, ddt, dac, dar, bias, alog):
        _, vjp = jax.vjp(_dtfn, pdt, bias, alog)
        return vjp((ddt, dac + dar))

    return _rowwise(name, fn, [(pdt, 0), (ddt, 0), (dadt_c, 0), (dadt_r, 0)], [(bias, 0), (alog, 0)],
                    [(1, f32)], [1, 1], width=DT_W)


def loss_fwd_bwd(name, y, target):
    def fn(y, t):
        e = y - t
        return e * (1.0 / D), _colsum(e * e) * (0.5 / D)

    return _rowwise(name, fn, [(y, 0), (target, 0)], [], [(1, f32)], [1], width=D)


def adamw(name, w, g, m, v, exchange=None):
    r, c = w.shape

    def fn(w, g, m, v):
        m = B1 * m + (1.0 - B1) * g
        v = B2 * v + (1.0 - B2) * (g * g)
        m_hat = m / (1.0 - B1 ** STEP)
        v_hat = v / (1.0 - B2 ** STEP)
        return -LR * (m_hat / (jnp.sqrt(v_hat) + EPS) + WD * w), m, v

    bt = 256
    while bt > 8 and (r % bt or bt * c * 4 * 7 * 2 > 16 * 1024 * 1024):
        bt //= 2
    if r % bt:
        bt = r
    return _rowwise(name, fn, [(w, 0), (g, 0), (m, 0), (v, 0)], [], [(1, f32)] * 3, [], width=c, bt=bt, exchange=exchange)


SB_BQ, SB_BK = 512, 256
SB_UNROLL = 2
SB_SUB = 256


def _dot(a, b):
    return jnp.dot(a, b, preferred_element_type=f32)


def _dot_nt(a, b):
    return lax.dot_general(a, b, (((1,), (1,)), ((), ())), preferred_element_type=f32)


def _dot_tn(a, b):
    return lax.dot_general(a, b, (((0,), (0,)), ((), ())), preferred_element_type=f32)


def _dot2(x, tri):
    hi = x.astype(bf16)
    lo = (x - hi.astype(f32)).astype(bf16)
    return _dot(hi, tri) + _dot(lo, tri)


def _running(x, tri, reverse=False, exclusive=False, two_pass=True):
    sub = tri.shape[0]
    n = x.shape[1] // sub
    out, carry = [None] * n, None
    for j in (range(n - 1, -1, -1) if reverse else range(n)):
        xj = x[:, j * sub:(j + 1) * sub]
        pj = _dot2(xj, tri) if two_pass else _dot(xj.astype(bf16), tri)
        if carry is not None:
            pj = pj + carry
        out[j] = pj
        carry = pj[:, 0:1] if reverse else pj[:, sub - 1:sub]
        if exclusive:
            carry = carry + xj[:, sub - 1:sub]
    return out[0] if n == 1 else jnp.concatenate(out, axis=1)


def _tri(n, rel):
    r = lax.broadcasted_iota(jnp.int32, (n, n), 0)
    c = lax.broadcasted_iota(jnp.int32, (n, n), 1)
    m = {"ge": r >= c, "lt": r < c, "le": r <= c}[rel]
    return jnp.where(m, 1.0, 0.0).astype(bf16)


def _add_rows(full, r0, upd):
    return full + upd if r0 == 0 else jnp.concatenate([full[:r0], full[r0:] + upd], axis=0)


def _grid_ends(grid):
    ids = [pl.program_id(d) for d in range(len(grid))]
    first = functools.reduce(jnp.logical_and, [i == 0 for i in ids])
    last = functools.reduce(jnp.logical_and, [i == n - 1 for i, n in zip(ids, grid)])
    return first, last


def sb_fwd(name, pm, bsz, seq, exchange=None):
    bq = min(SB_BQ, seq)
    bk = min(SB_BK, bq)
    nq, nd = seq // bq, bq // bk
    step = SB_UNROLL if nd % SB_UNROLL == 0 else 1
    scale = SB_HD ** -0.5
    qb, kb_, vb_ = OFF_Q // 128, OFF_K // 128, OFF_V // 128
    na = len(exchange[0]) if exchange else 0
    grid = (bsz, 8, nq)

    def body(*refs):
        q_ref, k_ref, v_ref = refs[:3]
        o_ref, tot_ref = refs[3 + na:5 + na]
        if exchange:
            _exchange_hook(exchange, refs[3:3 + na], refs[5 + na:5 + 2 * na], refs[5 + 2 * na:], *_grid_ends(grid))
        i = pl.program_id(2)
        lane = lax.broadcasted_iota(jnp.int32, (1, 128), 1)
        m0 = lane < SB_HD
        q = q_ref[...].astype(f32) * scale
        qs = (jnp.where(m0, q, 0.0).astype(bf16), jnp.where(m0, 0.0, q).astype(bf16))
        wide = step * bk
        neg_tri = -_tri(min(SB_SUB, bk), "ge")
        t_idx = i * bq + lax.broadcasted_iota(jnp.int32, (bq, 1), 0)

        def block(ks, carry, masked, width):
            o_acc, c0, c1 = carry
            kblk = k_ref[pl.ds(ks, width), :].astype(bf16)
            vblk = v_ref[pl.ds(ks, width), :].astype(bf16)
            vs = (jnp.where(m0, vblk, 0).astype(bf16), jnp.where(m0, 0, vblk).astype(bf16))
            if masked:
                valid = (ks + lax.broadcasted_iota(jnp.int32, (1, width), 1)) < t_idx
            cs = [c0, c1]
            for h in range(2):
                z = _dot_nt(qs[h], kblk)
                sp = _softplus(z)
                if masked:
                    sp = jnp.where(valid, sp, 0.0)
                tl = _running(sp, neg_tri, reverse=True)
                w = jnp.exp(z + tl + cs[h])
                if masked:
                    w = jnp.where(valid, w, 0.0)
                o_acc = o_acc + _dot(w.astype(bf16), vs[h])
                cs[h] = cs[h] + tl[:, 0:1]
            return o_acc, cs[0], cs[1]

        zc = jnp.zeros((bq, 1), f32)
        carry = (jnp.zeros((bq, 128), f32), zc, zc)
        for d in range(nd):
            carry = block(pl.multiple_of((i * nd + nd - 1 - d) * bk, bk), carry, True, bk)
        carry = lax.fori_loop(0, i * (nd // step),
                              lambda n, c: block(pl.multiple_of((i * nd - step * (n + 1)) * bk, wide), c, False, wide), carry)
        o, c0, c1 = carry
        o_ref[...] = o.astype(o_ref.dtype)
        tot_ref[0, 0] = jnp.where(m0, c0, c1)

    return pl.pallas_call(
        body, name=name, grid=grid,
        in_specs=[pl.BlockSpec((bq, 128), lambda b, p, i: (b * nq + i, qb + p)),
                  pl.BlockSpec((seq, 128), lambda b, p, i: (b, kb_ + p)),
                  pl.BlockSpec((seq, 128), lambda b, p, i: (b, vb_ + p))] + [HBM_SPEC] * na,
        out_specs=[pl.BlockSpec((bq, 128), lambda b, p, i: (b * nq + i, p)),
                   pl.BlockSpec((1, 1, bq, 128), lambda b, p, i: (b, p, i, 0))] + [HBM_SPEC] * na,
        out_shape=[jax.ShapeDtypeStruct((bsz * seq, 1024), bf16), jax.ShapeDtypeStruct((bsz, 8, seq, 128), f32)]
        + ([jax.ShapeDtypeStruct(shape, dtype) for shape, dtype in exchange[1]] if exchange else []),
        scratch_shapes=_copy_semaphores(na, exchange[2]) if exchange else [],
        compiler_params=_cparams(("arbitrary",) * 3 if exchange else ("parallel",) * 3),
    )(pm, pm, pm, *(exchange[0] if exchange else []))


def sb_bwd(name, pm, tot, do, bsz, seq, exchange=None):
    bq = min(SB_BQ, seq)
    bk = min(SB_BK, bq)
    nq, nd = seq // bq, bq // bk
    step = SB_UNROLL if nd % SB_UNROLL == 0 else 1
    scale = SB_HD ** -0.5
    qb, kb_, vb_ = OFF_Q // 128, OFF_K // 128, OFF_V // 128
    na = len(exchange[0]) if exchange else 0
    grid = (bsz, 8, nq)

    def body(*refs):
        q_ref, k_ref, v_ref, do_ref, tot_ref = refs[:5]
        dq_ref, dk_ref, dv_ref = refs[5 + na:8 + na]
        dk_acc, dv_acc = refs[8 + 2 * na:10 + 2 * na]
        if exchange:
            _exchange_hook(exchange, refs[5:5 + na], refs[8 + na:8 + 2 * na], refs[10 + 2 * na:], *_grid_ends(grid))
        i = pl.program_id(2)

        @pl.when(i == 0)
        def _():
            dk_acc[...] = jnp.zeros_like(dk_acc)
            dv_acc[...] = jnp.zeros_like(dv_acc)

        lane = lax.broadcasted_iota(jnp.int32, (1, 128), 1)
        m0 = lane < SB_HD
        ms = (m0, jnp.logical_not(m0))
        q = q_ref[...].astype(f32) * scale
        qpair = q.astype(bf16)
        qs = (jnp.where(m0, q, 0.0).astype(bf16), jnp.where(m0, 0.0, q).astype(bf16))
        dout = do_ref[...].astype(f32)
        dos = (jnp.where(m0, dout, 0.0).astype(bf16), jnp.where(m0, 0.0, dout).astype(bf16))
        tot = tot_ref[0, 0]
        tots = (tot[:, 0:1], tot[:, SB_HD:SB_HD + 1])
        tri_lt = _tri(min(SB_SUB, bk), "lt")
        tri_le = _tri(min(SB_SUB, bk), "le")
        t_idx = i * bq + lax.broadcasted_iota(jnp.int32, (bq, 1), 0)

        def block(ks, carry, masked, r0=0):
            dq_acc, p0, p1, g0, g1 = carry
            kblk = k_ref[pl.ds(ks, bk), :].astype(bf16)
            vblk = v_ref[pl.ds(ks, bk), :].astype(bf16)
            if masked:
                valid = (ks + lax.broadcasted_iota(jnp.int32, (1, bk), 1)) < t_idx[r0:]
            ps, gs = [p0, p1], [g0, g1]
            dk_blk = jnp.zeros((bk, 128), f32)
            dv_blk = jnp.zeros((bk, 128), f32)
            for h in range(2):
                z = _dot_nt(qs[h][r0:], kblk)
                sp = _softplus(z)
                sig = jnp.exp(z - sp)
                if masked:
                    sp = jnp.where(valid, sp, 0.0)
                w = jnp.exp(z + tots[h][r0:] + ps[h][r0:] + _running(sp, tri_lt, exclusive=True))
                if masked:
                    w = jnp.where(valid, w, 0.0)
                g = _dot_nt(dos[h][r0:], vblk) * w
                dz = g - sig * (gs[h][r0:] + _running(g, tri_le, two_pass=False))
                if masked:
                    dz = jnp.where(valid, dz, 0.0)
                dz = dz.astype(bf16)
                dq_acc = _add_rows(dq_acc, r0, jnp.where(ms[h], _dot(dz, kblk), 0.0))
                dk_blk = dk_blk + jnp.where(ms[h], _dot_tn(dz, qpair[r0:]), 0.0)
                dv_blk = dv_blk + _dot_tn(w.astype(bf16), dos[h][r0:])
                ps[h] = _add_rows(ps[h], r0, jnp.sum(sp, axis=1, keepdims=True))
                gs[h] = _add_rows(gs[h], r0, jnp.sum(g, axis=1, keepdims=True))
            dk_acc[pl.ds(ks, bk), :] += dk_blk
            dv_acc[pl.ds(ks, bk), :] += dv_blk
            return dq_acc, ps[0], ps[1], gs[0], gs[1]

        zc = jnp.zeros((bq, 1), f32)
        carry = (jnp.zeros((bq, 128), f32), zc, zc, zc, zc)
        def far(n, c):
            for u in range(step):
                c = block(pl.multiple_of((step * n + u) * bk, bk), c, False)
            return c

        carry = lax.fori_loop(0, i * (nd // step), far, carry)
        for d in range(nd):
            carry = block(pl.multiple_of((i * nd + d) * bk, bk), carry, True, d * bk)
        dq_ref[...] = (carry[0] * scale).astype(dq_ref.dtype)

        @pl.when(i == nq - 1)
        def _():
            dk_ref[...] = dk_acc[...].astype(dk_ref.dtype)
            dv_ref[...] = dv_acc[...].astype(dv_ref.dtype)

    return pl.pallas_call(
        body, name=name, grid=grid,
        in_specs=[pl.BlockSpec((bq, 128), lambda b, p, i: (b * nq + i, qb + p)),
                  pl.BlockSpec((seq, 128), lambda b, p, i: (b, kb_ + p)),
                  pl.BlockSpec((seq, 128), lambda b, p, i: (b, vb_ + p)),
                  pl.BlockSpec((bq, 128), lambda b, p, i: (b * nq + i, p)),
                  pl.BlockSpec((1, 1, bq, 128), lambda b, p, i: (b, p, i, 0))] + [HBM_SPEC] * na,
        out_specs=[pl.BlockSpec((bq, 128), lambda b, p, i: (b * nq + i, p)),
                   pl.BlockSpec((seq, 128), lambda b, p, i: (b, p)),
                   pl.BlockSpec((seq, 128), lambda b, p, i: (b, p))] + [HBM_SPEC] * na,
        out_shape=[jax.ShapeDtypeStruct((bsz * seq, 1024), bf16)] * 3
        + ([jax.ShapeDtypeStruct(shape, dtype) for shape, dtype in exchange[1]] if exchange else []),
        scratch_shapes=[pltpu.VMEM((seq, 128), f32), pltpu.VMEM((seq, 128), f32)]
        + (_copy_semaphores(na, exchange[2]) if exchange else []),
        compiler_params=_cparams(("arbitrary",) * 3 if exchange else ("parallel", "parallel", "arbitrary")),
    )(pm, pm, pm, do, tot, *(exchange[0] if exchange else []))


CONV_CB = 256


def _shift_down(x, d, rows):
    return x if d == 0 else jnp.where(rows >= d, pltpu.roll(x, d, axis=0), 0.0)


def _shift_up(x, d, rows, n):
    return x if d == 0 else jnp.where(rows < n - d, pltpu.roll(x, n - d, axis=0), 0.0)


def conv_fwd(name, pm, w, b, bsz, seq):
    nc = CONV_DIM // CONV_CB
    off = OFF_XBC // CONV_CB

    def body(x_ref, w_ref, b_ref, o_ref):
        x = x_ref[...].astype(f32)
        rows = lax.broadcasted_iota(jnp.int32, x.shape, 0)
        pre = b_ref[...] + jnp.zeros_like(x)
        for k in range(SSM_CONV):
            pre = pre + w_ref[k:k + 1, :] * _shift_down(x, SSM_CONV - 1 - k, rows)
        o_ref[...] = _silu(pre)

    return pl.pallas_call(
        body, name=name, grid=(nc, bsz),
        in_specs=[pl.BlockSpec((seq, CONV_CB), lambda j, bb: (bb, off + j)),
                  pl.BlockSpec((SSM_CONV, CONV_CB), lambda j, bb: (0, j)),
                  pl.BlockSpec((1, CONV_CB), lambda j, bb: (0, j))],
        out_specs=pl.BlockSpec((seq, CONV_CB), lambda j, bb: (bb, j)),
        out_shape=jax.ShapeDtypeStruct((bsz * seq, CONV_DIM), f32),
        compiler_params=_cparams(("parallel", "parallel")),
    )(pm, w, b)


def conv_bwd(name, pm, w, b, dxs, dbm, dcm, dskipx, bsz, seq):
    nc = CONV_DIM // CONV_CB
    off = OFF_XBC // CONV_CB
    nxs = SSM_INNER // CONV_CB
    nbc = SSM_GROUPS * SSM_STATE // CONV_CB

    def body(x_ref, w_ref, b_ref, dxs_ref, dbm_ref, dcm_ref, ds_ref, dx_ref, dw_ref, db_ref):
        j, bb = pl.program_id(0), pl.program_id(1)
        x = x_ref[...].astype(f32)
        rows = lax.broadcasted_iota(jnp.int32, x.shape, 0)
        xsh = [_shift_down(x, SSM_CONV - 1 - k, rows) for k in range(SSM_CONV)]
        pre = b_ref[...] + jnp.zeros_like(x)
        for k in range(SSM_CONV):
            pre = pre + w_ref[k:k + 1, :] * xsh[k]
        sig = jax.nn.sigmoid(pre)
        dout = jnp.where(j < nxs, dxs_ref[...] + ds_ref[...], jnp.where(j < nxs + nbc, dbm_ref[...], dcm_ref[...]))
        dpre = dout * (sig * (1.0 + pre * (1.0 - sig)))
        dx = jnp.zeros_like(x)
        for k in range(SSM_CONV):
            dx = dx + w_ref[k:k + 1, :] * _shift_up(dpre, SSM_CONV - 1 - k, rows, seq)
        dx_ref[...] = dx.astype(dx_ref.dtype)
        dw = jnp.concatenate([_colsum(dpre * xsh[k]) for k in range(SSM_CONV)], axis=0)
        db = _colsum(dpre)

        @pl.when(bb == 0)
        def _():
            dw_ref[...] = dw
            db_ref[...] = db

        @pl.when(bb > 0)
        def _():
            dw_ref[...] += dw
            db_ref[...] += db

    return pl.pallas_call(
        body, name=name, grid=(nc, bsz),
        in_specs=[pl.BlockSpec((seq, CONV_CB), lambda j, bb: (bb, off + j)),
                  pl.BlockSpec((SSM_CONV, CONV_CB), lambda j, bb: (0, j)),
                  pl.BlockSpec((1, CONV_CB), lambda j, bb: (0, j)),
                  pl.BlockSpec((seq, CONV_CB), lambda j, bb: (bb, jnp.minimum(j, nxs - 1))),
                  pl.BlockSpec((seq, CONV_CB), lambda j, bb: (bb, jnp.clip(j - nxs, 0, nbc - 1))),
                  pl.BlockSpec((seq, CONV_CB), lambda j, bb: (bb, jnp.clip(j - nxs - nbc, 0, nbc - 1))),
                  pl.BlockSpec((seq, CONV_CB), lambda j, bb: (bb, jnp.minimum(j, nxs - 1)))],
        out_specs=[pl.BlockSpec((seq, CONV_CB), lambda j, bb: (bb, j)),
                   pl.BlockSpec((SSM_CONV, CONV_CB), lambda j, bb: (0, j)),
                   pl.BlockSpec((1, CONV_CB), lambda j, bb: (0, j))],
        out_shape=[jax.ShapeDtypeStruct((bsz * seq, CONV_DIM), bf16), jax.ShapeDtypeStruct((SSM_CONV, CONV_DIM), f32),
                   jax.ShapeDtypeStruct((1, CONV_DIM), f32)],
        compiler_params=_cparams(("parallel", "arbitrary")),
    )(pm, w, b, dxs, dbm, dcm, dskipx)


CL = SSM_CHUNK


def _dot3(a, b, split_a):
    x = a if split_a else b
    t1 = x.astype(bf16)
    r1 = x - t1.astype(f32)
    t2 = r1.astype(bf16)
    t3 = (r1 - t2.astype(f32)).astype(bf16)
    if split_a:
        return _dot(t1, b) + _dot(t2, b) + _dot(t3, b)
    return _dot(a, t1) + _dot(a, t2) + _dot(a, t3)


def _ssd_specs(bsz, seq, rev):
    nch = seq // CL

    def ch(c):
        return (nch - 1 - c) if rev else c

    xg = pl.BlockSpec((CL, GW), lambda b, g, c: (b * nch + ch(c), g))
    lane128 = pl.BlockSpec((CL, 128), lambda b, g, c: (b * nch + ch(c), g))
    adt_t = pl.BlockSpec((128, CL), lambda b, g, c: (g, b * nch + ch(c)))
    bspec = pl.BlockSpec((CL, 128), lambda b, g, c: (b * nch + ch(c), SSM_INNER // 128 + g))
    cspec = pl.BlockSpec((CL, 128), lambda b, g, c: (b * nch + ch(c), SSM_INNER // 128 + SSM_GROUPS + g))
    st = pl.BlockSpec((1, 1, 1, SSM_STATE, GW), lambda b, g, c: (b, ch(c), g, 0, 0))
    return nch, xg, lane128, adt_t, bspec, cspec, st


def _expand_mat(width):
    r = lax.broadcasted_iota(jnp.int32, (128, HPG * width), 0)
    c = lax.broadcasted_iota(jnp.int32, (128, HPG * width), 1)
    return jnp.where((c >= r * width) & (c < (r + 1) * width), 1.0, 0.0).astype(bf16)


def _head_sums(z, e):
    hi = z.astype(bf16)
    lo = (z - hi.astype(f32)).astype(bf16)
    return _dot_nt(hi, e) + _dot_nt(lo, e)


def _ssd_common(dt_ref, adt_ref, adtt_ref):
    e64, e128 = _expand_mat(SSM_HD), _expand_mat(CL)
    csc = _dot3(_tri(CL, "ge"), adt_ref[...], False)
    csr = _dot3(adtt_ref[0:HPG, :], _tri(CL, "le"), True)
    return e64, csc, csr, _dot3(dt_ref[...], e64, True), _dot3(csc, e64, True), _dot3(csc, e128, True)


def ssd_fwd(name, xbc, dt, adt, adt_t, bsz, seq):
    nch, xg, lane128, adt_t_spec, bspec, cspec, st = _ssd_specs(bsz, seq, False)

    def body(x_ref, dt_ref, adt_ref, adtt_ref, b_ref, c_ref, y_ref, st_ref, s_scr, xd_scr):
        @pl.when(pl.program_id(2) == 0)
        def _():
            s_scr[...] = jnp.zeros_like(s_scr)

        _, _, csr, dt_e, cs_e, cs_b = _ssd_common(dt_ref, adt_ref, adtt_ref)
        cs_last = cs_e[CL - 1:CL, :]
        bm, cm = b_ref[...].astype(bf16), c_ref[...].astype(bf16)
        s_in = s_scr[...]
        st_ref[0, 0, 0] = s_in
        xd = x_ref[...] * dt_e
        xd_scr[...] = xd.astype(bf16)
        y_ref[...] = _dot(cm, s_in.astype(bf16)) * jnp.exp(cs_e)
        w = xd * jnp.exp(cs_last - cs_e)
        s_scr[...] = s_in * jnp.exp(cs_last) + _dot_tn(bm, w.astype(bf16))
        cb = _dot_nt(cm, bm)
        row = lax.broadcasted_iota(jnp.int32, (CL, CL), 0)
        col = lax.broadcasted_iota(jnp.int32, (CL, CL), 1)
        for h in range(HPG):
            hs = slice(h * SSM_HD, (h + 1) * SSM_HD)
            decay = jnp.exp(jnp.where(row >= col, cs_b[:, h * CL:(h + 1) * CL] - csr[h:h + 1, :], -1e30))
            y_ref[:, hs] += _dot((cb * decay).astype(bf16), xd_scr[:, hs])

    return pl.pallas_call(
        body, name=name, grid=(bsz, SSM_GROUPS, nch),
        in_specs=[xg, lane128, lane128, adt_t_spec, bspec, cspec],
        out_specs=[xg, st],
        out_shape=[jax.ShapeDtypeStruct((bsz * seq, SSM_INNER), f32),
                   jax.ShapeDtypeStruct((bsz, nch, SSM_GROUPS, SSM_STATE, GW), f32)],
        scratch_shapes=[pltpu.VMEM((SSM_STATE, GW), f32), pltpu.VMEM((CL, GW), bf16)],
        compiler_params=_cparams(("parallel", "parallel", "arbitrary")),
    )(xbc, dt, adt, adt_t, xbc, xbc)


def ssd_bwd(name, xbc, dt, adt, adt_t, states, dy, bsz, seq, exchange=None):
    nch, xg, lane128, adt_t_spec, bspec, cspec, st = _ssd_specs(bsz, seq, True)
    na = len(exchange[0]) if exchange else 0
    grid = (bsz, SSM_GROUPS, nch)

    def body(*refs):
        x_ref, dt_ref, adt_ref, adtt_ref, b_ref, c_ref, st_ref, dy_ref = refs[:8]
        dx_ref, db_ref, dc_ref, ddt_ref, dac_ref, dar_ref = refs[8 + na:14 + na]
        ds_scr, xd_scr, dxd_scr = refs[14 + 2 * na:17 + 2 * na]
        if exchange:
            _exchange_hook(exchange, refs[8:8 + na], refs[14 + na:14 + 2 * na], refs[17 + 2 * na:], *_grid_ends(grid))

        @pl.when(pl.program_id(2) == 0)
        def _():
            ds_scr[...] = jnp.zeros_like(ds_scr)

        e64, _, csr, dt_e, cs_e, cs_b = _ssd_common(dt_ref, adt_ref, adtt_ref)
        cs_last = cs_e[CL - 1:CL, :]
        bm, cm = b_ref[...].astype(bf16), c_ref[...].astype(bf16)
        x, dy, s_in, ds_out = x_ref[...], dy_ref[...], st_ref[0, 0, 0], ds_scr[...]
        e_last = jnp.exp(cs_last)
        d_end = jnp.exp(cs_last - cs_e)
        xd = x * dt_e
        xd_scr[...] = xd.astype(bf16)
        w = xd * d_end
        dq = dy * jnp.exp(cs_e)
        dc = _dot_nt(dq.astype(bf16), s_in.astype(bf16))
        ds_scr[...] = _dot_tn(cm, dq.astype(bf16)) + ds_out * e_last
        dw = _dot(bm, ds_out.astype(bf16))
        db = _dot_nt(w.astype(bf16), ds_out.astype(bf16))
        rw = dw * w
        dcs_e = dq * _dot(cm, s_in.astype(bf16)) - rw
        dcs_last = _colsum(rw) + _colsum(ds_out * s_in) * e_last
        is_last = lax.broadcasted_iota(jnp.int32, (CL, 1), 0) == CL - 1
        dcs_e = dcs_e + jnp.where(is_last, dcs_last, 0.0)
        dxd_scr[...] = dw * d_end
        cb, cbt = _dot_nt(cm, bm), _dot_nt(bm, cm)
        row = lax.broadcasted_iota(jnp.int32, (CL, CL), 0)
        col = lax.broadcasted_iota(jnp.int32, (CL, CL), 1)
        lane = lax.broadcasted_iota(jnp.int32, (CL, 128), 1)
        sub = lax.broadcasted_iota(jnp.int32, (HPG, CL), 0)
        dcb = jnp.zeros((CL, CL), f32)
        r_rows = jnp.zeros((CL, 128), f32)
        r_cols = jnp.zeros((HPG, CL), f32)
        for h in range(HPG):
            hs = slice(h * SSM_HD, (h + 1) * SSM_HD)
            diff = cs_b[:, h * CL:(h + 1) * CL] - csr[h:h + 1, :]
            decay = jnp.exp(jnp.where(row >= col, diff, -1e30))
            decay_t = jnp.exp(jnp.where(col >= row, -diff, -1e30))
            dy_h = dy_ref[:, hs].astype(bf16)
            dm = _dot_nt(dy_h, xd_scr[:, hs])
            dxd_scr[:, hs] += _dot((cbt * decay_t).astype(bf16), dy_h)
            r = dm * (cb * decay)
            dcb = dcb + dm * decay
            r_rows = r_rows + _dot2(r, jnp.where(lane == h, 1.0, 0.0).astype(bf16))
            r_cols = jnp.where(sub == h, _colsum(r), r_cols)
        dc_ref[...] = dc + _dot(dcb.astype(bf16), bm)
        db_ref[...] = db + _dot_tn(dcb.astype(bf16), cm)
        dxd = dxd_scr[...]
        dx_ref[...] = dxd * dt_e
        ddt_ref[...] = _head_sums(dxd * x, e64)
        dac_ref[...] = _dot3(_tri(CL, "le"), r_rows + _head_sums(dcs_e, e64), False)
        dar_ref[...] = jnp.zeros_like(dar_ref)
        dar_ref[0:HPG, :] = _dot3(-r_cols, _tri(CL, "ge"), True)

    t = bsz * seq
    return pl.pallas_call(
        body, name=name, grid=grid,
        in_specs=[xg, lane128, lane128, adt_t_spec, bspec, cspec, st, xg] + [HBM_SPEC] * na,
        out_specs=[xg, lane128, lane128, lane128, lane128, adt_t_spec] + [HBM_SPEC] * na,
        out_shape=[jax.ShapeDtypeStruct((t, SSM_INNER), f32), jax.ShapeDtypeStruct((t, DT_W), f32),
                   jax.ShapeDtypeStruct((t, DT_W), f32), jax.ShapeDtypeStruct((t, DT_W), f32),
                   jax.ShapeDtypeStruct((t, DT_W), f32), jax.ShapeDtypeStruct((DT_W, t), f32)]
        + ([jax.ShapeDtypeStruct(shape, dtype) for shape, dtype in exchange[1]] if exchange else []),
        scratch_shapes=[pltpu.VMEM((SSM_STATE, GW), f32), pltpu.VMEM((CL, GW), bf16), pltpu.VMEM((CL, GW), f32)]
        + (_copy_semaphores(na, exchange[2]) if exchange else []),
        compiler_params=_cparams(("arbitrary",) * 3 if exchange else ("parallel", "parallel", "arbitrary")),
    )(xbc, dt, adt, adt_t, xbc, xbc, states, dy, *(exchange[0] if exchange else []))


XA_BQ = 512


def _xattn(q, k, v):
    s = _dot_nt(q.astype(bf16), k.astype(bf16)) * (XA_HD ** -0.5)
    p = jnp.exp(s - jnp.max(s, axis=-1, keepdims=True))
    p = p / jnp.sum(p, axis=-1, keepdims=True)
    return _dot(p.astype(bf16), v.astype(bf16))


def xattn_fwd(name, q, kv, bsz, seq, mlen):
    bq = min(XA_BQ, seq)
    nq = seq // bq

    def body(q_ref, k_ref, v_ref, o_ref):
        o_ref[...] = _xattn(q_ref[...].astype(f32), k_ref[...].astype(f32), v_ref[...].astype(f32)).astype(o_ref.dtype)

    return pl.pallas_call(
        body, name=name, grid=(bsz, XA_HEADS, nq),
        in_specs=[pl.BlockSpec((bq, XA_HD), lambda b, h, i: (b * nq + i, h)),
                  pl.BlockSpec((mlen, XA_HD), lambda b, h, i: (b, h)),
                  pl.BlockSpec((mlen, XA_HD), lambda b, h, i: (b, XA_HEADS + h))],
        out_specs=pl.BlockSpec((bq, XA_HD), lambda b, h, i: (b * nq + i, h)),
        out_shape=jax.ShapeDtypeStruct((bsz * seq, D), bf16),
        compiler_params=_cparams(("parallel", "parallel", "parallel")),
    )(q, kv, kv)


def xattn_bwd(name, q, kv, do, bsz, seq, mlen):
    bq = min(XA_BQ, seq)
    nq = seq // bq

    def body(q_ref, k_ref, v_ref, do_ref, dq_ref, dk_ref, dv_ref):
        _, vjp = jax.vjp(_xattn, q_ref[...].astype(f32), k_ref[...].astype(f32), v_ref[...].astype(f32))
        dq, dk, dv = vjp(do_ref[...].astype(f32))
        dq_ref[...] = dq.astype(dq_ref.dtype)
        i = pl.program_id(2)

        @pl.when(i == 0)
        def _():
            dk_ref[...] = dk
            dv_ref[...] = dv

        @pl.when(i > 0)
        def _():
            dk_ref[...] += dk
            dv_ref[...] += dv

    kspec = pl.BlockSpec((mlen, XA_HD), lambda b, h, i: (b, h))
    vspec = pl.BlockSpec((mlen, XA_HD), lambda b, h, i: (b, XA_HEADS + h))
    qspec = pl.BlockSpec((bq, XA_HD), lambda b, h, i: (b * nq + i, h))
    return pl.pallas_call(
        body, name=name, grid=(bsz, XA_HEADS, nq),
        in_specs=[qspec, kspec, vspec, qspec],
        out_specs=[qspec, kspec, kspec],
        out_shape=[jax.ShapeDtypeStruct((bsz * seq, D), bf16), jax.ShapeDtypeStruct((bsz * mlen, D), f32),
                   jax.ShapeDtypeStruct((bsz * mlen, D), f32)],
        compiler_params=_cparams(("parallel", "parallel", "arbitrary")),
    )(q, kv, kv, do)


def _layer_fwd(l, x, mem, w, bsz, seq, mlen, exchange=None):
    n = f"l{l}_"
    sv = {"x0": x}
    sv["h1"] = h1 = rms_fwd(n + "rms_mix", x, w["g_pre_mix"])
    sv["pm"] = pm = _mm(n + "in_proj", h1, w["wm"], "nn", bf16)
    sv["pdt"] = pdt = _mm(n + "in_proj_dt", h1, w["wdt"], "nn")
    sv["o_att"], sv["tot"], *exchanged = sb_fwd(n + "sb_fwd", pm, bsz, seq, exchange)
    o_att = sv["o_att"]
    sv["xbc"] = xbc = conv_fwd(n + "conv_fwd", pm, w["conv_w"], w["conv_b"], bsz, seq)
    sv["dt"], sv["adt"] = dt, adt = dt_fwd(n + "dt_fwd", pdt, w["dt_bias"], w["a_log"])
    sv["adt_t"] = adt_t = adt.T
    sv["y_ssd"], sv["states"] = y_ssd, _ = ssd_fwd(n + "ssd_fwd", xbc, dt, adt, adt_t, bsz, seq)
    sv["o_ssm"] = o_ssm = gnorm_fwd(n + "gnorm_fwd", y_ssd, xbc, pm, w["d_skip"], w["g_ssm_norm"])
    sv["a"] = a = _mm(n + "br_att", o_att, w["w_br_att"], "nn", bf16)
    sv["s"] = s = _mm(n + "br_ssm", o_ssm, w["w_br_ssm"], "nn", bf16)
    sv["merged"] = merged = merge_fwd(n + "merge_fwd", pm, a, s)
    sv["u"] = u = _mm(n + "mix_out", merged, w["w_mix_out"], "nn", bf16)
    sv["x1"], sv["h2"] = x1, h2 = addnorm_rms_fwd(n + "post_mix", x, u, w["g_post_mix"], w["g_pre_xa"])
    sv["memn"] = memn = rms_fwd(n + "rms_mem", mem, w["g_mem"])
    sv["qx"] = qx = _mm(n + "xq", h2, w["w_xq"], "nn", bf16)
    sv["kv"] = kv = _mm(n + "xkv", memn, w["w_xkv"], "nn", bf16)
    sv["ox"] = ox = xattn_fwd(n + "xattn_fwd", qx, kv, bsz, seq, mlen)
    sv["yx"] = yx = _mm(n + "xo", ox, w["w_xo"], "nn", bf16)
    sv["x2"], sv["h3"] = x2, h3 = addnorm_rms_fwd(n + "post_xa", x1, yx, w["g_post_xa"], w["g_pre_ffn"])
    sv["gu"] = gu = _mm(n + "gu", h3, w["w_gu"], "nn", bf16)
    sv["act"] = act = swiglu_fwd(n + "swiglu_fwd", gu)
    sv["d"] = d = _mm(n + "down", act, w["w_down"], "nn", bf16)
    x3 = addnorm_fwd(n + "post_ffn", x2, d, w["g_post_ffn"])
    return x3, sv, exchanged


def _layer_bwd(l, dx, mem, w, sv, bsz, seq, mlen, exchange=None, early_exchange=None):
    n = f"l{l}_b_"
    g = {}
    dd, g["g_post_ffn"] = addnorm_bwd(n + "post_ffn", sv["d"], w["g_post_ffn"], dx)
    g["w_down"] = _mm(n + "dw_down", sv["act"], dd, "tn", bf16)
    dact = _mm(n + "dact", dd, w["w_down"], "nt", bf16)
    dgu = swiglu_bwd(n + "swiglu", sv["gu"], dact)
    g["w_gu"] = _mm(n + "dw_gu", sv["h3"], dgu, "tn", bf16)
    dh3 = _mm(n + "dh3", dgu, w["w_gu"], "nt", bf16)
    dx, g["g_pre_ffn"] = rms_bwd(n + "rms_ffn", sv["x2"], w["g_pre_ffn"], [dh3], dx)
    dyx, g["g_post_xa"] = addnorm_bwd(n + "post_xa", sv["yx"], w["g_post_xa"], dx)
    g["w_xo"] = _mm(n + "dw_xo", sv["ox"], dyx, "tn", bf16)
    dox = _mm(n + "dox", dyx, w["w_xo"], "nt", bf16)
    dqx, dk, dv = xattn_bwd(n + "xattn", sv["qx"], sv["kv"], dox, bsz, seq, mlen)
    g["w_xq"] = _mm(n + "dw_xq", sv["h2"], dqx, "tn", bf16)
    dh2 = _mm(n + "dh2", dqx, w["w_xq"], "nt", bf16)
    dkv = jnp.concatenate([dk, dv], axis=1)
    g["w_xkv"] = _mm(n + "dw_xkv", sv["memn"], dkv, "tn", bf16)
    dmemn = _mm(n + "dmemn", dkv, w["w_xkv"], "nt", bf16)
    _, g["g_mem"] = rms_bwd(n + "rms_mem", mem, w["g_mem"], [dmemn])
    dx, g["g_pre_xa"] = rms_bwd(n + "rms_xa", sv["x1"], w["g_pre_xa"], [dh2], dx)
    du, g["g_post_mix"] = addnorm_bwd(n + "post_mix", sv["u"], w["g_post_mix"], dx)
    g["w_mix_out"] = _mm(n + "dw_mix", sv["merged"], du, "tn", bf16)
    dmerged = _mm(n + "dmerged", du, w["w_mix_out"], "nt", bf16)
    dgates, da, ds = merge_bwd(n + "merge", sv["pm"], sv["a"], sv["s"], dmerged)
    g["w_br_att"] = _mm(n + "dw_att", sv["o_att"], da, "tn", bf16)
    do_att = _mm(n + "do_att", da, w["w_br_att"], "nt", bf16)
    g["w_br_ssm"] = _mm(n + "dw_ssm", sv["o_ssm"], ds, "tn", bf16)
    do_ssm = _mm(n + "do_ssm", ds, w["w_br_ssm"], "nt", bf16)
    dy_ssd, dxs_skip, dz, g["d_skip"], g["g_ssm_norm"] = gnorm_bwd(
        n + "gnorm", sv["y_ssd"], sv["xbc"], sv["pm"], w["d_skip"], w["g_ssm_norm"], do_ssm)
    dxs, dbm, dcm, ddt, dadt_c, dadt_r, *exchanged_early = ssd_bwd(
        n + "ssd", sv["xbc"], sv["dt"], sv["adt"], sv["adt_t"], sv["states"], dy_ssd, bsz, seq,
        early_exchange(g) if early_exchange else None)
    dxbc, g["conv_w"], g["conv_b"] = conv_bwd(n + "conv", sv["pm"], w["conv_w"], w["conv_b"], dxs, dbm, dcm, dxs_skip, bsz, seq)
    dpdt, g["dt_bias"], g["a_log"] = dt_bwd(n + "dt", sv["pdt"], w["dt_bias"], w["a_log"], ddt, dadt_c, dadt_r.T)
    dq, dk_, dv_, *exchanged = sb_bwd(n + "sb", sv["pm"], sv["tot"], do_att, bsz, seq, exchange)
    dpm = jnp.concatenate([dz, dxbc, dq, dk_, dv_, dgates], axis=1)
    g["wm"] = _mm(n + "dw_in", sv["h1"], dpm, "tn", bf16)
    g["wdt"] = _mm(n + "dw_in_dt", sv["h1"], dpdt, "tn", bf16)
    dh1 = _mm(n + "dh1", dpm, w["wm"], "nt", bf16)
    dh1_dt = _mm(n + "dh1_dt", dpdt, w["wdt"], "nt", bf16)
    dx, g["g_pre_mix"] = rms_bwd(n + "rms_mix", sv["x0"], w["g_pre_mix"], [dh1, dh1_dt], dx)
    return dx, g, exchanged, exchanged_early


def _group_pad(v):
    lead = v.shape[:-1]
    v = v.reshape(*lead, SSM_GROUPS, HPG)
    return jnp.pad(v, [(0, 0)] * (len(lead) + 1) + [(0, 128 - HPG)]).reshape(*lead, DT_W)


def _group_unpad(v):
    lead = v.shape[:-1]
    return v.reshape(*lead, SSM_GROUPS, 128)[..., :HPG].reshape(*lead, SSM_HEADS)


BIG = ("w_in", "w_br_att", "w_br_ssm", "w_mix_out", "w_xq", "w_xkv", "w_xo", "w_gu", "w_down")
GAINS = ("g_pre_mix", "g_post_mix", "g_pre_xa", "g_mem", "g_post_xa", "g_pre_ffn", "g_post_ffn")
HEAD_VECS = ("dt_bias", "a_log", "d_skip")
SMALL = GAINS + ("conv_w", "conv_b", "g_ssm_norm") + HEAD_VECS


def _prep_layer(p):
    w = {k: p[k] for k in BIG[1:]}
    if "wm" in p:
        w["wm"], w["wdt"] = p["wm"], p["wdt"]
    else:
        w_in = p["w_in"]
        w["wm"] = jnp.concatenate([w_in[:, 3072:8192], w_in[:, 0:3072], w_in[:, 8224:10272]], axis=1)
        w["wdt"] = _group_pad(w_in[:, 8192:8224])
    for k in GAINS + ("conv_b", "g_ssm_norm"):
        w[k] = p[k].reshape(1, -1)
    w["conv_w"] = p["conv_w"]
    w["dt_bias"] = _group_pad(p["dt_bias"]).reshape(1, DT_W)
    w["a_log"] = _group_pad(p["a_log"]).reshape(1, DT_W)
    w["d_skip"] = jnp.repeat(p["d_skip"], SSM_HD).reshape(1, SSM_INNER)
    return w


def _unprep_grads(g):
    out = {k: g[k] for k in BIG[1:]}
    gm = g["wm"]
    out["w_in"] = jnp.concatenate([gm[:, 5120:8192], gm[:, 0:5120], _group_unpad(g["wdt"]), gm[:, 8192:10240]], axis=1)
    for k in GAINS + ("conv_b", "g_ssm_norm"):
        out[k] = g[k].reshape(-1)
    out["conv_w"] = g["conv_w"]
    out["dt_bias"] = _group_unpad(g["dt_bias"]).reshape(-1)
    out["a_log"] = _group_unpad(g["a_log"]).reshape(-1)
    out["d_skip"] = g["d_skip"].reshape(SSM_HEADS, SSM_HD).sum(axis=1)
    return out


def _local_step(x, mem, target, ws, bsz, seq, mlen):
    saved = []
    for l in range(len(ws)):
        x, sv, _ = _layer_fwd(l, x, mem, ws[l], bsz, seq, mlen)
        saved.append(sv)
    dx, loss_lanes = loss_fwd_bwd("loss", x, target)
    grads = [None] * len(ws)
    for l in reversed(range(len(ws))):
        dx, grads[l], _, _ = _layer_bwd(l, dx, mem, ws[l], saved[l], bsz, seq, mlen)
    return loss_lanes, dx, grads


HBM_SPEC = pl.BlockSpec(memory_space=pltpu.HBM)
FLIP_C = (0, 0, 1)
FLIPS_CHIP = ((1, 0, 0), (0, 1, 0), (1, 1, 0))
FLIPS_ALL = tuple(((f >> 2) & 1, (f >> 1) & 1, f & 1) for f in range(1, 8))


def _view(ref, index):
    return ref.at[index] if index != () else ref


def _exchange(name, srcs, out_shapes, transfers, in_place=False):
    na = len(srcs)

    def body(*refs):
        out_refs = refs[na:2 * na]
        copies = _remote_copies(out_refs if in_place else refs[:na], out_refs, transfers, *refs[2 * na:])
        for cp in copies:
            cp.start()
        for cp in copies:
            cp.wait()

    if in_place:
        out_shape = [jax.ShapeDtypeStruct(s.shape, s.dtype) for s in srcs]
    else:
        out_shape = [jax.ShapeDtypeStruct(shape, dtype) for shape, dtype in out_shapes]
    return pl.pallas_call(
        body, name=name, out_shape=out_shape, in_specs=[HBM_SPEC] * na, out_specs=[HBM_SPEC] * na,
        input_output_aliases={a: a for a in range(na)} if in_place else {},
        scratch_shapes=_copy_semaphores(na, transfers),
    )(*srcs)


def _copy_semaphores(na, transfers):
    return [pltpu.SemaphoreType.DMA((na * len(transfers),)), pltpu.SemaphoreType.DMA((na * len(transfers),))]


def _remote_copies(src_refs, out_refs, transfers, send_sems, recv_sems):
    pos = (lax.axis_index("x"), lax.axis_index("y"), lax.axis_index("c"))
    nt, copies = len(transfers), []
    for a, (src_ref, out_ref) in enumerate(zip(src_refs, out_refs)):
        for t, (flip, src_index, dst_index) in enumerate(transfers):
            assert any(flip)
            peer = tuple(1 - p if f else p for p, f in zip(pos, flip))
            copies.append(pltpu.make_async_remote_copy(
                src_ref=_view(src_ref, src_index(*pos)), dst_ref=_view(out_ref, dst_index(*pos)),
                send_sem=send_sems.at[a * nt + t], recv_sem=recv_sems.at[a * nt + t],
                device_id=peer, device_id_type=MESH))
    return copies


def _exchange_hook(exchange, refs_in, refs_out, sems, first, last):
    copies = _remote_copies(refs_in, refs_out, exchange[2], *sems)

    @pl.when(first)
    def _():
        for cp in copies:
            cp.start()

    @pl.when(last)
    def _():
        for cp in copies:
            cp.wait()


def _at(*index):
    return lambda x, y, c: index


def _allgather8(name, v, me):
    got = _exchange(name, [v], [((7,) + v.shape, v.dtype)], [(fl, _at(), _at(j)) for j, fl in enumerate(FLIPS_ALL)])[0]
    rel = jnp.concatenate([v[None], got], axis=0)
    return jnp.stack([lax.dynamic_index_in_dim(rel, k ^ me, 0, keepdims=False) for k in range(8)])


def _sum8(name, parts):
    def fn(*p):
        acc = p[0]
        for q in p[1:]:
            acc = acc + q
        return acc

    return _rowwise(name, fn, [(parts[k], 0) for k in range(8)], [], [(1, f32)], [], width=128, bt=parts.shape[1])[0]


def _rows_block(r, w, bytes_per_row_elem):
    for bt in (512, 256, 128, 64, 32, 16, 8):
        if r % bt == 0 and bt * w * bytes_per_row_elem * 2 <= 16 * 1024 * 1024:
            return bt
    raise ValueError((r, w))


def _reduce8(name, wire, recv, shard, ci):
    _, _, h, w = wire.shape
    bt = _rows_block(h, w, 2 + 7 * 2 + 4)

    def body(s_ref, a_ref, b_ref, o_ref):
        acc = a_ref[0, 0].astype(f32)
        for j in range(7):
            acc = acc + b_ref[j].astype(f32)
        o_ref[0] = acc

    return pl.pallas_call(
        body, name=name,
        grid_spec=pltpu.PrefetchScalarGridSpec(
            num_scalar_prefetch=1, grid=(h // bt,),
            in_specs=[pl.BlockSpec((1, 1, bt, w), lambda i, s_ref: (s_ref[0], s_ref[1], i, 0)),
                      pl.BlockSpec((7, bt, w), lambda i, s_ref: (0, i, 0))],
            out_specs=pl.BlockSpec((1, bt, w), lambda i, s_ref: (s_ref[1], i, 0))),
        out_shape=jax.ShapeDtypeStruct((2, h, w), f32),
        compiler_params=_cparams(("parallel",)),
    )(jnp.stack([shard, ci]).astype(jnp.int32), wire, recv)


COL_SHARDED = ("w_in", "w_xkv", "w_gu")


def _ref_cols(pieces, lo, hi):
    c, out = pieces[0].shape[1], []
    for s, p in enumerate(pieces):
        a0, a1 = max(lo, s * c), min(hi, (s + 1) * c)
        if a0 < a1:
            out.append(p[:, a0 - s * c:a1 - s * c])
    return out


def _my_cols(gm, g32, lo, hi):
    out = []
    for r0, r1, src, shift in ((0, 3072, gm, 5120), (3072, 8192, gm, -3072), (8192, 8224, g32, -8192), (8224, IN_WIDTH, gm, -32)):
        a0, a1 = max(lo, r0), min(hi, r1)
        if a0 < a1:
            out.append(src[:, a0 + shift:a1 + shift])
    return out


def _pack(arrs, rows_multiple=8):
    flat = jnp.concatenate([a.reshape(-1) for a in arrs])
    pad = (-flat.shape[0]) % (128 * rows_multiple)
    return jnp.pad(flat, (0, pad)).reshape(-1, 128)


def _unpack(buf, shapes):
    flat, out, o = buf.reshape(-1), [], 0
    for s in shapes:
        n = math.prod(s)
        out.append(flat[o:o + n].reshape(s))
        o += n
    return out


def kernel(x, mem, g_pre_mix, w_in, conv_w, conv_b, dt_bias, a_log, d_skip, g_ssm_norm, w_br_att, w_br_ssm, w_mix_out, g_post_mix, g_pre_xa, g_mem, w_xq, w_xkv, w_xo, g_post_xa, g_pre_ffn, w_gu, w_down, g_post_ffn, loss_target, m_g_pre_mix, m_w_in, m_conv_w, m_conv_b, m_dt_bias, m_a_log, m_d_skip, m_g_ssm_norm, m_w_br_att, m_w_br_ssm, m_w_mix_out, m_g_post_mix, m_g_pre_xa, m_g_mem, m_w_xq, m_w_xkv, m_w_xo, m_g_post_xa, m_g_pre_ffn, m_w_gu, m_w_down, m_g_post_ffn, v_g_pre_mix, v_w_in, v_conv_w, v_conv_b, v_dt_bias, v_a_log, v_d_skip, v_g_ssm_norm, v_w_br_att, v_w_br_ssm, v_w_mix_out, v_g_post_mix, v_g_pre_xa, v_g_mem, v_w_xq, v_w_xkv, v_w_xo, v_g_post_xa, v_g_pre_ffn, v_w_gu, v_w_down, v_g_post_ffn):
    a = dict(locals())
    names = ("g_pre_mix", "w_in", "conv_w", "conv_b", "dt_bias", "a_log", "d_skip", "g_ssm_norm", "w_br_att", "w_br_ssm",
             "w_mix_out", "g_post_mix", "g_pre_xa", "g_mem", "w_xq", "w_xkv", "w_xo", "g_post_xa", "g_pre_ffn", "w_gu",
             "w_down", "g_post_ffn")
    depth = w_in.shape[0]
    bsz, seq, _ = x.shape
    mlen = mem.shape[1]
    xi, yi, ci = lax.axis_index("x"), lax.axis_index("y"), lax.axis_index("c")
    shard = 2 * xi + yi
    me = 2 * shard + ci

    cw_all = _allgather8("ag_conv_w", _pack([conv_w]), me)
    cw_shape = conv_w.shape
    conv_w_full = jnp.concatenate([_unpack(cw_all[2 * s], [cw_shape])[0] for s in range(4)], axis=2)

    halves = {k: (a[k].shape[1] // 2, a[k].shape[2]) for k in BIG}
    wbf = {k: a[k].astype(bf16) for k in BIG}
    ag_shapes = [((4, 2) + halves[k], bf16) for k in BIG]
    ag_transfers = [(fl, lambda x_, y_, c_: (c_,), lambda x_, y_, c_: (2 * x_ + y_, c_)) for fl in FLIPS_CHIP]
    fetched = [functools.partial(lambda x_, y_, c_, f: ((2 * x_ + y_) ^ f, c_), f=2 * fl[0] + fl[1]) for fl in FLIPS_CHIP]

    def ag_sources(l):
        return [wbf[k][l].reshape((2,) + halves[k]) for k in BIG]

    def layer_weights(l, got):
        got = [lax.dynamic_update_slice(g, s[None], (shard, 0, 0, 0)) for g, s in zip(got, ag_sources(l))]
        got = _exchange(f"ag_d2d_l{l}", got, None, [(FLIP_C, fn, fn) for fn in fetched], in_place=True)
        full = {k: g.reshape(4, 2 * halves[k][0], halves[k][1]) for k, g in zip(BIG, got)}
        p = {}
        for k in BIG[1:]:
            sh = full[k]
            p[k] = sh.transpose(1, 0, 2).reshape(sh.shape[1], -1) if k in COL_SHARDED else sh.reshape(-1, sh.shape[2])
        pieces = [full["w_in"][s] for s in range(4)]
        p["wm"] = jnp.concatenate(_ref_cols(pieces, 3072, 8192) + _ref_cols(pieces, 0, 3072)
                                  + _ref_cols(pieces, 8224, 10272), axis=1)
        p["wdt"] = _group_pad(jnp.concatenate(_ref_cols(pieces, 8192, 8224), axis=1))
        for k in SMALL:
            p[k] = conv_w_full[l] if k == "conv_w" else a[k][l]
        return _prep_layer(p)

    rs_transfers = [(fl, functools.partial(lambda x_, y_, c_, fs, fc: ((2 * x_ + y_) ^ fs, c_ ^ fc), fs=2 * fl[0] + fl[1], fc=fl[2]),
                     _at(j)) for j, fl in enumerate(FLIPS_ALL)]

    def rs_exchange(g, keys):
        srcs = []
        for k in keys:
            r, c = a[k].shape[1:]
            if k == "w_in":
                g32 = _group_unpad(g["wdt"])
                gk = jnp.stack([jnp.concatenate(_my_cols(g["wm"], g32, s * c, (s + 1) * c), axis=1) for s in range(4)])
            elif k in COL_SHARDED:
                gk = g[k].reshape(r, 4, c).transpose(1, 0, 2)
            else:
                gk = g[k]
            srcs.append(gk.astype(bf16).reshape((4, 2) + halves[k]))
        return srcs, [((7,) + halves[k], bf16) for k in keys], rs_transfers

    def layer_grads(l, keys, wires, got):
        red = [_reduce8(f"rs_sum_l{l}_{k}", w, r_, shard, ci) for k, w, r_ in zip(keys, wires, got)]
        my_half = lambda x_, y_, c_: (c_,)
        red = _exchange(f"rs_swap_l{l}_{len(keys)}", red, None, [(FLIP_C, my_half, my_half)], in_place=True)
        return {k: r_.reshape(a[k].shape[1:]) for k, r_ in zip(keys, red)}

    assert depth >= 2
    xt, memt = x.reshape(bsz * seq, D), mem.reshape(bsz * mlen, D)
    ws, saved = [], []
    got = _exchange("ag_ici_l0", ag_sources(0), ag_shapes, ag_transfers)
    for l in range(depth):
        ws.append(layer_weights(l, got))
        nxt = (ag_sources(l + 1), ag_shapes, ag_transfers) if l + 1 < depth else None
        xt, sv, got = _layer_fwd(l, xt, memt, ws[l], bsz, seq, mlen, nxt)
        saved.append(sv)
    gx, loss_lanes = loss_fwd_bwd("loss", xt, loss_target.reshape(bsz * seq, D))
    grads, gshard = [None] * depth, [None] * depth
    riding, early = None, {}
    for l in reversed(range(depth)):
        early_fn = (lambda g: early.setdefault("ex", rs_exchange(g, BIG[1:]))) if l == 0 else None
        gx, grads[l], got, got_early = _layer_bwd(l, gx, memt, ws[l], saved[l], bsz, seq, mlen, riding, early_fn)
        if riding is not None:
            gshard[l + 1] = layer_grads(l + 1, BIG, riding[0], got)
        riding = rs_exchange(grads[l], BIG) if l > 0 else None
    gshard[0] = layer_grads(0, BIG[1:], early["ex"][0], got_early)
    last = rs_exchange(grads[0], BIG[:1])
    grads = [_unprep_grads(g) for g in grads]

    out_g, out_d, out_m, out_v = {}, {}, {}, {}
    longest = max(BIG[1:], key=lambda k: math.prod(a[k].shape))
    for k in BIG[1:] + BIG[:1]:
        if k == "w_in":
            gshard[0].update(layer_grads(0, BIG[:1], last[0], got_last))
        shp = a[k].shape
        g = jnp.stack([gshard[l][k] for l in range(depth)])
        two_d = (shp[0] * shp[1], shp[2])
        d_, m_, v_, *got = adamw("adamw_" + k, a[k].reshape(two_d), g.reshape(two_d), a["m_" + k].reshape(two_d),
                                 a["v_" + k].reshape(two_d), last if k == longest else None)
        if k == longest:
            got_last = got
        out_g[k], out_d[k], out_m[k], out_v[k] = g, d_.reshape(shp), m_.reshape(shp), v_.reshape(shp)

    small_shapes = [(depth,) + (conv_w_full.shape[1:] if k == "conv_w" else a[k].shape[1:]) for k in SMALL]
    small = _pack([jnp.stack([grads[l][k] for l in range(depth)]) for k in SMALL] + [loss_lanes])
    total = _sum8("small_sum", _allgather8("ag_small", small, me))
    *gsmall, loss_l = _unpack(total, small_shapes + [loss_lanes.shape])
    gsmall = dict(zip(SMALL, gsmall))
    gsmall["conv_w"] = lax.dynamic_slice_in_dim(gsmall["conv_w"], shard * cw_shape[2], cw_shape[2], axis=2)
    loc_shapes = [a[k].shape for k in SMALL]
    d_, m_, v_ = adamw("adamw_small", _pack([a[k] for k in SMALL]), _pack([gsmall[k] for k in SMALL]),
                       _pack([a["m_" + k] for k in SMALL]), _pack([a["v_" + k] for k in SMALL]))
    for k, dd, mm, vv in zip(SMALL, _unpack(d_, loc_shapes), _unpack(m_, loc_shapes), _unpack(v_, loc_shapes)):
        out_g[k], out_d[k], out_m[k], out_v[k] = gsmall[k], dd, mm, vv

    loss = jnp.sum(loss_l)
    return (loss, gx.reshape(x.shape), *[out_g[k] for k in names], *[out_d[k] for k in names],
            *[out_m[k] for k in names], *[out_v[k] for k in names])
```

```python
import functools
import math

import jax
import jax.numpy as jnp
from jax import lax
from jax.experimental import pallas as pl
from jax.experimental.pallas import tpu as pltpu

f32, bf16 = jnp.float32, jnp.bfloat16

DEPTH = 4
D = 1024
SB_HEADS, SB_HD = 16, 64
SSM_INNER, SSM_HD, SSM_HEADS, SSM_GROUPS, SSM_STATE, SSM_CONV, SSM_CHUNK = 2048, 64, 32, 4, 128, 4, 128
HPG = SSM_HEADS // SSM_GROUPS
CONV_DIM = SSM_INNER + 2 * SSM_GROUPS * SSM_STATE
XA_HEADS, XA_HD = 4, 256
FFN = 2816
IN_WIDTH = 10272
RMS_EPS = 1e-6
LR, B1, B2, EPS, WD, STEP = 0.001, 0.9, 0.999, 1e-08, 0.01, 10

PM_W = 10240
OFF_Z, OFF_XBC, OFF_Q, OFF_K, OFF_V, OFF_GA, OFF_GS = 0, 2048, 5120, 6144, 7168, 8192, 9216
DT_W = SSM_GROUPS * 128

VMEM_LIMIT = 48 * 1024 * 1024
MESH = pl.DeviceIdType.MESH


def _cparams(sem):
    return pltpu.CompilerParams(dimension_semantics=sem, vmem_limit_bytes=VMEM_LIMIT)


def _tile(n):
    for t in (512, 256, 128):
        if n % t == 0:
            return t
    raise ValueError(f"dimension {n} is not a multiple of 128")


MM_VMEM_BUDGET = 34 * 1024 * 1024


def _mm_tiles(m, n, k, sa, sb, so):
    best = None
    for tm in (2048, 1024, 512, 256, 128):
        if m % tm:
            continue
        for tn in (2048, 1024, 512, 256, 128):
            if n % tn:
                continue
            for tk in (k, 2048, 1024, 512):
                if tk > k or k % tk:
                    continue
                vmem = 2 * (tm * tk * sa + tk * tn * sb + tm * tn * so) + tm * tn * 4 * (2 if tk < k else 1)
                if vmem > MM_VMEM_BUDGET:
                    continue
                traffic = m * k * sa * (n // tn) + k * n * sb * (m // tm) + m * n * so
                steps = (m // tm) * (n // tn) * (k // tk)
                accumulate = (k // tk > 1) * (k // tk) * m * n * 2
                key = (traffic + steps * 800_000 + accumulate, steps)
                if best is None or key < best[0]:
                    best = (key, (tm, tn, tk))
    assert best is not None, (m, n, k)
    return best[1]


def _mm(name, a, b, mode, out_dtype=f32):
    if mode == "nn":
        (m, k), (k2, n) = a.shape, b.shape
    elif mode == "nt":
        (m, k), (n, k2) = a.shape, b.shape
    else:
        (k, m), (k2, n) = a.shape, b.shape
    assert k == k2, (name, a.shape, b.shape, mode)
    tm, tn, tk = _mm_tiles(m, n, k, a.dtype.itemsize, b.dtype.itemsize, jnp.dtype(out_dtype).itemsize)
    nk = k // tk
    dn = {"nn": (((1,), (0,)), ((), ())), "nt": (((1,), (1,)), ((), ())), "tn": (((0,), (0,)), ((), ()))}[mode]

    def product(a_ref, b_ref):
        return lax.dot_general(a_ref[...].astype(bf16), b_ref[...].astype(bf16), dn, preferred_element_type=f32)

    def body_whole_k(a_ref, b_ref, o_ref):
        o_ref[...] = product(a_ref, b_ref).astype(o_ref.dtype)

    def body_k_loop(a_ref, b_ref, o_ref, acc_ref):
        kk = pl.program_id(2)

        @pl.when(kk == 0)
        def _():
            acc_ref[...] = product(a_ref, b_ref)

        @pl.when(kk > 0)
        def _():
            acc_ref[...] += product(a_ref, b_ref)

        @pl.when(kk == nk - 1)
        def _():
            o_ref[...] = acc_ref[...].astype(o_ref.dtype)

    a_spec = pl.BlockSpec((tk, tm), lambda i, j, kk: (kk, i)) if mode == "tn" else pl.BlockSpec((tm, tk), lambda i, j, kk: (i, kk))
    b_spec = pl.BlockSpec((tn, tk), lambda i, j, kk: (j, kk)) if mode == "nt" else pl.BlockSpec((tk, tn), lambda i, j, kk: (kk, j))
    return pl.pallas_call(
        body_whole_k if nk == 1 else body_k_loop, name=name, grid=(m // tm, n // tn, nk),
        in_specs=[a_spec, b_spec],
        out_specs=pl.BlockSpec((tm, tn), lambda i, j, kk: (i, j)),
        out_shape=jax.ShapeDtypeStruct((m, n), out_dtype),
        scratch_shapes=[] if nk == 1 else [pltpu.VMEM((tm, tn), f32)],
        compiler_params=_cparams(("parallel", "parallel", "arbitrary")),
    )(a, b)


def _rowwise(name, fn, rows, consts, out_rows, out_accs, *, width, ncol=1, bt=512, exchange=None):
    r = rows[0][0].shape[0]
    bt = min(bt, r)
    assert r % bt == 0, (name, r, bt)
    nrow = r // bt
    n_in = len(rows) + len(consts)
    n_or = len(out_rows)
    na = len(exchange[0]) if exchange else 0
    n_out = n_or + len(out_accs)

    def body(*refs):
        ins = [ref[...].astype(f32) for ref in refs[:n_in]]
        outs = fn(*ins)
        if not isinstance(outs, (tuple, list)):
            outs = (outs,)
        o_refs = refs[n_in + na:n_in + na + n_out]
        if exchange:
            _exchange_hook(exchange, refs[n_in:n_in + na], refs[n_in + na + n_out:n_in + 2 * na + n_out],
                           refs[n_in + 2 * na + n_out:], *_grid_ends((ncol, nrow)))
        for o_ref, val in zip(o_refs[:n_or], outs[:n_or]):
            o_ref[...] = val.astype(o_ref.dtype)
        if out_accs:
            i = pl.program_id(1)
            for o_ref, val in zip(o_refs[n_or:], outs[n_or:]):
                @pl.when(i == 0)
                def _(o_ref=o_ref, val=val):
                    o_ref[...] = val

                @pl.when(i > 0)
                def _(o_ref=o_ref, val=val):
                    o_ref[...] += val

    in_specs = [pl.BlockSpec((bt, width), functools.partial(lambda j, i, off: (i, off + j), off=off)) for _, off in rows]
    in_specs += [pl.BlockSpec((c.shape[0], width), functools.partial(lambda j, i, off: (0, off + j), off=off)) for c, off in consts]
    out_specs = [pl.BlockSpec((bt, mlt * width), lambda j, i: (i, j)) for mlt, _ in out_rows]
    out_specs += [pl.BlockSpec((k, width), lambda j, i: (0, j)) for k in out_accs]
    out_shape = [jax.ShapeDtypeStruct((r, ncol * mlt * width), dt) for mlt, dt in out_rows]
    out_shape += [jax.ShapeDtypeStruct((k, ncol * width), f32) for k in out_accs]
    if exchange:
        in_specs += [HBM_SPEC] * na
        out_specs += [HBM_SPEC] * na
        out_shape += [jax.ShapeDtypeStruct(shape, dtype) for shape, dtype in exchange[1]]
    return pl.pallas_call(
        body, name=name, grid=(ncol, nrow), in_specs=in_specs, out_specs=out_specs, out_shape=out_shape,
        scratch_shapes=_copy_semaphores(na, exchange[2]) if exchange else [],
        compiler_params=_cparams(("arbitrary", "arbitrary") if exchange else
                                 ("parallel", "arbitrary" if out_accs else "parallel")),
    )(*[a for a, _ in rows], *[c for c, _ in consts], *(exchange[0] if exchange else []))


def _rms(x, g):
    return x * lax.rsqrt(jnp.mean(x * x, axis=-1, keepdims=True) + RMS_EPS) * g


def _silu(x):
    return x * jax.nn.sigmoid(x)


def _softplus(x):
    return jnp.maximum(x, 0.0) + jnp.log(1.0 + jnp.exp(-jnp.abs(x)))


def _colsum(x):
    return jnp.sum(x, axis=0, keepdims=True)


def rms_fwd(name, x, g):
    return _rowwise(name, _rms, [(x, 0)], [(g, 0)], [(1, bf16)], [], width=D)[0]


def rms_bwd(name, x, g, dhs, dres=None):
    nd = len(dhs)

    def fn(x, *rest):
        dh = rest[0]
        for extra in rest[1:nd]:
            dh = dh + extra
        g = rest[-1]
        _, vjp = jax.vjp(_rms, x, g)
        dx, dg = vjp(dh.astype(f32))
        if dres is not None:
            dx = dx + rest[nd]
        return dx, dg

    rows = [(x, 0)] + [(d, 0) for d in dhs] + ([(dres, 0)] if dres is not None else [])
    return _rowwise(name, fn, rows, [(g, 0)], [(1, f32)], [1], width=D)


def addnorm_fwd(name, x, u, g):
    return _rowwise(name, lambda x, u, g: x + _rms(u, g), [(x, 0), (u, 0)], [(g, 0)], [(1, f32)], [], width=D)[0]


def addnorm_rms_fwd(name, x, u, g, g_next):
    def fn(x, u, g, g_next):
        x_new = x + _rms(u, g)
        return x_new, _rms(x_new, g_next)

    return _rowwise(name, fn, [(x, 0), (u, 0)], [(g, 0), (g_next, 0)], [(1, f32), (1, bf16)], [], width=D)


def addnorm_bwd(name, u, g, dx):
    def fn(u, dx, g):
        _, vjp = jax.vjp(_rms, u, g)
        return vjp(dx)

    return _rowwise(name, fn, [(u, 0), (dx, 0)], [(g, 0)], [(1, bf16)], [1], width=D)


def _merge(ga, gs, a, s):
    return jax.nn.sigmoid(ga) * a + jax.nn.sigmoid(gs) * s


def merge_fwd(name, pm, a, s):
    return _rowwise(name, _merge, [(pm, OFF_GA // D), (pm, OFF_GS // D), (a, 0), (s, 0)], [], [(1, bf16)], [], width=D)[0]


def merge_bwd(name, pm, a, s, dm):
    def fn(ga, gs, a, s, dm):
        _, vjp = jax.vjp(_merge, ga, gs, a, s)
        dga, dgs, da, ds = vjp(dm)
        return jnp.concatenate([dga, dgs], axis=1), da, ds

    return _rowwise(name, fn, [(pm, OFF_GA // D), (pm, OFF_GS // D), (a, 0), (s, 0), (dm, 0)], [],
                    [(2, bf16), (1, bf16), (1, bf16)], [], width=D)


def _swiglu(gate, up):
    return _silu(gate) * up


def swiglu_fwd(name, gu):
    return _rowwise(name, _swiglu, [(gu, 0), (gu, 1)], [], [(1, bf16)], [], width=FFN, bt=256)[0]


def swiglu_bwd(name, gu, dact):
    def fn(gate, up, dact):
        _, vjp = jax.vjp(_swiglu, gate, up)
        dg, du = vjp(dact.astype(f32))
        return jnp.concatenate([dg, du], axis=1)

    return _rowwise(name, fn, [(gu, 0), (gu, 1), (dact, 0)], [], [(2, bf16)], [], width=FFN, bt=128)[0]


GW = SSM_INNER // SSM_GROUPS


def _gnorm(y, xs, z, dskip, gn):
    yy = (y + dskip * xs) * _silu(z)
    return yy * lax.rsqrt(jnp.mean(yy * yy, axis=-1, keepdims=True) + RMS_EPS) * gn


def gnorm_fwd(name, y, xbc, pm, dskip, gn):
    return _rowwise(name, _gnorm, [(y, 0), (xbc, 0), (pm, OFF_Z // GW)], [(dskip, 0), (gn, 0)], [(1, bf16)], [],
                    width=GW, ncol=SSM_GROUPS)[0]


def gnorm_bwd(name, y, xbc, pm, dskip, gn, do):
    def fn(y, xs, z, do, dskip, gn):
        _, vjp = jax.vjp(_gnorm, y, xs, z, dskip, gn)
        return vjp(do.astype(f32))

    return _rowwise(name, fn, [(y, 0), (xbc, 0), (pm, OFF_Z // GW), (do, 0)], [(dskip, 0), (gn, 0)],
                    [(1, f32), (1, f32), (1, bf16)], [1, 1], width=GW, ncol=SSM_GROUPS)


def _dtfn(pdt, bias, alog):
    dt = _softplus(pdt + bias)
    return dt, -jnp.exp(alog) * dt


def dt_fwd(name, pdt, bias, alog):
    return _rowwise(name, _dtfn, [(pdt, 0)], [(bias, 0), (alog, 0)], [(1, f32), (1, f32)], [], width=DT_W)


def dt_bwd(name, pdt, bias, alog, ddt, dadt_c, dadt_r):
    def fn(pdt, ddt, dac, dar, bias, alog):
        _, vjp = jax.vjp(_dtfn, pdt, bias, alog)
        return vjp((ddt, dac + dar))

    return _rowwise(name, fn, [(pdt, 0), (ddt, 0), (dadt_c, 0), (dadt_r, 0)], [(bias, 0), (alog, 0)],
                    [(1, f32)], [1, 1], width=DT_W)


def loss_fwd_bwd(name, y, target):
    def fn(y, t):
        e = y - t
        return e * (1.0 / D), _colsum(e * e) * (0.5 / D)

    return _rowwise(name, fn, [(y, 0), (target, 0)], [], [(1, f32)], [1], width=D)


def adamw(name, w, g, m, v, exchange=None):
    r, c = w.shape

    def fn(w, g, m, v):
        m = B1 * m + (1.0 - B1) * g
        v = B2 * v + (1.0 - B2) * (g * g)
        m_hat = m / (1.0 - B1 ** STEP)
        v_hat = v / (1.0 - B2 ** STEP)
        return -LR * (m_hat / (jnp.sqrt(v_hat) + EPS) + WD * w), m, v

    bt = 256
    while bt > 8 and (r % bt or bt * c * 4 * 7 * 2 > 16 * 1024 * 1024):
        bt //= 2
    if r % bt:
        bt = r
    return _rowwise(name, fn, [(w, 0), (g, 0), (m, 0), (v, 0)], [], [(1, f32)] * 3, [], width=c, bt=bt, exchange=exchange)


SB_BQ, SB_BK = 512, 256
SB_UNROLL = 2
SB_SUB = 256


def _dot(a, b):
    return jnp.dot(a, b, preferred_element_type=f32)


def _dot_nt(a, b):
    return lax.dot_general(a, b, (((1,), (1,)), ((), ())), preferred_element_type=f32)


def _dot_tn(a, b):
    return lax.dot_general(a, b, (((0,), (0,)), ((), ())), preferred_element_type=f32)


def _dot2(x, tri):
    hi = x.astype(bf16)
    lo = (x - hi.astype(f32)).astype(bf16)
    return _dot(hi, tri) + _dot(lo, tri)


def _running(x, tri, reverse=False, exclusive=False, two_pass=True):
    sub = tri.shape[0]
    n = x.shape[1] // sub
    out, carry = [None] * n, None
    for j in (range(n - 1, -1, -1) if reverse else range(n)):
        xj = x[:, j * sub:(j + 1) * sub]
        pj = _dot2(xj, tri) if two_pass else _dot(xj.astype(bf16), tri)
        if carry is not None:
            pj = pj + carry
        out[j] = pj
        carry = pj[:, 0:1] if reverse else pj[:, sub - 1:sub]
        if exclusive:
            carry = carry + xj[:, sub - 1:sub]
    return out[0] if n == 1 else jnp.concatenate(out, axis=1)


def _tri(n, rel):
    r = lax.broadcasted_iota(jnp.int32, (n, n), 0)
    c = lax.broadcasted_iota(jnp.int32, (n, n), 1)
    m = {"ge": r >= c, "lt": r < c, "le": r <= c}[rel]
    return jnp.where(m, 1.0, 0.0).astype(bf16)


def _add_rows(full, r0, upd):
    return full + upd if r0 == 0 else jnp.concatenate([full[:r0], full[r0:] + upd], axis=0)


def _grid_ends(grid):
    ids = [pl.program_id(d) for d in range(len(grid))]
    first = functools.reduce(jnp.logical_and, [i == 0 for i in ids])
    last = functools.reduce(jnp.logical_and, [i == n - 1 for i, n in zip(ids, grid)])
    return first, last


def sb_fwd(name, pm, bsz, seq, exchange=None):
    bq = min(SB_BQ, seq)
    bk = min(SB_BK, bq)
    nq, nd = seq // bq, bq // bk
    step = SB_UNROLL if nd % SB_UNROLL == 0 else 1
    scale = SB_HD ** -0.5
    qb, kb_, vb_ = OFF_Q // 128, OFF_K // 128, OFF_V // 128
    na = len(exchange[0]) if exchange else 0
    grid = (bsz, 8, nq)

    def body(*refs):
        q_ref, k_ref, v_ref = refs[:3]
        o_ref, tot_ref = refs[3 + na:5 + na]
        if exchange:
            _exchange_hook(exchange, refs[3:3 + na], refs[5 + na:5 + 2 * na], refs[5 + 2 * na:], *_grid_ends(grid))
        i = pl.program_id(2)
        lane = lax.broadcasted_iota(jnp.int32, (1, 128), 1)
        m0 = lane < SB_HD
        q = q_ref[...].astype(f32) * scale
        qs = (jnp.where(m0, q, 0.0).astype(bf16), jnp.where(m0, 0.0, q).astype(bf16))
        wide = step * bk
        neg_tri = -_tri(min(SB_SUB, bk), "ge")
        t_idx = i * bq + lax.broadcasted_iota(jnp.int32, (bq, 1), 0)

        def block(ks, carry, masked, width):
            o_acc, c0, c1 = carry
            kblk = k_ref[pl.ds(ks, width), :].astype(bf16)
            vblk = v_ref[pl.ds(ks, width), :].astype(bf16)
            vs = (jnp.where(m0, vblk, 0).astype(bf16), jnp.where(m0, 0, vblk).astype(bf16))
            if masked:
                valid = (ks + lax.broadcasted_iota(jnp.int32, (1, width), 1)) < t_idx
            cs = [c0, c1]
            for h in range(2):
                z = _dot_nt(qs[h], kblk)
                sp = _softplus(z)
                if masked:
                    sp = jnp.where(valid, sp, 0.0)
                tl = _running(sp, neg_tri, reverse=True)
                w = jnp.exp(z + tl + cs[h])
                if masked:
                    w = jnp.where(valid, w, 0.0)
                o_acc = o_acc + _dot(w.astype(bf16), vs[h])
                cs[h] = cs[h] + tl[:, 0:1]
            return o_acc, cs[0], cs[1]

        zc = jnp.zeros((bq, 1), f32)
        carry = (jnp.zeros((bq, 128), f32), zc, zc)
        for d in range(nd):
            carry = block(pl.multiple_of((i * nd + nd - 1 - d) * bk, bk), carry, True, bk)
        carry = lax.fori_loop(0, i * (nd // step),
                              lambda n, c: block(pl.multiple_of((i * nd - step * (n + 1)) * bk, wide), c, False, wide), carry)
        o, c0, c1 = carry
        o_ref[...] = o.astype(o_ref.dtype)
        tot_ref[0, 0] = jnp.where(m0, c0, c1)

    return pl.pallas_call(
        body, name=name, grid=grid,
        in_specs=[pl.BlockSpec((bq, 128), lambda b, p, i: (b * nq + i, qb + p)),
                  pl.BlockSpec((seq, 128), lambda b, p, i: (b, kb_ + p)),
                  pl.BlockSpec((seq, 128), lambda b, p, i: (b, vb_ + p))] + [HBM_SPEC] * na,
        out_specs=[pl.BlockSpec((bq, 128), lambda b, p, i: (b * nq + i, p)),
                   pl.BlockSpec((1, 1, bq, 128), lambda b, p, i: (b, p, i, 0))] + [HBM_SPEC] * na,
        out_shape=[jax.ShapeDtypeStruct((bsz * seq, 1024), bf16), jax.ShapeDtypeStruct((bsz, 8, seq, 128), f32)]
        + ([jax.ShapeDtypeStruct(shape, dtype) for shape, dtype in exchange[1]] if exchange else []),
        scratch_shapes=_copy_semaphores(na, exchange[2]) if exchange else [],
        compiler_params=_cparams(("arbitrary",) * 3 if exchange else ("parallel",) * 3),
    )(pm, pm, pm, *(exchange[0] if exchange else []))


def sb_bwd(name, pm, tot, do, bsz, seq, exchange=None):
    bq = min(SB_BQ, seq)
    bk = min(SB_BK, bq)
    nq, nd = seq // bq, bq // bk
    step = SB_UNROLL if nd % SB_UNROLL == 0 else 1
    scale = SB_HD ** -0.5
    qb, kb_, vb_ = OFF_Q // 128, OFF_K // 128, OFF_V // 128
    na = len(exchange[0]) if exchange else 0
    grid = (bsz, 8, nq)

    def body(*refs):
        q_ref, k_ref, v_ref, do_ref, tot_ref = refs[:5]
        dq_ref, dk_ref, dv_ref = refs[5 + na:8 + na]
        dk_acc, dv_acc = refs[8 + 2 * na:10 + 2 * na]
        if exchange:
            _exchange_hook(exchange, refs[5:5 + na], refs[8 + na:8 + 2 * na], refs[10 + 2 * na:], *_grid_ends(grid))
        i = pl.program_id(2)

        @pl.when(i == 0)
        def _():
            dk_acc[...] = jnp.zeros_like(dk_acc)
            dv_acc[...] = jnp.zeros_like(dv_acc)

        lane = lax.broadcasted_iota(jnp.int32, (1, 128), 1)
        m0 = lane < SB_HD
        ms = (m0, jnp.logical_not(m0))
        q = q_ref[...].astype(f32) * scale
        qpair = q.astype(bf16)
        qs = (jnp.where(m0, q, 0.0).astype(bf16), jnp.where(m0, 0.0, q).astype(bf16))
        dout = do_ref[...].astype(f32)
        dos = (jnp.where(m0, dout, 0.0).astype(bf16), jnp.where(m0, 0.0, dout).astype(bf16))
        tot = tot_ref[0, 0]
        tots = (tot[:, 0:1], tot[:, SB_HD:SB_HD + 1])
        tri_lt = _tri(min(SB_SUB, bk), "lt")
        tri_le = _tri(min(SB_SUB, bk), "le")
        t_idx = i * bq + lax.broadcasted_iota(jnp.int32, (bq, 1), 0)

        def block(ks, carry, masked, r0=0):
            dq_acc, p0, p1, g0, g1 = carry
            kblk = k_ref[pl.ds(ks, bk), :].astype(bf16)
            vblk = v_ref[pl.ds(ks, bk), :].astype(bf16)
            if masked:
                valid = (ks + lax.broadcasted_iota(jnp.int32, (1, bk), 1)) < t_idx[r0:]
            ps, gs = [p0, p1], [g0, g1]
            dk_blk = jnp.zeros((bk, 128), f32)
            dv_blk = jnp.zeros((bk, 128), f32)
            for h in range(2):
                z = _dot_nt(qs[h][r0:], kblk)
                sp = _softplus(z)
                sig = jnp.exp(z - sp)
                if masked:
                    sp = jnp.where(valid, sp, 0.0)
                w = jnp.exp(z + tots[h][r0:] + ps[h][r0:] + _running(sp, tri_lt, exclusive=True))
                if masked:
                    w = jnp.where(valid, w, 0.0)
                g = _dot_nt(dos[h][r0:], vblk) * w
                dz = g - sig * (gs[h][r0:] + _running(g, tri_le, two_pass=False))
                if masked:
                    dz = jnp.where(valid, dz, 0.0)
                dz = dz.astype(bf16)
                dq_acc = _add_rows(dq_acc, r0, jnp.where(ms[h], _dot(dz, kblk), 0.0))
                dk_blk = dk_blk + jnp.where(ms[h], _dot_tn(dz, qpair[r0:]), 0.0)
                dv_blk = dv_blk + _dot_tn(w.astype(bf16), dos[h][r0:])
                ps[h] = _add_rows(ps[h], r0, jnp.sum(sp, axis=1, keepdims=True))
                gs[h] = _add_rows(gs[h], r0, jnp.sum(g, axis=1, keepdims=True))
            dk_acc[pl.ds(ks, bk), :] += dk_blk
            dv_acc[pl.ds(ks, bk), :] += dv_blk
            return dq_acc, ps[0], ps[1], gs[0], gs[1]

        zc = jnp.zeros((bq, 1), f32)
        carry = (jnp.zeros((bq, 128), f32), zc, zc, zc, zc)
        def far(n, c):
            for u in range(step):
                c = block(pl.multiple_of((step * n + u) * bk, bk), c, False)
            return c

        carry = lax.fori_loop(0, i * (nd // step), far, carry)
        for d in range(nd):
            carry = block(pl.multiple_of((i * nd + d) * bk, bk), carry, True, d * bk)
        dq_ref[...] = (carry[0] * scale).astype(dq_ref.dtype)

        @pl.when(i == nq - 1)
        def _():
            dk_ref[...] = dk_acc[...].astype(dk_ref.dtype)
            dv_ref[...] = dv_acc[...].astype(dv_ref.dtype)

    return pl.pallas_call(
        body, name=name, grid=grid,
        in_specs=[pl.BlockSpec((bq, 128), lambda b, p, i: (b * nq + i, qb + p)),
                  pl.BlockSpec((seq, 128), lambda b, p, i: (b, kb_ + p)),
                  pl.BlockSpec((seq, 128), lambda b, p, i: (b, vb_ + p)),
                  pl.BlockSpec((bq, 128), lambda b, p, i: (b * nq + i, p)),
                  pl.BlockSpec((1, 1, bq, 128), lambda b, p, i: (b, p, i, 0))] + [HBM_SPEC] * na,
        out_specs=[pl.BlockSpec((bq, 128), lambda b, p, i: (b * nq + i, p)),
                   pl.BlockSpec((seq, 128), lambda b, p, i: (b, p)),
                   pl.BlockSpec((seq, 128), lambda b, p, i: (b, p))] + [HBM_SPEC] * na,
        out_shape=[jax.ShapeDtypeStruct((bsz * seq, 1024), bf16)] * 3
        + ([jax.ShapeDtypeStruct(shape, dtype) for shape, dtype in exchange[1]] if exchange else []),
        scratch_shapes=[pltpu.VMEM((seq, 128), f32), pltpu.VMEM((seq, 128), f32)]
        + (_copy_semaphores(na, exchange[2]) if exchange else []),
        compiler_params=_cparams(("arbitrary",) * 3 if exchange else ("parallel", "parallel", "arbitrary")),
    )(pm, pm, pm, do, tot, *(exchange[0] if exchange else []))


CONV_CB = 256


def _shift_down(x, d, rows):
    return x if d == 0 else jnp.where(rows >= d, pltpu.roll(x, d, axis=0), 0.0)


def _shift_up(x, d, rows, n):
    return x if d == 0 else jnp.where(rows < n - d, pltpu.roll(x, n - d, axis=0), 0.0)


def conv_fwd(name, pm, w, b, bsz, seq):
    nc = CONV_DIM // CONV_CB
    off = OFF_XBC // CONV_CB

    def body(x_ref, w_ref, b_ref, o_ref):
        x = x_ref[...].astype(f32)
        rows = lax.broadcasted_iota(jnp.int32, x.shape, 0)
        pre = b_ref[...] + jnp.zeros_like(x)
        for k in range(SSM_CONV):
            pre = pre + w_ref[k:k + 1, :] * _shift_down(x, SSM_CONV - 1 - k, rows)
        o_ref[...] = _silu(pre)

    return pl.pallas_call(
        body, name=name, grid=(nc, bsz),
        in_specs=[pl.BlockSpec((seq, CONV_CB), lambda j, bb: (bb, off + j)),
                  pl.BlockSpec((SSM_CONV, CONV_CB), lambda j, bb: (0, j)),
                  pl.BlockSpec((1, CONV_CB), lambda j, bb: (0, j))],
        out_specs=pl.BlockSpec((seq, CONV_CB), lambda j, bb: (bb, j)),
        out_shape=jax.ShapeDtypeStruct((bsz * seq, CONV_DIM), f32),
        compiler_params=_cparams(("parallel", "parallel")),
    )(pm, w, b)


def conv_bwd(name, pm, w, b, dxs, dbm, dcm, dskipx, bsz, seq):
    nc = CONV_DIM // CONV_CB
    off = OFF_XBC // CONV_CB
    nxs = SSM_INNER // CONV_CB
    nbc = SSM_GROUPS * SSM_STATE // CONV_CB

    def body(x_ref, w_ref, b_ref, dxs_ref, dbm_ref, dcm_ref, ds_ref, dx_ref, dw_ref, db_ref):
        j, bb = pl.program_id(0), pl.program_id(1)
        x = x_ref[...].astype(f32)
        rows = lax.broadcasted_iota(jnp.int32, x.shape, 0)
        xsh = [_shift_down(x, SSM_CONV - 1 - k, rows) for k in range(SSM_CONV)]
        pre = b_ref[...] + jnp.zeros_like(x)
        for k in range(SSM_CONV):
            pre = pre + w_ref[k:k + 1, :] * xsh[k]
        sig = jax.nn.sigmoid(pre)
        dout = jnp.where(j < nxs, dxs_ref[...] + ds_ref[...], jnp.where(j < nxs + nbc, dbm_ref[...], dcm_ref[...]))
        dpre = dout * (sig * (1.0 + pre * (1.0 - sig)))
        dx = jnp.zeros_like(x)
        for k in range(SSM_CONV):
            dx = dx + w_ref[k:k + 1, :] * _shift_up(dpre, SSM_CONV - 1 - k, rows, seq)
        dx_ref[...] = dx.astype(dx_ref.dtype)
        dw = jnp.concatenate([_colsum(dpre * xsh[k]) for k in range(SSM_CONV)], axis=0)
        db = _colsum(dpre)

        @pl.when(bb == 0)
        def _():
            dw_ref[...] = dw
            db_ref[...] = db

        @pl.when(bb > 0)
        def _():
            dw_ref[...] += dw
            db_ref[...] += db

    return pl.pallas_call(
        body, name=name, grid=(nc, bsz),
        in_specs=[pl.BlockSpec((seq, CONV_CB), lambda j, bb: (bb, off + j)),
                  pl.BlockSpec((SSM_CONV, CONV_CB), lambda j, bb: (0, j)),
                  pl.BlockSpec((1, CONV_CB), lambda j, bb: (0, j)),
                  pl.BlockSpec((seq, CONV_CB), lambda j, bb: (bb, jnp.minimum(j, nxs - 1))),
                  pl.BlockSpec((seq, CONV_CB), lambda j, bb: (bb, jnp.clip(j - nxs, 0, nbc - 1))),
                  pl.BlockSpec((seq, CONV_CB), lambda j, bb: (bb, jnp.clip(j - nxs - nbc, 0, nbc - 1))),
                  pl.BlockSpec((seq, CONV_CB), lambda j, bb: (bb, jnp.minimum(j, nxs - 1)))],
        out_specs=[pl.BlockSpec((seq, CONV_CB), lambda j, bb: (bb, j)),
                   pl.BlockSpec((SSM_CONV, CONV_CB), lambda j, bb: (0, j)),
                   pl.BlockSpec((1, CONV_CB), lambda j, bb: (0, j))],
        out_shape=[jax.ShapeDtypeStruct((bsz * seq, CONV_DIM), bf16), jax.ShapeDtypeStruct((SSM_CONV, CONV_DIM), f32),
                   jax.ShapeDtypeStruct((1, CONV_DIM), f32)],
        compiler_params=_cparams(("parallel", "arbitrary")),
    )(pm, w, b, dxs, dbm, dcm, dskipx)


CL = SSM_CHUNK


def _dot3(a, b, split_a):
    x = a if split_a else b
    t1 = x.astype(bf16)
    r1 = x - t1.astype(f32)
    t2 = r1.astype(bf16)
    t3 = (r1 - t2.astype(f32)).astype(bf16)
    if split_a:
        return _dot(t1, b) + _dot(t2, b) + _dot(t3, b)
    return _dot(a, t1) + _dot(a, t2) + _dot(a, t3)


def _ssd_specs(bsz, seq, rev):
    nch = seq // CL

    def ch(c):
        return (nch - 1 - c) if rev else c

    xg = pl.BlockSpec((CL, GW), lambda b, g, c: (b * nch + ch(c), g))
    lane128 = pl.BlockSpec((CL, 128), lambda b, g, c: (b * nch + ch(c), g))
    adt_t = pl.BlockSpec((128, CL), lambda b, g, c: (g, b * nch + ch(c)))
    bspec = pl.BlockSpec((CL, 128), lambda b, g, c: (b * nch + ch(c), SSM_INNER // 128 + g))
    cspec = pl.BlockSpec((CL, 128), lambda b, g, c: (b * nch + ch(c), SSM_INNER // 128 + SSM_GROUPS + g))
    st = pl.BlockSpec((1, 1, 1, SSM_STATE, GW), lambda b, g, c: (b, ch(c), g, 0, 0))
    return nch, xg, lane128, adt_t, bspec, cspec, st


def _expand_mat(width):
    r = lax.broadcasted_iota(jnp.int32, (128, HPG * width), 0)
    c = lax.broadcasted_iota(jnp.int32, (128, HPG * width), 1)
    return jnp.where((c >= r * width) & (c < (r + 1) * width), 1.0, 0.0).astype(bf16)


def _head_sums(z, e):
    hi = z.astype(bf16)
    lo = (z - hi.astype(f32)).astype(bf16)
    return _dot_nt(hi, e) + _dot_nt(lo, e)


def _ssd_common(dt_ref, adt_ref, adtt_ref):
    e64, e128 = _expand_mat(SSM_HD), _expand_mat(CL)
    csc = _dot3(_tri(CL, "ge"), adt_ref[...], False)
    csr = _dot3(adtt_ref[0:HPG, :], _tri(CL, "le"), True)
    return e64, csc, csr, _dot3(dt_ref[...], e64, True), _dot3(csc, e64, True), _dot3(csc, e128, True)


def ssd_fwd(name, xbc, dt, adt, adt_t, bsz, seq):
    nch, xg, lane128, adt_t_spec, bspec, cspec, st = _ssd_specs(bsz, seq, False)

    def body(x_ref, dt_ref, adt_ref, adtt_ref, b_ref, c_ref, y_ref, st_ref, s_scr, xd_scr):
        @pl.when(pl.program_id(2) == 0)
        def _():
            s_scr[...] = jnp.zeros_like(s_scr)

        _, _, csr, dt_e, cs_e, cs_b = _ssd_common(dt_ref, adt_ref, adtt_ref)
        cs_last = cs_e[CL - 1:CL, :]
        bm, cm = b_ref[...].astype(bf16), c_ref[...].astype(bf16)
        s_in = s_scr[...]
        st_ref[0, 0, 0] = s_in
        xd = x_ref[...] * dt_e
        xd_scr[...] = xd.astype(bf16)
        y_ref[...] = _dot(cm, s_in.astype(bf16)) * jnp.exp(cs_e)
        w = xd * jnp.exp(cs_last - cs_e)
        s_scr[...] = s_in * jnp.exp(cs_last) + _dot_tn(bm, w.astype(bf16))
        cb = _dot_nt(cm, bm)
        row = lax.broadcasted_iota(jnp.int32, (CL, CL), 0)
        col = lax.broadcasted_iota(jnp.int32, (CL, CL), 1)
        for h in range(HPG):
            hs = slice(h * SSM_HD, (h + 1) * SSM_HD)
            decay = jnp.exp(jnp.where(row >= col, cs_b[:, h * CL:(h + 1) * CL] - csr[h:h + 1, :], -1e30))
            y_ref[:, hs] += _dot((cb * decay).astype(bf16), xd_scr[:, hs])

    return pl.pallas_call(
        body, name=name, grid=(bsz, SSM_GROUPS, nch),
        in_specs=[xg, lane128, lane128, adt_t_spec, bspec, cspec],
        out_specs=[xg, st],
        out_shape=[jax.ShapeDtypeStruct((bsz * seq, SSM_INNER), f32),
                   jax.ShapeDtypeStruct((bsz, nch, SSM_GROUPS, SSM_STATE, GW), f32)],
        scratch_shapes=[pltpu.VMEM((SSM_STATE, GW), f32), pltpu.VMEM((CL, GW), bf16)],
        compiler_params=_cparams(("parallel", "parallel", "arbitrary")),
    )(xbc, dt, adt, adt_t, xbc, xbc)


def ssd_bwd(name, xbc, dt, adt, adt_t, states, dy, bsz, seq, exchange=None):
    nch, xg, lane128, adt_t_spec, bspec, cspec, st = _ssd_specs(bsz, seq, True)
    na = len(exchange[0]) if exchange else 0
    grid = (bsz, SSM_GROUPS, nch)

    def body(*refs):
        x_ref, dt_ref, adt_ref, adtt_ref, b_ref, c_ref, st_ref, dy_ref = refs[:8]
        dx_ref, db_ref, dc_ref, ddt_ref, dac_ref, dar_ref = refs[8 + na:14 + na]
        ds_scr, xd_scr, dxd_scr = refs[14 + 2 * na:17 + 2 * na]
        if exchange:
            _exchange_hook(exchange, refs[8:8 + na], refs[14 + na:14 + 2 * na], refs[17 + 2 * na:], *_grid_ends(grid))

        @pl.when(pl.program_id(2) == 0)
        def _():
            ds_scr[...] = jnp.zeros_like(ds_scr)

        e64, _, csr, dt_e, cs_e, cs_b = _ssd_common(dt_ref, adt_ref, adtt_ref)
        cs_last = cs_e[CL - 1:CL, :]
        bm, cm = b_ref[...].astype(bf16), c_ref[...].astype(bf16)
        x, dy, s_in, ds_out = x_ref[...], dy_ref[...], st_ref[0, 0, 0], ds_scr[...]
        e_last = jnp.exp(cs_last)
        d_end = jnp.exp(cs_last - cs_e)
        xd = x * dt_e
        xd_scr[...] = xd.astype(bf16)
        w = xd * d_end
        dq = dy * jnp.exp(cs_e)
        dc = _dot_nt(dq.astype(bf16), s_in.astype(bf16))
        ds_scr[...] = _dot_tn(cm, dq.astype(bf16)) + ds_out * e_last
        dw = _dot(bm, ds_out.astype(bf16))
        db = _dot_nt(w.astype(bf16), ds_out.astype(bf16))
        rw = dw * w
        dcs_e = dq * _dot(cm, s_in.astype(bf16)) - rw
        dcs_last = _colsum(rw) + _colsum(ds_out * s_in) * e_last
        is_last = lax.broadcasted_iota(jnp.int32, (CL, 1), 0) == CL - 1
        dcs_e = dcs_e + jnp.where(is_last, dcs_last, 0.0)
        dxd_scr[...] = dw * d_end
        cb, cbt = _dot_nt(cm, bm), _dot_nt(bm, cm)
        row = lax.broadcasted_iota(jnp.int32, (CL, CL), 0)
        col = lax.broadcasted_iota(jnp.int32, (CL, CL), 1)
        lane = lax.broadcasted_iota(jnp.int32, (CL, 128), 1)
        sub = lax.broadcasted_iota(jnp.int32, (HPG, CL), 0)
        dcb = jnp.zeros((CL, CL), f32)
        r_rows = jnp.zeros((CL, 128), f32)
        r_cols = jnp.zeros((HPG, CL), f32)
        for h in range(HPG):
            hs = slice(h * SSM_HD, (h + 1) * SSM_HD)
            diff = cs_b[:, h * CL:(h + 1) * CL] - csr[h:h + 1, :]
            decay = jnp.exp(jnp.where(row >= col, diff, -1e30))
            decay_t = jnp.exp(jnp.where(col >= row, -diff, -1e30))
            dy_h = dy_ref[:, hs].astype(bf16)
            dm = _dot_nt(dy_h, xd_scr[:, hs])
            dxd_scr[:, hs] += _dot((cbt * decay_t).astype(bf16), dy_h)
            r = dm * (cb * decay)
            dcb = dcb + dm * decay
            r_rows = r_rows + _dot2(r, jnp.where(lane == h, 1.0, 0.0).astype(bf16))
            r_cols = jnp.where(sub == h, _colsum(r), r_cols)
        dc_ref[...] = dc + _dot(dcb.astype(bf16), bm)
        db_ref[...] = db + _dot_tn(dcb.astype(bf16), cm)
        dxd = dxd_scr[...]
        dx_ref[...] = dxd * dt_e
        ddt_ref[...] = _head_sums(dxd * x, e64)
        dac_ref[...] = _dot3(_tri(CL, "le"), r_rows + _head_sums(dcs_e, e64), False)
        dar_ref[...] = jnp.zeros_like(dar_ref)
        dar_ref[0:HPG, :] = _dot3(-r_cols, _tri(CL, "ge"), True)

    t = bsz * seq
    return pl.pallas_call(
        body, name=name, grid=grid,
        in_specs=[xg, lane128, lane128, adt_t_spec, bspec, cspec, st, xg] + [HBM_SPEC] * na,
        out_specs=[xg, lane128, lane128, lane128, lane128, adt_t_spec] + [HBM_SPEC] * na,
        out_shape=[jax.ShapeDtypeStruct((t, SSM_INNER), f32), jax.ShapeDtypeStruct((t, DT_W), f32),
                   jax.ShapeDtypeStruct((t, DT_W), f32), jax.ShapeDtypeStruct((t, DT_W), f32),
                   jax.ShapeDtypeStruct((t, DT_W), f32), jax.ShapeDtypeStruct((DT_W, t), f32)]
        + ([jax.ShapeDtypeStruct(shape, dtype) for shape, dtype in exchange[1]] if exchange else []),
        scratch_shapes=[pltpu.VMEM((SSM_STATE, GW), f32), pltpu.VMEM((CL, GW), bf16), pltpu.VMEM((CL, GW), f32)]
        + (_copy_semaphores(na, exchange[2]) if exchange else []),
        compiler_params=_cparams(("arbitrary",) * 3 if exchange else ("parallel", "parallel", "arbitrary")),
    )(xbc, dt, adt, adt_t, xbc, xbc, states, dy, *(exchange[0] if exchange else []))


XA_BQ = 512


def _xattn(q, k, v):
    s = _dot_nt(q.astype(bf16), k.astype(bf16)) * (XA_HD ** -0.5)
    p = jnp.exp(s - jnp.max(s, axis=-1, keepdims=True))
    p = p / jnp.sum(p, axis=-1, keepdims=True)
    return _dot(p.astype(bf16), v.astype(bf16))


def xattn_fwd(name, q, kv, bsz, seq, mlen):
    bq = min(XA_BQ, seq)
    nq = seq // bq

    def body(q_ref, k_ref, v_ref, o_ref):
        o_ref[...] = _xattn(q_ref[...].astype(f32), k_ref[...].astype(f32), v_ref[...].astype(f32)).astype(o_ref.dtype)

    return pl.pallas_call(
        body, name=name, grid=(bsz, XA_HEADS, nq),
        in_specs=[pl.BlockSpec((bq, XA_HD), lambda b, h, i: (b * nq + i, h)),
                  pl.BlockSpec((mlen, XA_HD), lambda b, h, i: (b, h)),
                  pl.BlockSpec((mlen, XA_HD), lambda b, h, i: (b, XA_HEADS + h))],
        out_specs=pl.BlockSpec((bq, XA_HD), lambda b, h, i: (b * nq + i, h)),
        out_shape=jax.ShapeDtypeStruct((bsz * seq, D), bf16),
        compiler_params=_cparams(("parallel", "parallel", "parallel")),
    )(q, kv, kv)


def xattn_bwd(name, q, kv, do, bsz, seq, mlen):
    bq = min(XA_BQ, seq)
    nq = seq // bq

    def body(q_ref, k_ref, v_ref, do_ref, dq_ref, dk_ref, dv_ref):
        _, vjp = jax.vjp(_xattn, q_ref[...].astype(f32), k_ref[...].astype(f32), v_ref[...].astype(f32))
        dq, dk, dv = vjp(do_ref[...].astype(f32))
        dq_ref[...] = dq.astype(dq_ref.dtype)
        i = pl.program_id(2)

        @pl.when(i == 0)
        def _():
            dk_ref[...] = dk
            dv_ref[...] = dv

        @pl.when(i > 0)
        def _():
            dk_ref[...] += dk
            dv_ref[...] += dv

    kspec = pl.BlockSpec((mlen, XA_HD), lambda b, h, i: (b, h))
    vspec = pl.BlockSpec((mlen, XA_HD), lambda b, h, i: (b, XA_HEADS + h))
    qspec = pl.BlockSpec((bq, XA_HD), lambda b, h, i: (b * nq + i, h))
    return pl.pallas_call(
        body, name=name, grid=(bsz, XA_HEADS, nq),
        in_specs=[qspec, kspec, vspec, qspec],
        out_specs=[qspec, kspec, kspec],
        out_shape=[jax.ShapeDtypeStruct((bsz * seq, D), bf16), jax.ShapeDtypeStruct((bsz * mlen, D), f32),
                   jax.ShapeDtypeStruct((bsz * mlen, D), f32)],
        compiler_params=_cparams(("parallel", "parallel", "arbitrary")),
    )(q, kv, kv, do)


def _layer_fwd(l, x, mem, w, bsz, seq, mlen, exchange=None, h1=None, g_next=None):
    n = f"l{l}_"
    sv = {"x0": x}
    sv["h1"] = h1 = rms_fwd(n + "rms_mix", x, w["g_pre_mix"]) if h1 is None else h1
    sv["pm"] = pm = _mm(n + "in_proj", h1, w["wm"], "nn", bf16)
    sv["pdt"] = pdt = _mm(n + "in_proj_dt", h1, w["wdt"], "nn")
    sv["o_att"], sv["tot"], *exchanged = sb_fwd(n + "sb_fwd", pm, bsz, seq, exchange)
    o_att = sv["o_att"]
    sv["xbc"] = xbc = conv_fwd(n + "conv_fwd", pm, w["conv_w"], w["conv_b"], bsz, seq)
    sv["dt"], sv["adt"] = dt, adt = dt_fwd(n + "dt_fwd", pdt, w["dt_bias"], w["a_log"])
    sv["adt_t"] = adt_t = adt.T
    sv["y_ssd"], sv["states"] = y_ssd, _ = ssd_fwd(n + "ssd_fwd", xbc, dt, adt, adt_t, bsz, seq)
    sv["o_ssm"] = o_ssm = gnorm_fwd(n + "gnorm_fwd", y_ssd, xbc, pm, w["d_skip"], w["g_ssm_norm"])
    sv["a"] = a = _mm(n + "br_att", o_att, w["w_br_att"], "nn", bf16)
    sv["s"] = s = _mm(n + "br_ssm", o_ssm, w["w_br_ssm"], "nn", bf16)
    sv["merged"] = merged = merge_fwd(n + "merge_fwd", pm, a, s)
    sv["u"] = u = _mm(n + "mix_out", merged, w["w_mix_out"], "nn", bf16)
    sv["x1"], sv["h2"] = x1, h2 = addnorm_rms_fwd(n + "post_mix", x, u, w["g_post_mix"], w["g_pre_xa"])
    sv["memn"] = memn = rms_fwd(n + "rms_mem", mem, w["g_mem"])
    sv["qx"] = qx = _mm(n + "xq", h2, w["w_xq"], "nn", bf16)
    sv["kv"] = kv = _mm(n + "xkv", memn, w["w_xkv"], "nn", bf16)
    sv["ox"] = ox = xattn_fwd(n + "xattn_fwd", qx, kv, bsz, seq, mlen)
    sv["yx"] = yx = _mm(n + "xo", ox, w["w_xo"], "nn", bf16)
    sv["x2"], sv["h3"] = x2, h3 = addnorm_rms_fwd(n + "post_xa", x1, yx, w["g_post_xa"], w["g_pre_ffn"])
    sv["gu"] = gu = _mm(n + "gu", h3, w["w_gu"], "nn", bf16)
    sv["act"] = act = swiglu_fwd(n + "swiglu_fwd", gu)
    sv["d"] = d = _mm(n + "down", act, w["w_down"], "nn", bf16)
    if g_next is None:
        x3 = addnorm_fwd(n + "post_ffn", x2, d, w["g_post_ffn"])
    else:
        x3, sv["h_next"] = addnorm_rms_fwd(n + "post_ffn", x2, d, w["g_post_ffn"], g_next)
    return x3, sv, exchanged


def _layer_bwd(l, dx, mem, w, sv, bsz, seq, mlen, exchange=None, early_exchange=None):
    n = f"l{l}_b_"
    g = {}
    dd, g["g_post_ffn"] = addnorm_bwd(n + "post_ffn", sv["d"], w["g_post_ffn"], dx)
    g["w_down"] = _mm(n + "dw_down", sv["act"], dd, "tn", bf16)
    dact = _mm(n + "dact", dd, w["w_down"], "nt", bf16)
    dgu = swiglu_bwd(n + "swiglu", sv["gu"], dact)
    g["w_gu"] = _mm(n + "dw_gu", sv["h3"], dgu, "tn", bf16)
    dh3 = _mm(n + "dh3", dgu, w["w_gu"], "nt", bf16)
    dx, g["g_pre_ffn"] = rms_bwd(n + "rms_ffn", sv["x2"], w["g_pre_ffn"], [dh3], dx)
    dyx, g["g_post_xa"] = addnorm_bwd(n + "post_xa", sv["yx"], w["g_post_xa"], dx)
    g["w_xo"] = _mm(n + "dw_xo", sv["ox"], dyx, "tn", bf16)
    dox = _mm(n + "dox", dyx, w["w_xo"], "nt", bf16)
    dqx, dk, dv = xattn_bwd(n + "xattn", sv["qx"], sv["kv"], dox, bsz, seq, mlen)
    g["w_xq"] = _mm(n + "dw_xq", sv["h2"], dqx, "tn", bf16)
    dh2 = _mm(n + "dh2", dqx, w["w_xq"], "nt", bf16)
    dkv = jnp.concatenate([dk, dv], axis=1)
    g["w_xkv"] = _mm(n + "dw_xkv", sv["memn"], dkv, "tn", bf16)
    dmemn = _mm(n + "dmemn", dkv, w["w_xkv"], "nt", bf16)
    _, g["g_mem"] = rms_bwd(n + "rms_mem", mem, w["g_mem"], [dmemn])
    dx, g["g_pre_xa"] = rms_bwd(n + "rms_xa", sv["x1"], w["g_pre_xa"], [dh2], dx)
    du, g["g_post_mix"] = addnorm_bwd(n + "post_mix", sv["u"], w["g_post_mix"], dx)
    g["w_mix_out"] = _mm(n + "dw_mix", sv["merged"], du, "tn", bf16)
    dmerged = _mm(n + "dmerged", du, w["w_mix_out"], "nt", bf16)
    dgates, da, ds = merge_bwd(n + "merge", sv["pm"], sv["a"], sv["s"], dmerged)
    g["w_br_att"] = _mm(n + "dw_att", sv["o_att"], da, "tn", bf16)
    do_att = _mm(n + "do_att", da, w["w_br_att"], "nt", bf16)
    g["w_br_ssm"] = _mm(n + "dw_ssm", sv["o_ssm"], ds, "tn", bf16)
    do_ssm = _mm(n + "do_ssm", ds, w["w_br_ssm"], "nt", bf16)
    dy_ssd, dxs_skip, dz, g["d_skip"], g["g_ssm_norm"] = gnorm_bwd(
        n + "gnorm", sv["y_ssd"], sv["xbc"], sv["pm"], w["d_skip"], w["g_ssm_norm"], do_ssm)
    dxs, dbm, dcm, ddt, dadt_c, dadt_r, *exchanged_early = ssd_bwd(
        n + "ssd", sv["xbc"], sv["dt"], sv["adt"], sv["adt_t"], sv["states"], dy_ssd, bsz, seq,
        early_exchange(g) if early_exchange else None)
    dxbc, g["conv_w"], g["conv_b"] = conv_bwd(n + "conv", sv["pm"], w["conv_w"], w["conv_b"], dxs, dbm, dcm, dxs_skip, bsz, seq)
    dpdt, g["dt_bias"], g["a_log"] = dt_bwd(n + "dt", sv["pdt"], w["dt_bias"], w["a_log"], ddt, dadt_c, dadt_r.T)
    dq, dk_, dv_, *exchanged = sb_bwd(n + "sb", sv["pm"], sv["tot"], do_att, bsz, seq, exchange)
    dpm = jnp.concatenate([dz, dxbc, dq, dk_, dv_, dgates], axis=1)
    g["wm"] = _mm(n + "dw_in", sv["h1"], dpm, "tn", bf16)
    g["wdt"] = _mm(n + "dw_in_dt", sv["h1"], dpdt, "tn", bf16)
    dh1 = _mm(n + "dh1", dpm, w["wm"], "nt", bf16)
    dh1_dt = _mm(n + "dh1_dt", dpdt, w["wdt"], "nt", bf16)
    dx, g["g_pre_mix"] = rms_bwd(n + "rms_mix", sv["x0"], w["g_pre_mix"], [dh1, dh1_dt], dx)
    return dx, g, exchanged, exchanged_early


def _group_pad(v):
    lead = v.shape[:-1]
    v = v.reshape(*lead, SSM_GROUPS, HPG)
    return jnp.pad(v, [(0, 0)] * (len(lead) + 1) + [(0, 128 - HPG)]).reshape(*lead, DT_W)


def _group_unpad(v):
    lead = v.shape[:-1]
    return v.reshape(*lead, SSM_GROUPS, 128)[..., :HPG].reshape(*lead, SSM_HEADS)


BIG = ("w_in", "w_br_att", "w_br_ssm", "w_mix_out", "w_xq", "w_xkv", "w_xo", "w_gu", "w_down")
GAINS = ("g_pre_mix", "g_post_mix", "g_pre_xa", "g_mem", "g_post_xa", "g_pre_ffn", "g_post_ffn")
HEAD_VECS = ("dt_bias", "a_log", "d_skip")
SMALL = GAINS + ("conv_w", "conv_b", "g_ssm_norm") + HEAD_VECS


def _prep_layer(p):
    w = {k: p[k] for k in BIG[1:]}
    if "wm" in p:
        w["wm"], w["wdt"] = p["wm"], p["wdt"]
    else:
        w_in = p["w_in"]
        w["wm"] = jnp.concatenate([w_in[:, 3072:8192], w_in[:, 0:3072], w_in[:, 8224:10272]], axis=1)
        w["wdt"] = _group_pad(w_in[:, 8192:8224])
    for k in GAINS + ("conv_b", "g_ssm_norm"):
        w[k] = p[k].reshape(1, -1)
    w["conv_w"] = p["conv_w"]
    w["dt_bias"] = _group_pad(p["dt_bias"]).reshape(1, DT_W)
    w["a_log"] = _group_pad(p["a_log"]).reshape(1, DT_W)
    w["d_skip"] = jnp.repeat(p["d_skip"], SSM_HD).reshape(1, SSM_INNER)
    return w


def _unprep_grads(g):
    out = {k: g[k] for k in BIG[1:]}
    gm = g["wm"]
    out["w_in"] = jnp.concatenate([gm[:, 5120:8192], gm[:, 0:5120], _group_unpad(g["wdt"]), gm[:, 8192:10240]], axis=1)
    for k in GAINS + ("conv_b", "g_ssm_norm"):
        out[k] = g[k].reshape(-1)
    out["conv_w"] = g["conv_w"]
    out["dt_bias"] = _group_unpad(g["dt_bias"]).reshape(-1)
    out["a_log"] = _group_unpad(g["a_log"]).reshape(-1)
    out["d_skip"] = g["d_skip"].reshape(SSM_HEADS, SSM_HD).sum(axis=1)
    return out


def _local_step(x, mem, target, ws, bsz, seq, mlen):
    saved = []
    for l in range(len(ws)):
        x, sv, _ = _layer_fwd(l, x, mem, ws[l], bsz, seq, mlen)
        saved.append(sv)
    dx, loss_lanes = loss_fwd_bwd("loss", x, target)
    grads = [None] * len(ws)
    for l in reversed(range(len(ws))):
        dx, grads[l], _, _ = _layer_bwd(l, dx, mem, ws[l], saved[l], bsz, seq, mlen)
    return loss_lanes, dx, grads


HBM_SPEC = pl.BlockSpec(memory_space=pltpu.HBM)
FLIP_C = (0, 0, 1)
FLIPS_CHIP = ((1, 0, 0), (0, 1, 0), (1, 1, 0))
FLIPS_ALL = tuple(((f >> 2) & 1, (f >> 1) & 1, f & 1) for f in range(1, 8))


def _view(ref, index):
    return ref.at[index] if index != () else ref


def _exchange(name, srcs, out_shapes, transfers, in_place=False):
    na = len(srcs)

    def body(*refs):
        out_refs = refs[na:2 * na]
        copies = _remote_copies(out_refs if in_place else refs[:na], out_refs, transfers, *refs[2 * na:])
        for cp in copies:
            cp.start()
        for cp in copies:
            cp.wait()

    if in_place:
        out_shape = [jax.ShapeDtypeStruct(s.shape, s.dtype) for s in srcs]
    else:
        out_shape = [jax.ShapeDtypeStruct(shape, dtype) for shape, dtype in out_shapes]
    return pl.pallas_call(
        body, name=name, out_shape=out_shape, in_specs=[HBM_SPEC] * na, out_specs=[HBM_SPEC] * na,
        input_output_aliases={a: a for a in range(na)} if in_place else {},
        scratch_shapes=_copy_semaphores(na, transfers),
    )(*srcs)


def _copy_semaphores(na, transfers):
    return [pltpu.SemaphoreType.DMA((na * len(transfers),)), pltpu.SemaphoreType.DMA((na * len(transfers),))]


def _remote_copies(src_refs, out_refs, transfers, send_sems, recv_sems):
    pos = (lax.axis_index("x"), lax.axis_index("y"), lax.axis_index("c"))
    nt, copies = len(transfers), []
    for a, (src_ref, out_ref) in enumerate(zip(src_refs, out_refs)):
        for t, (flip, src_index, dst_index) in enumerate(transfers):
            assert any(flip)
            peer = tuple(1 - p if f else p for p, f in zip(pos, flip))
            copies.append(pltpu.make_async_remote_copy(
                src_ref=_view(src_ref, src_index(*pos)), dst_ref=_view(out_ref, dst_index(*pos)),
                send_sem=send_sems.at[a * nt + t], recv_sem=recv_sems.at[a * nt + t],
                device_id=peer, device_id_type=MESH))
    return copies


def _exchange_hook(exchange, refs_in, refs_out, sems, first, last):
    copies = _remote_copies(refs_in, refs_out, exchange[2], *sems)

    @pl.when(first)
    def _():
        for cp in copies:
            cp.start()

    @pl.when(last)
    def _():
        for cp in copies:
            cp.wait()


def _at(*index):
    return lambda x, y, c: index


def _allgather8(name, v, me):
    got = _exchange(name, [v], [((7,) + v.shape, v.dtype)], [(fl, _at(), _at(j)) for j, fl in enumerate(FLIPS_ALL)])[0]
    rel = jnp.concatenate([v[None], got], axis=0)
    return jnp.stack([lax.dynamic_index_in_dim(rel, k ^ me, 0, keepdims=False) for k in range(8)])


def _sum8(name, parts):
    def fn(*p):
        acc = p[0]
        for q in p[1:]:
            acc = acc + q
        return acc

    return _rowwise(name, fn, [(parts[k], 0) for k in range(8)], [], [(1, f32)], [], width=128, bt=parts.shape[1])[0]


def _rows_block(r, w, bytes_per_row_elem):
    for bt in (512, 256, 128, 64, 32, 16, 8):
        if r % bt == 0 and bt * w * bytes_per_row_elem * 2 <= 16 * 1024 * 1024:
            return bt
    raise ValueError((r, w))


def _reduce8(name, wire, recv, shard, ci):
    _, _, h, w = wire.shape
    bt = _rows_block(h, w, 2 + 7 * 2 + 4)

    def body(s_ref, a_ref, b_ref, o_ref):
        acc = a_ref[0, 0].astype(f32)
        for j in range(7):
            acc = acc + b_ref[j].astype(f32)
        o_ref[0] = acc

    return pl.pallas_call(
        body, name=name,
        grid_spec=pltpu.PrefetchScalarGridSpec(
            num_scalar_prefetch=1, grid=(h // bt,),
            in_specs=[pl.BlockSpec((1, 1, bt, w), lambda i, s_ref: (s_ref[0], s_ref[1], i, 0)),
                      pl.BlockSpec((7, bt, w), lambda i, s_ref: (0, i, 0))],
            out_specs=pl.BlockSpec((1, bt, w), lambda i, s_ref: (s_ref[1], i, 0))),
        out_shape=jax.ShapeDtypeStruct((2, h, w), f32),
        compiler_params=_cparams(("parallel",)),
    )(jnp.stack([shard, ci]).astype(jnp.int32), wire, recv)


COL_SHARDED = ("w_in", "w_xkv", "w_gu")


def _ref_cols(pieces, lo, hi):
    c, out = pieces[0].shape[1], []
    for s, p in enumerate(pieces):
        a0, a1 = max(lo, s * c), min(hi, (s + 1) * c)
        if a0 < a1:
            out.append(p[:, a0 - s * c:a1 - s * c])
    return out


def _my_cols(gm, g32, lo, hi):
    out = []
    for r0, r1, src, shift in ((0, 3072, gm, 5120), (3072, 8192, gm, -3072), (8192, 8224, g32, -8192), (8224, IN_WIDTH, gm, -32)):
        a0, a1 = max(lo, r0), min(hi, r1)
        if a0 < a1:
            out.append(src[:, a0 + shift:a1 + shift])
    return out


def _pack(arrs, rows_multiple=8):
    flat = jnp.concatenate([a.reshape(-1) for a in arrs])
    pad = (-flat.shape[0]) % (128 * rows_multiple)
    return jnp.pad(flat, (0, pad)).reshape(-1, 128)


def _unpack(buf, shapes):
    flat, out, o = buf.reshape(-1), [], 0
    for s in shapes:
        n = math.prod(s)
        out.append(flat[o:o + n].reshape(s))
        o += n
    return out


def kernel(x, mem, g_pre_mix, w_in, conv_w, conv_b, dt_bias, a_log, d_skip, g_ssm_norm, w_br_att, w_br_ssm, w_mix_out, g_post_mix, g_pre_xa, g_mem, w_xq, w_xkv, w_xo, g_post_xa, g_pre_ffn, w_gu, w_down, g_post_ffn, loss_target, m_g_pre_mix, m_w_in, m_conv_w, m_conv_b, m_dt_bias, m_a_log, m_d_skip, m_g_ssm_norm, m_w_br_att, m_w_br_ssm, m_w_mix_out, m_g_post_mix, m_g_pre_xa, m_g_mem, m_w_xq, m_w_xkv, m_w_xo, m_g_post_xa, m_g_pre_ffn, m_w_gu, m_w_down, m_g_post_ffn, v_g_pre_mix, v_w_in, v_conv_w, v_conv_b, v_dt_bias, v_a_log, v_d_skip, v_g_ssm_norm, v_w_br_att, v_w_br_ssm, v_w_mix_out, v_g_post_mix, v_g_pre_xa, v_g_mem, v_w_xq, v_w_xkv, v_w_xo, v_g_post_xa, v_g_pre_ffn, v_w_gu, v_w_down, v_g_post_ffn):
    a = dict(locals())
    names = ("g_pre_mix", "w_in", "conv_w", "conv_b", "dt_bias", "a_log", "d_skip", "g_ssm_norm", "w_br_att", "w_br_ssm",
             "w_mix_out", "g_post_mix", "g_pre_xa", "g_mem", "w_xq", "w_xkv", "w_xo", "g_post_xa", "g_pre_ffn", "w_gu",
             "w_down", "g_post_ffn")
    depth = w_in.shape[0]
    bsz, seq, _ = x.shape
    mlen = mem.shape[1]
    xi, yi, ci = lax.axis_index("x"), lax.axis_index("y"), lax.axis_index("c")
    shard = 2 * xi + yi
    me = 2 * shard + ci

    cw_all = _allgather8("ag_conv_w", _pack([conv_w]), me)
    cw_shape = conv_w.shape
    conv_w_full = jnp.concatenate([_unpack(cw_all[2 * s], [cw_shape])[0] for s in range(4)], axis=2)

    halves = {k: (a[k].shape[1] // 2, a[k].shape[2]) for k in BIG}
    wbf = {k: a[k].astype(bf16) for k in BIG}
    ag_shapes = [((4, 2) + halves[k], bf16) for k in BIG]
    ag_transfers = [(fl, lambda x_, y_, c_: (c_,), lambda x_, y_, c_: (2 * x_ + y_, c_)) for fl in FLIPS_CHIP]
    fetched = [functools.partial(lambda x_, y_, c_, f: ((2 * x_ + y_) ^ f, c_), f=2 * fl[0] + fl[1]) for fl in FLIPS_CHIP]

    def ag_sources(l):
        return [wbf[k][l].reshape((2,) + halves[k]) for k in BIG]

    def layer_weights(l, got):
        got = [lax.dynamic_update_slice(g, s[None], (shard, 0, 0, 0)) for g, s in zip(got, ag_sources(l))]
        got = _exchange(f"ag_d2d_l{l}", got, None, [(FLIP_C, fn, fn) for fn in fetched], in_place=True)
        full = {k: g.reshape(4, 2 * halves[k][0], halves[k][1]) for k, g in zip(BIG, got)}
        p = {}
        for k in BIG[1:]:
            sh = full[k]
            p[k] = sh.transpose(1, 0, 2).reshape(sh.shape[1], -1) if k in COL_SHARDED else sh.reshape(-1, sh.shape[2])
        pieces = [full["w_in"][s] for s in range(4)]
        p["wm"] = jnp.concatenate(_ref_cols(pieces, 3072, 8192) + _ref_cols(pieces, 0, 3072)
                                  + _ref_cols(pieces, 8224, 10272), axis=1)
        p["wdt"] = _group_pad(jnp.concatenate(_ref_cols(pieces, 8192, 8224), axis=1))
        for k in SMALL:
            p[k] = conv_w_full[l] if k == "conv_w" else a[k][l]
        return _prep_layer(p)

    rs_transfers = [(fl, functools.partial(lambda x_, y_, c_, fs, fc: ((2 * x_ + y_) ^ fs, c_ ^ fc), fs=2 * fl[0] + fl[1], fc=fl[2]),
                     _at(j)) for j, fl in enumerate(FLIPS_ALL)]

    def rs_exchange(g, keys):
        srcs = []
        for k in keys:
            r, c = a[k].shape[1:]
            if k == "w_in":
                g32 = _group_unpad(g["wdt"])
                gk = jnp.stack([jnp.concatenate(_my_cols(g["wm"], g32, s * c, (s + 1) * c), axis=1) for s in range(4)])
            elif k in COL_SHARDED:
                gk = g[k].reshape(r, 4, c).transpose(1, 0, 2)
            else:
                gk = g[k]
            srcs.append(gk.astype(bf16).reshape((4, 2) + halves[k]))
        return srcs, [((7,) + halves[k], bf16) for k in keys], rs_transfers

    def layer_grads(l, keys, wires, got):
        red = [_reduce8(f"rs_sum_l{l}_{k}", w, r_, shard, ci) for k, w, r_ in zip(keys, wires, got)]
        my_half = lambda x_, y_, c_: (c_,)
        red = _exchange(f"rs_swap_l{l}_{len(keys)}", red, None, [(FLIP_C, my_half, my_half)], in_place=True)
        return {k: r_.reshape(a[k].shape[1:]) for k, r_ in zip(keys, red)}

    assert depth >= 2
    xt, memt = x.reshape(bsz * seq, D), mem.reshape(bsz * mlen, D)
    ws, saved = [], []
    got = _exchange("ag_ici_l0", ag_sources(0), ag_shapes, ag_transfers)
    for l in range(depth):
        ws.append(layer_weights(l, got))
        nxt = (ag_sources(l + 1), ag_shapes, ag_transfers) if l + 1 < depth else None
        g_next = g_pre_mix[l + 1].reshape(1, D) if l + 1 < depth else None
        xt, sv, got = _layer_fwd(l, xt, memt, ws[l], bsz, seq, mlen, nxt, saved[-1]["h_next"] if l else None, g_next)
        saved.append(sv)
    gx, loss_lanes = loss_fwd_bwd("loss", xt, loss_target.reshape(bsz * seq, D))
    grads, gshard = [None] * depth, [None] * depth
    riding, early = None, {}
    for l in reversed(range(depth)):
        early_fn = (lambda g: early.setdefault("ex", rs_exchange(g, BIG[1:]))) if l == 0 else None
        gx, grads[l], got, got_early = _layer_bwd(l, gx, memt, ws[l], saved[l], bsz, seq, mlen, riding, early_fn)
        if riding is not None:
            gshard[l + 1] = layer_grads(l + 1, BIG, riding[0], got)
        riding = rs_exchange(grads[l], BIG) if l > 0 else None
    gshard[0] = layer_grads(0, BIG[1:], early["ex"][0], got_early)
    last = rs_exchange(grads[0], BIG[:1])
    grads = [_unprep_grads(g) for g in grads]

    out_g, out_d, out_m, out_v = {}, {}, {}, {}
    longest = max(BIG[1:], key=lambda k: math.prod(a[k].shape))
    for k in BIG[1:] + BIG[:1]:
        if k == "w_in":
            gshard[0].update(layer_grads(0, BIG[:1], last[0], got_last))
        shp = a[k].shape
        g = jnp.stack([gshard[l][k] for l in range(depth)])
        two_d = (shp[0] * shp[1], shp[2])
        d_, m_, v_, *got = adamw("adamw_" + k, a[k].reshape(two_d), g.reshape(two_d), a["m_" + k].reshape(two_d),
                                 a["v_" + k].reshape(two_d), last if k == longest else None)
        if k == longest:
            got_last = got
        out_g[k], out_d[k], out_m[k], out_v[k] = g, d_.reshape(shp), m_.reshape(shp), v_.reshape(shp)

    small_shapes = [(depth,) + (conv_w_full.shape[1:] if k == "conv_w" else a[k].shape[1:]) for k in SMALL]
    small = _pack([jnp.stack([grads[l][k] for l in range(depth)]) for k in SMALL] + [loss_lanes])
    total = _sum8("small_sum", _allgather8("ag_small", small, me))
    *gsmall, loss_l = _unpack(total, small_shapes + [loss_lanes.shape])
    gsmall = dict(zip(SMALL, gsmall))
    gsmall["conv_w"] = lax.dynamic_slice_in_dim(gsmall["conv_w"], shard * cw_shape[2], cw_shape[2], axis=2)
    loc_shapes = [a[k].shape for k in SMALL]
    d_, m_, v_ = adamw("adamw_small", _pack([a[k] for k in SMALL]), _pack([gsmall[k] for k in SMALL]),
                       _pack([a["m_" + k] for k in SMALL]), _pack([a["v_" + k] for k in SMALL]))
    for k, dd, mm, vv in zip(SMALL, _unpack(d_, loc_shapes), _unpack(m_, loc_shapes), _unpack(v_, loc_shapes)):
        out_g[k], out_d[k], out_m[k], out_v[k] = gsmall[k], dd, mm, vv

    loss = jnp.sum(loss_l)
    return (loss, gx.reshape(x.shape), *[out_g[k] for k in names], *[out_d[k] for k in names],
            *[out_m[k] for k in names], *[out_v[k] for k in names])
```
